```python
import jax
import jax.numpy as jnp
from jax import lax
import numpy as np

D_MODEL = 2048
BATCH = 1
SEQ = 16384
DEPTH = 2
DEC_BATCH = 16
DEC_SEQ = 2048
PAST_LEN = 128

GLA_HEADS = 4
GLA_DK = 64
GLA_DV = 128
GLA_RANK = 16
GLA_TAU = 16.0
GLA_CHUNK = 64
SWA_GROUPS = ((128, 1), (512, 4), (2048, 16))
SWA_HEADS = 4
SWA_DH = 128
SWA_NH = SWA_HEADS * len(SWA_GROUPS)
SWA_BLOCK = 64
ROPE_THETA = 500000.0
ROPE_DIM = SWA_DH // 4
FNET_GROUPS = 4
FNET_DG = 128
RET_HEADS = 4
RET_DH = 128
RET_CHUNK = 128
RET_THETA = 10000.0
N_BRANCH = 4
BRANCH_W = 512
N_EXPERTS = 16
EXPERT_FF = 2048
CAPACITY_FACTOR = 2
DN_ALPHA = (2 * DEPTH) ** 0.25
DN_BETA = (8 * DEPTH) ** -0.25
LN_EPS = 1e-5

IN_SPLITS = (
    GLA_HEADS * GLA_DK, GLA_HEADS * GLA_DK, GLA_HEADS * GLA_DV, GLA_HEADS * GLA_DV, 2 * GLA_RANK,
    SWA_NH * SWA_DH, SWA_NH * SWA_DH, SWA_NH * SWA_DH,
    FNET_GROUPS * FNET_DG,
    RET_HEADS * RET_DH, RET_HEADS * RET_DH, RET_HEADS * RET_DH, RET_HEADS * RET_DH,
)
IN_COLS = sum(IN_SPLITS)

kernel_name = "hybrid_bidir_gla_dilated_fnet_retnet_ecmoe"


def _layernorm(x, g, b):
    xf = x.astype(jnp.float32)
    mu = jnp.mean(xf, -1, keepdims=True)
    var = jnp.mean(jnp.square(xf - mu), -1, keepdims=True)
    y = (xf - mu) * lax.rsqrt(var + LN_EPS)
    return (y * g.astype(jnp.float32) + b.astype(jnp.float32)).astype(x.dtype)


def _headnorm(o):
    mu = jnp.mean(o, -1, keepdims=True)
    var = jnp.mean(jnp.square(o - mu), -1, keepdims=True)
    return (o - mu) * lax.rsqrt(var + LN_EPS)


def _split_cols(h):
    parts = []
    start = 0
    for n in IN_SPLITS:
        parts.append(h[..., start:start + n])
        start += n
    return parts


def _rotary(x, rot_dim, theta):
    T = x.shape[1]
    half = rot_dim // 2
    inv = 1.0 / (theta ** (jnp.arange(half, dtype=jnp.float32) / half))
    ang = jnp.arange(T, dtype=jnp.float32)[:, None] * inv[None, :]
    cos = jnp.cos(ang)[None, :, None, :]
    sin = jnp.sin(ang)[None, :, None, :]
    x1 = x[..., :half]
    x2 = x[..., half:rot_dim]
    return jnp.concatenate([x1 * cos - x2 * sin, x1 * sin + x2 * cos, x[..., rot_dim:]], axis=-1)


def _gla_scan(q, k, v, logg, include_diag):
    B, H, T, dk = q.shape
    dv = v.shape[-1]
    C = GLA_CHUNK
    N = T // C

    def chunks(a):
        return jnp.moveaxis(a.reshape(B, H, N, C, a.shape[-1]), 2, 0)

    idx = jnp.arange(C)
    mask = (idx[:, None] >= idx[None, :]) if include_diag else (idx[:, None] > idx[None, :])

    def step(S, inp):
        qi, ki, vi, gi = inp
        b = jnp.cumsum(gi, axis=2)
        o_inter = jnp.einsum('bhcd,bhde->bhce', qi * jnp.exp(b), S)
        rel = b[:, :, :, None, :] - b[:, :, None, :, :]
        decay = jnp.where(mask[None, None, :, :, None], jnp.exp(jnp.minimum(rel, 0.0)), 0.0)
        att = jnp.einsum('bhid,bhjd,bhijd->bhij', qi, ki, decay)
        o_intra = jnp.einsum('bhij,bhje->bhie', att, vi)
        b_last = b[:, :, -1, :]
        S_new = jnp.exp(b_last)[..., None] * S + jnp.einsum(
            'bhcd,bhce->bhde', ki * jnp.exp(b_last[:, :, None, :] - b), vi)
        return S_new, o_inter + o_intra

    S0 = jnp.zeros((B, H, dk, dv), jnp.float32)
    _, o = lax.scan(step, S0, (chunks(q), chunks(k), chunks(v), chunks(logg)))
    return jnp.moveaxis(o, 0, 2).reshape(B, H, T, dv)


def _gla_mixer(q, k, v, g, lr, w_up, b_up):
    B, T, _ = q.shape
    H = GLA_HEADS
    f32 = jnp.float32

    def heads(a, dh):
        return a.astype(f32).reshape(B, T, H, dh).transpose(0, 2, 1, 3)

    qh = heads(q, GLA_DK) * GLA_DK ** -0.5
    kh = heads(k, GLA_DK)
    vh = heads(v, GLA_DV)
    lrf = lr.astype(f32).reshape(B, T, 2, GLA_RANK)
    z = jnp.einsum('btur,urc->btuc', lrf, w_up.astype(f32)) + b_up.astype(f32)
    logg = jax.nn.log_sigmoid(z) / GLA_TAU
    lg_f = heads(logg[:, :, 0], GLA_DK)
    lg_b = heads(logg[:, :, 1], GLA_DK)
    rev = lambda a: a[:, :, ::-1]
    o = _gla_scan(qh, kh, vh, lg_f, True) + rev(_gla_scan(rev(qh), rev(kh), rev(vh), rev(lg_b), False))
    o = _headnorm(o).transpose(0, 2, 1, 3).reshape(B, T, H * GLA_DV)
    return o * jax.nn.silu(g.astype(f32))


def _dilated_group(q, k, v, window, dil):
    B, T, H, dh = q.shape
    L = SWA_BLOCK
    half = window // (2 * dil)
    M = T // dil
    nb = -(-M // L)
    Mp = nb * L

    def strided(a):
        a = a.reshape(B, M, dil, H, dh).transpose(0, 3, 2, 1, 4)
        return jnp.pad(a, ((0, 0), (0, 0), (0, 0), (0, Mp - M), (0, 0)))

    def band(a):
        ap = jnp.pad(a, ((0, 0), (0, 0), (0, 0), (L, L), (0, 0)))
        parts = [ap[:, :, :, s:s + Mp].reshape(B, H, dil, nb, L, dh) for s in (0, L, 2 * L)]
        return jnp.concatenate(parts, axis=4)

    qb = strided(q).reshape(B, H, dil, nb, L, dh)
    kb = band(strided(k))
    vb = band(strided(v))
    s = jnp.einsum('bhrnid,bhrnjd->bhrnij', qb, kb)
    i = jnp.arange(L)
    j = jnp.arange(3 * L)
    blk = jnp.arange(nb)
    rel = j[None, :] - L - i[:, None]
    mk = blk[:, None] * L - L + j[None, :]
    valid = (jnp.abs(rel) <= half)[None] & ((mk >= 0) & (mk < M))[:, None, :]
    s = jnp.where(valid, s, -jnp.inf)
    m = jnp.max(s, -1, keepdims=True)
    p = jnp.exp(s - m)
    den = jnp.sum(p, -1)
    o = jnp.einsum('bhrnij,bhrnjd->bhrnid', p, vb) / den[..., None]
    lse = m[..., 0] + jnp.log(den)
    o = o.reshape(B, H, dil, Mp, dh)[:, :, :, :M].transpose(0, 3, 2, 1, 4).reshape(B, T, H, dh)
    lse = lse.reshape(B, H, dil, Mp)[:, :, :, :M].transpose(0, 3, 2, 1).reshape(B, T, H)
    return o, lse


def _dilated_mixer(q, k, v):
    B, T, _ = q.shape
    f32 = jnp.float32
    qh = _rotary(q.astype(f32).reshape(B, T, SWA_NH, SWA_DH), ROPE_DIM, ROPE_THETA) * SWA_DH ** -0.5
    kh = _rotary(k.astype(f32).reshape(B, T, SWA_NH, SWA_DH), ROPE_DIM, ROPE_THETA)
    vh = v.astype(f32).reshape(B, T, SWA_NH, SWA_DH)
    outs, lses = [], []
    for gi, (window, dil) in enumerate(SWA_GROUPS):
        sl = slice(gi * SWA_HEADS, (gi + 1) * SWA_HEADS)
        o, lse = _dilated_group(qh[:, :, sl], kh[:, :, sl], vh[:, :, sl], window, dil)
        outs.append(o)
        lses.append(lse)
    w = jax.nn.softmax(jnp.stack(lses, 0), axis=0)
    o = jnp.sum(w[..., None] * jnp.stack(outs, 0), axis=0)
    return o.reshape(B, T, SWA_HEADS * SWA_DH)


def _fourier_mixer(c):
    B, T, _ = c.shape
    cf = c.astype(jnp.float32).reshape(B, T, FNET_GROUPS, FNET_DG)
    y = jnp.fft.fftn(cf, axes=(1, 3), norm='ortho').real
    return y.reshape(B, T, FNET_GROUPS * FNET_DG)


def _retention_dir(q, k, v, log_gamma, include_diag):
    B, H, T, dk = q.shape
    dv = v.shape[-1]
    C = RET_CHUNK
    N = T // C
    qc = q.reshape(B, H, N, C, dk)
    kc = k.reshape(B, H, N, C, dk)
    vc = v.reshape(B, H, N, C, dv)
    idx = jnp.arange(C, dtype=jnp.float32)
    diff = idx[:, None] - idx[None, :]
    mask = (diff >= 0) if include_diag else (diff > 0)
    dmat = jnp.where(mask, jnp.exp(log_gamma[:, None, None] * jnp.maximum(diff, 0.0)), 0.0)
    scores = jnp.einsum('bhnid,bhnjd->bhnij', qc, kc) * dmat[None, :, None]
    inner = jnp.einsum('bhnij,bhnje->bhnie', scores, vc)
    zeta = jnp.exp(log_gamma[:, None] * (C - 1.0 - idx))
    kv = jnp.einsum('bhnjd,bhnje->bhnde', kc * zeta[None, :, None, :, None], vc)
    g_chunk = jnp.exp(log_gamma * C)[None, :, None, None]

    def step(S, kv_n):
        return g_chunk * S + kv_n, S

    _, s_prev = lax.scan(step, jnp.zeros((B, H, dk, dv), jnp.float32), jnp.moveaxis(kv, 2, 0))
    s_prev = jnp.moveaxis(s_prev, 0, 2)
    xi = jnp.exp(log_gamma[:, None] * (idx + 1.0))
    cross = jnp.einsum('bhnid,bhnde->bhnie', qc, s_prev) * xi[None, :, None, :, None]
    return (inner + cross).reshape(B, H, T, dv)


def _retention_mixer(q, k, v, g, decay_param):
    B, T, _ = q.shape
    H, dh = RET_HEADS, RET_DH
    f32 = jnp.float32
    qh = _rotary(q.astype(f32).reshape(B, T, H, dh), dh, RET_THETA).transpose(0, 2, 1, 3)
    kh = (_rotary(k.astype(f32).reshape(B, T, H, dh), dh, RET_THETA) * dh ** -0.5).transpose(0, 2, 1, 3)
    vh = v.astype(f32).reshape(B, T, H, dh).transpose(0, 2, 1, 3)
    log_gamma = -jax.nn.softplus(decay_param.astype(f32))
    rev = lambda a: a[:, :, ::-1]
    o = _retention_dir(qh, kh, vh, log_gamma[0], True) + rev(
        _retention_dir(rev(qh), rev(kh), rev(vh), log_gamma[1], False))
    o = _headnorm(o).transpose(0, 2, 1, 3).reshape(B, T, H * dh)
    return o * jax.nn.silu(g.astype(f32))


def _expert_choice(x, w_router, w1, w3, w2):
    B, T, D = x.shape
    n = B * T
    xt = x.reshape(n, D)
    aff = jax.nn.softmax((xt @ w_router).astype(jnp.float32), axis=-1)
    cap = max(1, CAPACITY_FACTOR * n // N_EXPERTS)
    gate, idx = lax.top_k(aff.T, cap)
    xe = xt[idx]
    h = jax.nn.silu(jnp.einsum('ecd,edf->ecf', xe, w1)) * jnp.einsum('ecd,edf->ecf', xe, w3)
    ye = jnp.einsum('ecf,efd->ecd', h, w2) * gate[..., None].astype(x.dtype)
    out = jnp.zeros((n, D), x.dtype).at[idx.reshape(-1)].add(ye.reshape(-1, D))
    return out.reshape(B, T, D)


def _trunk(x, ln_in_g, ln_in_b, w_in, gla_w_up, gla_b_up, ret_decay, w_gate, b_gate, w_branch,
           w_out, ln1_g, ln1_b, w_router, w_e1, w_e3, w_e2, ln2_g, ln2_b):
    x = _layernorm(x, ln_in_g, ln_in_b)
    for l in range(DEPTH):
        h = x @ w_in[l]
        (a_q, a_k, a_v, a_g, a_lr, b_q, b_k, b_v, c_in, d_q, d_k, d_v, d_g) = _split_cols(h)
        ys = (
            _gla_mixer(a_q, a_k, a_v, a_g, a_lr, gla_w_up[l], gla_b_up[l]),
            _dilated_mixer(b_q, b_k, b_v),
            _fourier_mixer(c_in),
            _retention_mixer(d_q, d_k, d_v, d_g, ret_decay[l]),
        )
        merged = jnp.zeros_like(x)
        for i in range(N_BRANCH):
            gate = jax.nn.sigmoid(x @ w_gate[l, i] + b_gate[l, i])
            merged = merged + gate * (ys[i].astype(x.dtype) @ w_branch[l, i])
        x = _layernorm(DN_ALPHA * x + merged @ w_out[l], ln1_g[l], ln1_b[l])
        moe = _expert_choice(x, w_router[l], w_e1[l], w_e3[l], w_e2[l])
        x = _layernorm(DN_ALPHA * x + moe, ln2_g[l], ln2_b[l])
    return x


def setup_inputs(seed: int = 0) -> dict:
    key = jax.random.key(seed)
    ks = jax.random.split(key, 20)
    f32 = jnp.float32

    def nrm(k, shape, scale):
        return jax.random.normal(k, shape, f32) * scale

    base = np.log(np.expm1(-np.log1p(-(2.0 ** (-5.0 - np.arange(RET_HEADS)))))).astype(np.float32)
    return {
        "x_prompt": nrm(ks[0], (BATCH, SEQ, D_MODEL), 1.0),
        "x_sample": nrm(ks[1], (DEC_BATCH, DEC_SEQ, D_MODEL), 1.0),
        "ln_in_g": 1.0 + nrm(ks[2], (D_MODEL,), 0.01),
        "ln_in_b": nrm(ks[3], (D_MODEL,), 0.01),
        "w_in": nrm(ks[4], (DEPTH, D_MODEL, IN_COLS), D_MODEL ** -0.5),
        "gla_w_up": nrm(ks[5], (DEPTH, 2, GLA_RANK, GLA_HEADS * GLA_DK), GLA_RANK ** -0.5),
        "gla_b_up": nrm(ks[6], (DEPTH, 2, GLA_HEADS * GLA_DK), 0.01),
        "ret_decay": jnp.asarray(base)[None, None, :] + nrm(ks[7], (DEPTH, 2, RET_HEADS), 0.05),
        "w_gate": nrm(ks[8], (DEPTH, N_BRANCH, D_MODEL, D_MODEL), D_MODEL ** -0.5),
        "b_gate": nrm(ks[9], (DEPTH, N_BRANCH, D_MODEL), 0.01),
        "w_branch": nrm(ks[10], (DEPTH, N_BRANCH, BRANCH_W, D_MODEL), BRANCH_W ** -0.5),
        "w_out": nrm(ks[11], (DEPTH, D_MODEL, D_MODEL), DN_BETA * D_MODEL ** -0.5),
        "ln1_g": 1.0 + nrm(ks[12], (DEPTH, D_MODEL), 0.01),
        "ln1_b": nrm(ks[13], (DEPTH, D_MODEL), 0.01),
        "w_router": nrm(ks[14], (DEPTH, D_MODEL, N_EXPERTS), D_MODEL ** -0.5),
        "w_e1": nrm(ks[15], (DEPTH, N_EXPERTS, D_MODEL, EXPERT_FF), D_MODEL ** -0.5),
        "w_e3": nrm(ks[16], (DEPTH, N_EXPERTS, D_MODEL, EXPERT_FF), D_MODEL ** -0.5),
        "w_e2": nrm(ks[17], (DEPTH, N_EXPERTS, EXPERT_FF, D_MODEL), DN_BETA * EXPERT_FF ** -0.5),
        "ln2_g": 1.0 + nrm(ks[18], (DEPTH, D_MODEL), 0.01),
        "ln2_b": nrm(ks[19], (DEPTH, D_MODEL), 0.01),
    }


def reference(x_prompt, x_sample, ln_in_g, ln_in_b, w_in, gla_w_up, gla_b_up, ret_decay, w_gate, b_gate,
              w_branch, w_out, ln1_g, ln1_b, w_router, w_e1, w_e3, w_e2, ln2_g, ln2_b):
    y_prompt = _trunk(x_prompt, ln_in_g, ln_in_b, w_in, gla_w_up, gla_b_up, ret_decay, w_gate, b_gate,
                      w_branch, w_out, ln1_g, ln1_b, w_router, w_e1, w_e3, w_e2, ln2_g, ln2_b)
    y_sample = _trunk(x_sample, ln_in_g, ln_in_b, w_in, gla_w_up, gla_b_up, ret_decay, w_gate, b_gate,
                      w_branch, w_out, ln1_g, ln1_b, w_router, w_e1, w_e3, w_e2, ln2_g, ln2_b)
    return (y_prompt, y_sample)
```

```python
import functools

import jax
import jax.numpy as jnp
import numpy as np
from jax import lax
from jax.experimental import pallas as pl
from jax.experimental.pallas import tpu as pltpu

D_MODEL = 2048
DEPTH = 2
GLA_HEADS, GLA_DK, GLA_DV, GLA_RANK, GLA_TAU, GLA_CHUNK = 4, 64, 128, 16, 16.0, 64
SWA_GROUPS = ((128, 1), (512, 4), (2048, 16))
SWA_HEADS, SWA_DH = 4, 128
SWA_NH = SWA_HEADS * len(SWA_GROUPS)
SWA_BLOCK = 64
ROPE_THETA, ROPE_DIM = 500000.0, SWA_DH // 4
FNET_GROUPS, FNET_DG = 4, 128
RET_HEADS, RET_DH, RET_CHUNK, RET_THETA = 4, 128, 128, 10000.0
N_BRANCH, BRANCH_W = 4, 512
N_EXPERTS, EXPERT_FF, CAPACITY_FACTOR = 16, 2048, 2
DN_ALPHA = (2 * DEPTH) ** 0.25
LN_EPS = 1e-5

IN_SPLITS = (
    GLA_HEADS * GLA_DK, GLA_HEADS * GLA_DK, GLA_HEADS * GLA_DV, GLA_HEADS * GLA_DV, 2 * GLA_RANK,
    SWA_NH * SWA_DH, SWA_NH * SWA_DH, SWA_NH * SWA_DH,
    FNET_GROUPS * FNET_DG,
    RET_HEADS * RET_DH, RET_HEADS * RET_DH, RET_HEADS * RET_DH, RET_HEADS * RET_DH,
)

VMEM_LIMIT_BYTES = 48 * 1024 * 1024
LANE = 128


def _round_up(n, m):
    return -(-n // m) * m


def _mm_kernel(a_ref, b_ref, o_ref):
    o_ref[...] = jnp.dot(a_ref[...], b_ref[...], preferred_element_type=jnp.float32).astype(o_ref.dtype)


def _matmul(a, b, out_dtype, bm=1024, bn=512):
    batched = a.ndim == 3
    M, K = a.shape[-2:]
    N = b.shape[-1]
    bm = min(bm, M)
    bn = min(bn, N)
    assert M % bm == 0 and N % bn == 0, (M, N, bm, bn)
    if batched:
        E = a.shape[0]
        grid = (E, M // bm, N // bn)
        in_specs = [pl.BlockSpec((None, bm, K), lambda e, i, j: (e, i, 0)),
                    pl.BlockSpec((None, K, bn), lambda e, i, j: (e, 0, j))]
        out_specs = pl.BlockSpec((None, bm, bn), lambda e, i, j: (e, i, j))
        out_shape = jax.ShapeDtypeStruct((E, M, N), out_dtype)
        sem = ("parallel", "parallel", "arbitrary")
    else:
        grid = (M // bm, N // bn)
        in_specs = [pl.BlockSpec((bm, K), lambda i, j: (i, 0)),
                    pl.BlockSpec((K, bn), lambda i, j: (0, j))]
        out_specs = pl.BlockSpec((bm, bn), lambda i, j: (i, j))
        out_shape = jax.ShapeDtypeStruct((M, N), out_dtype)
        sem = ("parallel", "arbitrary")
    return pl.pallas_call(
        _mm_kernel, grid=grid, in_specs=in_specs, out_specs=out_specs, out_shape=out_shape,
        compiler_params=pltpu.CompilerParams(dimension_semantics=sem, vmem_limit_bytes=VMEM_LIMIT_BYTES),
    )(a, b)


def _layernorm(x, g, b):
    xf = x.astype(jnp.float32)
    mu = jnp.mean(xf, -1, keepdims=True)
    var = jnp.mean(jnp.square(xf - mu), -1, keepdims=True)
    y = (xf - mu) * lax.rsqrt(var + LN_EPS)
    return (y * g.astype(jnp.float32) + b.astype(jnp.float32)).astype(x.dtype)


def _headnorm(o):
    mu = jnp.mean(o, -1, keepdims=True)
    var = jnp.mean(jnp.square(o - mu), -1, keepdims=True)
    return (o - mu) * lax.rsqrt(var + LN_EPS)


def _split_cols(h):
    parts = []
    start = 0
    for n in IN_SPLITS:
        parts.append(h[..., start:start + n])
        start += n
    return parts


def _rotary(x, rot_dim, theta):
    T = x.shape[1]
    half = rot_dim // 2
    inv = 1.0 / (theta ** (jnp.arange(half, dtype=jnp.float32) / half))
    ang = jnp.arange(T, dtype=jnp.float32)[:, None] * inv[None, :]
    cos = jnp.cos(ang)[None, :, None, :]
    sin = jnp.sin(ang)[None, :, None, :]
    x1 = x[..., :half]
    x2 = x[..., half:rot_dim]
    return jnp.concatenate([x1 * cos - x2 * sin, x1 * sin + x2 * cos, x[..., rot_dim:]], axis=-1)


def _gla_scan(q, k, v, logg, include_diag):
    B, H, T, dk = q.shape
    C = GLA_CHUNK
    N = T // C

    def chunks(a):
        return jnp.moveaxis(a.reshape(B, H, N, C, a.shape[-1]), 2, 0)

    idx = jnp.arange(C)
    mask = (idx[:, None] >= idx[None, :]) if include_diag else (idx[:, None] > idx[None, :])

    def step(S, inp):
        qi, ki, vi, gi = inp
        b = jnp.cumsum(gi, axis=2)
        o_inter = jnp.einsum('bhcd,bhde->bhce', qi * jnp.exp(b), S)
        rel = b[:, :, :, None, :] - b[:, :, None, :, :]
        decay = jnp.where(mask[None, None, :, :, None], jnp.exp(jnp.minimum(rel, 0.0)), 0.0)
        att = jnp.einsum('bhid,bhjd,bhijd->bhij', qi, ki, decay)
        o_intra = jnp.einsum('bhij,bhje->bhie', att, vi)
        b_last = b[:, :, -1, :]
        S_new = jnp.exp(b_last)[..., None] * S + jnp.einsum(
            'bhcd,bhce->bhde', ki * jnp.exp(b_last[:, :, None, :] - b), vi)
        return S_new, o_inter + o_intra

    S0 = jnp.zeros((B, H, dk, v.shape[-1]), jnp.float32)
    _, o = lax.scan(step, S0, (chunks(q), chunks(k), chunks(v), chunks(logg)))
    return jnp.moveaxis(o, 0, 2).reshape(B, H, T, v.shape[-1])


def _gla_mixer(q, k, v, g, lr, w_up, b_up):
    B, T, _ = q.shape
    H = GLA_HEADS
    f32 = jnp.float32

    def heads(a, dh):
        return a.astype(f32).reshape(B, T, H, dh).transpose(0, 2, 1, 3)

    qh = heads(q, GLA_DK) * GLA_DK ** -0.5
    kh = heads(k, GLA_DK)
    vh = heads(v, GLA_DV)
    lrf = lr.astype(f32).reshape(B, T, 2, GLA_RANK)
    z = jnp.einsum('btur,urc->btuc', lrf, w_up.astype(f32)) + b_up.astype(f32)
    logg = jax.nn.log_sigmoid(z) / GLA_TAU
    lg_f = heads(logg[:, :, 0], GLA_DK)
    lg_b = heads(logg[:, :, 1], GLA_DK)
    rev = lambda a: a[:, :, ::-1]
    o = _gla_scan(qh, kh, vh, lg_f, True) + rev(_gla_scan(rev(qh), rev(kh), rev(vh), rev(lg_b), False))
    o = _headnorm(o).transpose(0, 2, 1, 3).reshape(B, T, H * GLA_DV)
    return o * jax.nn.silu(g.astype(f32))


def _dilated_group(q, k, v, window, dil):
    B, T, H, dh = q.shape
    L = SWA_BLOCK
    half = window // (2 * dil)
    M = T // dil
    nb = -(-M // L)
    Mp = nb * L

    def strided(a):
        a = a.reshape(B, M, dil, H, dh).transpose(0, 3, 2, 1, 4)
        return jnp.pad(a, ((0, 0), (0, 0), (0, 0), (0, Mp - M), (0, 0)))

    def band(a):
        ap = jnp.pad(a, ((0, 0), (0, 0), (0, 0), (L, L), (0, 0)))
        parts = [ap[:, :, :, s:s + Mp].reshape(B, H, dil, nb, L, dh) for s in (0, L, 2 * L)]
        return jnp.concatenate(parts, axis=4)

    qb = strided(q).reshape(B, H, dil, nb, L, dh)
    kb = band(strided(k))
    vb = band(strided(v))
    s = jnp.einsum('bhrnid,bhrnjd->bhrnij', qb, kb)
    i = jnp.arange(L)
    j = jnp.arange(3 * L)
    blk = jnp.arange(nb)
    rel = j[None, :] - L - i[:, None]
    mk = blk[:, None] * L - L + j[None, :]
    valid = (jnp.abs(rel) <= half)[None] & ((mk >= 0) & (mk < M))[:, None, :]
    s = jnp.where(valid, s, -jnp.inf)
    m = jnp.max(s, -1, keepdims=True)
    p = jnp.exp(s - m)
    den = jnp.sum(p, -1)
    o = jnp.einsum('bhrnij,bhrnjd->bhrnid', p, vb) / den[..., None]
    lse = m[..., 0] + jnp.log(den)
    o = o.reshape(B, H, dil, Mp, dh)[:, :, :, :M].transpose(0, 3, 2, 1, 4).reshape(B, T, H, dh)
    lse = lse.reshape(B, H, dil, Mp)[:, :, :, :M].transpose(0, 3, 2, 1).reshape(B, T, H)
    return o, lse


def _dilated_mixer(q, k, v):
    B, T, _ = q.shape
    f32 = jnp.float32
    qh = _rotary(q.astype(f32).reshape(B, T, SWA_NH, SWA_DH), ROPE_DIM, ROPE_THETA) * SWA_DH ** -0.5
    kh = _rotary(k.astype(f32).reshape(B, T, SWA_NH, SWA_DH), ROPE_DIM, ROPE_THETA)
    vh = v.astype(f32).reshape(B, T, SWA_NH, SWA_DH)
    outs, lses = [], []
    for gi, (window, dil) in enumerate(SWA_GROUPS):
        sl = slice(gi * SWA_HEADS, (gi + 1) * SWA_HEADS)
        o, lse = _dilated_group(qh[:, :, sl], kh[:, :, sl], vh[:, :, sl], window, dil)
        outs.append(o)
        lses.append(lse)
    w = jax.nn.softmax(jnp.stack(lses, 0), axis=0)
    o = jnp.sum(w[..., None] * jnp.stack(outs, 0), axis=0)
    return o.reshape(B, T, SWA_HEADS * SWA_DH)


def _fourier_mixer(c):
    B, T, _ = c.shape
    cf = c.astype(jnp.float32).reshape(B, T, FNET_GROUPS, FNET_DG)
    y = jnp.fft.fftn(cf, axes=(1, 3), norm='ortho').real
    return y.reshape(B, T, FNET_GROUPS * FNET_DG)


def _retention_dir(q, k, v, log_gamma, include_diag):
    B, H, T, dk = q.shape
    dv = v.shape[-1]
    C = RET_CHUNK
    N = T // C
    qc = q.reshape(B, H, N, C, dk)
    kc = k.reshape(B, H, N, C, dk)
    vc = v.reshape(B, H, N, C, dv)
    idx = jnp.arange(C, dtype=jnp.float32)
    diff = idx[:, None] - idx[None, :]
    mask = (diff >= 0) if include_diag else (diff > 0)
    dmat = jnp.where(mask, jnp.exp(log_gamma[:, None, None] * jnp.maximum(diff, 0.0)), 0.0)
    scores = jnp.einsum('bhnid,bhnjd->bhnij', qc, kc) * dmat[None, :, None]
    inner = jnp.einsum('bhnij,bhnje->bhnie', scores, vc)
    zeta = jnp.exp(log_gamma[:, None] * (C - 1.0 - idx))
    kv = jnp.einsum('bhnjd,bhnje->bhnde', kc * zeta[None, :, None, :, None], vc)
    g_chunk = jnp.exp(log_gamma * C)[None, :, None, None]

    def step(S, kv_n):
        return g_chunk * S + kv_n, S

    _, s_prev = lax.scan(step, jnp.zeros((B, H, dk, dv), jnp.float32), jnp.moveaxis(kv, 2, 0))
    s_prev = jnp.moveaxis(s_prev, 0, 2)
    xi = jnp.exp(log_gamma[:, None] * (idx + 1.0))
    cross = jnp.einsum('bhnid,bhnde->bhnie', qc, s_prev) * xi[None, :, None, :, None]
    return (inner + cross).reshape(B, H, T, dv)


def _retention_mixer(q, k, v, g, decay_param):
    B, T, _ = q.shape
    H, dh = RET_HEADS, RET_DH
    f32 = jnp.float32
    qh = _rotary(q.astype(f32).reshape(B, T, H, dh), dh, RET_THETA).transpose(0, 2, 1, 3)
    kh = (_rotary(k.astype(f32).reshape(B, T, H, dh), dh, RET_THETA) * dh ** -0.5).transpose(0, 2, 1, 3)
    vh = v.astype(f32).reshape(B, T, H, dh).transpose(0, 2, 1, 3)
    log_gamma = -jax.nn.softplus(decay_param.astype(f32))
    rev = lambda a: a[:, :, ::-1]
    o = _retention_dir(qh, kh, vh, log_gamma[0], True) + rev(
        _retention_dir(rev(qh), rev(kh), rev(vh), log_gamma[1], False))
    o = _headnorm(o).transpose(0, 2, 1, 3).reshape(B, T, H * dh)
    return o * jax.nn.silu(g.astype(f32))


def _expert_choice(x, w_router, w1, w3, w2):
    B, T, D = x.shape
    n = B * T
    bf16 = jnp.bfloat16
    xt = x.reshape(n, D)
    aff = jax.nn.softmax((xt @ w_router).astype(jnp.float32), axis=-1)
    cap = max(1, CAPACITY_FACTOR * n // N_EXPERTS)
    gate, idx = lax.top_k(aff.T, cap)
    xe = xt.astype(bf16)[idx]
    h1 = _matmul(xe, w1, jnp.float32)
    h3 = _matmul(xe, w3, jnp.float32)
    h = (jax.nn.silu(h1) * h3).astype(bf16)
    ye = _matmul(h, w2, jnp.float32) * gate[..., None]
    out = jnp.zeros((n, D), x.dtype).at[idx.reshape(-1)].add(ye.reshape(-1, D))
    return out.reshape(B, T, D)


def _trunk(x, p):
    B, T, D = x.shape
    n = B * T
    bf16 = jnp.bfloat16
    x = _layernorm(x, p["ln_in_g"], p["ln_in_b"])
    for l in range(DEPTH):
        xb = x.reshape(n, D).astype(bf16)
        h = _matmul(xb, p["w_in"][l], jnp.float32, bn=1280)[:, :sum(IN_SPLITS)].reshape(B, T, -1)
        (a_q, a_k, a_v, a_g, a_lr, b_q, b_k, b_v, c_in, d_q, d_k, d_v, d_g) = _split_cols(h)
        ys = (
            _gla_mixer(a_q, a_k, a_v, a_g, a_lr, p["gla_w_up"][l], p["gla_b_up"][l]),
            _dilated_mixer(b_q, b_k, b_v),
            _fourier_mixer(c_in),
            _retention_mixer(d_q, d_k, d_v, d_g, p["ret_decay"][l]),
        )
        merged = jnp.zeros((n, D), jnp.float32)
        for i in range(N_BRANCH):
            gate = jax.nn.sigmoid(_matmul(xb, p["w_gate"][l, i], jnp.float32) + p["b_gate"][l, i])
            merged = merged + gate * _matmul(ys[i].reshape(n, -1).astype(bf16), p["w_branch"][l, i], jnp.float32)
        mo = _matmul(merged.astype(bf16), p["w_out"][l], jnp.float32).reshape(B, T, D)
        x = _layernorm(DN_ALPHA * x + mo, p["ln1_g"][l], p["ln1_b"][l])
        moe = _expert_choice(x, p["w_router"][l], p["w_e1"][l], p["w_e3"][l], p["w_e2"][l])
        x = _layernorm(DN_ALPHA * x + moe, p["ln2_g"][l], p["ln2_b"][l])
    return x


def kernel(x_prompt, x_sample, ln_in_g, ln_in_b, w_in, gla_w_up, gla_b_up, ret_decay, w_gate, b_gate, w_branch,
           w_out, ln1_g, ln1_b, w_router, w_e1, w_e3, w_e2, ln2_g, ln2_b):
    bf16 = jnp.bfloat16
    n_in = sum(IN_SPLITS)
    w_in_p = jnp.pad(w_in, ((0, 0), (0, 0), (0, _round_up(n_in, 1280) - n_in))).astype(bf16)
    p = dict(ln_in_g=ln_in_g, ln_in_b=ln_in_b, w_in=w_in_p, gla_w_up=gla_w_up, gla_b_up=gla_b_up,
             ret_decay=ret_decay, w_gate=w_gate.astype(bf16), b_gate=b_gate, w_branch=w_branch.astype(bf16),
             w_out=w_out.astype(bf16), ln1_g=ln1_g, ln1_b=ln1_b, w_router=w_router,
             w_e1=w_e1.astype(bf16), w_e3=w_e3.astype(bf16), w_e2=w_e2.astype(bf16), ln2_g=ln2_g, ln2_b=ln2_b)
    return (_trunk(x_prompt, p), _trunk(x_sample, p))
```

```python
import functools

import jax
import jax.numpy as jnp
import numpy as np
from jax import lax
from jax.experimental import pallas as pl
from jax.experimental.pallas import tpu as pltpu

D_MODEL = 2048
DEPTH = 2
GLA_HEADS, GLA_DK, GLA_DV, GLA_RANK, GLA_TAU, GLA_CHUNK = 4, 64, 128, 16, 16.0, 64
SWA_GROUPS = ((128, 1), (512, 4), (2048, 16))
SWA_HEADS, SWA_DH = 4, 128
SWA_NH = SWA_HEADS * len(SWA_GROUPS)
SWA_BLOCK = 64
ROPE_THETA, ROPE_DIM = 500000.0, SWA_DH // 4
FNET_GROUPS, FNET_DG = 4, 128
RET_HEADS, RET_DH, RET_THETA = 4, 128, 10000.0
N_BRANCH, BRANCH_W = 4, 512
N_EXPERTS, EXPERT_FF, CAPACITY_FACTOR = 16, 2048, 2
DN_ALPHA = (2 * DEPTH) ** 0.25
LN_EPS = 1e-5

VMEM_LIMIT_BYTES = 48 * 1024 * 1024
LANE = 128

QK_OFF = 0
AV_OFF = QK_OFF + GLA_HEADS * LANE
AG_OFF = AV_OFF + GLA_HEADS * GLA_DV
LR_OFF = AG_OFF + GLA_HEADS * GLA_DV
BQ_OFF = LR_OFF + 4 * LANE
BK_OFF = BQ_OFF + SWA_NH * SWA_DH
BV_OFF = BK_OFF + SWA_NH * SWA_DH
C_OFF = BV_OFF + SWA_NH * SWA_DH
DQ_OFF = C_OFF + FNET_GROUPS * FNET_DG
DK_OFF = DQ_OFF + RET_HEADS * RET_DH
DV_OFF = DK_OFF + RET_HEADS * RET_DH
DG_OFF = DV_OFF + RET_HEADS * RET_DH
IN_USED = DG_OFF + RET_HEADS * RET_DH
IN_BN = 1536
IN_COLS_P = -(-IN_USED // IN_BN) * IN_BN
assert IN_COLS_P == IN_USED

SEQ_BLOCK = 512


def _cparams(*sem):
    return pltpu.CompilerParams(dimension_semantics=sem, vmem_limit_bytes=VMEM_LIMIT_BYTES)


def _mm_kernel(a_ref, b_ref, o_ref):
    o_ref[...] = jnp.dot(a_ref[...], b_ref[...], preferred_element_type=jnp.float32).astype(o_ref.dtype)


def _matmul(a, b, out_dtype, bm=1024, bn=512, a_col_blk=0):
    batched = a.ndim == 3
    M = a.shape[-2]
    K, N = b.shape[-2:]
    bm = min(bm, M)
    bn = min(bn, N)
    assert M % bm == 0 and N % bn == 0, (M, N, bm, bn)
    if batched:
        E = a.shape[0]
        grid = (E, M // bm, N // bn)
        in_specs = [pl.BlockSpec((None, bm, K), lambda e, i, j: (e, i, 0)),
                    pl.BlockSpec((None, K, bn), lambda e, i, j: (e, 0, j))]
        out_specs = pl.BlockSpec((None, bm, bn), lambda e, i, j: (e, i, j))
        out_shape = jax.ShapeDtypeStruct((E, M, N), out_dtype)
        sem = ("parallel", "parallel", "arbitrary")
    else:
        grid = (M // bm, N // bn)
        in_specs = [pl.BlockSpec((bm, K), lambda i, j: (i, a_col_blk)),
                    pl.BlockSpec((K, bn), lambda i, j: (0, j))]
        out_specs = pl.BlockSpec((bm, bn), lambda i, j: (i, j))
        out_shape = jax.ShapeDtypeStruct((M, N), out_dtype)
        sem = ("parallel", "arbitrary")
    return pl.pallas_call(
        _mm_kernel, grid=grid, in_specs=in_specs, out_specs=out_specs, out_shape=out_shape,
        compiler_params=_cparams(*sem),
    )(a, b)


def _ln_rows(x, g, b):
    mu = jnp.mean(x, -1, keepdims=True)
    xc = x - mu
    var = jnp.mean(xc * xc, -1, keepdims=True)
    return xc * lax.rsqrt(var + LN_EPS) * g + b


def _ln_in_kernel(x_ref, g_ref, b_ref, xf_ref, xb_ref):
    y = _ln_rows(x_ref[...], g_ref[...], b_ref[...])
    xf_ref[...] = y
    xb_ref[...] = y.astype(jnp.bfloat16)


def _ln_in_kernel_into(x_ref, g_ref, b_ref, pf_ref, pb_ref, xf_ref, xb_ref):
    del pf_ref, pb_ref
    _ln_in_kernel(x_ref, g_ref, b_ref, xf_ref, xb_ref)


def _ln_in(x, g, b, n_total, row_off, prev=None, bm=512):
    n, D = x.shape
    ob = row_off // bm
    row = pl.BlockSpec((bm, D), lambda i: (i, 0))
    orow = pl.BlockSpec((bm, D), lambda i: (i + ob, 0))
    vec = pl.BlockSpec((1, D), lambda i: (0, 0))
    out_shape = [jax.ShapeDtypeStruct((n_total, D), jnp.float32), jax.ShapeDtypeStruct((n_total, D), jnp.bfloat16)]
    args = (x, g.reshape(1, D), b.reshape(1, D))
    if prev is None:
        return pl.pallas_call(
            _ln_in_kernel, grid=(n // bm,), in_specs=[row, vec, vec], out_specs=[orow, orow], out_shape=out_shape,
            compiler_params=_cparams("parallel"))(*args)
    anyspec = pl.BlockSpec(memory_space=pl.ANY)
    return pl.pallas_call(
        _ln_in_kernel_into, grid=(n // bm,), in_specs=[row, vec, vec, anyspec, anyspec], out_specs=[orow, orow],
        out_shape=out_shape, input_output_aliases={3: 0, 4: 1},
        compiler_params=_cparams("parallel"))(*args, *prev)


def _res_ln_kernel(x_ref, r_ref, g_ref, b_ref, xf_ref, xb_ref):
    y = _ln_rows(DN_ALPHA * x_ref[...] + r_ref[...], g_ref[...], b_ref[...])
    xf_ref[...] = y
    xb_ref[...] = y.astype(jnp.bfloat16)


def _res_ln(x, r, g, b, row_off=0, rows=None, bm=512):
    n, D = x.shape
    rows = n if rows is None else rows
    ob = row_off // bm
    irow = pl.BlockSpec((bm, D), lambda i: (i + ob, 0))
    orow = pl.BlockSpec((bm, D), lambda i: (i, 0))
    vec = pl.BlockSpec((1, D), lambda i: (0, 0))
    return pl.pallas_call(
        _res_ln_kernel, grid=(rows // bm,), in_specs=[irow, irow, vec, vec], out_specs=[orow, orow],
        out_shape=[jax.ShapeDtypeStruct((rows, D), jnp.float32), jax.ShapeDtypeStruct((rows, D), jnp.bfloat16)],
        compiler_params=_cparams("parallel"),
    )(x, r, g.reshape(1, D), b.reshape(1, D))


def _merge_kernel(xb_ref, wg_ref, bg_ref, ya_ref, yb_ref, yc_ref, yd_ref, wb_ref, o_ref):
    xb = xb_ref[...]
    acc = None
    for i, y_ref in enumerate((ya_ref, yb_ref, yc_ref, yd_ref)):
        gate = jax.nn.sigmoid(jnp.dot(xb, wg_ref[i], preferred_element_type=jnp.float32) + bg_ref[i])
        term = gate * jnp.dot(y_ref[...], wb_ref[i], preferred_element_type=jnp.float32)
        acc = term if acc is None else acc + term
    o_ref[...] = acc.astype(o_ref.dtype)


def _merge(xb, wg, bg, ys, wb, bm=1024, bn=256):
    n, D = xb.shape
    W = ys[0].shape[1]
    y_spec = pl.BlockSpec((bm, W), lambda i, j: (i, 0))
    return pl.pallas_call(
        _merge_kernel, grid=(n // bm, D // bn),
        in_specs=[pl.BlockSpec((bm, D), lambda i, j: (i, 0)),
                  pl.BlockSpec((N_BRANCH, D, bn), lambda i, j: (0, 0, j)),
                  pl.BlockSpec((N_BRANCH, 1, bn), lambda i, j: (0, 0, j)),
                  y_spec, y_spec, y_spec, y_spec,
                  pl.BlockSpec((N_BRANCH, W, bn), lambda i, j: (0, 0, j))],
        out_specs=pl.BlockSpec((bm, bn), lambda i, j: (i, j)),
        out_shape=jax.ShapeDtypeStruct((n, D), jnp.bfloat16),
        compiler_params=_cparams("parallel", "arbitrary"),
    )(xb, wg, bg.reshape(N_BRANCH, 1, D), *ys, wb)


def _outproj_kernel(m_ref, w_ref, x_ref, g_ref, b_ref, wr_ref, xf_ref, xb_ref, aff_ref):
    r = jnp.dot(m_ref[...], w_ref[...], preferred_element_type=jnp.float32)
    y = _ln_rows(DN_ALPHA * x_ref[...] + r, g_ref[...], b_ref[...])
    yb = y.astype(jnp.bfloat16)
    xf_ref[...] = y
    xb_ref[...] = yb
    logits = jnp.dot(yb, wr_ref[...], preferred_element_type=jnp.float32)
    lane = lax.broadcasted_iota(jnp.int32, logits.shape, 1)
    logits = jnp.where(lane < N_EXPERTS, logits, -jnp.inf)
    e = jnp.exp(logits - jnp.max(logits, -1, keepdims=True))
    aff_ref[...] = e / jnp.sum(e, -1, keepdims=True)


def _outproj(merged, w_out, x, g, b, w_router_p, bm=512):
    n, D = x.shape
    row = pl.BlockSpec((bm, D), lambda i: (i, 0))
    vec = pl.BlockSpec((1, D), lambda i: (0, 0))
    return pl.pallas_call(
        _outproj_kernel, grid=(n // bm,),
        in_specs=[row, pl.BlockSpec((D, D), lambda i: (0, 0)), row, vec, vec,
                  pl.BlockSpec((D, LANE), lambda i: (0, 0))],
        out_specs=[row, row, pl.BlockSpec((bm, LANE), lambda i: (i, 0))],
        out_shape=[jax.ShapeDtypeStruct((n, D), jnp.float32), jax.ShapeDtypeStruct((n, D), jnp.bfloat16),
                   jax.ShapeDtypeStruct((n, LANE), jnp.float32)],
        compiler_params=_cparams("parallel"),
    )(merged, w_out, x, g.reshape(1, D), b.reshape(1, D), w_router_p)


def _log_sigmoid(z):
    return jnp.minimum(z, 0.0) - jnp.log(1.0 + jnp.exp(-jnp.abs(z)))


def _headnorm(o):
    mu = jnp.mean(o, -1, keepdims=True)
    oc = o - mu
    var = jnp.mean(oc * oc, -1, keepdims=True)
    return oc * lax.rsqrt(var + LN_EPS)


def _silu(x):
    return x * jax.nn.sigmoid(x)


def _chunk_cumsum(g, chunk, inclusive):
    row = lax.broadcasted_iota(jnp.int32, g.shape, 0) % chunk
    b = g
    s = 1
    while s < chunk:
        b = b + jnp.where(row >= s, pltpu.roll(b, s, axis=0), 0.0)
        s *= 2
    return b if inclusive else b - g


def _gla_kernel(flag_ref, qk_ref, v_ref, lr_ref, wup_ref, bup_ref, *rest, reverse, nblk):
    if reverse:
        gate_ref, of_ref, o_ref, st_ref = rest
    else:
        o_ref, st_ref = rest
    f32, bf16 = jnp.float32, jnp.bfloat16
    C = GLA_CHUNK
    TB = qk_ref.shape[0]
    nc = TB // C
    s_idx = pl.program_id(1)
    blk = nblk - 1 - s_idx if reverse else s_idx

    @pl.when(flag_ref[blk] == 1)
    def _():
        st_ref[...] = jnp.zeros_like(st_ref)

    qk = qk_ref[...].astype(f32)
    q = qk[:, :GLA_DK] * (GLA_DK ** -0.5)
    k = qk[:, GLA_DK:]
    z = jnp.dot(lr_ref[...], wup_ref[...], preferred_element_type=f32) + bup_ref[...]
    g = _log_sigmoid(z) * (1.0 / GLA_TAU)
    cs = _chunk_cumsum(g, C, inclusive=not reverse)
    cs3 = cs.reshape(nc, C, GLA_DK)
    mid = cs3[:, C // 2:C // 2 + 1, :]
    if reverse:
        tot3 = cs3[:, C - 1:C, :] + g.reshape(nc, C, GLA_DK)[:, C - 1:C, :]
        pq3, pk3 = tot3 - cs3, cs3
        qm3, km3 = mid - cs3, cs3 - mid
    else:
        tot3 = cs3[:, C - 1:C, :]
        pq3, pk3 = cs3, tot3 - cs3
        qm3, km3 = cs3 - mid, mid - cs3
    q3 = q.reshape(nc, C, GLA_DK)
    k3 = k.reshape(nc, C, GLA_DK)
    qe = (q3 * jnp.exp(pq3)).astype(bf16)
    kd = (k3 * jnp.exp(pk3)).astype(bf16)
    qm = (q3 * jnp.exp(qm3)).astype(bf16)
    km = (k3 * jnp.exp(km3)).astype(bf16)
    etot = jnp.exp(tot3)
    v = v_ref[...]
    ri = lax.broadcasted_iota(jnp.int32, (C, C), 0)
    ci = lax.broadcasted_iota(jnp.int32, (C, C), 1)
    keep = (ci > ri) if reverse else (ci <= ri)

    st = st_ref[...]
    outs = [None] * nc
    for c in (range(nc - 1, -1, -1) if reverse else range(nc)):
        vc = v[c * C:(c + 1) * C, :]
        att = lax.dot_general(qm[c], km[c], (((1,), (1,)), ((), ())), preferred_element_type=f32)
        att = jnp.where(keep, att, 0.0).astype(bf16)
        o_c = jnp.dot(att, vc, preferred_element_type=f32)
        o_c = o_c + lax.dot_general(qe[c], st.astype(bf16), (((1,), (1,)), ((), ())), preferred_element_type=f32)
        upd = lax.dot_general(vc, kd[c], (((0,), (0,)), ((), ())), preferred_element_type=f32)
        st = st * etot[c] + upd
        outs[c] = o_c
    st_ref[...] = st
    o = jnp.concatenate(outs, axis=0)
    if reverse:
        o = _headnorm(o + of_ref[...])
        o_ref[...] = (o * _silu(gate_ref[...].astype(f32))).astype(o_ref.dtype)
    else:
        o_ref[...] = o


def _gla(h, seq_first, seq_last, wup, bup):
    n = h.shape[0]
    TB = SEQ_BLOCK
    nblk = n // TB
    H = GLA_HEADS

    def col(off):
        return off // LANE

    def specs(reverse):
        blk = (lambda s: nblk - 1 - s) if reverse else (lambda s: s)
        sp = [pl.BlockSpec((TB, LANE), lambda hh, s, f: (blk(s), col(QK_OFF) + hh)),
              pl.BlockSpec((TB, LANE), lambda hh, s, f: (blk(s), col(AV_OFF) + hh)),
              pl.BlockSpec((TB, LANE), lambda hh, s, f: (blk(s), col(LR_OFF))),
              pl.BlockSpec((None, None, LANE, GLA_DK), lambda hh, s, f: (1 if reverse else 0, hh, 0, 0)),
              pl.BlockSpec((None, None, 1, GLA_DK), lambda hh, s, f: (1 if reverse else 0, hh, 0, 0))]
        out = pl.BlockSpec((TB, LANE), lambda hh, s, f: (blk(s), hh))
        return sp, out, blk

    scratch = [pltpu.VMEM((GLA_DV, GLA_DK), jnp.float32)]
    sp, out, _ = specs(False)
    o_f = pl.pallas_call(
        functools.partial(_gla_kernel, reverse=False, nblk=nblk),
        grid_spec=pltpu.PrefetchScalarGridSpec(num_scalar_prefetch=1, grid=(H, nblk), in_specs=sp, out_specs=out,
                                               scratch_shapes=scratch),
        out_shape=jax.ShapeDtypeStruct((n, H * GLA_DV), jnp.float32),
        compiler_params=_cparams("parallel", "arbitrary"),
    )(seq_first, h, h, h, wup, bup)
    sp, out, blk = specs(True)
    sp = sp + [pl.BlockSpec((TB, LANE), lambda hh, s, f: (blk(s), col(AG_OFF) + hh)),
               pl.BlockSpec((TB, LANE), lambda hh, s, f: (blk(s), hh))]
    return pl.pallas_call(
        functools.partial(_gla_kernel, reverse=True, nblk=nblk),
        grid_spec=pltpu.PrefetchScalarGridSpec(num_scalar_prefetch=1, grid=(H, nblk), in_specs=sp, out_specs=out,
                                               scratch_shapes=scratch),
        out_shape=jax.ShapeDtypeStruct((n, H * GLA_DV), jnp.bfloat16),
        compiler_params=_cparams("parallel", "arbitrary"),
    )(seq_last, h, h, h, wup, bup, h, o_f)


def _pack_gla_up(w_up, b_up):
    H, dk, r = GLA_HEADS, GLA_DK, GLA_RANK
    w = w_up.reshape(2, r, H, dk).transpose(0, 2, 1, 3)
    wp = jnp.zeros((2, H, LANE, dk), jnp.float32)
    wp = wp.at[0, :, 0:r].set(w[0]).at[1, :, r:2 * r].set(w[1])
    return wp.astype(jnp.bfloat16), b_up.reshape(2, H, 1, dk).astype(jnp.float32)


def _rope_tables(pos, dim, theta, width):
    half = dim // 2
    T = pos.shape[0]
    inv = 1.0 / (theta ** (jnp.arange(half, dtype=jnp.float32) / half))
    ang = pos.astype(jnp.float32)[:, None] * inv[None, :]
    c, s = jnp.cos(ang), jnp.sin(ang)
    rest = width - dim
    cos = jnp.concatenate([c, c, jnp.ones((T, rest), jnp.float32)], axis=1)
    sin_lo = jnp.concatenate([-s, jnp.zeros((T, half + rest), jnp.float32)], axis=1)
    sin_hi = jnp.concatenate([jnp.zeros((T, half), jnp.float32), s, jnp.zeros((T, rest), jnp.float32)], axis=1)
    return cos, sin_lo, sin_hi


def _ret_kernel(flag_ref, pos_ref, q_ref, k_ref, v_ref, cos_ref, sin_ref, dec_ref, *rest, reverse, nblk):
    if reverse:
        gate_ref, o1_ref, o_ref, st_ref = rest
    else:
        o_ref, st_ref, dm_ref = rest
    f32, bf16 = jnp.float32, jnp.bfloat16
    TB = q_ref.shape[0]
    s_idx = pl.program_id(1)
    blk = nblk - 1 - s_idx if reverse else s_idx

    dec = dec_ref[...]
    lg = -(jnp.maximum(dec, 0.0) + jnp.log1p(jnp.exp(-jnp.abs(dec))))
    lg_f, lg_b = lg[0, 0:1, 0:1], lg[1, 0:1, 0:1]

    @pl.when(flag_ref[blk] == 1)
    def _():
        st_ref[...] = jnp.zeros_like(st_ref)

    cos, sin = cos_ref[...], sin_ref[...]

    def rot(x):
        return x * cos + pltpu.roll(x, RET_DH // 2, axis=1) * sin

    qr = rot(q_ref[...].astype(f32))
    kr = rot(k_ref[...].astype(f32)) * (RET_DH ** -0.5)
    v = v_ref[...]
    idx = lax.broadcasted_iota(jnp.int32, (TB, 1), 0).astype(f32)
    st = st_ref[...]
    cross = jnp.dot(qr.astype(bf16), st.astype(bf16), preferred_element_type=f32)
    if reverse:
        cross = cross * jnp.exp(lg_b * (TB - idx))
        kz = (kr * jnp.exp(lg_b * idx)).astype(bf16)
        st_ref[...] = st * jnp.exp(lg_b * TB) + lax.dot_general(kz, v, (((0,), (0,)), ((), ())),
                                                                 preferred_element_type=f32)
        o = _headnorm(o1_ref[...] + cross)
        o_ref[...] = (o * _silu(gate_ref[...].astype(f32))).astype(o_ref.dtype)
    else:
        @pl.when(s_idx == 0)
        def _():
            ri = lax.broadcasted_iota(jnp.int32, (TB, TB), 0)
            ci = lax.broadcasted_iota(jnp.int32, (TB, TB), 1)
            d = (ri - ci).astype(f32)
            dm_ref[...] = jnp.where(d >= 0, jnp.exp(lg_f * jnp.maximum(d, 0.0)), jnp.exp(lg_b * jnp.maximum(-d, 0.0)))

        cross = cross * jnp.exp(lg_f * (idx + 1.0))
        kz = (kr * jnp.exp(lg_f * (TB - 1.0 - idx))).astype(bf16)
        st_ref[...] = st * jnp.exp(lg_f * TB) + lax.dot_general(kz, v, (((0,), (0,)), ((), ())),
                                                                 preferred_element_type=f32)
        sc = lax.dot_general(qr.astype(bf16), kr.astype(bf16), (((1,), (1,)), ((), ())), preferred_element_type=f32)
        sc = (sc * dm_ref[...]).astype(bf16)
        o_ref[...] = jnp.dot(sc, v, preferred_element_type=f32) + cross


def _retention(h, seq_first, seq_last, pos_blk, decay, t_max):
    n = h.shape[0]
    TB = SEQ_BLOCK
    nblk = n // TB
    H, dh = RET_HEADS, RET_DH
    cos, sin_lo, sin_hi = _rope_tables(jnp.arange(t_max), dh, RET_THETA, dh)
    sin = sin_lo + sin_hi
    dec = jnp.broadcast_to(decay.astype(jnp.float32).T[:, :, None, None], (H, 2, 8, LANE))

    def col(off):
        return off // LANE

    def specs(reverse):
        blk = (lambda s: nblk - 1 - s) if reverse else (lambda s: s)
        tok = lambda off: pl.BlockSpec((TB, LANE), lambda hh, s, f, p: (blk(s), col(off) + hh))
        tab = pl.BlockSpec((TB, LANE), lambda hh, s, f, p: (p[blk(s)], 0))
        sp = [tok(DQ_OFF), tok(DK_OFF), tok(DV_OFF), tab, tab,
              pl.BlockSpec((None, 2, 8, LANE), lambda hh, s, f, p: (hh, 0, 0, 0))]
        return sp, tok, pl.BlockSpec((TB, LANE), lambda hh, s, f, p: (blk(s), hh))

    sp, _, out = specs(False)
    o1 = pl.pallas_call(
        functools.partial(_ret_kernel, reverse=False, nblk=nblk),
        grid_spec=pltpu.PrefetchScalarGridSpec(
            num_scalar_prefetch=2, grid=(H, nblk), in_specs=sp, out_specs=out,
            scratch_shapes=[pltpu.VMEM((dh, dh), jnp.float32), pltpu.VMEM((TB, TB), jnp.float32)]),
        out_shape=jax.ShapeDtypeStruct((n, H * dh), jnp.float32),
        compiler_params=_cparams("arbitrary", "arbitrary"),
    )(seq_first, pos_blk, h, h, h, cos, sin, dec)
    sp, tok, out = specs(True)
    sp = sp + [tok(DG_OFF), out]
    return pl.pallas_call(
        functools.partial(_ret_kernel, reverse=True, nblk=nblk),
        grid_spec=pltpu.PrefetchScalarGridSpec(
            num_scalar_prefetch=2, grid=(H, nblk), in_specs=sp, out_specs=out,
            scratch_shapes=[pltpu.VMEM((dh, dh), jnp.float32)]),
        out_shape=jax.ShapeDtypeStruct((n, H * dh), jnp.bfloat16),
        compiler_params=_cparams("arbitrary", "arbitrary"),
    )(seq_last, pos_blk, h, h, h, cos, sin, dec, h, o1)


SWA_Q = 128
SWA_HALF = 64
assert all(w // (2 * d) == SWA_HALF for w, d in SWA_GROUPS)
NEG_BIG = -1e30


def _swa_kernel(hl_ref, hr_ref, q_ref, kl_ref, kc_ref, kr_ref, vl_ref, vc_ref, vr_ref,
                qcos_ref, qs1_ref, qs2_ref, kcos_ref, ks1_ref, ks2_ref, o_ref, lse_ref):
    f32, bf16 = jnp.float32, jnp.bfloat16
    j = pl.program_id(1)
    Q, HW = SWA_Q, SWA_HALF
    W = Q + 2 * HW
    ri = lax.broadcasted_iota(jnp.int32, (Q, W), 0)
    ci = lax.broadcasted_iota(jnp.int32, (Q, W), 1)
    rel = ci - HW - ri
    ok = (rel <= HW) & (rel >= -HW)
    ok = ok & ((ci >= HW) | (hl_ref[j] == 1)) & ((ci < HW + Q) | (hr_ref[j] == 1))
    lane = lax.broadcasted_iota(jnp.int32, (Q, LANE), 1)

    def rot(x, c_ref, s1_ref, s2_ref):
        return (x * c_ref[...] + pltpu.roll(x, LANE - ROPE_DIM // 2, axis=1) * s1_ref[...]
                + pltpu.roll(x, ROPE_DIM // 2, axis=1) * s2_ref[...])

    lse_tile = jnp.zeros((Q, LANE), f32)
    for hh in range(SWA_HEADS):
        sl = slice(hh * SWA_DH, (hh + 1) * SWA_DH)
        q = rot(q_ref[:, sl].astype(f32), qcos_ref, qs1_ref, qs2_ref) * (SWA_DH ** -0.5)
        k = jnp.concatenate([kl_ref[:, sl], kc_ref[:, sl], kr_ref[:, sl]], axis=0).astype(f32)
        k = rot(k, kcos_ref, ks1_ref, ks2_ref)
        v = jnp.concatenate([vl_ref[:, sl], vc_ref[:, sl], vr_ref[:, sl]], axis=0)
        s = lax.dot_general(q.astype(bf16), k.astype(bf16), (((1,), (1,)), ((), ())), preferred_element_type=f32)
        s = jnp.where(ok, s, NEG_BIG)
        m = jnp.max(s, -1, keepdims=True)
        p = jnp.exp(s - m)
        den = jnp.sum(p, -1, keepdims=True)
        o = jnp.dot(p.astype(bf16), v, preferred_element_type=f32) / den
        o_ref[:, sl] = o
        lse_tile = jnp.where(lane == hh, m + jnp.log(den), lse_tile)
    lse_ref[...] = lse_tile


def _swa_merge_kernel(o0_ref, o1_ref, o2_ref, l0_ref, l1_ref, l2_ref, y_ref):
    for hh in range(SWA_HEADS):
        sl = slice(hh * SWA_DH, (hh + 1) * SWA_DH)
        ls = [r[:, hh:hh + 1] for r in (l0_ref, l1_ref, l2_ref)]
        m = jnp.maximum(jnp.maximum(ls[0], ls[1]), ls[2])
        ws = [jnp.exp(l - m) for l in ls]
        tot = ws[0] + ws[1] + ws[2]
        y = ws[0] * o0_ref[:, sl] + ws[1] * o1_ref[:, sl] + ws[2] * o2_ref[:, sl]
        y_ref[:, sl] = (y / tot).astype(y_ref.dtype)


def _swa(h, seqs):
    n, NP = h.shape
    Q, HW = SWA_Q, SWA_HALF
    GW = SWA_HEADS * SWA_DH
    outs, lses = [], []
    for gi, (window, dil) in enumerate(SWA_GROUPS):
        rows = n // dil
        nblk = rows // Q
        has_l, has_r = [], []
        for num, T in seqs:
            per = T // dil // Q
            assert per * Q * dil == T
            has_l += ([0] + [1] * (per - 1)) * num
            has_r += ([1] * (per - 1) + [0]) * num
        has_l = jnp.asarray(np.array(has_l, np.int32))
        has_r = jnp.asarray(np.array(has_r, np.int32))
        qpos = jnp.arange(Q) * dil
        kpos = (jnp.arange(Q + 2 * HW) - HW) * dil
        qt = _rope_tables(qpos, ROPE_DIM, ROPE_THETA, SWA_DH)
        kt = _rope_tables(kpos, ROPE_DIM, ROPE_THETA, SWA_DH)
        hv = h.reshape(rows, dil * NP)
        npb = NP // GW
        last64 = rows // HW - 1

        def cblk(off):
            return lambda r, j, hl, hr: r * npb + (off + gi * GW) // GW

        def center(off):
            f = cblk(off)
            return pl.BlockSpec((Q, GW), lambda r, j, hl, hr: (j, f(r, j, hl, hr)))

        def left(off):
            f = cblk(off)
            return pl.BlockSpec((HW, GW), lambda r, j, hl, hr: (jnp.maximum(2 * j - 1, 0), f(r, j, hl, hr)))

        def right(off):
            f = cblk(off)
            return pl.BlockSpec((HW, GW), lambda r, j, hl, hr: (jnp.minimum(2 * j + 2, last64), f(r, j, hl, hr)))

        qtab = pl.BlockSpec((Q, SWA_DH), lambda r, j, hl, hr: (0, 0))
        ktab = pl.BlockSpec((Q + 2 * HW, SWA_DH), lambda r, j, hl, hr: (0, 0))
        o_g, lse_g = pl.pallas_call(
            _swa_kernel,
            grid_spec=pltpu.PrefetchScalarGridSpec(
                num_scalar_prefetch=2, grid=(dil, nblk),
                in_specs=[center(BQ_OFF), left(BK_OFF), center(BK_OFF), right(BK_OFF),
                          left(BV_OFF), center(BV_OFF), right(BV_OFF), qtab, qtab, qtab, ktab, ktab, ktab],
                out_specs=[pl.BlockSpec((Q, GW), lambda r, j, hl, hr: (j, r)),
                           pl.BlockSpec((Q, LANE), lambda r, j, hl, hr: (j, r))]),
            out_shape=[jax.ShapeDtypeStruct((rows, dil * GW), jnp.float32),
                       jax.ShapeDtypeStruct((rows, dil * LANE), jnp.float32)],
            compiler_params=_cparams("parallel", "parallel"),
        )(has_l, has_r, hv, hv, hv, hv, hv, hv, hv, *qt, *kt)
        outs.append(o_g.reshape(n, GW))
        lses.append(lse_g.reshape(n, LANE))
    bm = 1024
    osp = pl.BlockSpec((bm, GW), lambda i: (i, 0))
    lsp = pl.BlockSpec((bm, LANE), lambda i: (i, 0))
    return pl.pallas_call(
        _swa_merge_kernel, grid=(n // bm,), in_specs=[osp, osp, osp, lsp, lsp, lsp], out_specs=osp,
        out_shape=jax.ShapeDtypeStruct((n, GW), jnp.bfloat16),
        compiler_params=_cparams("parallel"),
    )(*outs, *lses)


FFT_T2 = 128
FW = FNET_GROUPS * FNET_DG


def _dft_cos_sin(n_out, n_in, period):
    ang = (2.0 * np.pi / period) * np.mod(np.outer(np.arange(n_out), np.arange(n_in)), period)
    return np.cos(ang), np.sin(ang)


def _fnet_stage1_kernel(z_ref, m_ref, a_ref, *, cols):
    mat = m_ref[...]
    for c in range(cols):
        zr = z_ref[:, c * 2 * FW:c * 2 * FW + FW]
        zi = z_ref[:, c * 2 * FW + FW:(c + 1) * 2 * FW]
        a = jnp.dot(mat, jnp.concatenate([zr, zi], axis=0), preferred_element_type=jnp.float32)
        t1 = zr.shape[0]
        a_ref[:, c * 2 * FW:c * 2 * FW + FW] = a[:t1].astype(a_ref.dtype)
        a_ref[:, c * 2 * FW + FW:(c + 1) * 2 * FW] = a[t1:].astype(a_ref.dtype)


def _fnet_stage2_kernel(a_ref, g_ref, y_ref, *, scale):
    y = jnp.dot(g_ref[...], a_ref[...], preferred_element_type=jnp.float32)
    y_ref[...] = (y * scale).astype(y_ref.dtype)


def _fnet(h, seqs):
    n = h.shape[0]
    bf16 = jnp.bfloat16
    T2 = FFT_T2
    cc, ss = _dft_cos_sin(FNET_DG, FNET_DG, FNET_DG)
    eye = np.eye(FNET_GROUPS)
    wc = jnp.asarray(np.concatenate([np.kron(eye, cc), -np.kron(eye, ss)], axis=1), bf16)
    z = _matmul(h, wc, bf16, bn=2 * FW, a_col_blk=C_OFF // FW)
    zz = z.reshape(n // T2, T2 * 2 * FW)
    outs = []
    row0 = 0
    for num, T in seqs:
        T1 = T // T2
        assert T1 * T2 == T and T1 % 16 == 0
        c1, s1 = _dft_cos_sin(T1, T1, T1)
        mat = jnp.asarray(np.block([[c1, s1], [-s1, c1]]), bf16)
        cols = min(T2, max(1, 1024 // T1))
        rb0 = row0 // T1
        a = pl.pallas_call(
            functools.partial(_fnet_stage1_kernel, cols=cols), grid=(num, T2 // cols),
            in_specs=[pl.BlockSpec((T1, cols * 2 * FW), lambda b, j: (rb0 + b, j)),
                      pl.BlockSpec((2 * T1, 2 * T1), lambda b, j: (0, 0))],
            out_specs=pl.BlockSpec((T1, cols * 2 * FW), lambda b, j: (b, j)),
            out_shape=jax.ShapeDtypeStruct((num * T1, T2 * 2 * FW), bf16),
            compiler_params=_cparams("parallel", "parallel"),
        )(zz, mat)
        ec, es = _dft_cos_sin(T, T2, T)
        g = jnp.asarray(np.stack([ec, es], axis=-1).reshape(T2, T1 * 2 * T2), bf16)
        y = pl.pallas_call(
            functools.partial(_fnet_stage2_kernel, scale=float(1.0 / np.sqrt(T * FNET_DG))), grid=(num, T1),
            in_specs=[pl.BlockSpec((None, 2 * T2, FW), lambda b, k1: (b * T1 + k1, 0, 0)),
                      pl.BlockSpec((T2, 2 * T2), lambda b, k1: (0, k1))],
            out_specs=pl.BlockSpec((T2, FW), lambda b, k1: (b, k1)),
            out_shape=jax.ShapeDtypeStruct((num * T2, T1 * FW), bf16),
            compiler_params=_cparams("parallel", "parallel"),
        )(a.reshape(num * T1, 2 * T2, FW), g)
        outs.append(y.reshape(num * T, FW))
        row0 += num * T1
    return jnp.concatenate(outs, axis=0)


def _ffn_kernel(x_ref, w1_ref, w3_ref, w2_ref, g_ref, o_ref, acc_ref):
    f = pl.program_id(2)
    x = x_ref[...]
    h1 = jnp.dot(x, w1_ref[...], preferred_element_type=jnp.float32)
    h3 = jnp.dot(x, w3_ref[...], preferred_element_type=jnp.float32)
    part = jnp.dot((_silu(h1) * h3).astype(jnp.bfloat16), w2_ref[...], preferred_element_type=jnp.float32)

    @pl.when(f == 0)
    def _():
        acc_ref[...] = part

    @pl.when(f > 0)
    def _():
        acc_ref[...] += part

    @pl.when(f == pl.num_programs(2) - 1)
    def _():
        o_ref[...] = (acc_ref[...] * g_ref[...]).astype(o_ref.dtype)


def _expert_ffn(xe, w1, w3, w2, gate, bm=1024, bf=512):
    E, R, D = xe.shape
    F = w1.shape[-1]
    return pl.pallas_call(
        _ffn_kernel, grid=(E, R // bm, F // bf),
        in_specs=[pl.BlockSpec((None, bm, D), lambda e, i, f: (e, i, 0)),
                  pl.BlockSpec((None, D, bf), lambda e, i, f: (e, 0, f)),
                  pl.BlockSpec((None, D, bf), lambda e, i, f: (e, 0, f)),
                  pl.BlockSpec((None, bf, D), lambda e, i, f: (e, f, 0)),
                  pl.BlockSpec((None, bm, 1), lambda e, i, f: (e, i, 0))],
        out_specs=pl.BlockSpec((None, bm, D), lambda e, i, f: (e, i, 0)),
        out_shape=jax.ShapeDtypeStruct((E, R, D), jnp.bfloat16),
        scratch_shapes=[pltpu.VMEM((bm, D), jnp.float32)],
        compiler_params=_cparams("parallel", "parallel", "arbitrary"),
    )(xe, w1, w3, w2, gate)


def _expert_choice(xb, aff, seqs, w1, w3, w2):
    n, D = xb.shape
    aff_t = aff[:, :N_EXPERTS].T
    gates, idxs = [], []
    start = 0
    for num, T in seqs:
        m = num * T
        cap = max(1, CAPACITY_FACTOR * m // N_EXPERTS)
        gate, idx = lax.top_k(aff_t[:, start:start + m], cap)
        gates.append(gate)
        idxs.append(idx + start)
        start += m
    gate = jnp.concatenate(gates, axis=1)
    idx = jnp.concatenate(idxs, axis=1)
    ye = _expert_ffn(xb[idx], w1, w3, w2, gate[..., None])
    return jnp.zeros((n, D), jnp.float32).at[idx.reshape(-1)].add(ye.reshape(-1, D).astype(jnp.float32))


def _pack_w_in(w_in):
    sizes = (GLA_HEADS * GLA_DK, GLA_HEADS * GLA_DK, GLA_HEADS * GLA_DV, GLA_HEADS * GLA_DV, 2 * GLA_RANK,
             SWA_NH * SWA_DH, SWA_NH * SWA_DH, SWA_NH * SWA_DH, FNET_GROUPS * FNET_DG,
             RET_HEADS * RET_DH, RET_HEADS * RET_DH, RET_HEADS * RET_DH, RET_HEADS * RET_DH)
    offs = np.concatenate([[0], np.cumsum(sizes)])
    part = lambda i: w_in[..., offs[i]:offs[i + 1]]
    a_q, a_k = part(0), part(1)
    cols = []
    for hh in range(GLA_HEADS):
        cols += [a_q[..., hh * GLA_DK:(hh + 1) * GLA_DK], a_k[..., hh * GLA_DK:(hh + 1) * GLA_DK]]
    cols += [part(2), part(3), part(4),
             jnp.zeros(w_in.shape[:2] + (BQ_OFF - LR_OFF - 2 * GLA_RANK,), w_in.dtype)]
    cols += [part(i) for i in range(5, 13)]
    out = jnp.concatenate(cols, axis=-1).astype(jnp.bfloat16)
    assert out.shape[-1] == IN_COLS_P
    return out


def _block_tables(seqs, blk):
    first, last, pos = [], [], []
    for num, T in seqs:
        per = T // blk
        assert per * blk == T
        for _ in range(num):
            first += [1] + [0] * (per - 1)
            last += [0] * (per - 1) + [1]
            pos += list(range(per))
    mk = lambda a: jnp.asarray(np.array(a, np.int32))
    return mk(first), mk(last), mk(pos)


def kernel(x_prompt, x_sample, ln_in_g, ln_in_b, w_in, gla_w_up, gla_b_up, ret_decay, w_gate, b_gate, w_branch,
           w_out, ln1_g, ln1_b, w_router, w_e1, w_e3, w_e2, ln2_g, ln2_b):
    bf16 = jnp.bfloat16
    D = D_MODEL
    seqs = (x_prompt.shape[:2], x_sample.shape[:2])
    n_p, n_s = x_prompt.shape[0] * x_prompt.shape[1], x_sample.shape[0] * x_sample.shape[1]
    n = n_p + n_s
    t_max = max(T for _, T in seqs)
    seq_first, seq_last, pos_blk = _block_tables(seqs, SEQ_BLOCK)

    w_in_p = _pack_w_in(w_in)
    wg, wb, wo = w_gate.astype(bf16), w_branch.astype(bf16), w_out.astype(bf16)
    wr = jnp.pad(w_router, ((0, 0), (0, 0), (0, LANE - N_EXPERTS))).astype(bf16)
    we1, we3, we2 = w_e1.astype(bf16), w_e3.astype(bf16), w_e2.astype(bf16)

    bufs = _ln_in(x_prompt.reshape(n_p, D), ln_in_g, ln_in_b, n, 0)
    xf, xb = _ln_in(x_sample.reshape(n_s, D), ln_in_g, ln_in_b, n, n_p, prev=bufs)
    for l in range(DEPTH):
        h = _matmul(xb, w_in_p[l], bf16, bm=1024, bn=IN_BN)
        wup, bup = _pack_gla_up(gla_w_up[l], gla_b_up[l])
        ys = (_gla(h, seq_first, seq_last, wup, bup),
              _swa(h, seqs),
              _fnet(h, seqs),
              _retention(h, seq_first, seq_last, pos_blk, ret_decay[l], t_max))
        merged = _merge(xb, wg[l], b_gate[l], ys, wb[l])
        x1f, x1b, aff = _outproj(merged, wo[l], xf, ln1_g[l], ln1_b[l], wr[l])
        moe = _expert_choice(x1b, aff, seqs, we1[l], we3[l], we2[l])
        if l + 1 < DEPTH:
            xf, xb = _res_ln(x1f, moe, ln2_g[l], ln2_b[l])
        else:
            y_p, _ = _res_ln(x1f, moe, ln2_g[l], ln2_b[l], row_off=0, rows=n_p)
            y_s, _ = _res_ln(x1f, moe, ln2_g[l], ln2_b[l], row_off=n_p, rows=n_s)
    return (y_p.reshape(x_prompt.shape), y_s.reshape(x_sample.shape))
```

```python
import functools

import jax
import jax.numpy as jnp
import numpy as np
from jax import lax
from jax.experimental import pallas as pl
from jax.experimental.pallas import tpu as pltpu

D_MODEL = 2048
DEPTH = 2
GLA_HEADS, GLA_DK, GLA_DV, GLA_RANK, GLA_TAU, GLA_CHUNK = 4, 64, 128, 16, 16.0, 64
SWA_GROUPS = ((128, 1), (512, 4), (2048, 16))
SWA_HEADS, SWA_DH = 4, 128
SWA_NH = SWA_HEADS * len(SWA_GROUPS)
SWA_BLOCK = 64
ROPE_THETA, ROPE_DIM = 500000.0, SWA_DH // 4
FNET_GROUPS, FNET_DG = 4, 128
RET_HEADS, RET_DH, RET_THETA = 4, 128, 10000.0
N_BRANCH, BRANCH_W = 4, 512
N_EXPERTS, EXPERT_FF, CAPACITY_FACTOR = 16, 2048, 2
DN_ALPHA = (2 * DEPTH) ** 0.25
LN_EPS = 1e-5

VMEM_LIMIT_BYTES = 48 * 1024 * 1024
LANE = 128

QK_OFF = 0
AV_OFF = QK_OFF + GLA_HEADS * LANE
AG_OFF = AV_OFF + GLA_HEADS * GLA_DV
LR_OFF = AG_OFF + GLA_HEADS * GLA_DV
C_OFF = LR_OFF + 4 * LANE
DQ_OFF = C_OFF + FNET_GROUPS * FNET_DG
DK_OFF = DQ_OFF + RET_HEADS * RET_DH
DV_OFF = DK_OFF + RET_HEADS * RET_DH
DG_OFF = DV_OFF + RET_HEADS * RET_DH
IN_USED = DG_OFF + RET_HEADS * RET_DH
IN_BN = 1536
IN_COLS_P = -(-IN_USED // IN_BN) * IN_BN
assert IN_COLS_P == IN_USED

SEQ_BLOCK = 512


def _cparams(*sem):
    return pltpu.CompilerParams(dimension_semantics=sem, vmem_limit_bytes=VMEM_LIMIT_BYTES)


def _mm_kernel(a_ref, b_ref, o_ref):
    o_ref[...] = jnp.dot(a_ref[...], b_ref[...], preferred_element_type=jnp.float32).astype(o_ref.dtype)


def _matmul(a, b, out_dtype, bm=1024, bn=512, a_col_blk=0, name="matmul"):
    batched = a.ndim == 3
    M = a.shape[-2]
    K, N = b.shape[-2:]
    bm = min(bm, M)
    bn = min(bn, N)
    assert M % bm == 0 and N % bn == 0, (M, N, bm, bn)
    if batched:
        E = a.shape[0]
        grid = (E, M // bm, N // bn)
        in_specs = [pl.BlockSpec((None, bm, K), lambda e, i, j: (e, i, 0)),
                    pl.BlockSpec((None, K, bn), lambda e, i, j: (e, 0, j))]
        out_specs = pl.BlockSpec((None, bm, bn), lambda e, i, j: (e, i, j))
        out_shape = jax.ShapeDtypeStruct((E, M, N), out_dtype)
        sem = ("parallel", "parallel", "arbitrary")
    else:
        grid = (M // bm, N // bn)
        in_specs = [pl.BlockSpec((bm, K), lambda i, j: (i, a_col_blk)),
                    pl.BlockSpec((K, bn), lambda i, j: (0, j))]
        out_specs = pl.BlockSpec((bm, bn), lambda i, j: (i, j))
        out_shape = jax.ShapeDtypeStruct((M, N), out_dtype)
        sem = ("parallel", "arbitrary")
    return pl.pallas_call(
        _mm_kernel, name=name, grid=grid, in_specs=in_specs, out_specs=out_specs, out_shape=out_shape,
        compiler_params=_cparams(*sem),
    )(a, b)


def _ln_rows(x, g, b):
    mu = jnp.mean(x, -1, keepdims=True)
    xc = x - mu
    var = jnp.mean(xc * xc, -1, keepdims=True)
    return xc * lax.rsqrt(var + LN_EPS) * g + b


def _ln_in_kernel(xa_ref, xc_ref, g_ref, b_ref, xf_ref, xb_ref, *, na_blocks):
    i = pl.program_id(0)

    def emit(x_ref):
        y = _ln_rows(x_ref[...], g_ref[...], b_ref[...])
        xf_ref[...] = y
        xb_ref[...] = y.astype(jnp.bfloat16)

    pl.when(i < na_blocks)(lambda: emit(xa_ref))
    pl.when(i >= na_blocks)(lambda: emit(xc_ref))


def _ln_in(xa, xc, g, b, bm=512):
    (na, D), nc = xa.shape, xc.shape[0]
    nab, ncb = na // bm, nc // bm
    row = pl.BlockSpec((bm, D), lambda i: (i, 0))
    vec = pl.BlockSpec((1, D), lambda i: (0, 0))
    return pl.pallas_call(
        functools.partial(_ln_in_kernel, na_blocks=nab), name="ln_in", grid=(nab + ncb,),
        in_specs=[pl.BlockSpec((bm, D), lambda i: (jnp.minimum(i, nab - 1), 0)),
                  pl.BlockSpec((bm, D), lambda i: (jnp.maximum(i - nab, 0), 0)), vec, vec],
        out_specs=[row, row],
        out_shape=[jax.ShapeDtypeStruct((na + nc, D), jnp.float32), jax.ShapeDtypeStruct((na + nc, D), jnp.bfloat16)],
        compiler_params=_cparams("arbitrary"))(xa, xc, g.reshape(1, D), b.reshape(1, D))


def _merge_kernel(xb_ref, wg_ref, bg_ref, ya_ref, yb_ref, yc_ref, yd_ref, wb_ref, o_ref):
    xb = xb_ref[...]
    acc = None
    for i, y_ref in enumerate((ya_ref, yb_ref, yc_ref, yd_ref)):
        gate = jax.nn.sigmoid(jnp.dot(xb, wg_ref[i], preferred_element_type=jnp.float32) + bg_ref[i])
        term = gate * jnp.dot(y_ref[...], wb_ref[i], preferred_element_type=jnp.float32)
        acc = term if acc is None else acc + term
    o_ref[...] = acc.astype(o_ref.dtype)


def _merge(xb, wg, bg, ys, wb, bm=1024, bn=256):
    n, D = xb.shape
    W = ys[0].shape[1]
    y_spec = pl.BlockSpec((bm, W), lambda i, j: (i, 0))
    return pl.pallas_call(
        _merge_kernel, name="branch_merge", grid=(n // bm, D // bn),
        in_specs=[pl.BlockSpec((bm, D), lambda i, j: (i, 0)),
                  pl.BlockSpec((N_BRANCH, D, bn), lambda i, j: (0, 0, j)),
                  pl.BlockSpec((N_BRANCH, 1, bn), lambda i, j: (0, 0, j)),
                  y_spec, y_spec, y_spec, y_spec,
                  pl.BlockSpec((N_BRANCH, W, bn), lambda i, j: (0, 0, j))],
        out_specs=pl.BlockSpec((bm, bn), lambda i, j: (i, j)),
        out_shape=jax.ShapeDtypeStruct((n, D), jnp.bfloat16),
        compiler_params=_cparams("parallel", "arbitrary"),
    )(xb, wg, bg.reshape(N_BRANCH, 1, D), *ys, wb)


def _outproj_kernel(m_ref, w_ref, x_ref, g_ref, b_ref, wr_ref, xf_ref, xb_ref, aff_ref):
    r = jnp.dot(m_ref[...], w_ref[...], preferred_element_type=jnp.float32)
    y = _ln_rows(DN_ALPHA * x_ref[...] + r, g_ref[...], b_ref[...])
    yb = y.astype(jnp.bfloat16)
    xf_ref[...] = y
    xb_ref[...] = yb
    logits = jnp.dot(yb, wr_ref[...], preferred_element_type=jnp.float32)
    lane = lax.broadcasted_iota(jnp.int32, logits.shape, 1)
    logits = jnp.where(lane < N_EXPERTS, logits, -jnp.inf)
    e = jnp.exp(logits - jnp.max(logits, -1, keepdims=True))
    aff_ref[...] = e / jnp.sum(e, -1, keepdims=True)


def _outproj(merged, w_out, x, g, b, w_router_p, bm=512):
    n, D = x.shape
    row = pl.BlockSpec((bm, D), lambda i: (i, 0))
    vec = pl.BlockSpec((1, D), lambda i: (0, 0))
    return pl.pallas_call(
        _outproj_kernel, name="outproj_ln_router", grid=(n // bm,),
        in_specs=[row, pl.BlockSpec((D, D), lambda i: (0, 0)), row, vec, vec,
                  pl.BlockSpec((D, LANE), lambda i: (0, 0))],
        out_specs=[row, row, pl.BlockSpec((bm, LANE), lambda i: (i, 0))],
        out_shape=[jax.ShapeDtypeStruct((n, D), jnp.float32), jax.ShapeDtypeStruct((n, D), jnp.bfloat16),
                   jax.ShapeDtypeStruct((n, LANE), jnp.float32)],
        compiler_params=_cparams("parallel"),
    )(merged, w_out, x, g.reshape(1, D), b.reshape(1, D), w_router_p)


def _log_sigmoid(z):
    return jnp.minimum(z, 0.0) - jnp.log(1.0 + jnp.exp(-jnp.abs(z)))


def _headnorm(o):
    mu = jnp.mean(o, -1, keepdims=True)
    oc = o - mu
    var = jnp.mean(oc * oc, -1, keepdims=True)
    return oc * lax.rsqrt(var + LN_EPS)


def _silu(x):
    return x * jax.nn.sigmoid(x)


def _chunk_cumsum(g, chunk, inclusive):
    row = lax.broadcasted_iota(jnp.int32, g.shape, 0) % chunk
    b = g
    s = 1
    while s < chunk:
        b = b + jnp.where(row >= s, pltpu.roll(b, s, axis=0), 0.0)
        s *= 2
    return b if inclusive else b - g


def _gla_kernel(flag_ref, qk_ref, v_ref, lr_ref, wup_ref, bup_ref, *rest, reverse, nblk):
    if reverse:
        gate_ref, of_ref, o_ref, st_ref = rest
    else:
        o_ref, st_ref = rest
    f32, bf16 = jnp.float32, jnp.bfloat16
    C = GLA_CHUNK
    TB = qk_ref.shape[0]
    nc = TB // C
    s_idx = pl.program_id(1)
    blk = nblk - 1 - s_idx if reverse else s_idx

    @pl.when(flag_ref[blk] == 1)
    def _():
        st_ref[...] = jnp.zeros_like(st_ref)

    qk = qk_ref[...].astype(f32)
    q = qk[:, :GLA_DK] * (GLA_DK ** -0.5)
    k = qk[:, GLA_DK:]
    z = jnp.dot(lr_ref[...], wup_ref[...], preferred_element_type=f32) + bup_ref[...]
    g = _log_sigmoid(z) * (1.0 / GLA_TAU)
    cs = _chunk_cumsum(g, C, inclusive=not reverse)
    cs3 = cs.reshape(nc, C, GLA_DK)
    mid = cs3[:, C // 2:C // 2 + 1, :]
    if reverse:
        tot3 = cs3[:, C - 1:C, :] + g.reshape(nc, C, GLA_DK)[:, C - 1:C, :]
        pq3, pk3 = tot3 - cs3, cs3
        qm3, km3 = mid - cs3, cs3 - mid
    else:
        tot3 = cs3[:, C - 1:C, :]
        pq3, pk3 = cs3, tot3 - cs3
        qm3, km3 = cs3 - mid, mid - cs3
    q3 = q.reshape(nc, C, GLA_DK)
    k3 = k.reshape(nc, C, GLA_DK)
    qe = (q3 * jnp.exp(pq3)).astype(bf16)
    kd = (k3 * jnp.exp(pk3)).astype(bf16)
    qm = (q3 * jnp.exp(qm3)).astype(bf16)
    km = (k3 * jnp.exp(km3)).astype(bf16)
    etot = jnp.exp(tot3)
    v = v_ref[...]
    ri = lax.broadcasted_iota(jnp.int32, (C, C), 0)
    ci = lax.broadcasted_iota(jnp.int32, (C, C), 1)
    keep = (ci > ri) if reverse else (ci <= ri)

    st = st_ref[...]
    outs = [None] * nc
    for c in (range(nc - 1, -1, -1) if reverse else range(nc)):
        vc = v[c * C:(c + 1) * C, :]
        att = lax.dot_general(qm[c], km[c], (((1,), (1,)), ((), ())), preferred_element_type=f32)
        att = jnp.where(keep, att, 0.0).astype(bf16)
        o_c = jnp.dot(att, vc, preferred_element_type=f32)
        o_c = o_c + lax.dot_general(qe[c], st.astype(bf16), (((1,), (1,)), ((), ())), preferred_element_type=f32)
        upd = lax.dot_general(vc, kd[c], (((0,), (0,)), ((), ())), preferred_element_type=f32)
        st = st * etot[c] + upd
        outs[c] = o_c
    st_ref[...] = st
    o = jnp.concatenate(outs, axis=0)
    if reverse:
        o = _headnorm(o + of_ref[...])
        o_ref[...] = (o * _silu(gate_ref[...].astype(f32))).astype(o_ref.dtype)
    else:
        o_ref[...] = o


def _gla(h, seq_first, seq_last, wup, bup):
    n = h.shape[0]
    TB = SEQ_BLOCK
    nblk = n // TB
    H = GLA_HEADS

    def col(off):
        return off // LANE

    def specs(reverse):
        blk = (lambda s: nblk - 1 - s) if reverse else (lambda s: s)
        sp = [pl.BlockSpec((TB, LANE), lambda hh, s, f: (blk(s), col(QK_OFF) + hh)),
              pl.BlockSpec((TB, LANE), lambda hh, s, f: (blk(s), col(AV_OFF) + hh)),
              pl.BlockSpec((TB, LANE), lambda hh, s, f: (blk(s), col(LR_OFF))),
              pl.BlockSpec((None, None, LANE, GLA_DK), lambda hh, s, f: (1 if reverse else 0, hh, 0, 0)),
              pl.BlockSpec((None, None, 1, GLA_DK), lambda hh, s, f: (1 if reverse else 0, hh, 0, 0))]
        out = pl.BlockSpec((TB, LANE), lambda hh, s, f: (blk(s), hh))
        return sp, out, blk

    scratch = [pltpu.VMEM((GLA_DV, GLA_DK), jnp.float32)]
    sp, out, _ = specs(False)
    o_f = pl.pallas_call(
        functools.partial(_gla_kernel, reverse=False, nblk=nblk), name="gla_fwd",
        grid_spec=pltpu.PrefetchScalarGridSpec(num_scalar_prefetch=1, grid=(H, nblk), in_specs=sp, out_specs=out,
                                               scratch_shapes=scratch),
        out_shape=jax.ShapeDtypeStruct((n, H * GLA_DV), jnp.float32),
        compiler_params=_cparams("parallel", "arbitrary"),
    )(seq_first, h, h, h, wup, bup)
    sp, out, blk = specs(True)
    sp = sp + [pl.BlockSpec((TB, LANE), lambda hh, s, f: (blk(s), col(AG_OFF) + hh)),
               pl.BlockSpec((TB, LANE), lambda hh, s, f: (blk(s), hh))]
    return pl.pallas_call(
        functools.partial(_gla_kernel, reverse=True, nblk=nblk), name="gla_bwd",
        grid_spec=pltpu.PrefetchScalarGridSpec(num_scalar_prefetch=1, grid=(H, nblk), in_specs=sp, out_specs=out,
                                               scratch_shapes=scratch),
        out_shape=jax.ShapeDtypeStruct((n, H * GLA_DV), jnp.bfloat16),
        compiler_params=_cparams("parallel", "arbitrary"),
    )(seq_last, h, h, h, wup, bup, h, o_f)


def _pack_gla_up(w_up, b_up):
    H, dk, r = GLA_HEADS, GLA_DK, GLA_RANK
    w = w_up.reshape(2, r, H, dk).transpose(0, 2, 1, 3)
    wp = jnp.zeros((2, H, LANE, dk), jnp.float32)
    wp = wp.at[0, :, 0:r].set(w[0]).at[1, :, r:2 * r].set(w[1])
    return wp.astype(jnp.bfloat16), b_up.reshape(2, H, 1, dk).astype(jnp.float32)


def _rope_tables(pos, dim, theta, width):
    half = dim // 2
    T = pos.shape[0]
    inv = 1.0 / (theta ** (jnp.arange(half, dtype=jnp.float32) / half))
    ang = pos.astype(jnp.float32)[:, None] * inv[None, :]
    c, s = jnp.cos(ang), jnp.sin(ang)
    rest = width - dim
    cos = jnp.concatenate([c, c, jnp.ones((T, rest), jnp.float32)], axis=1)
    sin_lo = jnp.concatenate([-s, jnp.zeros((T, half + rest), jnp.float32)], axis=1)
    sin_hi = jnp.concatenate([jnp.zeros((T, half), jnp.float32), s, jnp.zeros((T, rest), jnp.float32)], axis=1)
    return cos, sin_lo, sin_hi


def _ret_kernel(flag_ref, pos_ref, q_ref, k_ref, v_ref, cos_ref, sin_ref, dec_ref, *rest, reverse, nblk):
    if reverse:
        gate_ref, o1_ref, o_ref, st_ref = rest
    else:
        o_ref, st_ref, dm_ref = rest
    f32, bf16 = jnp.float32, jnp.bfloat16
    TB = q_ref.shape[0]
    s_idx = pl.program_id(1)
    blk = nblk - 1 - s_idx if reverse else s_idx

    dec = dec_ref[...]
    lg = -(jnp.maximum(dec, 0.0) + jnp.log1p(jnp.exp(-jnp.abs(dec))))
    lg_f, lg_b = lg[0, 0:1, 0:1], lg[1, 0:1, 0:1]

    @pl.when(flag_ref[blk] == 1)
    def _():
        st_ref[...] = jnp.zeros_like(st_ref)

    cos, sin = cos_ref[...], sin_ref[...]

    def rot(x):
        return x * cos + pltpu.roll(x, RET_DH // 2, axis=1) * sin

    qr = rot(q_ref[...].astype(f32))
    kr = rot(k_ref[...].astype(f32)) * (RET_DH ** -0.5)
    v = v_ref[...]
    idx = lax.broadcasted_iota(jnp.int32, (TB, 1), 0).astype(f32)
    st = st_ref[...]
    cross = jnp.dot(qr.astype(bf16), st.astype(bf16), preferred_element_type=f32)
    if reverse:
        cross = cross * jnp.exp(lg_b * (TB - idx))
        kz = (kr * jnp.exp(lg_b * idx)).astype(bf16)
        st_ref[...] = st * jnp.exp(lg_b * TB) + lax.dot_general(kz, v, (((0,), (0,)), ((), ())),
                                                                 preferred_element_type=f32)
        o = _headnorm(o1_ref[...] + cross)
        o_ref[...] = (o * _silu(gate_ref[...].astype(f32))).astype(o_ref.dtype)
    else:
        @pl.when(s_idx == 0)
        def _():
            ri = lax.broadcasted_iota(jnp.int32, (TB, TB), 0)
            ci = lax.broadcasted_iota(jnp.int32, (TB, TB), 1)
            d = (ri - ci).astype(f32)
            dm_ref[...] = jnp.where(d >= 0, jnp.exp(lg_f * jnp.maximum(d, 0.0)), jnp.exp(lg_b * jnp.maximum(-d, 0.0)))

        cross = cross * jnp.exp(lg_f * (idx + 1.0))
        kz = (kr * jnp.exp(lg_f * (TB - 1.0 - idx))).astype(bf16)
        st_ref[...] = st * jnp.exp(lg_f * TB) + lax.dot_general(kz, v, (((0,), (0,)), ((), ())),
                                                                 preferred_element_type=f32)
        sc = lax.dot_general(qr.astype(bf16), kr.astype(bf16), (((1,), (1,)), ((), ())), preferred_element_type=f32)
        sc = (sc * dm_ref[...]).astype(bf16)
        o_ref[...] = jnp.dot(sc, v, preferred_element_type=f32) + cross


def _retention(h, seq_first, seq_last, pos_blk, decay, t_max):
    n = h.shape[0]
    TB = SEQ_BLOCK
    nblk = n // TB
    H, dh = RET_HEADS, RET_DH
    cos, sin_lo, sin_hi = _rope_tables(jnp.arange(t_max), dh, RET_THETA, dh)
    sin = sin_lo + sin_hi
    dec = jnp.broadcast_to(decay.astype(jnp.float32).T[:, :, None, None], (H, 2, 8, LANE))

    def col(off):
        return off // LANE

    def specs(reverse):
        blk = (lambda s: nblk - 1 - s) if reverse else (lambda s: s)
        tok = lambda off: pl.BlockSpec((TB, LANE), lambda hh, s, f, p: (blk(s), col(off) + hh))
        tab = pl.BlockSpec((TB, LANE), lambda hh, s, f, p: (p[blk(s)], 0))
        sp = [tok(DQ_OFF), tok(DK_OFF), tok(DV_OFF), tab, tab,
              pl.BlockSpec((None, 2, 8, LANE), lambda hh, s, f, p: (hh, 0, 0, 0))]
        return sp, tok, pl.BlockSpec((TB, LANE), lambda hh, s, f, p: (blk(s), hh))

    sp, _, out = specs(False)
    o1 = pl.pallas_call(
        functools.partial(_ret_kernel, reverse=False, nblk=nblk), name="ret_fwd",
        grid_spec=pltpu.PrefetchScalarGridSpec(
            num_scalar_prefetch=2, grid=(H, nblk), in_specs=sp, out_specs=out,
            scratch_shapes=[pltpu.VMEM((dh, dh), jnp.float32), pltpu.VMEM((TB, TB), jnp.float32)]),
        out_shape=jax.ShapeDtypeStruct((n, H * dh), jnp.float32),
        compiler_params=_cparams("arbitrary", "arbitrary"),
    )(seq_first, pos_blk, h, h, h, cos, sin, dec)
    sp, tok, out = specs(True)
    sp = sp + [tok(DG_OFF), out]
    return pl.pallas_call(
        functools.partial(_ret_kernel, reverse=True, nblk=nblk), name="ret_bwd",
        grid_spec=pltpu.PrefetchScalarGridSpec(
            num_scalar_prefetch=2, grid=(H, nblk), in_specs=sp, out_specs=out,
            scratch_shapes=[pltpu.VMEM((dh, dh), jnp.float32)]),
        out_shape=jax.ShapeDtypeStruct((n, H * dh), jnp.bfloat16),
        compiler_params=_cparams("arbitrary", "arbitrary"),
    )(seq_last, pos_blk, h, h, h, cos, sin, dec, h, o1)


SWA_Q = 128
SWA_HALF = 64
assert all(w // (2 * d) == SWA_HALF for w, d in SWA_GROUPS)
NEG_BIG = -1e30


def _swa_kernel(hl_ref, hr_ref, q_ref, kl_ref, kc_ref, kr_ref, vl_ref, vc_ref, vr_ref,
                qcos_ref, qs1_ref, qs2_ref, kcos_ref, ks1_ref, ks2_ref, o_ref, lse_ref):
    f32, bf16 = jnp.float32, jnp.bfloat16
    j = pl.program_id(1)
    Q, HW = SWA_Q, SWA_HALF
    W = Q + 2 * HW
    ri = lax.broadcasted_iota(jnp.int32, (Q, W), 0)
    ci = lax.broadcasted_iota(jnp.int32, (Q, W), 1)
    rel = ci - HW - ri
    ok = (rel <= HW) & (rel >= -HW)
    ok = ok & ((ci >= HW) | (hl_ref[j] == 1)) & ((ci < HW + Q) | (hr_ref[j] == 1))
    lane = lax.broadcasted_iota(jnp.int32, (Q, LANE), 1)

    def rot(x, c_ref, s1_ref, s2_ref):
        return (x * c_ref[...] + pltpu.roll(x, LANE - ROPE_DIM // 2, axis=1) * s1_ref[...]
                + pltpu.roll(x, ROPE_DIM // 2, axis=1) * s2_ref[...])

    lse_tile = jnp.zeros((Q, LANE), f32)
    for hh in range(SWA_HEADS):
        sl = slice(hh * SWA_DH, (hh + 1) * SWA_DH)
        q = rot(q_ref[:, sl].astype(f32), qcos_ref, qs1_ref, qs2_ref) * (SWA_DH ** -0.5)
        k = jnp.concatenate([kl_ref[:, sl], kc_ref[:, sl], kr_ref[:, sl]], axis=0).astype(f32)
        k = rot(k, kcos_ref, ks1_ref, ks2_ref)
        v = jnp.concatenate([vl_ref[:, sl], vc_ref[:, sl], vr_ref[:, sl]], axis=0)
        s = lax.dot_general(q.astype(bf16), k.astype(bf16), (((1,), (1,)), ((), ())), preferred_element_type=f32)
        s = jnp.where(ok, s, NEG_BIG)
        m = jnp.max(s, -1, keepdims=True)
        p = jnp.exp(s - m)
        den = jnp.sum(p, -1, keepdims=True)
        o = jnp.dot(p.astype(bf16), v, preferred_element_type=f32) / den
        o_ref[:, sl] = o
        lse_tile = jnp.where(lane == hh, m + jnp.log(den), lse_tile)
    lse_ref[...] = lse_tile


def _swa_merge_kernel(*refs):
    ng = len(SWA_GROUPS)
    o_refs, l_refs, y_ref = refs[:ng], refs[ng:2 * ng], refs[2 * ng]
    scratch = refs[2 * ng + 1:]
    assert SWA_DH == LANE
    os_, ls_ = [], []
    si = 0
    for gi, (_, dil) in enumerate(SWA_GROUPS):
        if dil == 1:
            os_.append(lambda hh, r=o_refs[gi]: r[0, :, hh * LANE:(hh + 1) * LANE])
            ls_.append(l_refs[gi].at[0])
            continue
        o_s, l_s = scratch[si], scratch[si + 1]
        si += 2
        rows = l_s.shape[0] // dil
        for r in range(dil):
            for hh in range(SWA_HEADS):
                o_s[hh, pl.ds(r, rows, stride=dil), :] = o_refs[gi][r, :, hh * LANE:(hh + 1) * LANE]
            l_s[pl.ds(r, rows, stride=dil), :] = l_refs[gi][r]
        os_.append(lambda hh, s=o_s: s[hh])
        ls_.append(l_s)
    for hh in range(SWA_HEADS):
        ls = [r[:, hh:hh + 1] for r in ls_]
        m = functools.reduce(jnp.maximum, ls)
        ws = [jnp.exp(l - m) for l in ls]
        tot = functools.reduce(lambda a, b: a + b, ws)
        y = functools.reduce(lambda a, b: a + b, [w * o(hh) for w, o in zip(ws, os_)])
        y_ref[:, hh * LANE:(hh + 1) * LANE] = (y / tot).astype(y_ref.dtype)


def _proj_strided_kernel(a_ref, b_ref, o_ref, acc_ref, *, dil):
    acc = jnp.dot(a_ref[...], b_ref[...], preferred_element_type=jnp.float32)
    if dil == 1:
        o_ref[0] = acc.astype(o_ref.dtype)
        return
    rows = acc_ref.shape[1] // dil
    for s in range(acc_ref.shape[0]):
        acc_ref[s] = acc[:, s * LANE:(s + 1) * LANE]
        for r in range(dil):
            o_ref[r, :, s * LANE:(s + 1) * LANE] = acc_ref[s, pl.ds(r, rows, stride=dil), :].astype(o_ref.dtype)


def _proj_strided(a, b, dil, bm=1024, bn=512):
    n, K = a.shape
    N = b.shape[1]
    return pl.pallas_call(
        functools.partial(_proj_strided_kernel, dil=dil), name=f"swa_proj_dil{dil}", grid=(n // bm, N // bn),
        in_specs=[pl.BlockSpec((bm, K), lambda i, j: (i, 0)), pl.BlockSpec((K, bn), lambda i, j: (0, j))],
        out_specs=pl.BlockSpec((dil, bm // dil, bn), lambda i, j: (0, i, j)),
        out_shape=jax.ShapeDtypeStruct((dil, n // dil, N), jnp.bfloat16),
        scratch_shapes=[pltpu.VMEM((bn // LANE, bm, LANE), jnp.float32)],
        compiler_params=_cparams("parallel", "arbitrary"),
    )(a, b)


def _swa(xb, w_groups, seqs):
    n = xb.shape[0]
    Q, HW = SWA_Q, SWA_HALF
    GW = SWA_HEADS * SWA_DH
    outs, lses = [], []
    for gi, (window, dil) in enumerate(SWA_GROUPS):
        hs = _proj_strided(xb, w_groups[gi], dil)
        rows = n // dil
        nblk = rows // Q
        has_l, has_r = [], []
        for num, T in seqs:
            per = T // dil // Q
            assert per * Q * dil == T
            has_l += ([0] + [1] * (per - 1)) * num
            has_r += ([1] * (per - 1) + [0]) * num
        has_l = jnp.asarray(np.array(has_l, np.int32))
        has_r = jnp.asarray(np.array(has_r, np.int32))
        qpos = jnp.arange(Q) * dil
        kpos = (jnp.arange(Q + 2 * HW) - HW) * dil
        qt = _rope_tables(qpos, ROPE_DIM, ROPE_THETA, SWA_DH)
        kt = _rope_tables(kpos, ROPE_DIM, ROPE_THETA, SWA_DH)
        last64 = rows // HW - 1

        def center(c):
            return pl.BlockSpec((None, Q, GW), lambda r, j, hl, hr: (r, j, c))

        def left(c):
            return pl.BlockSpec((None, HW, GW), lambda r, j, hl, hr: (r, jnp.maximum(2 * j - 1, 0), c))

        def right(c):
            return pl.BlockSpec((None, HW, GW), lambda r, j, hl, hr: (r, jnp.minimum(2 * j + 2, last64), c))

        qtab = pl.BlockSpec((Q, SWA_DH), lambda r, j, hl, hr: (0, 0))
        ktab = pl.BlockSpec((Q + 2 * HW, SWA_DH), lambda r, j, hl, hr: (0, 0))
        o_g, lse_g = pl.pallas_call(
            _swa_kernel, name=f"swa_dil{dil}",
            grid_spec=pltpu.PrefetchScalarGridSpec(
                num_scalar_prefetch=2, grid=(dil, nblk),
                in_specs=[center(0), left(1), center(1), right(1), left(2), center(2), right(2),
                          qtab, qtab, qtab, ktab, ktab, ktab],
                out_specs=[pl.BlockSpec((None, Q, GW), lambda r, j, hl, hr: (r, j, 0)),
                           pl.BlockSpec((None, Q, LANE), lambda r, j, hl, hr: (r, j, 0))]),
            out_shape=[jax.ShapeDtypeStruct((dil, rows, GW), jnp.float32),
                       jax.ShapeDtypeStruct((dil, rows, LANE), jnp.float32)],
            compiler_params=_cparams("parallel", "parallel"),
        )(has_l, has_r, hs, hs, hs, hs, hs, hs, hs, *qt, *kt)
        outs.append(o_g)
        lses.append(lse_g)
    bm = 1024
    dils = [d for _, d in SWA_GROUPS]
    osp = [pl.BlockSpec((d, bm // d, GW), lambda i: (0, i, 0)) for d in dils]
    lsp = [pl.BlockSpec((d, bm // d, LANE), lambda i: (0, i, 0)) for d in dils]
    scratch = []
    for d in dils:
        if d > 1:
            scratch += [pltpu.VMEM((SWA_HEADS, bm, LANE), jnp.float32), pltpu.VMEM((bm, LANE), jnp.float32)]
    return pl.pallas_call(
        _swa_merge_kernel, name="swa_merge", grid=(n // bm,), in_specs=osp + lsp,
        out_specs=pl.BlockSpec((bm, GW), lambda i: (i, 0)),
        out_shape=jax.ShapeDtypeStruct((n, GW), jnp.bfloat16), scratch_shapes=scratch,
        compiler_params=_cparams("parallel"),
    )(*outs, *lses)


FFT_T2 = 128
FW = FNET_GROUPS * FNET_DG


def _dft_cos_sin(n_out, n_in, period):
    ang = (2.0 * np.pi / period) * np.mod(np.outer(np.arange(n_out), np.arange(n_in)), period)
    return np.cos(ang), np.sin(ang)


def _fnet_stage1_kernel(z_ref, m_ref, a_ref, *, cols):
    mat = m_ref[...]
    for c in range(cols):
        zr = z_ref[:, c * 2 * FW:c * 2 * FW + FW]
        zi = z_ref[:, c * 2 * FW + FW:(c + 1) * 2 * FW]
        a = jnp.dot(mat, jnp.concatenate([zr, zi], axis=0), preferred_element_type=jnp.float32)
        t1 = zr.shape[0]
        a_ref[:, c * 2 * FW:c * 2 * FW + FW] = a[:t1].astype(a_ref.dtype)
        a_ref[:, c * 2 * FW + FW:(c + 1) * 2 * FW] = a[t1:].astype(a_ref.dtype)


def _fnet_stage2_kernel(a_ref, g_ref, y_ref, *, scale):
    y = jnp.dot(g_ref[...], a_ref[...], preferred_element_type=jnp.float32)
    y_ref[...] = (y * scale).astype(y_ref.dtype)


def _fnet(h, seqs):
    n = h.shape[0]
    bf16 = jnp.bfloat16
    T2 = FFT_T2
    cc, ss = _dft_cos_sin(FNET_DG, FNET_DG, FNET_DG)
    eye = np.eye(FNET_GROUPS)
    wc = jnp.asarray(np.concatenate([np.kron(eye, cc), -np.kron(eye, ss)], axis=1), bf16)
    z = _matmul(h, wc, bf16, bn=2 * FW, a_col_blk=C_OFF // FW, name="fnet_channel_dft")
    zz = z.reshape(n // T2, T2 * 2 * FW)
    outs = []
    row0 = 0
    for num, T in seqs:
        T1 = T // T2
        assert T1 * T2 == T and T1 % 16 == 0
        c1, s1 = _dft_cos_sin(T1, T1, T1)
        mat = jnp.asarray(np.block([[c1, s1], [-s1, c1]]), bf16)
        cols = min(T2, max(1, 1024 // T1))
        rb0 = row0 // T1
        a = pl.pallas_call(
            functools.partial(_fnet_stage1_kernel, cols=cols), name=f"fnet_dft_t1_{T1}", grid=(num, T2 // cols),
            in_specs=[pl.BlockSpec((T1, cols * 2 * FW), lambda b, j: (rb0 + b, j)),
                      pl.BlockSpec((2 * T1, 2 * T1), lambda b, j: (0, 0))],
            out_specs=pl.BlockSpec((T1, cols * 2 * FW), lambda b, j: (b, j)),
            out_shape=jax.ShapeDtypeStruct((num * T1, T2 * 2 * FW), bf16),
            compiler_params=_cparams("parallel", "parallel"),
        )(zz, mat)
        ec, es = _dft_cos_sin(T, T2, T)
        g = jnp.asarray(np.stack([ec, es], axis=-1).reshape(T2, T1 * 2 * T2), bf16)
        y = pl.pallas_call(
            functools.partial(_fnet_stage2_kernel, scale=float(1.0 / np.sqrt(T * FNET_DG))), name=f"fnet_dft_t2_{T1}",
            grid=(num, T1),
            in_specs=[pl.BlockSpec((None, 2 * T2, FW), lambda b, k1: (b * T1 + k1, 0, 0)),
                      pl.BlockSpec((T2, 2 * T2), lambda b, k1: (0, k1))],
            out_specs=pl.BlockSpec((T2, FW), lambda b, k1: (b, k1)),
            out_shape=jax.ShapeDtypeStruct((num * T2, T1 * FW), bf16),
            compiler_params=_cparams("parallel", "parallel"),
        )(a.reshape(num * T1, 2 * T2, FW), g)
        outs.append(y.reshape(num * T, FW))
        row0 += num * T1
    return jnp.concatenate(outs, axis=0)


TOK_RADIX = 256


def _ffn_kernel(x_ref, w1_ref, w3_ref, w2_ref, g_ref, t_ref, o_ref, acc_ref):
    f = pl.program_id(2)
    D = x_ref.shape[1]
    x = x_ref[...]
    h1 = jnp.dot(x, w1_ref[...], preferred_element_type=jnp.float32)
    h3 = jnp.dot(x, w3_ref[...], preferred_element_type=jnp.float32)
    part = jnp.dot((_silu(h1) * h3).astype(jnp.bfloat16), w2_ref[...], preferred_element_type=jnp.float32)

    @pl.when(f == 0)
    def _():
        acc_ref[...] = part

    @pl.when(f > 0)
    def _():
        acc_ref[...] += part

    @pl.when(f == pl.num_programs(2) - 1)
    def _():
        o_ref[:, :D] = (acc_ref[...] * g_ref[...]).astype(o_ref.dtype)
        tok = t_ref[...]
        lane = lax.broadcasted_iota(jnp.int32, (tok.shape[0], LANE), 1)
        digits = jnp.where(lane == 0, tok // TOK_RADIX, jnp.where(lane == 1, tok % TOK_RADIX, 0))
        o_ref[:, D:] = digits.astype(jnp.float32).astype(o_ref.dtype)


def _expert_ffn(xe, w1, w3, w2, gate, tok, bm=1024, bf=512):
    E, R, D = xe.shape
    F = w1.shape[-1]
    col = pl.BlockSpec((None, bm, 1), lambda e, i, f: (e, i, 0))
    return pl.pallas_call(
        _ffn_kernel, name="expert_ffn", grid=(E, R // bm, F // bf),
        in_specs=[pl.BlockSpec((None, bm, D), lambda e, i, f: (e, i, 0)),
                  pl.BlockSpec((None, D, bf), lambda e, i, f: (e, 0, f)),
                  pl.BlockSpec((None, D, bf), lambda e, i, f: (e, 0, f)),
                  pl.BlockSpec((None, bf, D), lambda e, i, f: (e, f, 0)),
                  col, col],
        out_specs=pl.BlockSpec((None, bm, D + LANE), lambda e, i, f: (e, i, 0)),
        out_shape=jax.ShapeDtypeStruct((E, R, D + LANE), jnp.bfloat16),
        scratch_shapes=[pltpu.VMEM((bm, D), jnp.float32)],
        compiler_params=_cparams("parallel", "parallel", "arbitrary"),
    )(xe, w1, w3, w2, gate, tok)


COMB_BM = 256
COMB_G = 16
COMB_K = 256


def _combine_kernel(lo_ref, x_ref, g_ref, b_ref, ye_hbm, *rest, nblk_all, blk_off, want_bf16):
    if want_bf16:
        xf_ref, xb_ref, buf_ref, acc_ref, sem = rest
    else:
        xf_ref, buf_ref, acc_ref, sem = rest
    f32, bf16 = jnp.float32, jnp.bfloat16
    E = ye_hbm.shape[0]
    D = x_ref.shape[1]
    BM, G, KC = COMB_BM, COMB_G, COMB_K
    i = pl.program_id(0)
    blk = i + blk_off

    @pl.when(i == 0)
    def _():
        buf_ref[...] = jnp.zeros_like(buf_ref)

    def granule_copy(e, src_row, dst_row):
        return pltpu.make_async_copy(ye_hbm.at[e, pl.ds(src_row, G)], buf_ref.at[pl.ds(dst_row, G)], sem)

    off = jnp.int32(0)
    for e in range(E):
        lo = lo_ref[e * (nblk_all + 1) + blk]
        hi = lo_ref[e * (nblk_all + 1) + blk + 1]
        lo_al = (lo // G) * G
        ng = jnp.where(hi > lo, (hi - lo_al + G - 1) // G, 0)

        def issue(g, carry, e=e, lo_al=lo_al, off=off):
            granule_copy(e, pl.multiple_of(lo_al + g * G, G), pl.multiple_of(off + g * G, G)).start()
            return carry

        lax.fori_loop(0, ng, issue, 0)
        off = off + ng * G

    def wait_one(g, carry):
        granule_copy(0, 0, 0).wait()
        return carry

    lax.fori_loop(0, off // G, wait_one, 0)

    acc_ref[...] = jnp.zeros_like(acc_ref)
    t0 = (blk * BM).astype(f32)
    lane_tok = lax.broadcasted_iota(jnp.int32, (KC, BM), 1).astype(f32)
    row = lax.broadcasted_iota(jnp.int32, (KC, 1), 0)

    def chunk(c, carry):
        win = buf_ref[pl.ds(pl.multiple_of(c * KC, KC), KC), :]
        tok = win[:, D:D + 1].astype(f32) * TOK_RADIX + win[:, D + 1:D + 2].astype(f32)
        tok = jnp.where(row + c * KC < off, tok, -1.0)
        onehot_t = (tok - t0 == lane_tok).astype(bf16)
        acc_ref[...] += lax.dot_general(onehot_t, win[:, :D], (((0,), (0,)), ((), ())), preferred_element_type=f32)
        return carry

    lax.fori_loop(0, (off + KC - 1) // KC, chunk, 0)
    y = _ln_rows(DN_ALPHA * x_ref[...] + acc_ref[...], g_ref[...], b_ref[...])
    xf_ref[...] = y
    if want_bf16:
        xb_ref[...] = y.astype(bf16)


def _combine_ln(x, ye, lo, g, b, row_off, rows, want_bf16):
    n, D = x.shape
    E, R, DY = ye.shape
    BM = COMB_BM
    nblk_all = n // BM
    ob = row_off // BM
    kmax = -(-E * (BM + COMB_G) // COMB_K) * COMB_K
    irow = pl.BlockSpec((BM, D), lambda i, lo: (i + ob, 0))
    orow = pl.BlockSpec((BM, D), lambda i, lo: (i, 0))
    vec = pl.BlockSpec((1, D), lambda i, lo: (0, 0))
    out_shape = [jax.ShapeDtypeStruct((rows, D), jnp.float32)]
    out_specs = [orow]
    if want_bf16:
        out_shape.append(jax.ShapeDtypeStruct((rows, D), jnp.bfloat16))
        out_specs.append(orow)
    return pl.pallas_call(
        functools.partial(_combine_kernel, nblk_all=nblk_all, blk_off=ob, want_bf16=want_bf16), name="moe_combine_ln",
        grid_spec=pltpu.PrefetchScalarGridSpec(
            num_scalar_prefetch=1, grid=(rows // BM,),
            in_specs=[irow, vec, vec, pl.BlockSpec(memory_space=pl.ANY)], out_specs=out_specs,
            scratch_shapes=[pltpu.VMEM((kmax, DY), jnp.bfloat16), pltpu.VMEM((BM, D), jnp.float32),
                            pltpu.SemaphoreType.DMA(())]),
        out_shape=out_shape,
        compiler_params=_cparams("arbitrary"),
    )(lo, x, g.reshape(1, D), b.reshape(1, D), ye)


def _expert_choice(xb, aff, seqs, w1, w3, w2):
    n, D = xb.shape
    aff_t = aff[:, :N_EXPERTS].T
    gates, idxs = [], []
    start = 0
    for num, T in seqs:
        m = num * T
        cap = max(1, CAPACITY_FACTOR * m // N_EXPERTS)
        gate, idx = lax.top_k(aff_t[:, start:start + m], cap)
        gates.append(gate)
        idxs.append(idx + start)
        start += m
    idx, gate = lax.sort((jnp.concatenate(idxs, axis=1), jnp.concatenate(gates, axis=1)), dimension=1, num_keys=1)
    bounds = jnp.arange(n // COMB_BM + 1, dtype=jnp.int32) * COMB_BM
    lo = jax.vmap(lambda row: jnp.searchsorted(row, bounds, side="left"))(idx).astype(jnp.int32).reshape(-1)
    ye = _expert_ffn(xb[idx], w1, w3, w2, gate[..., None], idx[..., None])
    return ye, lo


def _pack_w_in(w_in):
    sizes = (GLA_HEADS * GLA_DK, GLA_HEADS * GLA_DK, GLA_HEADS * GLA_DV, GLA_HEADS * GLA_DV, 2 * GLA_RANK,
             SWA_NH * SWA_DH, SWA_NH * SWA_DH, SWA_NH * SWA_DH, FNET_GROUPS * FNET_DG,
             RET_HEADS * RET_DH, RET_HEADS * RET_DH, RET_HEADS * RET_DH, RET_HEADS * RET_DH)
    offs = np.concatenate([[0], np.cumsum(sizes)])
    part = lambda i: w_in[..., offs[i]:offs[i + 1]]
    a_q, a_k = part(0), part(1)
    cols = []
    for hh in range(GLA_HEADS):
        cols += [a_q[..., hh * GLA_DK:(hh + 1) * GLA_DK], a_k[..., hh * GLA_DK:(hh + 1) * GLA_DK]]
    cols += [part(2), part(3), part(4),
             jnp.zeros(w_in.shape[:2] + (C_OFF - LR_OFF - 2 * GLA_RANK,), w_in.dtype)]
    cols += [part(i) for i in range(8, 13)]
    out = jnp.concatenate(cols, axis=-1).astype(jnp.bfloat16)
    assert out.shape[-1] == IN_COLS_P
    GW = SWA_HEADS * SWA_DH
    swa = [jnp.concatenate([part(i)[..., g * GW:(g + 1) * GW] for i in (5, 6, 7)], axis=-1).astype(jnp.bfloat16)
           for g in range(len(SWA_GROUPS))]
    return out, swa


def _block_tables(seqs, blk):
    first, last, pos = [], [], []
    for num, T in seqs:
        per = T // blk
        assert per * blk == T
        for _ in range(num):
            first += [1] + [0] * (per - 1)
            last += [0] * (per - 1) + [1]
            pos += list(range(per))
    mk = lambda a: jnp.asarray(np.array(a, np.int32))
    return mk(first), mk(last), mk(pos)


def kernel(x_prompt, x_sample, ln_in_g, ln_in_b, w_in, gla_w_up, gla_b_up, ret_decay, w_gate, b_gate, w_branch,
           w_out, ln1_g, ln1_b, w_router, w_e1, w_e3, w_e2, ln2_g, ln2_b):
    bf16 = jnp.bfloat16
    D = D_MODEL
    seqs = (x_prompt.shape[:2], x_sample.shape[:2])
    n_p, n_s = x_prompt.shape[0] * x_prompt.shape[1], x_sample.shape[0] * x_sample.shape[1]
    n = n_p + n_s
    t_max = max(T for _, T in seqs)
    seq_first, seq_last, pos_blk = _block_tables(seqs, SEQ_BLOCK)

    w_in_p, w_swa = _pack_w_in(w_in)
    wg, wb, wo = w_gate.astype(bf16), w_branch.astype(bf16), w_out.astype(bf16)
    wr = jnp.pad(w_router, ((0, 0), (0, 0), (0, LANE - N_EXPERTS))).astype(bf16)
    we1, we3, we2 = w_e1.astype(bf16), w_e3.astype(bf16), w_e2.astype(bf16)

    xf, xb = _ln_in(x_prompt.reshape(n_p, D), x_sample.reshape(n_s, D), ln_in_g, ln_in_b)
    for l in range(DEPTH):
        h = _matmul(xb, w_in_p[l], bf16, bm=1024, bn=IN_BN, name="in_proj")
        wup, bup = _pack_gla_up(gla_w_up[l], gla_b_up[l])
        ys = (_gla(h, seq_first, seq_last, wup, bup),
              _swa(xb, [w[l] for w in w_swa], seqs),
              _fnet(h, seqs),
              _retention(h, seq_first, seq_last, pos_blk, ret_decay[l], t_max))
        merged = _merge(xb, wg[l], b_gate[l], ys, wb[l])
        x1f, x1b, aff = _outproj(merged, wo[l], xf, ln1_g[l], ln1_b[l], wr[l])
        ye, lo = _expert_choice(x1b, aff, seqs, we1[l], we3[l], we2[l])
        if l + 1 < DEPTH:
            xf, xb = _combine_ln(x1f, ye, lo, ln2_g[l], ln2_b[l], 0, n, True)
        else:
            y_p, = _combine_ln(x1f, ye, lo, ln2_g[l], ln2_b[l], 0, n_p, False)
            y_s, = _combine_ln(x1f, ye, lo, ln2_g[l], ln2_b[l], n_p, n_s, False)
    return (y_p.reshape(x_prompt.shape), y_s.reshape(x_sample.shape))
```

```python
import functools

import jax
import jax.numpy as jnp
import numpy as np
from jax import lax
from jax.experimental import pallas as pl
from jax.experimental.pallas import tpu as pltpu

D_MODEL = 2048
DEPTH = 2
GLA_HEADS, GLA_DK, GLA_DV, GLA_RANK, GLA_TAU, GLA_CHUNK = 4, 64, 128, 16, 16.0, 64
SWA_GROUPS = ((128, 1), (512, 4), (2048, 16))
SWA_HEADS, SWA_DH = 4, 128
SWA_NH = SWA_HEADS * len(SWA_GROUPS)
SWA_BLOCK = 64
ROPE_THETA, ROPE_DIM = 500000.0, SWA_DH // 4
FNET_GROUPS, FNET_DG = 4, 128
RET_HEADS, RET_DH, RET_THETA = 4, 128, 10000.0
N_BRANCH, BRANCH_W = 4, 512
N_EXPERTS, EXPERT_FF, CAPACITY_FACTOR = 16, 2048, 2
DN_ALPHA = (2 * DEPTH) ** 0.25
LN_EPS = 1e-5

VMEM_LIMIT_BYTES = 48 * 1024 * 1024
LANE = 128

QK_OFF = 0
AV_OFF = QK_OFF + GLA_HEADS * LANE
AG_OFF = AV_OFF + GLA_HEADS * GLA_DV
LR_OFF = AG_OFF + GLA_HEADS * GLA_DV
C_OFF = LR_OFF + 4 * LANE
DQ_OFF = C_OFF + FNET_GROUPS * FNET_DG
DK_OFF = DQ_OFF + RET_HEADS * RET_DH
DV_OFF = DK_OFF + RET_HEADS * RET_DH
DG_OFF = DV_OFF + RET_HEADS * RET_DH
IN_USED = DG_OFF + RET_HEADS * RET_DH
IN_BN = 768
IN_COLS_P = -(-IN_USED // IN_BN) * IN_BN
assert IN_COLS_P == IN_USED

SEQ_BLOCK = 512


def _cparams(*sem):
    return pltpu.CompilerParams(dimension_semantics=sem, vmem_limit_bytes=VMEM_LIMIT_BYTES)


def _mm_kernel(a_ref, b_ref, o_ref):
    o_ref[...] = jnp.dot(a_ref[...], b_ref[...], preferred_element_type=jnp.float32).astype(o_ref.dtype)


def _matmul(a, b, out_dtype, bm=1024, bn=512, a_col_blk=0, name="matmul"):
    batched = a.ndim == 3
    M = a.shape[-2]
    K, N = b.shape[-2:]
    bm = min(bm, M)
    bn = min(bn, N)
    assert M % bm == 0 and N % bn == 0, (M, N, bm, bn)
    if batched:
        E = a.shape[0]
        grid = (E, M // bm, N // bn)
        in_specs = [pl.BlockSpec((None, bm, K), lambda e, i, j: (e, i, 0)),
                    pl.BlockSpec((None, K, bn), lambda e, i, j: (e, 0, j))]
        out_specs = pl.BlockSpec((None, bm, bn), lambda e, i, j: (e, i, j))
        out_shape = jax.ShapeDtypeStruct((E, M, N), out_dtype)
        sem = ("parallel", "parallel", "arbitrary")
    else:
        grid = (M // bm, N // bn)
        in_specs = [pl.BlockSpec((bm, K), lambda i, j: (i, a_col_blk)),
                    pl.BlockSpec((K, bn), lambda i, j: (0, j))]
        out_specs = pl.BlockSpec((bm, bn), lambda i, j: (i, j))
        out_shape = jax.ShapeDtypeStruct((M, N), out_dtype)
        sem = ("parallel", "arbitrary")
    return pl.pallas_call(
        _mm_kernel, name=name, grid=grid, in_specs=in_specs, out_specs=out_specs, out_shape=out_shape,
        compiler_params=_cparams(*sem),
    )(a, b)


def _mm_w32_kernel(a_ref, w_ref, o_ref, wb_ref):
    @pl.when(pl.program_id(1) == 0)
    def _():
        wb_ref[...] = w_ref[...].astype(wb_ref.dtype)

    o_ref[...] = jnp.dot(a_ref[...], wb_ref[...], preferred_element_type=jnp.float32).astype(o_ref.dtype)


def _matmul_w32(a, w, out_dtype, bm=1024, bn=768, name="matmul_w32"):
    M, K = a.shape
    N = w.shape[1]
    assert M % bm == 0 and N % bn == 0, (M, N, bm, bn)
    return pl.pallas_call(
        _mm_w32_kernel, name=name, grid=(N // bn, M // bm),
        in_specs=[pl.BlockSpec((bm, K), lambda j, i: (i, 0)), pl.BlockSpec((K, bn), lambda j, i: (0, j))],
        out_specs=pl.BlockSpec((bm, bn), lambda j, i: (i, j)),
        out_shape=jax.ShapeDtypeStruct((M, N), out_dtype),
        scratch_shapes=[pltpu.VMEM((K, bn), jnp.bfloat16)],
        compiler_params=_cparams("arbitrary", "arbitrary"),
    )(a, w)


def _ln_rows(x, g, b):
    mu = jnp.mean(x, -1, keepdims=True)
    xc = x - mu
    var = jnp.mean(xc * xc, -1, keepdims=True)
    return xc * lax.rsqrt(var + LN_EPS) * g + b


def _ln_in_kernel(xa_ref, xc_ref, g_ref, b_ref, xf_ref, xb_ref, *, na_blocks):
    i = pl.program_id(0)

    def emit(x_ref):
        y = _ln_rows(x_ref[...], g_ref[...], b_ref[...])
        xf_ref[...] = y
        xb_ref[...] = y.astype(jnp.bfloat16)

    pl.when(i < na_blocks)(lambda: emit(xa_ref))
    pl.when(i >= na_blocks)(lambda: emit(xc_ref))


def _ln_in(xa, xc, g, b, bm=512):
    (na, D), nc = xa.shape, xc.shape[0]
    nab, ncb = na // bm, nc // bm
    row = pl.BlockSpec((bm, D), lambda i: (i, 0))
    vec = pl.BlockSpec((1, D), lambda i: (0, 0))
    return pl.pallas_call(
        functools.partial(_ln_in_kernel, na_blocks=nab), name="ln_in", grid=(nab + ncb,),
        in_specs=[pl.BlockSpec((bm, D), lambda i: (jnp.minimum(i, nab - 1), 0)),
                  pl.BlockSpec((bm, D), lambda i: (jnp.maximum(i - nab, 0), 0)), vec, vec],
        out_specs=[row, row],
        out_shape=[jax.ShapeDtypeStruct((na + nc, D), jnp.float32), jax.ShapeDtypeStruct((na + nc, D), jnp.bfloat16)],
        compiler_params=_cparams("arbitrary"))(xa, xc, g.reshape(1, D), b.reshape(1, D))


def _merge_kernel(xb_ref, wg_ref, bg_ref, ya_ref, yb_ref, yc_ref, yd_ref, wb_ref, o_ref, wgb_ref, wbb_ref):
    @pl.when(pl.program_id(1) == 0)
    def _():
        wgb_ref[...] = wg_ref[...].astype(wgb_ref.dtype)
        wbb_ref[...] = wb_ref[...].astype(wbb_ref.dtype)

    xb = xb_ref[...]
    acc = None
    for i, y_ref in enumerate((ya_ref, yb_ref, yc_ref, yd_ref)):
        gate = jax.nn.sigmoid(jnp.dot(xb, wgb_ref[i], preferred_element_type=jnp.float32) + bg_ref[i])
        term = gate * jnp.dot(y_ref[...], wbb_ref[i], preferred_element_type=jnp.float32)
        acc = term if acc is None else acc + term
    o_ref[...] = acc.astype(o_ref.dtype)


def _merge(xb, wg, bg, ys, wb, bm=1024, bn=256):
    n, D = xb.shape
    W = ys[0].shape[1]
    y_spec = pl.BlockSpec((bm, W), lambda j, i: (i, 0))
    return pl.pallas_call(
        _merge_kernel, name="branch_merge", grid=(D // bn, n // bm),
        in_specs=[pl.BlockSpec((bm, D), lambda j, i: (i, 0)),
                  pl.BlockSpec((N_BRANCH, D, bn), lambda j, i: (0, 0, j)),
                  pl.BlockSpec((N_BRANCH, 1, bn), lambda j, i: (0, 0, j)),
                  y_spec, y_spec, y_spec, y_spec,
                  pl.BlockSpec((N_BRANCH, W, bn), lambda j, i: (0, 0, j))],
        out_specs=pl.BlockSpec((bm, bn), lambda j, i: (i, j)),
        out_shape=jax.ShapeDtypeStruct((n, D), jnp.bfloat16),
        scratch_shapes=[pltpu.VMEM((N_BRANCH, D, bn), jnp.bfloat16), pltpu.VMEM((N_BRANCH, W, bn), jnp.bfloat16)],
        compiler_params=_cparams("arbitrary", "arbitrary"),
    )(xb, wg, bg.reshape(N_BRANCH, 1, D), *ys, wb)


def _outproj_kernel(m_ref, w_ref, x_ref, g_ref, b_ref, wr_ref, xf_ref, xb_ref, aff_ref):
    r = jnp.dot(m_ref[...], w_ref[...], preferred_element_type=jnp.float32)
    y = _ln_rows(DN_ALPHA * x_ref[...] + r, g_ref[...], b_ref[...])
    yb = y.astype(jnp.bfloat16)
    xf_ref[...] = y
    xb_ref[...] = yb
    logits = jnp.dot(yb, wr_ref[...], preferred_element_type=jnp.float32)
    lane = lax.broadcasted_iota(jnp.int32, logits.shape, 1)
    logits = jnp.where(lane < N_EXPERTS, logits, -jnp.inf)
    e = jnp.exp(logits - jnp.max(logits, -1, keepdims=True))
    aff_ref[...] = e / jnp.sum(e, -1, keepdims=True)


def _outproj(merged, w_out, x, g, b, w_router_p, bm=512):
    n, D = x.shape
    row = pl.BlockSpec((bm, D), lambda i: (i, 0))
    vec = pl.BlockSpec((1, D), lambda i: (0, 0))
    return pl.pallas_call(
        _outproj_kernel, name="outproj_ln_router", grid=(n // bm,),
        in_specs=[row, pl.BlockSpec((D, D), lambda i: (0, 0)), row, vec, vec,
                  pl.BlockSpec((D, LANE), lambda i: (0, 0))],
        out_specs=[row, row, pl.BlockSpec((bm, LANE), lambda i: (i, 0))],
        out_shape=[jax.ShapeDtypeStruct((n, D), jnp.float32), jax.ShapeDtypeStruct((n, D), jnp.bfloat16),
                   jax.ShapeDtypeStruct((n, LANE), jnp.float32)],
        compiler_params=_cparams("parallel"),
    )(merged, w_out, x, g.reshape(1, D), b.reshape(1, D), w_router_p)


def _log_sigmoid(z):
    return jnp.minimum(z, 0.0) - jnp.log(1.0 + jnp.exp(-jnp.abs(z)))


def _headnorm(o):
    mu = jnp.mean(o, -1, keepdims=True)
    oc = o - mu
    var = jnp.mean(oc * oc, -1, keepdims=True)
    return oc * lax.rsqrt(var + LN_EPS)


def _silu(x):
    return x * jax.nn.sigmoid(x)


def _chunk_cumsum(g, chunk, inclusive):
    row = lax.broadcasted_iota(jnp.int32, g.shape, 0) % chunk
    b = g
    s = 1
    while s < chunk:
        b = b + jnp.where(row >= s, pltpu.roll(b, s, axis=0), 0.0)
        s *= 2
    return b if inclusive else b - g


def _gla_kernel(flag_ref, qk_ref, v_ref, lr_ref, wup_ref, bup_ref, *rest, reverse, nblk):
    if reverse:
        gate_ref, of_ref, o_ref, st_ref = rest
    else:
        o_ref, st_ref = rest
    f32, bf16 = jnp.float32, jnp.bfloat16
    C = GLA_CHUNK
    TB = qk_ref.shape[0]
    nc = TB // C
    s_idx = pl.program_id(1)
    blk = nblk - 1 - s_idx if reverse else s_idx

    @pl.when(flag_ref[blk] == 1)
    def _():
        st_ref[...] = jnp.zeros_like(st_ref)

    qk = qk_ref[...].astype(f32)
    q = qk[:, :GLA_DK] * (GLA_DK ** -0.5)
    k = qk[:, GLA_DK:]
    z = jnp.dot(lr_ref[...], wup_ref[...], preferred_element_type=f32) + bup_ref[...]
    g = _log_sigmoid(z) * (1.0 / GLA_TAU)
    cs = _chunk_cumsum(g, C, inclusive=not reverse)
    cs3 = cs.reshape(nc, C, GLA_DK)
    mid = cs3[:, C // 2:C // 2 + 1, :]
    if reverse:
        tot3 = cs3[:, C - 1:C, :] + g.reshape(nc, C, GLA_DK)[:, C - 1:C, :]
        pq3, pk3 = tot3 - cs3, cs3
        qm3, km3 = mid - cs3, cs3 - mid
    else:
        tot3 = cs3[:, C - 1:C, :]
        pq3, pk3 = cs3, tot3 - cs3
        qm3, km3 = cs3 - mid, mid - cs3
    q3 = q.reshape(nc, C, GLA_DK)
    k3 = k.reshape(nc, C, GLA_DK)
    qe = (q3 * jnp.exp(pq3)).astype(bf16)
    kd = (k3 * jnp.exp(pk3)).astype(bf16)
    qm = (q3 * jnp.exp(qm3)).astype(bf16)
    km = (k3 * jnp.exp(km3)).astype(bf16)
    etot = jnp.exp(tot3)
    v = v_ref[...]
    ri = lax.broadcasted_iota(jnp.int32, (C, C), 0)
    ci = lax.broadcasted_iota(jnp.int32, (C, C), 1)
    keep = (ci > ri) if reverse else (ci <= ri)

    st = st_ref[...]
    outs = [None] * nc
    for c in (range(nc - 1, -1, -1) if reverse else range(nc)):
        vc = v[c * C:(c + 1) * C, :]
        att = lax.dot_general(qm[c], km[c], (((1,), (1,)), ((), ())), preferred_element_type=f32)
        att = jnp.where(keep, att, 0.0).astype(bf16)
        o_c = jnp.dot(att, vc, preferred_element_type=f32)
        o_c = o_c + lax.dot_general(qe[c], st.astype(bf16), (((1,), (1,)), ((), ())), preferred_element_type=f32)
        upd = lax.dot_general(vc, kd[c], (((0,), (0,)), ((), ())), preferred_element_type=f32)
        st = st * etot[c] + upd
        outs[c] = o_c
    st_ref[...] = st
    o = jnp.concatenate(outs, axis=0)
    if reverse:
        o = _headnorm(o + of_ref[...])
        o_ref[...] = (o * _silu(gate_ref[...].astype(f32))).astype(o_ref.dtype)
    else:
        o_ref[...] = o


def _gla(h, seq_first, seq_last, wup, bup):
    n = h.shape[0]
    TB = SEQ_BLOCK
    nblk = n // TB
    H = GLA_HEADS

    def col(off):
        return off // LANE

    def specs(reverse):
        blk = (lambda s: nblk - 1 - s) if reverse else (lambda s: s)
        sp = [pl.BlockSpec((TB, LANE), lambda hh, s, f: (blk(s), col(QK_OFF) + hh)),
              pl.BlockSpec((TB, LANE), lambda hh, s, f: (blk(s), col(AV_OFF) + hh)),
              pl.BlockSpec((TB, LANE), lambda hh, s, f: (blk(s), col(LR_OFF))),
              pl.BlockSpec((None, None, LANE, GLA_DK), lambda hh, s, f: (1 if reverse else 0, hh, 0, 0)),
              pl.BlockSpec((None, None, 1, GLA_DK), lambda hh, s, f: (1 if reverse else 0, hh, 0, 0))]
        out = pl.BlockSpec((TB, LANE), lambda hh, s, f: (blk(s), hh))
        return sp, out, blk

    scratch = [pltpu.VMEM((GLA_DV, GLA_DK), jnp.float32)]
    sp, out, _ = specs(False)
    o_f = pl.pallas_call(
        functools.partial(_gla_kernel, reverse=False, nblk=nblk), name="gla_fwd",
        grid_spec=pltpu.PrefetchScalarGridSpec(num_scalar_prefetch=1, grid=(H, nblk), in_specs=sp, out_specs=out,
                                               scratch_shapes=scratch),
        out_shape=jax.ShapeDtypeStruct((n, H * GLA_DV), jnp.float32),
        compiler_params=_cparams("parallel", "arbitrary"),
    )(seq_first, h, h, h, wup, bup)
    sp, out, blk = specs(True)
    sp = sp + [pl.BlockSpec((TB, LANE), lambda hh, s, f: (blk(s), col(AG_OFF) + hh)),
               pl.BlockSpec((TB, LANE), lambda hh, s, f: (blk(s), hh))]
    return pl.pallas_call(
        functools.partial(_gla_kernel, reverse=True, nblk=nblk), name="gla_bwd",
        grid_spec=pltpu.PrefetchScalarGridSpec(num_scalar_prefetch=1, grid=(H, nblk), in_specs=sp, out_specs=out,
                                               scratch_shapes=scratch),
        out_shape=jax.ShapeDtypeStruct((n, H * GLA_DV), jnp.bfloat16),
        compiler_params=_cparams("parallel", "arbitrary"),
    )(seq_last, h, h, h, wup, bup, h, o_f)


def _pack_gla_up(w_up, b_up):
    H, dk, r = GLA_HEADS, GLA_DK, GLA_RANK
    w = w_up.reshape(2, r, H, dk).transpose(0, 2, 1, 3)
    wp = jnp.zeros((2, H, LANE, dk), jnp.float32)
    wp = wp.at[0, :, 0:r].set(w[0]).at[1, :, r:2 * r].set(w[1])
    return wp.astype(jnp.bfloat16), b_up.reshape(2, H, 1, dk).astype(jnp.float32)


def _rope_tables(pos, dim, theta, width):
    half = dim // 2
    T = pos.shape[0]
    inv = 1.0 / (theta ** (jnp.arange(half, dtype=jnp.float32) / half))
    ang = pos.astype(jnp.float32)[:, None] * inv[None, :]
    c, s = jnp.cos(ang), jnp.sin(ang)
    rest = width - dim
    cos = jnp.concatenate([c, c, jnp.ones((T, rest), jnp.float32)], axis=1)
    sin_lo = jnp.concatenate([-s, jnp.zeros((T, half + rest), jnp.float32)], axis=1)
    sin_hi = jnp.concatenate([jnp.zeros((T, half), jnp.float32), s, jnp.zeros((T, rest), jnp.float32)], axis=1)
    return cos, sin_lo, sin_hi


def _ret_kernel(flag_ref, pos_ref, q_ref, k_ref, v_ref, cos_ref, sin_ref, dec_ref, *rest, reverse, nblk):
    if reverse:
        gate_ref, o1_ref, o_ref, st_ref = rest
    else:
        o_ref, st_ref, dm_ref = rest
    f32, bf16 = jnp.float32, jnp.bfloat16
    TB = q_ref.shape[0]
    s_idx = pl.program_id(1)
    blk = nblk - 1 - s_idx if reverse else s_idx

    dec = dec_ref[...]
    lg = -(jnp.maximum(dec, 0.0) + jnp.log1p(jnp.exp(-jnp.abs(dec))))
    lg_f, lg_b = lg[0, 0:1, 0:1], lg[1, 0:1, 0:1]

    @pl.when(flag_ref[blk] == 1)
    def _():
        st_ref[...] = jnp.zeros_like(st_ref)

    cos, sin = cos_ref[...], sin_ref[...]

    def rot(x):
        return x * cos + pltpu.roll(x, RET_DH // 2, axis=1) * sin

    qr = rot(q_ref[...].astype(f32))
    kr = rot(k_ref[...].astype(f32)) * (RET_DH ** -0.5)
    v = v_ref[...]
    idx = lax.broadcasted_iota(jnp.int32, (TB, 1), 0).astype(f32)
    st = st_ref[...]
    cross = jnp.dot(qr.astype(bf16), st.astype(bf16), preferred_element_type=f32)
    if reverse:
        cross = cross * jnp.exp(lg_b * (TB - idx))
        kz = (kr * jnp.exp(lg_b * idx)).astype(bf16)
        st_ref[...] = st * jnp.exp(lg_b * TB) + lax.dot_general(kz, v, (((0,), (0,)), ((), ())),
                                                                 preferred_element_type=f32)
        o = _headnorm(o1_ref[...] + cross)
        o_ref[...] = (o * _silu(gate_ref[...].astype(f32))).astype(o_ref.dtype)
    else:
        @pl.when(s_idx == 0)
        def _():
            ri = lax.broadcasted_iota(jnp.int32, (TB, TB), 0)
            ci = lax.broadcasted_iota(jnp.int32, (TB, TB), 1)
            d = (ri - ci).astype(f32)
            dm_ref[...] = jnp.where(d >= 0, jnp.exp(lg_f * jnp.maximum(d, 0.0)), jnp.exp(lg_b * jnp.maximum(-d, 0.0)))

        cross = cross * jnp.exp(lg_f * (idx + 1.0))
        kz = (kr * jnp.exp(lg_f * (TB - 1.0 - idx))).astype(bf16)
        st_ref[...] = st * jnp.exp(lg_f * TB) + lax.dot_general(kz, v, (((0,), (0,)), ((), ())),
                                                                 preferred_element_type=f32)
        sc = lax.dot_general(qr.astype(bf16), kr.astype(bf16), (((1,), (1,)), ((), ())), preferred_element_type=f32)
        sc = (sc * dm_ref[...]).astype(bf16)
        o_ref[...] = jnp.dot(sc, v, preferred_element_type=f32) + cross


def _retention(h, seq_first, seq_last, pos_blk, decay, t_max):
    n = h.shape[0]
    TB = SEQ_BLOCK
    nblk = n // TB
    H, dh = RET_HEADS, RET_DH
    cos, sin_lo, sin_hi = _rope_tables(jnp.arange(t_max), dh, RET_THETA, dh)
    sin = sin_lo + sin_hi
    dec = jnp.broadcast_to(decay.astype(jnp.float32).T[:, :, None, None], (H, 2, 8, LANE))

    def col(off):
        return off // LANE

    def specs(reverse):
        blk = (lambda s: nblk - 1 - s) if reverse else (lambda s: s)
        tok = lambda off: pl.BlockSpec((TB, LANE), lambda hh, s, f, p: (blk(s), col(off) + hh))
        tab = pl.BlockSpec((TB, LANE), lambda hh, s, f, p: (p[blk(s)], 0))
        sp = [tok(DQ_OFF), tok(DK_OFF), tok(DV_OFF), tab, tab,
              pl.BlockSpec((None, 2, 8, LANE), lambda hh, s, f, p: (hh, 0, 0, 0))]
        return sp, tok, pl.BlockSpec((TB, LANE), lambda hh, s, f, p: (blk(s), hh))

    sp, _, out = specs(False)
    o1 = pl.pallas_call(
        functools.partial(_ret_kernel, reverse=False, nblk=nblk), name="ret_fwd",
        grid_spec=pltpu.PrefetchScalarGridSpec(
            num_scalar_prefetch=2, grid=(H, nblk), in_specs=sp, out_specs=out,
            scratch_shapes=[pltpu.VMEM((dh, dh), jnp.float32), pltpu.VMEM((TB, TB), jnp.float32)]),
        out_shape=jax.ShapeDtypeStruct((n, H * dh), jnp.float32),
        compiler_params=_cparams("arbitrary", "arbitrary"),
    )(seq_first, pos_blk, h, h, h, cos, sin, dec)
    sp, tok, out = specs(True)
    sp = sp + [tok(DG_OFF), out]
    return pl.pallas_call(
        functools.partial(_ret_kernel, reverse=True, nblk=nblk), name="ret_bwd",
        grid_spec=pltpu.PrefetchScalarGridSpec(
            num_scalar_prefetch=2, grid=(H, nblk), in_specs=sp, out_specs=out,
            scratch_shapes=[pltpu.VMEM((dh, dh), jnp.float32)]),
        out_shape=jax.ShapeDtypeStruct((n, H * dh), jnp.bfloat16),
        compiler_params=_cparams("arbitrary", "arbitrary"),
    )(seq_last, pos_blk, h, h, h, cos, sin, dec, h, o1)


SWA_Q = 128
SWA_HALF = 64
assert all(w // (2 * d) == SWA_HALF for w, d in SWA_GROUPS)
NEG_BIG = -1e30


def _swa_kernel(hl_ref, hr_ref, q_ref, kl_ref, kc_ref, kr_ref, vl_ref, vc_ref, vr_ref, o_ref, lse_ref):
    f32, bf16 = jnp.float32, jnp.bfloat16
    j = pl.program_id(1)
    Q, HW = SWA_Q, SWA_HALF
    W = Q + 2 * HW
    ri = lax.broadcasted_iota(jnp.int32, (Q, W), 0)
    ci = lax.broadcasted_iota(jnp.int32, (Q, W), 1)
    rel = ci - HW - ri
    ok = (rel <= HW) & (rel >= -HW)
    ok = ok & ((ci >= HW) | (hl_ref[j] == 1)) & ((ci < HW + Q) | (hr_ref[j] == 1))
    lane = lax.broadcasted_iota(jnp.int32, (Q, LANE), 1)

    lse_tile = jnp.zeros((Q, LANE), f32)
    for hh in range(SWA_HEADS):
        sl = slice(hh * SWA_DH, (hh + 1) * SWA_DH)
        k = jnp.concatenate([kl_ref[:, sl], kc_ref[:, sl], kr_ref[:, sl]], axis=0)
        v = jnp.concatenate([vl_ref[:, sl], vc_ref[:, sl], vr_ref[:, sl]], axis=0)
        s = lax.dot_general(q_ref[:, sl], k, (((1,), (1,)), ((), ())), preferred_element_type=f32)
        s = jnp.where(ok, s, NEG_BIG)
        m = jnp.max(s, -1, keepdims=True)
        p = jnp.exp(s - m)
        den = jnp.sum(p, -1, keepdims=True)
        o = jnp.dot(p.astype(bf16), v, preferred_element_type=f32) / den
        o_ref[:, sl] = o
        lse_tile = jnp.where(lane == hh, m + jnp.log(den), lse_tile)
    lse_ref[...] = lse_tile


def _swa_merge_kernel(*refs):
    ng = len(SWA_GROUPS)
    o_refs, l_refs, y_ref = refs[:ng], refs[ng:2 * ng], refs[2 * ng]
    scratch = refs[2 * ng + 1:]
    assert SWA_DH == LANE
    os_, ls_ = [], []
    si = 0
    for gi, (_, dil) in enumerate(SWA_GROUPS):
        if dil == 1:
            os_.append(lambda hh, r=o_refs[gi]: r[0, :, hh * LANE:(hh + 1) * LANE])
            ls_.append(l_refs[gi].at[0])
            continue
        o_s, l_s = scratch[si], scratch[si + 1]
        si += 2
        rows = l_s.shape[0] // dil
        for r in range(dil):
            for hh in range(SWA_HEADS):
                o_s[hh, pl.ds(r, rows, stride=dil), :] = o_refs[gi][r, :, hh * LANE:(hh + 1) * LANE]
            l_s[pl.ds(r, rows, stride=dil), :] = l_refs[gi][r]
        os_.append(lambda hh, s=o_s: s[hh])
        ls_.append(l_s)
    for hh in range(SWA_HEADS):
        ls = [r[:, hh:hh + 1] for r in ls_]
        m = functools.reduce(jnp.maximum, ls)
        ws = [jnp.exp(l - m) for l in ls]
        tot = functools.reduce(lambda a, b: a + b, ws)
        y = functools.reduce(lambda a, b: a + b, [w * o(hh) for w, o in zip(ws, os_)])
        y_ref[:, hh * LANE:(hh + 1) * LANE] = (y / tot).astype(y_ref.dtype)


SWA_PROJ_BM = 1024


def _swa_proj_kernel(pos_ref, a_ref, w_ref, cos_ref, s1_ref, s2_ref, o_ref, wb_ref, acc_ref, *, dil):
    del pos_ref

    @pl.when(pl.program_id(1) == 0)
    def _():
        wb_ref[...] = w_ref[...].astype(wb_ref.dtype)

    acc = jnp.dot(a_ref[...], wb_ref[...], preferred_element_type=jnp.float32)
    cos, s1, s2 = cos_ref[...], s1_ref[...], s2_ref[...]
    rows = acc.shape[0] // dil
    for s in range(SWA_HEADS):
        sl = slice(s * SWA_DH, (s + 1) * SWA_DH)
        x = acc[:, sl]
        x = x * cos + pltpu.roll(x, SWA_DH - ROPE_DIM // 2, axis=1) * s1 + pltpu.roll(x, ROPE_DIM // 2, axis=1) * s2
        if dil == 1:
            o_ref[0, :, sl] = x.astype(o_ref.dtype)
        else:
            acc_ref[s] = x
            for r in range(dil):
                o_ref[r, :, sl] = acc_ref[s, pl.ds(r, rows, stride=dil), :].astype(o_ref.dtype)


def _swa_proj(xb, w, tables, pos_blk, dil):
    n, D = xb.shape
    GW = SWA_HEADS * SWA_DH
    bm = SWA_PROJ_BM
    tab = pl.BlockSpec((None, bm, SWA_DH), lambda j, i, p: (j, p[i], 0))
    return pl.pallas_call(
        functools.partial(_swa_proj_kernel, dil=dil), name=f"swa_proj_dil{dil}",
        grid_spec=pltpu.PrefetchScalarGridSpec(
            num_scalar_prefetch=1, grid=(3, n // bm),
            in_specs=[pl.BlockSpec((bm, D), lambda j, i, p: (i, 0)), pl.BlockSpec((D, GW), lambda j, i, p: (0, j)),
                      tab, tab, tab],
            out_specs=pl.BlockSpec((dil, bm // dil, GW), lambda j, i, p: (0, i, j)),
            scratch_shapes=[pltpu.VMEM((D, GW), jnp.bfloat16), pltpu.VMEM((SWA_HEADS, bm, SWA_DH), jnp.float32)]),
        out_shape=jax.ShapeDtypeStruct((dil, n // dil, 3 * GW), jnp.bfloat16),
        compiler_params=_cparams("arbitrary", "arbitrary"),
    )(pos_blk, xb, w, *tables)


def _swa_rope_tables(t_max):
    cos, s1, s2 = _rope_tables(jnp.arange(t_max), ROPE_DIM, ROPE_THETA, SWA_DH)
    sc = SWA_DH ** -0.5
    one, zero = jnp.ones_like(cos), jnp.zeros_like(cos)
    return (jnp.stack([cos * sc, cos, one]), jnp.stack([s1 * sc, s1, zero]), jnp.stack([s2 * sc, s2, zero]))


def _swa(xb, w_groups, seqs):
    n = xb.shape[0]
    Q, HW = SWA_Q, SWA_HALF
    GW = SWA_HEADS * SWA_DH
    tables = _swa_rope_tables(max(T for _, T in seqs))
    _, _, pos_blk = _block_tables(seqs, SWA_PROJ_BM)
    outs, lses = [], []
    for gi, (window, dil) in enumerate(SWA_GROUPS):
        hs = _swa_proj(xb, w_groups[gi], tables, pos_blk, dil)
        rows = n // dil
        nblk = rows // Q
        has_l, has_r = [], []
        for num, T in seqs:
            per = T // dil // Q
            assert per * Q * dil == T
            has_l += ([0] + [1] * (per - 1)) * num
            has_r += ([1] * (per - 1) + [0]) * num
        has_l = jnp.asarray(np.array(has_l, np.int32))
        has_r = jnp.asarray(np.array(has_r, np.int32))
        last64 = rows // HW - 1

        def center(c):
            return pl.BlockSpec((None, Q, GW), lambda r, j, hl, hr: (r, j, c))

        def left(c):
            return pl.BlockSpec((None, HW, GW), lambda r, j, hl, hr: (r, jnp.maximum(2 * j - 1, 0), c))

        def right(c):
            return pl.BlockSpec((None, HW, GW), lambda r, j, hl, hr: (r, jnp.minimum(2 * j + 2, last64), c))

        o_g, lse_g = pl.pallas_call(
            _swa_kernel, name=f"swa_dil{dil}",
            grid_spec=pltpu.PrefetchScalarGridSpec(
                num_scalar_prefetch=2, grid=(dil, nblk),
                in_specs=[center(0), left(1), center(1), right(1), left(2), center(2), right(2)],
                out_specs=[pl.BlockSpec((None, Q, GW), lambda r, j, hl, hr: (r, j, 0)),
                           pl.BlockSpec((None, Q, LANE), lambda r, j, hl, hr: (r, j, 0))]),
            out_shape=[jax.ShapeDtypeStruct((dil, rows, GW), jnp.float32),
                       jax.ShapeDtypeStruct((dil, rows, LANE), jnp.float32)],
            compiler_params=_cparams("parallel", "parallel"),
        )(has_l, has_r, hs, hs, hs, hs, hs, hs, hs)
        outs.append(o_g)
        lses.append(lse_g)
    bm = 1024
    dils = [d for _, d in SWA_GROUPS]
    osp = [pl.BlockSpec((d, bm // d, GW), lambda i: (0, i, 0)) for d in dils]
    lsp = [pl.BlockSpec((d, bm // d, LANE), lambda i: (0, i, 0)) for d in dils]
    scratch = []
    for d in dils:
        if d > 1:
            scratch += [pltpu.VMEM((SWA_HEADS, bm, LANE), jnp.float32), pltpu.VMEM((bm, LANE), jnp.float32)]
    return pl.pallas_call(
        _swa_merge_kernel, name="swa_merge", grid=(n // bm,), in_specs=osp + lsp,
        out_specs=pl.BlockSpec((bm, GW), lambda i: (i, 0)),
        out_shape=jax.ShapeDtypeStruct((n, GW), jnp.bfloat16), scratch_shapes=scratch,
        compiler_params=_cparams("parallel"),
    )(*outs, *lses)


FFT_T2 = 128
FW = FNET_GROUPS * FNET_DG


def _dft_cos_sin(n_out, n_in, period):
    ang = (2.0 * np.pi / period) * np.mod(np.outer(np.arange(n_out), np.arange(n_in)), period)
    return np.cos(ang), np.sin(ang)


def _fnet_stage1_kernel(z_ref, m_ref, a_ref, *, cols):
    mat = m_ref[...]
    for c in range(cols):
        zr = z_ref[:, c * 2 * FW:c * 2 * FW + FW]
        zi = z_ref[:, c * 2 * FW + FW:(c + 1) * 2 * FW]
        a = jnp.dot(mat, jnp.concatenate([zr, zi], axis=0), preferred_element_type=jnp.float32)
        t1 = zr.shape[0]
        a_ref[:, c * 2 * FW:c * 2 * FW + FW] = a[:t1].astype(a_ref.dtype)
        a_ref[:, c * 2 * FW + FW:(c + 1) * 2 * FW] = a[t1:].astype(a_ref.dtype)


def _fnet_stage2_kernel(a_ref, g_ref, y_ref, *, scale):
    y = jnp.dot(g_ref[...], a_ref[...], preferred_element_type=jnp.float32)
    y_ref[...] = (y * scale).astype(y_ref.dtype)


def _fnet(h, seqs):
    n = h.shape[0]
    bf16 = jnp.bfloat16
    T2 = FFT_T2
    cc, ss = _dft_cos_sin(FNET_DG, FNET_DG, FNET_DG)
    eye = np.eye(FNET_GROUPS)
    wc = jnp.asarray(np.concatenate([np.kron(eye, cc), -np.kron(eye, ss)], axis=1), bf16)
    z = _matmul(h, wc, bf16, bn=2 * FW, a_col_blk=C_OFF // FW, name="fnet_channel_dft")
    zz = z.reshape(n // T2, T2 * 2 * FW)
    outs = []
    row0 = 0
    for num, T in seqs:
        T1 = T // T2
        assert T1 * T2 == T and T1 % 16 == 0
        c1, s1 = _dft_cos_sin(T1, T1, T1)
        mat = jnp.asarray(np.block([[c1, s1], [-s1, c1]]), bf16)
        cols = min(T2, max(1, 1024 // T1))
        rb0 = row0 // T1
        a = pl.pallas_call(
            functools.partial(_fnet_stage1_kernel, cols=cols), name=f"fnet_dft_t1_{T1}", grid=(num, T2 // cols),
            in_specs=[pl.BlockSpec((T1, cols * 2 * FW), lambda b, j: (rb0 + b, j)),
                      pl.BlockSpec((2 * T1, 2 * T1), lambda b, j: (0, 0))],
            out_specs=pl.BlockSpec((T1, cols * 2 * FW), lambda b, j: (b, j)),
            out_shape=jax.ShapeDtypeStruct((num * T1, T2 * 2 * FW), bf16),
            compiler_params=_cparams("parallel", "parallel"),
        )(zz, mat)
        ec, es = _dft_cos_sin(T, T2, T)
        g = jnp.asarray(np.stack([ec, es], axis=-1).reshape(T2, T1 * 2 * T2), bf16)
        y = pl.pallas_call(
            functools.partial(_fnet_stage2_kernel, scale=float(1.0 / np.sqrt(T * FNET_DG))), name=f"fnet_dft_t2_{T1}",
            grid=(num, T1),
            in_specs=[pl.BlockSpec((None, 2 * T2, FW), lambda b, k1: (b * T1 + k1, 0, 0)),
                      pl.BlockSpec((T2, 2 * T2), lambda b, k1: (0, k1))],
            out_specs=pl.BlockSpec((T2, FW), lambda b, k1: (b, k1)),
            out_shape=jax.ShapeDtypeStruct((num * T2, T1 * FW), bf16),
            compiler_params=_cparams("parallel", "parallel"),
        )(a.reshape(num * T1, 2 * T2, FW), g)
        outs.append(y.reshape(num * T, FW))
        row0 += num * T1
    return jnp.concatenate(outs, axis=0)


TOK_RADIX = 256


def _ffn_up_kernel(x_ref, w1_ref, w3_ref, h_ref, w1b_ref, w3b_ref):
    @pl.when(pl.program_id(2) == 0)
    def _():
        w1b_ref[...] = w1_ref[...].astype(w1b_ref.dtype)
        w3b_ref[...] = w3_ref[...].astype(w3b_ref.dtype)

    x = x_ref[...]
    h1 = jnp.dot(x, w1b_ref[...], preferred_element_type=jnp.float32)
    h3 = jnp.dot(x, w3b_ref[...], preferred_element_type=jnp.float32)
    h_ref[...] = (_silu(h1) * h3).astype(h_ref.dtype)


def _ffn_down_kernel(h_ref, w2_ref, g_ref, t_ref, o_ref, w2b_ref):
    @pl.when(pl.program_id(1) == 0)
    def _():
        w2b_ref[...] = w2_ref[...].astype(w2b_ref.dtype)

    D = w2_ref.shape[1]
    y = jnp.dot(h_ref[...], w2b_ref[...], preferred_element_type=jnp.float32)
    o_ref[:, :D] = (y * g_ref[...]).astype(o_ref.dtype)
    tok = t_ref[...]
    lane = lax.broadcasted_iota(jnp.int32, (tok.shape[0], LANE), 1)
    digits = jnp.where(lane == 0, tok // TOK_RADIX, jnp.where(lane == 1, tok % TOK_RADIX, 0))
    o_ref[:, D:] = digits.astype(jnp.float32).astype(o_ref.dtype)


def _expert_ffn(xe, w1, w3, w2, gate, tok, bm=1024, bf=512):
    E, R, D = xe.shape
    F = w1.shape[-1]
    wspec = pl.BlockSpec((None, D, bf), lambda e, f, i: (e, 0, f))
    hmid = pl.pallas_call(
        _ffn_up_kernel, name="expert_ffn_up", grid=(E, F // bf, R // bm),
        in_specs=[pl.BlockSpec((None, bm, D), lambda e, f, i: (e, i, 0)), wspec, wspec],
        out_specs=pl.BlockSpec((None, bm, bf), lambda e, f, i: (e, i, f)),
        out_shape=jax.ShapeDtypeStruct((E, R, F), jnp.bfloat16),
        scratch_shapes=[pltpu.VMEM((D, bf), jnp.bfloat16), pltpu.VMEM((D, bf), jnp.bfloat16)],
        compiler_params=_cparams("arbitrary", "arbitrary", "arbitrary"),
    )(xe, w1, w3)
    col = pl.BlockSpec((None, bm, 1), lambda e, i: (e, i, 0))
    return pl.pallas_call(
        _ffn_down_kernel, name="expert_ffn_down", grid=(E, R // bm),
        in_specs=[pl.BlockSpec((None, bm, F), lambda e, i: (e, i, 0)),
                  pl.BlockSpec((None, F, D), lambda e, i: (e, 0, 0), pipeline_mode=pl.Buffered(1)),
                  col, col],
        out_specs=pl.BlockSpec((None, bm, D + LANE), lambda e, i: (e, i, 0)),
        out_shape=jax.ShapeDtypeStruct((E, R, D + LANE), jnp.bfloat16),
        scratch_shapes=[pltpu.VMEM((F, D), jnp.bfloat16)],
        compiler_params=_cparams("arbitrary", "arbitrary"),
    )(hmid, w2, gate, tok)


COMB_BM = 128
COMB_G = 16
COMB_K = 256


def _combine_kernel(lo_ref, x_ref, g_ref, b_ref, ye_hbm, *rest, nblk_all, blk_off, want_bf16):
    if want_bf16:
        xf_ref, xb_ref, buf_ref, acc_ref, sem = rest
    else:
        xf_ref, buf_ref, acc_ref, sem = rest
    f32, bf16 = jnp.float32, jnp.bfloat16
    E = ye_hbm.shape[0]
    D = x_ref.shape[1]
    BM, G, KC = COMB_BM, COMB_G, COMB_K
    i = pl.program_id(0)
    nsteps = pl.num_programs(0)
    blk = i + blk_off
    slot = i % 2

    @pl.when(i == 0)
    def _():
        buf_ref[...] = jnp.zeros_like(buf_ref)

    def granule_copy(sl, e, src_row, dst_row):
        return pltpu.make_async_copy(ye_hbm.at[e, pl.ds(src_row, G)], buf_ref.at[sl, pl.ds(dst_row, G)], sem.at[sl])

    def gather_block(b, sl, start):
        off = jnp.int32(0)
        for e in range(E):
            lo = lo_ref[e * (nblk_all + 1) + b]
            hi = lo_ref[e * (nblk_all + 1) + b + 1]
            lo_al = (lo // G) * G
            ng = jnp.where(hi > lo, (hi - lo_al + G - 1) // G, 0)
            if start:
                def issue(g, carry, e=e, lo_al=lo_al, off=off):
                    granule_copy(sl, e, pl.multiple_of(lo_al + g * G, G), pl.multiple_of(off + g * G, G)).start()
                    return carry

                lax.fori_loop(0, ng, issue, 0)
            off = off + ng * G
        return off

    @pl.when(i == 0)
    def _():
        gather_block(blk, slot, True)

    @pl.when(i + 1 < nsteps)
    def _():
        gather_block(blk + 1, 1 - slot, True)

    off = gather_block(blk, slot, False)

    def wait_one(g, carry):
        granule_copy(slot, 0, 0, 0).wait()
        return carry

    lax.fori_loop(0, off // G, wait_one, 0)

    acc_ref[...] = jnp.zeros_like(acc_ref)
    t0 = (blk * BM).astype(f32)
    lane_tok = lax.broadcasted_iota(jnp.int32, (KC, BM), 1).astype(f32)
    row = lax.broadcasted_iota(jnp.int32, (KC, 1), 0)

    def chunk(c, carry):
        win = buf_ref[slot, pl.ds(pl.multiple_of(c * KC, KC), KC), :]
        tok = win[:, D:D + 1].astype(f32) * TOK_RADIX + win[:, D + 1:D + 2].astype(f32)
        tok = jnp.where(row + c * KC < off, tok, -1.0)
        onehot_t = (tok - t0 == lane_tok).astype(bf16)
        acc_ref[...] += lax.dot_general(onehot_t, win[:, :D], (((0,), (0,)), ((), ())), preferred_element_type=f32)
        return carry

    lax.fori_loop(0, (off + KC - 1) // KC, chunk, 0)
    y = _ln_rows(DN_ALPHA * x_ref[...] + acc_ref[...], g_ref[...], b_ref[...])
    xf_ref[...] = y
    if want_bf16:
        xb_ref[...] = y.astype(bf16)


def _combine_ln(x, ye, lo, g, b, row_off, rows, want_bf16):
    n, D = x.shape
    E, R, DY = ye.shape
    BM = COMB_BM
    nblk_all = n // BM
    ob = row_off // BM
    kmax = -(-E * (BM + COMB_G) // COMB_K) * COMB_K
    irow = pl.BlockSpec((BM, D), lambda i, lo: (i + ob, 0))
    orow = pl.BlockSpec((BM, D), lambda i, lo: (i, 0))
    vec = pl.BlockSpec((1, D), lambda i, lo: (0, 0))
    out_shape = [jax.ShapeDtypeStruct((rows, D), jnp.float32)]
    out_specs = [orow]
    if want_bf16:
        out_shape.append(jax.ShapeDtypeStruct((rows, D), jnp.bfloat16))
        out_specs.append(orow)
    return pl.pallas_call(
        functools.partial(_combine_kernel, nblk_all=nblk_all, blk_off=ob, want_bf16=want_bf16), name="moe_combine_ln",
        grid_spec=pltpu.PrefetchScalarGridSpec(
            num_scalar_prefetch=1, grid=(rows // BM,),
            in_specs=[irow, vec, vec, pl.BlockSpec(memory_space=pl.ANY)], out_specs=out_specs,
            scratch_shapes=[pltpu.VMEM((2, kmax, DY), jnp.bfloat16), pltpu.VMEM((BM, D), jnp.float32),
                            pltpu.SemaphoreType.DMA((2,))]),
        out_shape=out_shape,
        compiler_params=_cparams("arbitrary"),
    )(lo, x, g.reshape(1, D), b.reshape(1, D), ye)


def _expert_choice(xb, aff, seqs, w1, w3, w2):
    n, D = xb.shape
    aff_t = aff[:, :N_EXPERTS].T
    gates, idxs = [], []
    start = 0
    for num, T in seqs:
        m = num * T
        cap = max(1, CAPACITY_FACTOR * m // N_EXPERTS)
        gate, idx = lax.top_k(aff_t[:, start:start + m], cap)
        gates.append(gate)
        idxs.append(idx + start)
        start += m
    idx, gate = lax.sort((jnp.concatenate(idxs, axis=1), jnp.concatenate(gates, axis=1)), dimension=1, num_keys=1)
    bounds = jnp.arange(n // COMB_BM + 1, dtype=jnp.int32) * COMB_BM
    lo = jax.vmap(lambda row: jnp.searchsorted(row, bounds, side="left"))(idx).astype(jnp.int32).reshape(-1)
    ye = _expert_ffn(xb[idx], w1, w3, w2, gate[..., None], idx[..., None])
    return ye, lo


def _pack_w_in(w_in):
    sizes = (GLA_HEADS * GLA_DK, GLA_HEADS * GLA_DK, GLA_HEADS * GLA_DV, GLA_HEADS * GLA_DV, 2 * GLA_RANK,
             SWA_NH * SWA_DH, SWA_NH * SWA_DH, SWA_NH * SWA_DH, FNET_GROUPS * FNET_DG,
             RET_HEADS * RET_DH, RET_HEADS * RET_DH, RET_HEADS * RET_DH, RET_HEADS * RET_DH)
    offs = np.concatenate([[0], np.cumsum(sizes)])
    part = lambda i: w_in[..., offs[i]:offs[i + 1]]
    a_q, a_k = part(0), part(1)
    cols = []
    for hh in range(GLA_HEADS):
        cols += [a_q[..., hh * GLA_DK:(hh + 1) * GLA_DK], a_k[..., hh * GLA_DK:(hh + 1) * GLA_DK]]
    cols += [part(2), part(3), part(4),
             jnp.zeros(w_in.shape[:2] + (C_OFF - LR_OFF - 2 * GLA_RANK,), w_in.dtype)]
    cols += [part(i) for i in range(8, 13)]
    out = jnp.concatenate(cols, axis=-1)
    assert out.shape[-1] == IN_COLS_P
    GW = SWA_HEADS * SWA_DH
    swa = [jnp.concatenate([part(i)[..., g * GW:(g + 1) * GW] for i in (5, 6, 7)], axis=-1)
           for g in range(len(SWA_GROUPS))]
    return out, swa


def _block_tables(seqs, blk):
    first, last, pos = [], [], []
    for num, T in seqs:
        per = T // blk
        assert per * blk == T
        for _ in range(num):
            first += [1] + [0] * (per - 1)
            last += [0] * (per - 1) + [1]
            pos += list(range(per))
    mk = lambda a: jnp.asarray(np.array(a, np.int32))
    return mk(first), mk(last), mk(pos)


def kernel(x_prompt, x_sample, ln_in_g, ln_in_b, w_in, gla_w_up, gla_b_up, ret_decay, w_gate, b_gate, w_branch,
           w_out, ln1_g, ln1_b, w_router, w_e1, w_e3, w_e2, ln2_g, ln2_b):
    bf16 = jnp.bfloat16
    D = D_MODEL
    seqs = (x_prompt.shape[:2], x_sample.shape[:2])
    n_p, n_s = x_prompt.shape[0] * x_prompt.shape[1], x_sample.shape[0] * x_sample.shape[1]
    n = n_p + n_s
    t_max = max(T for _, T in seqs)
    seq_first, seq_last, pos_blk = _block_tables(seqs, SEQ_BLOCK)

    w_in_p, w_swa = _pack_w_in(w_in)
    wo = w_out.astype(bf16)
    wr = jnp.pad(w_router, ((0, 0), (0, 0), (0, LANE - N_EXPERTS))).astype(bf16)
    wg, wb, we1, we3, we2 = w_gate, w_branch, w_e1, w_e3, w_e2

    xf, xb = _ln_in(x_prompt.reshape(n_p, D), x_sample.reshape(n_s, D), ln_in_g, ln_in_b)
    for l in range(DEPTH):
        h = _matmul_w32(xb, w_in_p[l], bf16, bm=1024, bn=IN_BN, name="in_proj")
        wup, bup = _pack_gla_up(gla_w_up[l], gla_b_up[l])
        ys = (_gla(h, seq_first, seq_last, wup, bup),
              _swa(xb, [w[l] for w in w_swa], seqs),
              _fnet(h, seqs),
              _retention(h, seq_first, seq_last, pos_blk, ret_decay[l], t_max))
        merged = _merge(xb, wg[l], b_gate[l], ys, wb[l])
        x1f, x1b, aff = _outproj(merged, wo[l], xf, ln1_g[l], ln1_b[l], wr[l])
        ye, lo = _expert_choice(x1b, aff, seqs, we1[l], we3[l], we2[l])
        if l + 1 < DEPTH:
            xf, xb = _combine_ln(x1f, ye, lo, ln2_g[l], ln2_b[l], 0, n, True)
        else:
            y_p, = _combine_ln(x1f, ye, lo, ln2_g[l], ln2_b[l], 0, n_p, False)
            y_s, = _combine_ln(x1f, ye, lo, ln2_g[l], ln2_b[l], n_p, n_s, False)
    return (y_p.reshape(x_prompt.shape), y_s.reshape(x_sample.shape))
```

```python
import functools

import jax
import jax.numpy as jnp
import numpy as np
from jax import lax
from jax.experimental import pallas as pl
from jax.experimental.pallas import tpu as pltpu

D_MODEL = 2048
DEPTH = 2
GLA_HEADS, GLA_DK, GLA_DV, GLA_RANK, GLA_TAU, GLA_CHUNK = 4, 64, 128, 16, 16.0, 64
SWA_GROUPS = ((128, 1), (512, 4), (2048, 16))
SWA_HEADS, SWA_DH = 4, 128
SWA_NH = SWA_HEADS * len(SWA_GROUPS)
SWA_BLOCK = 64
ROPE_THETA, ROPE_DIM = 500000.0, SWA_DH // 4
FNET_GROUPS, FNET_DG = 4, 128
RET_HEADS, RET_DH, RET_THETA = 4, 128, 10000.0
N_BRANCH, BRANCH_W = 4, 512
N_EXPERTS, EXPERT_FF, CAPACITY_FACTOR = 16, 2048, 2
DN_ALPHA = (2 * DEPTH) ** 0.25
LN_EPS = 1e-5

VMEM_LIMIT_BYTES = 48 * 1024 * 1024
LANE = 128

QK_OFF = 0
AV_OFF = QK_OFF + GLA_HEADS * LANE
AG_OFF = AV_OFF + GLA_HEADS * GLA_DV
LR_OFF = AG_OFF + GLA_HEADS * GLA_DV
C_OFF = LR_OFF + 4 * LANE
DQ_OFF = C_OFF + FNET_GROUPS * FNET_DG
DK_OFF = DQ_OFF + RET_HEADS * RET_DH
DV_OFF = DK_OFF + RET_HEADS * RET_DH
DG_OFF = DV_OFF + RET_HEADS * RET_DH
IN_USED = DG_OFF + RET_HEADS * RET_DH
IN_BN = 1536
IN_COLS_P = -(-IN_USED // IN_BN) * IN_BN
assert IN_COLS_P == IN_USED

SEQ_BLOCK = 512


def _cparams(*sem):
    return pltpu.CompilerParams(dimension_semantics=sem, vmem_limit_bytes=VMEM_LIMIT_BYTES)


def _mm_kernel(a_ref, b_ref, o_ref):
    o_ref[...] = jnp.dot(a_ref[...], b_ref[...], preferred_element_type=jnp.float32).astype(o_ref.dtype)


def _matmul(a, b, out_dtype, bm=1024, bn=512, a_col_blk=0, name="matmul"):
    batched = a.ndim == 3
    M = a.shape[-2]
    K, N = b.shape[-2:]
    bm = min(bm, M)
    bn = min(bn, N)
    assert M % bm == 0 and N % bn == 0, (M, N, bm, bn)
    if batched:
        E = a.shape[0]
        grid = (E, M // bm, N // bn)
        in_specs = [pl.BlockSpec((None, bm, K), lambda e, i, j: (e, i, 0)),
                    pl.BlockSpec((None, K, bn), lambda e, i, j: (e, 0, j))]
        out_specs = pl.BlockSpec((None, bm, bn), lambda e, i, j: (e, i, j))
        out_shape = jax.ShapeDtypeStruct((E, M, N), out_dtype)
        sem = ("parallel", "parallel", "arbitrary")
    else:
        grid = (M // bm, N // bn)
        in_specs = [pl.BlockSpec((bm, K), lambda i, j: (i, a_col_blk)),
                    pl.BlockSpec((K, bn), lambda i, j: (0, j))]
        out_specs = pl.BlockSpec((bm, bn), lambda i, j: (i, j))
        out_shape = jax.ShapeDtypeStruct((M, N), out_dtype)
        sem = ("parallel", "arbitrary")
    return pl.pallas_call(
        _mm_kernel, name=name, grid=grid, in_specs=in_specs, out_specs=out_specs, out_shape=out_shape,
        compiler_params=_cparams(*sem),
    )(a, b)


def _mm_w32_kernel(a_ref, w_ref, o_ref, wb_ref):
    @pl.when(pl.program_id(1) == 0)
    def _():
        wb_ref[...] = w_ref[...].astype(wb_ref.dtype)

    o_ref[...] = jnp.dot(a_ref[...], wb_ref[...], preferred_element_type=jnp.float32).astype(o_ref.dtype)


def _matmul_w32(a, w, out_dtype, bm=1024, bn=768, name="matmul_w32"):
    M, K = a.shape
    N = w.shape[1]
    assert M % bm == 0 and N % bn == 0, (M, N, bm, bn)
    return pl.pallas_call(
        _mm_w32_kernel, name=name, grid=(N // bn, M // bm),
        in_specs=[pl.BlockSpec((bm, K), lambda j, i: (i, 0)), pl.BlockSpec((K, bn), lambda j, i: (0, j))],
        out_specs=pl.BlockSpec((bm, bn), lambda j, i: (i, j)),
        out_shape=jax.ShapeDtypeStruct((M, N), out_dtype),
        scratch_shapes=[pltpu.VMEM((K, bn), jnp.bfloat16)],
        compiler_params=_cparams("arbitrary", "arbitrary"),
    )(a, w)


def _ln_rows(x, g, b):
    mu = jnp.mean(x, -1, keepdims=True)
    xc = x - mu
    var = jnp.mean(xc * xc, -1, keepdims=True)
    return xc * lax.rsqrt(var + LN_EPS) * g + b


def _ln_in_kernel(xa_ref, xc_ref, g_ref, b_ref, xf_ref, xb_ref, *, na_blocks):
    i = pl.program_id(0)

    def emit(x_ref):
        y = _ln_rows(x_ref[...], g_ref[...], b_ref[...])
        xf_ref[...] = y
        xb_ref[...] = y.astype(jnp.bfloat16)

    pl.when(i < na_blocks)(lambda: emit(xa_ref))
    pl.when(i >= na_blocks)(lambda: emit(xc_ref))


def _ln_in(xa, xc, g, b, bm=512):
    (na, D), nc = xa.shape, xc.shape[0]
    nab, ncb = na // bm, nc // bm
    row = pl.BlockSpec((bm, D), lambda i: (i, 0))
    vec = pl.BlockSpec((1, D), lambda i: (0, 0))
    return pl.pallas_call(
        functools.partial(_ln_in_kernel, na_blocks=nab), name="ln_in", grid=(nab + ncb,),
        in_specs=[pl.BlockSpec((bm, D), lambda i: (jnp.minimum(i, nab - 1), 0)),
                  pl.BlockSpec((bm, D), lambda i: (jnp.maximum(i - nab, 0), 0)), vec, vec],
        out_specs=[row, row],
        out_shape=[jax.ShapeDtypeStruct((na + nc, D), jnp.float32), jax.ShapeDtypeStruct((na + nc, D), jnp.bfloat16)],
        compiler_params=_cparams("arbitrary"))(xa, xc, g.reshape(1, D), b.reshape(1, D))


def _merge_kernel(xb_ref, wg_ref, bg_ref, ya_ref, yb_ref, yc_ref, yd_ref, wb_ref, o_ref, wgb_ref, wbb_ref):
    @pl.when(pl.program_id(1) == 0)
    def _():
        wgb_ref[...] = wg_ref[...].astype(wgb_ref.dtype)
        wbb_ref[...] = wb_ref[...].astype(wbb_ref.dtype)

    xb = xb_ref[...]
    acc = None
    for i, y_ref in enumerate((ya_ref, yb_ref, yc_ref, yd_ref)):
        gate = jax.nn.sigmoid(jnp.dot(xb, wgb_ref[i], preferred_element_type=jnp.float32) + bg_ref[i])
        term = gate * jnp.dot(y_ref[...], wbb_ref[i], preferred_element_type=jnp.float32)
        acc = term if acc is None else acc + term
    o_ref[...] = acc.astype(o_ref.dtype)


def _merge(xb, wg, bg, ys, wb, bm=1024, bn=256):
    n, D = xb.shape
    W = ys[0].shape[1]
    y_spec = pl.BlockSpec((bm, W), lambda j, i: (i, 0))
    return pl.pallas_call(
        _merge_kernel, name="branch_merge", grid=(D // bn, n // bm),
        in_specs=[pl.BlockSpec((bm, D), lambda j, i: (i, 0)),
                  pl.BlockSpec((N_BRANCH, D, bn), lambda j, i: (0, 0, j)),
                  pl.BlockSpec((N_BRANCH, 1, bn), lambda j, i: (0, 0, j)),
                  y_spec, y_spec, y_spec, y_spec,
                  pl.BlockSpec((N_BRANCH, W, bn), lambda j, i: (0, 0, j))],
        out_specs=pl.BlockSpec((bm, bn), lambda j, i: (i, j)),
        out_shape=jax.ShapeDtypeStruct((n, D), jnp.bfloat16),
        scratch_shapes=[pltpu.VMEM((N_BRANCH, D, bn), jnp.bfloat16), pltpu.VMEM((N_BRANCH, W, bn), jnp.bfloat16)],
        compiler_params=_cparams("arbitrary", "arbitrary"),
    )(xb, wg, bg.reshape(N_BRANCH, 1, D), *ys, wb)


def _outproj_kernel(m_ref, w_ref, x_ref, g_ref, b_ref, wr_ref, xf_ref, xb_ref, aff_ref):
    r = jnp.dot(m_ref[...], w_ref[...], preferred_element_type=jnp.float32)
    y = _ln_rows(DN_ALPHA * x_ref[...] + r, g_ref[...], b_ref[...])
    yb = y.astype(jnp.bfloat16)
    xf_ref[...] = y
    xb_ref[...] = yb
    logits = jnp.dot(yb, wr_ref[...], preferred_element_type=jnp.float32)
    lane = lax.broadcasted_iota(jnp.int32, logits.shape, 1)
    logits = jnp.where(lane < N_EXPERTS, logits, -jnp.inf)
    e = jnp.exp(logits - jnp.max(logits, -1, keepdims=True))
    aff_ref[...] = e / jnp.sum(e, -1, keepdims=True)


def _outproj(merged, w_out, x, g, b, w_router_p, bm=512):
    n, D = x.shape
    row = pl.BlockSpec((bm, D), lambda i: (i, 0))
    vec = pl.BlockSpec((1, D), lambda i: (0, 0))
    return pl.pallas_call(
        _outproj_kernel, name="outproj_ln_router", grid=(n // bm,),
        in_specs=[row, pl.BlockSpec((D, D), lambda i: (0, 0)), row, vec, vec,
                  pl.BlockSpec((D, LANE), lambda i: (0, 0))],
        out_specs=[row, row, pl.BlockSpec((bm, LANE), lambda i: (i, 0))],
        out_shape=[jax.ShapeDtypeStruct((n, D), jnp.float32), jax.ShapeDtypeStruct((n, D), jnp.bfloat16),
                   jax.ShapeDtypeStruct((n, LANE), jnp.float32)],
        compiler_params=_cparams("parallel"),
    )(merged, w_out, x, g.reshape(1, D), b.reshape(1, D), w_router_p)


def _log_sigmoid(z):
    return jnp.minimum(z, 0.0) - jnp.log(1.0 + jnp.exp(-jnp.abs(z)))


def _headnorm(o):
    mu = jnp.mean(o, -1, keepdims=True)
    oc = o - mu
    var = jnp.mean(oc * oc, -1, keepdims=True)
    return oc * lax.rsqrt(var + LN_EPS)


def _silu(x):
    return x * jax.nn.sigmoid(x)


def _chunk_cumsum(g, chunk, inclusive):
    row = lax.broadcasted_iota(jnp.int32, g.shape, 0) % chunk
    b = g
    s = 1
    while s < chunk:
        b = b + jnp.where(row >= s, pltpu.roll(b, s, axis=0), 0.0)
        s *= 2
    return b if inclusive else b - g


def _gla_kernel(flag_ref, qk_ref, v_ref, lr_ref, wup_ref, bup_ref, *rest, reverse, nblk):
    s_idx = pl.program_id(0)
    blk = nblk - 1 - s_idx if reverse else s_idx

    @pl.when(flag_ref[blk] == 1)
    def _():
        rest[-1][...] = jnp.zeros_like(rest[-1])

    for hh in range(GLA_HEADS):
        _gla_head(hh, qk_ref, v_ref, lr_ref, wup_ref, bup_ref, *rest, reverse=reverse)


def _gla_head(hh, qk_ref, v_ref, lr_ref, wup_ref, bup_ref, *rest, reverse):
    if reverse:
        gate_ref, of_ref, o_ref, st_ref = rest
    else:
        o_ref, st_ref = rest
    f32, bf16 = jnp.float32, jnp.bfloat16
    C = GLA_CHUNK
    TB = qk_ref.shape[0]
    nc = TB // C
    sl = slice(hh * LANE, (hh + 1) * LANE)

    qk = qk_ref[:, sl].astype(f32)
    q = qk[:, :GLA_DK] * (GLA_DK ** -0.5)
    k = qk[:, GLA_DK:]
    z = jnp.dot(lr_ref[...], wup_ref[hh], preferred_element_type=f32) + bup_ref[hh]
    g = _log_sigmoid(z) * (1.0 / GLA_TAU)
    cs = _chunk_cumsum(g, C, inclusive=not reverse)
    cs3 = cs.reshape(nc, C, GLA_DK)
    mid = cs3[:, C // 2:C // 2 + 1, :]
    if reverse:
        tot3 = cs3[:, C - 1:C, :] + g.reshape(nc, C, GLA_DK)[:, C - 1:C, :]
        pq3, pk3 = tot3 - cs3, cs3
        qm3, km3 = mid - cs3, cs3 - mid
    else:
        tot3 = cs3[:, C - 1:C, :]
        pq3, pk3 = cs3, tot3 - cs3
        qm3, km3 = cs3 - mid, mid - cs3
    q3 = q.reshape(nc, C, GLA_DK)
    k3 = k.reshape(nc, C, GLA_DK)
    qe = (q3 * jnp.exp(pq3)).astype(bf16)
    kd = (k3 * jnp.exp(pk3)).astype(bf16)
    qm = (q3 * jnp.exp(qm3)).astype(bf16)
    km = (k3 * jnp.exp(km3)).astype(bf16)
    etot = jnp.exp(tot3)
    v = v_ref[:, sl]
    ri = lax.broadcasted_iota(jnp.int32, (C, C), 0)
    ci = lax.broadcasted_iota(jnp.int32, (C, C), 1)
    keep = (ci > ri) if reverse else (ci <= ri)

    st = st_ref[hh]
    outs = [None] * nc
    for c in (range(nc - 1, -1, -1) if reverse else range(nc)):
        vc = v[c * C:(c + 1) * C, :]
        att = lax.dot_general(qm[c], km[c], (((1,), (1,)), ((), ())), preferred_element_type=f32)
        att = jnp.where(keep, att, 0.0).astype(bf16)
        o_c = jnp.dot(att, vc, preferred_element_type=f32)
        o_c = o_c + lax.dot_general(qe[c], st.astype(bf16), (((1,), (1,)), ((), ())), preferred_element_type=f32)
        upd = lax.dot_general(vc, kd[c], (((0,), (0,)), ((), ())), preferred_element_type=f32)
        st = st * etot[c] + upd
        outs[c] = o_c
    st_ref[hh] = st
    o = jnp.concatenate(outs, axis=0)
    if reverse:
        o = _headnorm(o + of_ref[:, sl])
        o_ref[:, sl] = (o * _silu(gate_ref[:, sl].astype(f32))).astype(o_ref.dtype)
    else:
        o_ref[:, sl] = o


def _gla(h, seq_first, seq_last, wup, bup):
    n = h.shape[0]
    TB = SEQ_BLOCK
    nblk = n // TB
    H = GLA_HEADS

    def col(off):
        return off // LANE

    HW = H * LANE

    def specs(reverse):
        blk = (lambda s: nblk - 1 - s) if reverse else (lambda s: s)
        sp = [pl.BlockSpec((TB, HW), lambda s, f: (blk(s), QK_OFF // HW)),
              pl.BlockSpec((TB, HW), lambda s, f: (blk(s), AV_OFF // HW)),
              pl.BlockSpec((TB, LANE), lambda s, f: (blk(s), col(LR_OFF))),
              pl.BlockSpec((None, H, LANE, GLA_DK), lambda s, f: (1 if reverse else 0, 0, 0, 0)),
              pl.BlockSpec((None, H, 1, GLA_DK), lambda s, f: (1 if reverse else 0, 0, 0, 0))]
        out = pl.BlockSpec((TB, HW), lambda s, f: (blk(s), 0))
        return sp, out, blk

    scratch = [pltpu.VMEM((H, GLA_DV, GLA_DK), jnp.float32)]
    sp, out, _ = specs(False)
    o_f = pl.pallas_call(
        functools.partial(_gla_kernel, reverse=False, nblk=nblk), name="gla_fwd",
        grid_spec=pltpu.PrefetchScalarGridSpec(num_scalar_prefetch=1, grid=(nblk,), in_specs=sp, out_specs=out,
                                               scratch_shapes=scratch),
        out_shape=jax.ShapeDtypeStruct((n, H * GLA_DV), jnp.float32),
        compiler_params=_cparams("arbitrary"),
    )(seq_first, h, h, h, wup, bup)
    sp, out, blk = specs(True)
    sp = sp + [pl.BlockSpec((TB, HW), lambda s, f: (blk(s), AG_OFF // HW)), out]
    return pl.pallas_call(
        functools.partial(_gla_kernel, reverse=True, nblk=nblk), name="gla_bwd",
        grid_spec=pltpu.PrefetchScalarGridSpec(num_scalar_prefetch=1, grid=(nblk,), in_specs=sp, out_specs=out,
                                               scratch_shapes=scratch),
        out_shape=jax.ShapeDtypeStruct((n, H * GLA_DV), jnp.bfloat16),
        compiler_params=_cparams("arbitrary"),
    )(seq_last, h, h, h, wup, bup, h, o_f)


def _pack_gla_up(w_up, b_up):
    H, dk, r = GLA_HEADS, GLA_DK, GLA_RANK
    w = w_up.reshape(2, r, H, dk).transpose(0, 2, 1, 3)
    wp = jnp.zeros((2, H, LANE, dk), jnp.float32)
    wp = wp.at[0, :, 0:r].set(w[0]).at[1, :, r:2 * r].set(w[1])
    return wp.astype(jnp.bfloat16), b_up.reshape(2, H, 1, dk).astype(jnp.float32)


def _rope_tables(pos, dim, theta, width):
    half = dim // 2
    T = pos.shape[0]
    inv = 1.0 / (theta ** (jnp.arange(half, dtype=jnp.float32) / half))
    ang = pos.astype(jnp.float32)[:, None] * inv[None, :]
    c, s = jnp.cos(ang), jnp.sin(ang)
    rest = width - dim
    cos = jnp.concatenate([c, c, jnp.ones((T, rest), jnp.float32)], axis=1)
    sin_lo = jnp.concatenate([-s, jnp.zeros((T, half + rest), jnp.float32)], axis=1)
    sin_hi = jnp.concatenate([jnp.zeros((T, half), jnp.float32), s, jnp.zeros((T, rest), jnp.float32)], axis=1)
    return cos, sin_lo, sin_hi


def _ret_kernel(flag_ref, pos_ref, q_ref, k_ref, v_ref, cos_ref, sin_ref, dec_ref, *rest, reverse, nblk):
    s_idx = pl.program_id(0)
    blk = nblk - 1 - s_idx if reverse else s_idx

    @pl.when(flag_ref[blk] == 1)
    def _():
        st_ref = rest[-1] if reverse else rest[-2]
        st_ref[...] = jnp.zeros_like(st_ref)

    for hh in range(RET_HEADS):
        _ret_head(hh, s_idx, q_ref, k_ref, v_ref, cos_ref, sin_ref, dec_ref, *rest, reverse=reverse)


def _ret_head(hh, s_idx, q_ref, k_ref, v_ref, cos_ref, sin_ref, dec_ref, *rest, reverse):
    if reverse:
        gate_ref, o1_ref, o_ref, st_ref = rest
    else:
        o_ref, st_ref, dm_ref = rest
    f32, bf16 = jnp.float32, jnp.bfloat16
    TB = q_ref.shape[0]
    sl = slice(hh * RET_DH, (hh + 1) * RET_DH)

    dec = dec_ref[hh]
    lg = -(jnp.maximum(dec, 0.0) + jnp.log1p(jnp.exp(-jnp.abs(dec))))
    lg_f, lg_b = lg[0, 0:1, 0:1], lg[1, 0:1, 0:1]

    cos, sin = cos_ref[...], sin_ref[...]

    def rot(x):
        return x * cos + pltpu.roll(x, RET_DH // 2, axis=1) * sin

    qr = rot(q_ref[:, sl].astype(f32))
    kr = rot(k_ref[:, sl].astype(f32)) * (RET_DH ** -0.5)
    v = v_ref[:, sl]
    idx = lax.broadcasted_iota(jnp.int32, (TB, 1), 0).astype(f32)
    st = st_ref[hh]
    cross = jnp.dot(qr.astype(bf16), st.astype(bf16), preferred_element_type=f32)
    if reverse:
        cross = cross * jnp.exp(lg_b * (TB - idx))
        kz = (kr * jnp.exp(lg_b * idx)).astype(bf16)
        st_ref[hh] = st * jnp.exp(lg_b * TB) + lax.dot_general(kz, v, (((0,), (0,)), ((), ())),
                                                               preferred_element_type=f32)
        o = _headnorm(o1_ref[:, sl] + cross)
        o_ref[:, sl] = (o * _silu(gate_ref[:, sl].astype(f32))).astype(o_ref.dtype)
    else:
        @pl.when(s_idx == 0)
        def _():
            ri = lax.broadcasted_iota(jnp.int32, (TB, TB), 0)
            ci = lax.broadcasted_iota(jnp.int32, (TB, TB), 1)
            d = (ri - ci).astype(f32)
            dm_ref[hh] = jnp.where(d >= 0, jnp.exp(lg_f * jnp.maximum(d, 0.0)), jnp.exp(lg_b * jnp.maximum(-d, 0.0)))

        cross = cross * jnp.exp(lg_f * (idx + 1.0))
        kz = (kr * jnp.exp(lg_f * (TB - 1.0 - idx))).astype(bf16)
        st_ref[hh] = st * jnp.exp(lg_f * TB) + lax.dot_general(kz, v, (((0,), (0,)), ((), ())),
                                                               preferred_element_type=f32)
        sc = lax.dot_general(qr.astype(bf16), kr.astype(bf16), (((1,), (1,)), ((), ())), preferred_element_type=f32)
        sc = (sc * dm_ref[hh]).astype(bf16)
        o_ref[:, sl] = jnp.dot(sc, v, preferred_element_type=f32) + cross


def _retention(h, seq_first, seq_last, pos_blk, decay, t_max):
    n = h.shape[0]
    TB = SEQ_BLOCK
    nblk = n // TB
    H, dh = RET_HEADS, RET_DH
    cos, sin_lo, sin_hi = _rope_tables(jnp.arange(t_max), dh, RET_THETA, dh)
    sin = sin_lo + sin_hi
    dec = jnp.broadcast_to(decay.astype(jnp.float32).T[:, :, None, None], (H, 2, 8, LANE))

    HW = H * dh

    def specs(reverse):
        blk = (lambda s: nblk - 1 - s) if reverse else (lambda s: s)
        tok = lambda off: pl.BlockSpec((TB, HW), lambda s, f, p: (blk(s), off // HW))
        tab = pl.BlockSpec((TB, LANE), lambda s, f, p: (p[blk(s)], 0))
        sp = [tok(DQ_OFF), tok(DK_OFF), tok(DV_OFF), tab, tab,
              pl.BlockSpec((H, 2, 8, LANE), lambda s, f, p: (0, 0, 0, 0))]
        return sp, tok, pl.BlockSpec((TB, HW), lambda s, f, p: (blk(s), 0))

    sp, _, out = specs(False)
    o1 = pl.pallas_call(
        functools.partial(_ret_kernel, reverse=False, nblk=nblk), name="ret_fwd",
        grid_spec=pltpu.PrefetchScalarGridSpec(
            num_scalar_prefetch=2, grid=(nblk,), in_specs=sp, out_specs=out,
            scratch_shapes=[pltpu.VMEM((H, dh, dh), jnp.float32), pltpu.VMEM((H, TB, TB), jnp.float32)]),
        out_shape=jax.ShapeDtypeStruct((n, H * dh), jnp.float32),
        compiler_params=_cparams("arbitrary"),
    )(seq_first, pos_blk, h, h, h, cos, sin, dec)
    sp, tok, out = specs(True)
    sp = sp + [tok(DG_OFF), out]
    return pl.pallas_call(
        functools.partial(_ret_kernel, reverse=True, nblk=nblk), name="ret_bwd",
        grid_spec=pltpu.PrefetchScalarGridSpec(
            num_scalar_prefetch=2, grid=(nblk,), in_specs=sp, out_specs=out,
            scratch_shapes=[pltpu.VMEM((H, dh, dh), jnp.float32)]),
        out_shape=jax.ShapeDtypeStruct((n, H * dh), jnp.bfloat16),
        compiler_params=_cparams("arbitrary"),
    )(seq_last, pos_blk, h, h, h, cos, sin, dec, h, o1)


SWA_Q = 128
SWA_HALF = 64
assert all(w // (2 * d) == SWA_HALF for w, d in SWA_GROUPS)
NEG_BIG = -1e30


def _swa_kernel(hl_ref, hr_ref, q_ref, kl_ref, kc_ref, kr_ref, vl_ref, vc_ref, vr_ref,
                qcos_ref, qs1_ref, qs2_ref, kcos_ref, ks1_ref, ks2_ref, o_ref, lse_ref):
    f32, bf16 = jnp.float32, jnp.bfloat16
    j = pl.program_id(1)
    Q, HW = SWA_Q, SWA_HALF
    W = Q + 2 * HW
    ri = lax.broadcasted_iota(jnp.int32, (Q, W), 0)
    ci = lax.broadcasted_iota(jnp.int32, (Q, W), 1)
    rel = ci - HW - ri
    ok = (rel <= HW) & (rel >= -HW)
    ok = ok & ((ci >= HW) | (hl_ref[j] == 1)) & ((ci < HW + Q) | (hr_ref[j] == 1))
    lane = lax.broadcasted_iota(jnp.int32, (Q, LANE), 1)

    def rot(x, c_ref, s1_ref, s2_ref):
        return (x * c_ref[...] + pltpu.roll(x, LANE - ROPE_DIM // 2, axis=1) * s1_ref[...]
                + pltpu.roll(x, ROPE_DIM // 2, axis=1) * s2_ref[...])

    lse_tile = jnp.zeros((Q, LANE), f32)
    for hh in range(SWA_HEADS):
        sl = slice(hh * SWA_DH, (hh + 1) * SWA_DH)
        q = rot(q_ref[:, sl].astype(f32), qcos_ref, qs1_ref, qs2_ref) * (SWA_DH ** -0.5)
        k = jnp.concatenate([kl_ref[:, sl], kc_ref[:, sl], kr_ref[:, sl]], axis=0).astype(f32)
        k = rot(k, kcos_ref, ks1_ref, ks2_ref)
        v = jnp.concatenate([vl_ref[:, sl], vc_ref[:, sl], vr_ref[:, sl]], axis=0)
        s = lax.dot_general(q.astype(bf16), k.astype(bf16), (((1,), (1,)), ((), ())), preferred_element_type=f32)
        s = jnp.where(ok, s, NEG_BIG)
        m = jnp.max(s, -1, keepdims=True)
        p = jnp.exp(s - m)
        den = jnp.sum(p, -1, keepdims=True)
        o = jnp.dot(p.astype(bf16), v, preferred_element_type=f32) / den
        o_ref[:, sl] = o
        lse_tile = jnp.where(lane == hh, m + jnp.log(den), lse_tile)
    lse_ref[...] = lse_tile


def _swa_merge_kernel(*refs):
    ng = len(SWA_GROUPS)
    o_refs, l_refs, y_ref = refs[:ng], refs[ng:2 * ng], refs[2 * ng]
    scratch = refs[2 * ng + 1:]
    assert SWA_DH == LANE
    os_, ls_ = [], []
    si = 0
    for gi, (_, dil) in enumerate(SWA_GROUPS):
        if dil == 1:
            os_.append(lambda hh, r=o_refs[gi]: r[0, :, hh * LANE:(hh + 1) * LANE])
            ls_.append(l_refs[gi].at[0])
            continue
        o_s, l_s = scratch[si], scratch[si + 1]
        si += 2
        rows = l_s.shape[0] // dil
        for r in range(dil):
            for hh in range(SWA_HEADS):
                o_s[hh, pl.ds(r, rows, stride=dil), :] = o_refs[gi][r, :, hh * LANE:(hh + 1) * LANE]
            l_s[pl.ds(r, rows, stride=dil), :] = l_refs[gi][r]
        os_.append(lambda hh, s=o_s: s[hh])
        ls_.append(l_s)
    for hh in range(SWA_HEADS):
        ls = [r[:, hh:hh + 1] for r in ls_]
        m = functools.reduce(jnp.maximum, ls)
        ws = [jnp.exp(l - m) for l in ls]
        tot = functools.reduce(lambda a, b: a + b, ws)
        y = functools.reduce(lambda a, b: a + b, [w * o(hh) for w, o in zip(ws, os_)])
        y_ref[:, hh * LANE:(hh + 1) * LANE] = (y / tot).astype(y_ref.dtype)


def _proj_strided_kernel(a_ref, b_ref, o_ref, acc_ref, *, dil):
    acc = jnp.dot(a_ref[...], b_ref[...], preferred_element_type=jnp.float32)
    if dil == 1:
        o_ref[0] = acc.astype(o_ref.dtype)
        return
    rows = acc_ref.shape[1] // dil
    for s in range(acc_ref.shape[0]):
        acc_ref[s] = acc[:, s * LANE:(s + 1) * LANE]
        for r in range(dil):
            o_ref[r, :, s * LANE:(s + 1) * LANE] = acc_ref[s, pl.ds(r, rows, stride=dil), :].astype(o_ref.dtype)


def _proj_strided(a, b, dil, bm=1024, bn=512):
    n, K = a.shape
    N = b.shape[1]
    return pl.pallas_call(
        functools.partial(_proj_strided_kernel, dil=dil), name=f"swa_proj_dil{dil}", grid=(n // bm, N // bn),
        in_specs=[pl.BlockSpec((bm, K), lambda i, j: (i, 0)), pl.BlockSpec((K, bn), lambda i, j: (0, j))],
        out_specs=pl.BlockSpec((dil, bm // dil, bn), lambda i, j: (0, i, j)),
        out_shape=jax.ShapeDtypeStruct((dil, n // dil, N), jnp.bfloat16),
        scratch_shapes=[pltpu.VMEM((bn // LANE, bm, LANE), jnp.float32)],
        compiler_params=_cparams("parallel", "arbitrary"),
    )(a, b)


def _swa(xb, w_groups, seqs):
    n = xb.shape[0]
    Q, HW = SWA_Q, SWA_HALF
    GW = SWA_HEADS * SWA_DH
    outs, lses = [], []
    for gi, (window, dil) in enumerate(SWA_GROUPS):
        hs = _proj_strided(xb, w_groups[gi], dil)
        rows = n // dil
        nblk = rows // Q
        has_l, has_r = [], []
        for num, T in seqs:
            per = T // dil // Q
            assert per * Q * dil == T
            has_l += ([0] + [1] * (per - 1)) * num
            has_r += ([1] * (per - 1) + [0]) * num
        has_l = jnp.asarray(np.array(has_l, np.int32))
        has_r = jnp.asarray(np.array(has_r, np.int32))
        qpos = jnp.arange(Q) * dil
        kpos = (jnp.arange(Q + 2 * HW) - HW) * dil
        qt = _rope_tables(qpos, ROPE_DIM, ROPE_THETA, SWA_DH)
        kt = _rope_tables(kpos, ROPE_DIM, ROPE_THETA, SWA_DH)
        last64 = rows // HW - 1

        def center(c):
            return pl.BlockSpec((None, Q, GW), lambda r, j, hl, hr: (r, j, c))

        def left(c):
            return pl.BlockSpec((None, HW, GW), lambda r, j, hl, hr: (r, jnp.maximum(2 * j - 1, 0), c))

        def right(c):
            return pl.BlockSpec((None, HW, GW), lambda r, j, hl, hr: (r, jnp.minimum(2 * j + 2, last64), c))

        qtab = pl.BlockSpec((Q, SWA_DH), lambda r, j, hl, hr: (0, 0))
        ktab = pl.BlockSpec((Q + 2 * HW, SWA_DH), lambda r, j, hl, hr: (0, 0))
        o_g, lse_g = pl.pallas_call(
            _swa_kernel, name=f"swa_dil{dil}",
            grid_spec=pltpu.PrefetchScalarGridSpec(
                num_scalar_prefetch=2, grid=(dil, nblk),
                in_specs=[center(0), left(1), center(1), right(1), left(2), center(2), right(2),
                          qtab, qtab, qtab, ktab, ktab, ktab],
                out_specs=[pl.BlockSpec((None, Q, GW), lambda r, j, hl, hr: (r, j, 0)),
                           pl.BlockSpec((None, Q, LANE), lambda r, j, hl, hr: (r, j, 0))]),
            out_shape=[jax.ShapeDtypeStruct((dil, rows, GW), jnp.float32),
                       jax.ShapeDtypeStruct((dil, rows, LANE), jnp.float32)],
            compiler_params=_cparams("parallel", "parallel"),
        )(has_l, has_r, hs, hs, hs, hs, hs, hs, hs, *qt, *kt)
        outs.append(o_g)
        lses.append(lse_g)
    bm = 1024
    dils = [d for _, d in SWA_GROUPS]
    osp = [pl.BlockSpec((d, bm // d, GW), lambda i: (0, i, 0)) for d in dils]
    lsp = [pl.BlockSpec((d, bm // d, LANE), lambda i: (0, i, 0)) for d in dils]
    scratch = []
    for d in dils:
        if d > 1:
            scratch += [pltpu.VMEM((SWA_HEADS, bm, LANE), jnp.float32), pltpu.VMEM((bm, LANE), jnp.float32)]
    return pl.pallas_call(
        _swa_merge_kernel, name="swa_merge", grid=(n // bm,), in_specs=osp + lsp,
        out_specs=pl.BlockSpec((bm, GW), lambda i: (i, 0)),
        out_shape=jax.ShapeDtypeStruct((n, GW), jnp.bfloat16), scratch_shapes=scratch,
        compiler_params=_cparams("parallel"),
    )(*outs, *lses)


FFT_T2 = 128
FW = FNET_GROUPS * FNET_DG


def _dft_cos_sin(n_out, n_in, period):
    ang = (2.0 * np.pi / period) * np.mod(np.outer(np.arange(n_out), np.arange(n_in)), period)
    return np.cos(ang), np.sin(ang)


def _fnet_stage1_kernel(z_ref, m_ref, a_ref, *, cols):
    mat = m_ref[...]
    for c in range(cols):
        zr = z_ref[:, c * 2 * FW:c * 2 * FW + FW]
        zi = z_ref[:, c * 2 * FW + FW:(c + 1) * 2 * FW]
        a = jnp.dot(mat, jnp.concatenate([zr, zi], axis=0), preferred_element_type=jnp.float32)
        t1 = zr.shape[0]
        a_ref[:, c * 2 * FW:c * 2 * FW + FW] = a[:t1].astype(a_ref.dtype)
        a_ref[:, c * 2 * FW + FW:(c + 1) * 2 * FW] = a[t1:].astype(a_ref.dtype)


def _fnet_stage2_kernel(a_ref, g_ref, y_ref, *, scale):
    y = jnp.dot(g_ref[...], a_ref[...], preferred_element_type=jnp.float32)
    y_ref[...] = (y * scale).astype(y_ref.dtype)


def _fnet(h, seqs):
    n = h.shape[0]
    bf16 = jnp.bfloat16
    T2 = FFT_T2
    cc, ss = _dft_cos_sin(FNET_DG, FNET_DG, FNET_DG)
    eye = np.eye(FNET_GROUPS)
    wc = jnp.asarray(np.concatenate([np.kron(eye, cc), -np.kron(eye, ss)], axis=1), bf16)
    z = _matmul(h, wc, bf16, bn=2 * FW, a_col_blk=C_OFF // FW, name="fnet_channel_dft")
    zz = z.reshape(n // T2, T2 * 2 * FW)
    outs = []
    row0 = 0
    for num, T in seqs:
        T1 = T // T2
        assert T1 * T2 == T and T1 % 16 == 0
        c1, s1 = _dft_cos_sin(T1, T1, T1)
        mat = jnp.asarray(np.block([[c1, s1], [-s1, c1]]), bf16)
        cols = min(T2, max(1, 1024 // T1))
        rb0 = row0 // T1
        a = pl.pallas_call(
            functools.partial(_fnet_stage1_kernel, cols=cols), name=f"fnet_dft_t1_{T1}", grid=(num, T2 // cols),
            in_specs=[pl.BlockSpec((T1, cols * 2 * FW), lambda b, j: (rb0 + b, j)),
                      pl.BlockSpec((2 * T1, 2 * T1), lambda b, j: (0, 0))],
            out_specs=pl.BlockSpec((T1, cols * 2 * FW), lambda b, j: (b, j)),
            out_shape=jax.ShapeDtypeStruct((num * T1, T2 * 2 * FW), bf16),
            compiler_params=_cparams("parallel", "parallel"),
        )(zz, mat)
        ec, es = _dft_cos_sin(T, T2, T)
        g = jnp.asarray(np.stack([ec, es], axis=-1).reshape(T2, T1 * 2 * T2), bf16)
        y = pl.pallas_call(
            functools.partial(_fnet_stage2_kernel, scale=float(1.0 / np.sqrt(T * FNET_DG))), name=f"fnet_dft_t2_{T1}",
            grid=(num, T1),
            in_specs=[pl.BlockSpec((None, 2 * T2, FW), lambda b, k1: (b * T1 + k1, 0, 0)),
                      pl.BlockSpec((T2, 2 * T2), lambda b, k1: (0, k1))],
            out_specs=pl.BlockSpec((T2, FW), lambda b, k1: (b, k1)),
            out_shape=jax.ShapeDtypeStruct((num * T2, T1 * FW), bf16),
            compiler_params=_cparams("parallel", "parallel"),
        )(a.reshape(num * T1, 2 * T2, FW), g)
        outs.append(y.reshape(num * T, FW))
        row0 += num * T1
    return jnp.concatenate(outs, axis=0)


TOK_RADIX = 256


def _ffn_up_kernel(x_ref, w1_ref, w3_ref, h_ref, w1b_ref, w3b_ref):
    @pl.when(pl.program_id(2) == 0)
    def _():
        w1b_ref[...] = w1_ref[...].astype(w1b_ref.dtype)
        w3b_ref[...] = w3_ref[...].astype(w3b_ref.dtype)

    x = x_ref[...]
    h1 = jnp.dot(x, w1b_ref[...], preferred_element_type=jnp.float32)
    h3 = jnp.dot(x, w3b_ref[...], preferred_element_type=jnp.float32)
    h_ref[...] = (_silu(h1) * h3).astype(h_ref.dtype)


def _ffn_down_kernel(h_ref, w2_ref, g_ref, t_ref, o_ref, w2b_ref):
    @pl.when(pl.program_id(1) == 0)
    def _():
        w2b_ref[...] = w2_ref[...].astype(w2b_ref.dtype)

    D = w2_ref.shape[1]
    y = jnp.dot(h_ref[...], w2b_ref[...], preferred_element_type=jnp.float32)
    o_ref[:, :D] = (y * g_ref[...]).astype(o_ref.dtype)
    tok = t_ref[...]
    lane = lax.broadcasted_iota(jnp.int32, (tok.shape[0], LANE), 1)
    digits = jnp.where(lane == 0, tok // TOK_RADIX, jnp.where(lane == 1, tok % TOK_RADIX, 0))
    o_ref[:, D:] = digits.astype(jnp.float32).astype(o_ref.dtype)


def _expert_ffn(xe, w1, w3, w2, gate, tok, bm=1024, bf=512):
    E, R, D = xe.shape
    F = w1.shape[-1]
    wspec = pl.BlockSpec((None, D, bf), lambda e, f, i: (e, 0, f))
    hmid = pl.pallas_call(
        _ffn_up_kernel, name="expert_ffn_up", grid=(E, F // bf, R // bm),
        in_specs=[pl.BlockSpec((None, bm, D), lambda e, f, i: (e, i, 0)), wspec, wspec],
        out_specs=pl.BlockSpec((None, bm, bf), lambda e, f, i: (e, i, f)),
        out_shape=jax.ShapeDtypeStruct((E, R, F), jnp.bfloat16),
        scratch_shapes=[pltpu.VMEM((D, bf), jnp.bfloat16), pltpu.VMEM((D, bf), jnp.bfloat16)],
        compiler_params=_cparams("arbitrary", "arbitrary", "arbitrary"),
    )(xe, w1, w3)
    col = pl.BlockSpec((None, bm, 1), lambda e, i: (e, i, 0))
    return pl.pallas_call(
        _ffn_down_kernel, name="expert_ffn_down", grid=(E, R // bm),
        in_specs=[pl.BlockSpec((None, bm, F), lambda e, i: (e, i, 0)),
                  pl.BlockSpec((None, F, D), lambda e, i: (e, 0, 0), pipeline_mode=pl.Buffered(1)),
                  col, col],
        out_specs=pl.BlockSpec((None, bm, D + LANE), lambda e, i: (e, i, 0)),
        out_shape=jax.ShapeDtypeStruct((E, R, D + LANE), jnp.bfloat16),
        scratch_shapes=[pltpu.VMEM((F, D), jnp.bfloat16)],
        compiler_params=_cparams("arbitrary", "arbitrary"),
    )(hmid, w2, gate, tok)


COMB_BM = 128
COMB_G = 16
COMB_K = 256


def _combine_kernel(lo_ref, x_ref, g_ref, b_ref, ye_hbm, *rest, nblk_all, blk_off, want_bf16):
    if want_bf16:
        xf_ref, xb_ref, buf_ref, acc_ref, sem = rest
    else:
        xf_ref, buf_ref, acc_ref, sem = rest
    f32, bf16 = jnp.float32, jnp.bfloat16
    E = ye_hbm.shape[0]
    D = x_ref.shape[1]
    BM, G, KC = COMB_BM, COMB_G, COMB_K
    i = pl.program_id(0)
    nsteps = pl.num_programs(0)
    blk = i + blk_off
    slot = i % 2

    @pl.when(i == 0)
    def _():
        buf_ref[...] = jnp.zeros_like(buf_ref)

    def granule_copy(sl, e, src_row, dst_row):
        return pltpu.make_async_copy(ye_hbm.at[e, pl.ds(src_row, G)], buf_ref.at[sl, pl.ds(dst_row, G)], sem.at[sl])

    def gather_block(b, sl, start):
        off = jnp.int32(0)
        for e in range(E):
            lo = lo_ref[e * (nblk_all + 1) + b]
            hi = lo_ref[e * (nblk_all + 1) + b + 1]
            lo_al = (lo // G) * G
            ng = jnp.where(hi > lo, (hi - lo_al + G - 1) // G, 0)
            if start:
                def issue(g, carry, e=e, lo_al=lo_al, off=off):
                    granule_copy(sl, e, pl.multiple_of(lo_al + g * G, G), pl.multiple_of(off + g * G, G)).start()
                    return carry

                lax.fori_loop(0, ng, issue, 0)
            off = off + ng * G
        return off

    @pl.when(i == 0)
    def _():
        gather_block(blk, slot, True)

    @pl.when(i + 1 < nsteps)
    def _():
        gather_block(blk + 1, 1 - slot, True)

    off = gather_block(blk, slot, False)

    def wait_one(g, carry):
        granule_copy(slot, 0, 0, 0).wait()
        return carry

    lax.fori_loop(0, off // G, wait_one, 0)

    acc_ref[...] = jnp.zeros_like(acc_ref)
    t0 = (blk * BM).astype(f32)
    lane_tok = lax.broadcasted_iota(jnp.int32, (KC, BM), 1).astype(f32)
    row = lax.broadcasted_iota(jnp.int32, (KC, 1), 0)

    def chunk(c, carry):
        win = buf_ref[slot, pl.ds(pl.multiple_of(c * KC, KC), KC), :]
        tok = win[:, D:D + 1].astype(f32) * TOK_RADIX + win[:, D + 1:D + 2].astype(f32)
        tok = jnp.where(row + c * KC < off, tok, -1.0)
        onehot_t = (tok - t0 == lane_tok).astype(bf16)
        acc_ref[...] += lax.dot_general(onehot_t, win[:, :D], (((0,), (0,)), ((), ())), preferred_element_type=f32)
        return carry

    lax.fori_loop(0, (off + KC - 1) // KC, chunk, 0)
    y = _ln_rows(DN_ALPHA * x_ref[...] + acc_ref[...], g_ref[...], b_ref[...])
    xf_ref[...] = y
    if want_bf16:
        xb_ref[...] = y.astype(bf16)


def _combine_ln(x, ye, lo, g, b, row_off, rows, want_bf16):
    n, D = x.shape
    E, R, DY = ye.shape
    BM = COMB_BM
    nblk_all = n // BM
    ob = row_off // BM
    kmax = -(-E * (BM + COMB_G) // COMB_K) * COMB_K
    irow = pl.BlockSpec((BM, D), lambda i, lo: (i + ob, 0))
    orow = pl.BlockSpec((BM, D), lambda i, lo: (i, 0))
    vec = pl.BlockSpec((1, D), lambda i, lo: (0, 0))
    out_shape = [jax.ShapeDtypeStruct((rows, D), jnp.float32)]
    out_specs = [orow]
    if want_bf16:
        out_shape.append(jax.ShapeDtypeStruct((rows, D), jnp.bfloat16))
        out_specs.append(orow)
    return pl.pallas_call(
        functools.partial(_combine_kernel, nblk_all=nblk_all, blk_off=ob, want_bf16=want_bf16), name="moe_combine_ln",
        grid_spec=pltpu.PrefetchScalarGridSpec(
            num_scalar_prefetch=1, grid=(rows // BM,),
            in_specs=[irow, vec, vec, pl.BlockSpec(memory_space=pl.ANY)], out_specs=out_specs,
            scratch_shapes=[pltpu.VMEM((2, kmax, DY), jnp.bfloat16), pltpu.VMEM((BM, D), jnp.float32),
                            pltpu.SemaphoreType.DMA((2,))]),
        out_shape=out_shape,
        compiler_params=_cparams("arbitrary"),
    )(lo, x, g.reshape(1, D), b.reshape(1, D), ye)


def _expert_choice(xb, aff, seqs, w1, w3, w2):
    n, D = xb.shape
    aff_t = aff[:, :N_EXPERTS].T
    gates, idxs = [], []
    start = 0
    for num, T in seqs:
        m = num * T
        cap = max(1, CAPACITY_FACTOR * m // N_EXPERTS)
        gate, idx = lax.top_k(aff_t[:, start:start + m], cap)
        gates.append(gate)
        idxs.append(idx + start)
        start += m
    idx, gate = lax.sort((jnp.concatenate(idxs, axis=1), jnp.concatenate(gates, axis=1)), dimension=1, num_keys=1)
    bounds = jnp.arange(n // COMB_BM + 1, dtype=jnp.int32) * COMB_BM
    lo = jax.vmap(lambda row: jnp.searchsorted(row, bounds, side="left"))(idx).astype(jnp.int32).reshape(-1)
    ye = _expert_ffn(xb[idx], w1, w3, w2, gate[..., None], idx[..., None])
    return ye, lo


def _pack_w_in(w_in):
    sizes = (GLA_HEADS * GLA_DK, GLA_HEADS * GLA_DK, GLA_HEADS * GLA_DV, GLA_HEADS * GLA_DV, 2 * GLA_RANK,
             SWA_NH * SWA_DH, SWA_NH * SWA_DH, SWA_NH * SWA_DH, FNET_GROUPS * FNET_DG,
             RET_HEADS * RET_DH, RET_HEADS * RET_DH, RET_HEADS * RET_DH, RET_HEADS * RET_DH)
    offs = np.concatenate([[0], np.cumsum(sizes)])
    part = lambda i: w_in[..., offs[i]:offs[i + 1]]
    a_q, a_k = part(0), part(1)
    cols = []
    for hh in range(GLA_HEADS):
        cols += [a_q[..., hh * GLA_DK:(hh + 1) * GLA_DK], a_k[..., hh * GLA_DK:(hh + 1) * GLA_DK]]
    cols += [part(2), part(3), part(4),
             jnp.zeros(w_in.shape[:2] + (C_OFF - LR_OFF - 2 * GLA_RANK,), w_in.dtype)]
    cols += [part(i) for i in range(8, 13)]
    out = jnp.concatenate(cols, axis=-1)
    assert out.shape[-1] == IN_COLS_P
    GW = SWA_HEADS * SWA_DH
    swa = [jnp.concatenate([part(i)[..., g * GW:(g + 1) * GW] for i in (5, 6, 7)], axis=-1)
           for g in range(len(SWA_GROUPS))]
    return out, swa


def _block_tables(seqs, blk):
    first, last, pos = [], [], []
    for num, T in seqs:
        per = T // blk
        assert per * blk == T
        for _ in range(num):
            first += [1] + [0] * (per - 1)
            last += [0] * (per - 1) + [1]
            pos += list(range(per))
    mk = lambda a: jnp.asarray(np.array(a, np.int32))
    return mk(first), mk(last), mk(pos)


def kernel(x_prompt, x_sample, ln_in_g, ln_in_b, w_in, gla_w_up, gla_b_up, ret_decay, w_gate, b_gate, w_branch,
           w_out, ln1_g, ln1_b, w_router, w_e1, w_e3, w_e2, ln2_g, ln2_b):
    bf16 = jnp.bfloat16
    D = D_MODEL
    seqs = (x_prompt.shape[:2], x_sample.shape[:2])
    n_p, n_s = x_prompt.shape[0] * x_prompt.shape[1], x_sample.shape[0] * x_sample.shape[1]
    n = n_p + n_s
    t_max = max(T for _, T in seqs)
    seq_first, seq_last, pos_blk = _block_tables(seqs, SEQ_BLOCK)

    w_in_p, w_swa = _pack_w_in(w_in.astype(bf16))
    wo = w_out.astype(bf16)
    wr = jnp.pad(w_router, ((0, 0), (0, 0), (0, LANE - N_EXPERTS))).astype(bf16)
    wg, wb, we1, we3, we2 = w_gate, w_branch, w_e1, w_e3, w_e2

    xf, xb = _ln_in(x_prompt.reshape(n_p, D), x_sample.reshape(n_s, D), ln_in_g, ln_in_b)
    for l in range(DEPTH):
        h = _matmul(xb, w_in_p[l], bf16, bm=1024, bn=IN_BN, name="in_proj")
        wup, bup = _pack_gla_up(gla_w_up[l], gla_b_up[l])
        ys = (_gla(h, seq_first, seq_last, wup, bup),
              _swa(xb, [w[l] for w in w_swa], seqs),
              _fnet(h, seqs),
              _retention(h, seq_first, seq_last, pos_blk, ret_decay[l], t_max))
        merged = _merge(xb, wg[l], b_gate[l], ys, wb[l])
        x1f, x1b, aff = _outproj(merged, wo[l], xf, ln1_g[l], ln1_b[l], wr[l])
        ye, lo = _expert_choice(x1b, aff, seqs, we1[l], we3[l], we2[l])
        if l + 1 < DEPTH:
            xf, xb = _combine_ln(x1f, ye, lo, ln2_g[l], ln2_b[l], 0, n, True)
        else:
            y_p, = _combine_ln(x1f, ye, lo, ln2_g[l], ln2_b[l], 0, n_p, False)
            y_s, = _combine_ln(x1f, ye, lo, ln2_g[l], ln2_b[l], n_p, n_s, False)
    return (y_p.reshape(x_prompt.shape), y_s.reshape(x_sample.shape))
```

```python
import functools

import jax
import jax.numpy as jnp
import numpy as np
from jax import lax
from jax.experimental import pallas as pl
from jax.experimental.pallas import tpu as pltpu

D_MODEL = 2048
DEPTH = 2
GLA_HEADS, GLA_DK, GLA_DV, GLA_RANK, GLA_TAU, GLA_CHUNK = 4, 64, 128, 16, 16.0, 64
SWA_GROUPS = ((128, 1), (512, 4), (2048, 16))
SWA_HEADS, SWA_DH = 4, 128
SWA_NH = SWA_HEADS * len(SWA_GROUPS)
SWA_BLOCK = 64
ROPE_THETA, ROPE_DIM = 500000.0, SWA_DH // 4
FNET_GROUPS, FNET_DG = 4, 128
RET_HEADS, RET_DH, RET_THETA = 4, 128, 10000.0
N_BRANCH, BRANCH_W = 4, 512
N_EXPERTS, EXPERT_FF, CAPACITY_FACTOR = 16, 2048, 2
DN_ALPHA = (2 * DEPTH) ** 0.25
LN_EPS = 1e-5

VMEM_LIMIT_BYTES = 48 * 1024 * 1024
LANE = 128

QK_OFF = 0
AV_OFF = QK_OFF + GLA_HEADS * LANE
AG_OFF = AV_OFF + GLA_HEADS * GLA_DV
LR_OFF = AG_OFF + GLA_HEADS * GLA_DV
C_OFF = LR_OFF + 4 * LANE
DQ_OFF = C_OFF + FNET_GROUPS * FNET_DG
DK_OFF = DQ_OFF + RET_HEADS * RET_DH
DV_OFF = DK_OFF + RET_HEADS * RET_DH
DG_OFF = DV_OFF + RET_HEADS * RET_DH
IN_USED = DG_OFF + RET_HEADS * RET_DH
IN_BN = 1536
IN_COLS_P = -(-IN_USED // IN_BN) * IN_BN
assert IN_COLS_P == IN_USED

SEQ_BLOCK = 512


def _cparams(*sem):
    return pltpu.CompilerParams(dimension_semantics=sem, vmem_limit_bytes=VMEM_LIMIT_BYTES)


def _mm_kernel(a_ref, b_ref, o_ref):
    o_ref[...] = jnp.dot(a_ref[...], b_ref[...], preferred_element_type=jnp.float32).astype(o_ref.dtype)


def _matmul(a, b, out_dtype, bm=1024, bn=512, a_col_blk=0, name="matmul"):
    batched = a.ndim == 3
    M = a.shape[-2]
    K, N = b.shape[-2:]
    bm = min(bm, M)
    bn = min(bn, N)
    assert M % bm == 0 and N % bn == 0, (M, N, bm, bn)
    if batched:
        E = a.shape[0]
        grid = (E, M // bm, N // bn)
        in_specs = [pl.BlockSpec((None, bm, K), lambda e, i, j: (e, i, 0)),
                    pl.BlockSpec((None, K, bn), lambda e, i, j: (e, 0, j))]
        out_specs = pl.BlockSpec((None, bm, bn), lambda e, i, j: (e, i, j))
        out_shape = jax.ShapeDtypeStruct((E, M, N), out_dtype)
        sem = ("parallel", "parallel", "arbitrary")
    else:
        grid = (M // bm, N // bn)
        in_specs = [pl.BlockSpec((bm, K), lambda i, j: (i, a_col_blk)),
                    pl.BlockSpec((K, bn), lambda i, j: (0, j))]
        out_specs = pl.BlockSpec((bm, bn), lambda i, j: (i, j))
        out_shape = jax.ShapeDtypeStruct((M, N), out_dtype)
        sem = ("parallel", "arbitrary")
    return pl.pallas_call(
        _mm_kernel, name=name, grid=grid, in_specs=in_specs, out_specs=out_specs, out_shape=out_shape,
        compiler_params=_cparams(*sem),
    )(a, b)


def _mm_w32_kernel(a_ref, w_ref, o_ref, wb_ref):
    @pl.when(pl.program_id(1) == 0)
    def _():
        wb_ref[...] = w_ref[...].astype(wb_ref.dtype)

    o_ref[...] = jnp.dot(a_ref[...], wb_ref[...], preferred_element_type=jnp.float32).astype(o_ref.dtype)


def _matmul_w32(a, w, out_dtype, bm=1024, bn=768, name="matmul_w32"):
    M, K = a.shape
    N = w.shape[1]
    assert M % bm == 0 and N % bn == 0, (M, N, bm, bn)
    return pl.pallas_call(
        _mm_w32_kernel, name=name, grid=(N // bn, M // bm),
        in_specs=[pl.BlockSpec((bm, K), lambda j, i: (i, 0)), pl.BlockSpec((K, bn), lambda j, i: (0, j))],
        out_specs=pl.BlockSpec((bm, bn), lambda j, i: (i, j)),
        out_shape=jax.ShapeDtypeStruct((M, N), out_dtype),
        scratch_shapes=[pltpu.VMEM((K, bn), jnp.bfloat16)],
        compiler_params=_cparams("arbitrary", "arbitrary"),
    )(a, w)


def _ln_rows(x, g, b):
    mu = jnp.mean(x, -1, keepdims=True)
    xc = x - mu
    var = jnp.mean(xc * xc, -1, keepdims=True)
    return xc * lax.rsqrt(var + LN_EPS) * g + b


def _ln_in_kernel(xa_ref, xc_ref, g_ref, b_ref, xf_ref, xb_ref, *, na_blocks):
    i = pl.program_id(0)

    def emit(x_ref):
        y = _ln_rows(x_ref[...], g_ref[...], b_ref[...])
        xf_ref[...] = y
        xb_ref[...] = y.astype(jnp.bfloat16)

    pl.when(i < na_blocks)(lambda: emit(xa_ref))
    pl.when(i >= na_blocks)(lambda: emit(xc_ref))


def _ln_in(xa, xc, g, b, bm=512):
    (na, D), nc = xa.shape, xc.shape[0]
    nab, ncb = na // bm, nc // bm
    row = pl.BlockSpec((bm, D), lambda i: (i, 0))
    vec = pl.BlockSpec((1, D), lambda i: (0, 0))
    return pl.pallas_call(
        functools.partial(_ln_in_kernel, na_blocks=nab), name="ln_in", grid=(nab + ncb,),
        in_specs=[pl.BlockSpec((bm, D), lambda i: (jnp.minimum(i, nab - 1), 0)),
                  pl.BlockSpec((bm, D), lambda i: (jnp.maximum(i - nab, 0), 0)), vec, vec],
        out_specs=[row, row],
        out_shape=[jax.ShapeDtypeStruct((na + nc, D), jnp.float32), jax.ShapeDtypeStruct((na + nc, D), jnp.bfloat16)],
        compiler_params=_cparams("arbitrary"))(xa, xc, g.reshape(1, D), b.reshape(1, D))


def _merge_kernel(xb_ref, wg_ref, bg_ref, ya_ref, yb_ref, yc_ref, yd_ref, wb_ref, o_ref, wgb_ref, wbb_ref):
    @pl.when(pl.program_id(1) == 0)
    def _():
        wgb_ref[...] = wg_ref[...].astype(wgb_ref.dtype)
        wbb_ref[...] = wb_ref[...].astype(wbb_ref.dtype)

    xb = xb_ref[...]
    acc = None
    for i, y_ref in enumerate((ya_ref, yb_ref, yc_ref, yd_ref)):
        gate = jax.nn.sigmoid(jnp.dot(xb, wgb_ref[i], preferred_element_type=jnp.float32) + bg_ref[i])
        term = gate * jnp.dot(y_ref[...], wbb_ref[i], preferred_element_type=jnp.float32)
        acc = term if acc is None else acc + term
    o_ref[...] = acc.astype(o_ref.dtype)


def _merge(xb, wg, bg, ys, wb, layer, bm=1024, bn=256):
    n, D = xb.shape
    W = ys[0].shape[1]
    y_spec = pl.BlockSpec((bm, W), lambda j, i: (i, 0))
    return pl.pallas_call(
        _merge_kernel, name="branch_merge", grid=(D // bn, n // bm),
        in_specs=[pl.BlockSpec((bm, D), lambda j, i: (i, 0)),
                  pl.BlockSpec((None, N_BRANCH, D, bn), lambda j, i: (layer, 0, 0, j)),
                  pl.BlockSpec((N_BRANCH, 1, bn), lambda j, i: (0, 0, j)),
                  y_spec, y_spec, y_spec, y_spec,
                  pl.BlockSpec((None, N_BRANCH, W, bn), lambda j, i: (layer, 0, 0, j))],
        out_specs=pl.BlockSpec((bm, bn), lambda j, i: (i, j)),
        out_shape=jax.ShapeDtypeStruct((n, D), jnp.bfloat16),
        scratch_shapes=[pltpu.VMEM((N_BRANCH, D, bn), jnp.bfloat16), pltpu.VMEM((N_BRANCH, W, bn), jnp.bfloat16)],
        compiler_params=_cparams("arbitrary", "arbitrary"),
    )(xb, wg, bg.reshape(N_BRANCH, 1, D), *ys, wb)


OUTPROJ_SUB = 128


def _outproj_kernel(m_ref, w_ref, x_ref, g_ref, b_ref, wr_ref, xf_ref, xb_ref, aff_ref):
    for s in range(m_ref.shape[0] // OUTPROJ_SUB):
        rows = pl.ds(s * OUTPROJ_SUB, OUTPROJ_SUB)
        r = jnp.dot(m_ref[rows, :], w_ref[...], preferred_element_type=jnp.float32)
        y = _ln_rows(DN_ALPHA * x_ref[rows, :] + r, g_ref[...], b_ref[...])
        yb = y.astype(jnp.bfloat16)
        xf_ref[rows, :] = y
        xb_ref[rows, :] = yb
        logits = jnp.dot(yb, wr_ref[...], preferred_element_type=jnp.float32)
        lane = lax.broadcasted_iota(jnp.int32, logits.shape, 1)
        logits = jnp.where(lane < N_EXPERTS, logits, -jnp.inf)
        e = jnp.exp(logits - jnp.max(logits, -1, keepdims=True))
        aff_ref[rows, :] = e / jnp.sum(e, -1, keepdims=True)


def _outproj(merged, w_out, x, g, b, w_router_p, bm=512):
    n, D = x.shape
    row = pl.BlockSpec((bm, D), lambda i: (i, 0))
    vec = pl.BlockSpec((1, D), lambda i: (0, 0))
    return pl.pallas_call(
        _outproj_kernel, name="outproj_ln_router", grid=(n // bm,),
        in_specs=[row, pl.BlockSpec((D, D), lambda i: (0, 0)), row, vec, vec,
                  pl.BlockSpec((D, LANE), lambda i: (0, 0))],
        out_specs=[row, row, pl.BlockSpec((bm, LANE), lambda i: (i, 0))],
        out_shape=[jax.ShapeDtypeStruct((n, D), jnp.float32), jax.ShapeDtypeStruct((n, D), jnp.bfloat16),
                   jax.ShapeDtypeStruct((n, LANE), jnp.float32)],
        compiler_params=_cparams("parallel"),
    )(merged, w_out, x, g.reshape(1, D), b.reshape(1, D), w_router_p)


def _log_sigmoid(z):
    return jnp.minimum(z, 0.0) - jnp.log(1.0 + jnp.exp(-jnp.abs(z)))


def _headnorm(o):
    mu = jnp.mean(o, -1, keepdims=True)
    oc = o - mu
    var = jnp.mean(oc * oc, -1, keepdims=True)
    return oc * lax.rsqrt(var + LN_EPS)


def _silu(x):
    return x * jax.nn.sigmoid(x)


def _chunk_cumsum(g, chunk, inclusive):
    row = lax.broadcasted_iota(jnp.int32, g.shape, 0) % chunk
    b = g
    s = 1
    while s < chunk:
        b = b + jnp.where(row >= s, pltpu.roll(b, s, axis=0), 0.0)
        s *= 2
    return b if inclusive else b - g


def _gla_kernel(flag_ref, qk_ref, v_ref, lr_ref, wup_ref, bup_ref, *rest, reverse, nblk):
    s_idx = pl.program_id(0)
    blk = nblk - 1 - s_idx if reverse else s_idx

    @pl.when(flag_ref[blk] == 1)
    def _():
        rest[-1][...] = jnp.zeros_like(rest[-1])

    for hh in range(GLA_HEADS):
        _gla_head(hh, qk_ref, v_ref, lr_ref, wup_ref, bup_ref, *rest, reverse=reverse)


def _gla_head(hh, qk_ref, v_ref, lr_ref, wup_ref, bup_ref, *rest, reverse):
    if reverse:
        gate_ref, of_ref, o_ref, st_ref = rest
    else:
        o_ref, st_ref = rest
    f32, bf16 = jnp.float32, jnp.bfloat16
    C = GLA_CHUNK
    TB = qk_ref.shape[0]
    nc = TB // C
    sl = slice(hh * LANE, (hh + 1) * LANE)

    qk = qk_ref[:, sl].astype(f32)
    q = qk[:, :GLA_DK] * (GLA_DK ** -0.5)
    k = qk[:, GLA_DK:]
    z = jnp.dot(lr_ref[...], wup_ref[hh], preferred_element_type=f32) + bup_ref[hh]
    g = _log_sigmoid(z) * (1.0 / GLA_TAU)
    cs = _chunk_cumsum(g, C, inclusive=not reverse)
    cs3 = cs.reshape(nc, C, GLA_DK)
    mid = cs3[:, C // 2:C // 2 + 1, :]
    if reverse:
        tot3 = cs3[:, C - 1:C, :] + g.reshape(nc, C, GLA_DK)[:, C - 1:C, :]
        pq3, pk3 = tot3 - cs3, cs3
        qm3, km3 = mid - cs3, cs3 - mid
    else:
        tot3 = cs3[:, C - 1:C, :]
        pq3, pk3 = cs3, tot3 - cs3
        qm3, km3 = cs3 - mid, mid - cs3
    q3 = q.reshape(nc, C, GLA_DK)
    k3 = k.reshape(nc, C, GLA_DK)
    qe = (q3 * jnp.exp(pq3)).astype(bf16)
    kd = (k3 * jnp.exp(pk3)).astype(bf16)
    qm = (q3 * jnp.exp(qm3)).astype(bf16)
    km = (k3 * jnp.exp(km3)).astype(bf16)
    etot = jnp.exp(tot3)
    v = v_ref[:, sl]
    ri = lax.broadcasted_iota(jnp.int32, (C, C), 0)
    ci = lax.broadcasted_iota(jnp.int32, (C, C), 1)
    keep = (ci > ri) if reverse else (ci <= ri)

    st = st_ref[hh]
    outs = [None] * nc
    for c in (range(nc - 1, -1, -1) if reverse else range(nc)):
        vc = v[c * C:(c + 1) * C, :]
        att = lax.dot_general(qm[c], km[c], (((1,), (1,)), ((), ())), preferred_element_type=f32)
        att = jnp.where(keep, att, 0.0).astype(bf16)
        o_c = jnp.dot(att, vc, preferred_element_type=f32)
        o_c = o_c + lax.dot_general(qe[c], st.astype(bf16), (((1,), (1,)), ((), ())), preferred_element_type=f32)
        upd = lax.dot_general(vc, kd[c], (((0,), (0,)), ((), ())), preferred_element_type=f32)
        st = st * etot[c] + upd
        outs[c] = o_c
    st_ref[hh] = st
    o = jnp.concatenate(outs, axis=0)
    if reverse:
        o = _headnorm(o + of_ref[:, sl])
        o_ref[:, sl] = (o * _silu(gate_ref[:, sl].astype(f32))).astype(o_ref.dtype)
    else:
        o_ref[:, sl] = o


def _gla(h, seq_first, seq_last, wup, bup):
    n = h.shape[0]
    TB = SEQ_BLOCK
    nblk = n // TB
    H = GLA_HEADS

    def col(off):
        return off // LANE

    HW = H * LANE

    def specs(reverse):
        blk = (lambda s: nblk - 1 - s) if reverse else (lambda s: s)
        sp = [pl.BlockSpec((TB, HW), lambda s, f: (blk(s), QK_OFF // HW)),
              pl.BlockSpec((TB, HW), lambda s, f: (blk(s), AV_OFF // HW)),
              pl.BlockSpec((TB, LANE), lambda s, f: (blk(s), col(LR_OFF))),
              pl.BlockSpec((None, H, LANE, GLA_DK), lambda s, f: (1 if reverse else 0, 0, 0, 0)),
              pl.BlockSpec((None, H, 1, GLA_DK), lambda s, f: (1 if reverse else 0, 0, 0, 0))]
        out = pl.BlockSpec((TB, HW), lambda s, f: (blk(s), 0))
        return sp, out, blk

    scratch = [pltpu.VMEM((H, GLA_DV, GLA_DK), jnp.float32)]
    sp, out, _ = specs(False)
    o_f = pl.pallas_call(
        functools.partial(_gla_kernel, reverse=False, nblk=nblk), name="gla_fwd",
        grid_spec=pltpu.PrefetchScalarGridSpec(num_scalar_prefetch=1, grid=(nblk,), in_specs=sp, out_specs=out,
                                               scratch_shapes=scratch),
        out_shape=jax.ShapeDtypeStruct((n, H * GLA_DV), jnp.float32),
        compiler_params=_cparams("arbitrary"),
    )(seq_first, h, h, h, wup, bup)
    sp, out, blk = specs(True)
    sp = sp + [pl.BlockSpec((TB, HW), lambda s, f: (blk(s), AG_OFF // HW)), out]
    return pl.pallas_call(
        functools.partial(_gla_kernel, reverse=True, nblk=nblk), name="gla_bwd",
        grid_spec=pltpu.PrefetchScalarGridSpec(num_scalar_prefetch=1, grid=(nblk,), in_specs=sp, out_specs=out,
                                               scratch_shapes=scratch),
        out_shape=jax.ShapeDtypeStruct((n, H * GLA_DV), jnp.bfloat16),
        compiler_params=_cparams("arbitrary"),
    )(seq_last, h, h, h, wup, bup, h, o_f)


def _pack_gla_up(w_up, b_up):
    H, dk, r = GLA_HEADS, GLA_DK, GLA_RANK
    w = w_up.reshape(2, r, H, dk).transpose(0, 2, 1, 3)
    wp = jnp.zeros((2, H, LANE, dk), jnp.float32)
    wp = wp.at[0, :, 0:r].set(w[0]).at[1, :, r:2 * r].set(w[1])
    return wp.astype(jnp.bfloat16), b_up.reshape(2, H, 1, dk).astype(jnp.float32)


def _rope_tables(pos, dim, theta, width):
    half = dim // 2
    T = pos.shape[0]
    inv = 1.0 / (theta ** (jnp.arange(half, dtype=jnp.float32) / half))
    ang = pos.astype(jnp.float32)[:, None] * inv[None, :]
    c, s = jnp.cos(ang), jnp.sin(ang)
    rest = width - dim
    cos = jnp.concatenate([c, c, jnp.ones((T, rest), jnp.float32)], axis=1)
    sin_lo = jnp.concatenate([-s, jnp.zeros((T, half + rest), jnp.float32)], axis=1)
    sin_hi = jnp.concatenate([jnp.zeros((T, half), jnp.float32), s, jnp.zeros((T, rest), jnp.float32)], axis=1)
    return cos, sin_lo, sin_hi


def _ret_kernel(flag_ref, pos_ref, q_ref, k_ref, v_ref, cos_ref, sin_ref, dec_ref, *rest, reverse, nblk):
    s_idx = pl.program_id(0)
    blk = nblk - 1 - s_idx if reverse else s_idx

    @pl.when(flag_ref[blk] == 1)
    def _():
        st_ref = rest[-1] if reverse else rest[-2]
        st_ref[...] = jnp.zeros_like(st_ref)

    for hh in range(RET_HEADS):
        _ret_head(hh, s_idx, q_ref, k_ref, v_ref, cos_ref, sin_ref, dec_ref, *rest, reverse=reverse)


def _ret_head(hh, s_idx, q_ref, k_ref, v_ref, cos_ref, sin_ref, dec_ref, *rest, reverse):
    if reverse:
        gate_ref, o1_ref, o_ref, st_ref = rest
    else:
        o_ref, st_ref, dm_ref = rest
    f32, bf16 = jnp.float32, jnp.bfloat16
    TB = q_ref.shape[0]
    sl = slice(hh * RET_DH, (hh + 1) * RET_DH)

    dec = dec_ref[hh]
    lg = -(jnp.maximum(dec, 0.0) + jnp.log1p(jnp.exp(-jnp.abs(dec))))
    lg_f, lg_b = lg[0, 0:1, 0:1], lg[1, 0:1, 0:1]

    cos, sin = cos_ref[...], sin_ref[...]

    def rot(x):
        return x * cos + pltpu.roll(x, RET_DH // 2, axis=1) * sin

    qr = rot(q_ref[:, sl].astype(f32))
    kr = rot(k_ref[:, sl].astype(f32)) * (RET_DH ** -0.5)
    v = v_ref[:, sl]
    idx = lax.broadcasted_iota(jnp.int32, (TB, 1), 0).astype(f32)
    st = st_ref[hh]
    cross = jnp.dot(qr.astype(bf16), st.astype(bf16), preferred_element_type=f32)
    if reverse:
        cross = cross * jnp.exp(lg_b * (TB - idx))
        kz = (kr * jnp.exp(lg_b * idx)).astype(bf16)
        st_ref[hh] = st * jnp.exp(lg_b * TB) + lax.dot_general(kz, v, (((0,), (0,)), ((), ())),
                                                               preferred_element_type=f32)
        o = _headnorm(o1_ref[:, sl] + cross)
        o_ref[:, sl] = (o * _silu(gate_ref[:, sl].astype(f32))).astype(o_ref.dtype)
    else:
        @pl.when(s_idx == 0)
        def _():
            ri = lax.broadcasted_iota(jnp.int32, (TB, TB), 0)
            ci = lax.broadcasted_iota(jnp.int32, (TB, TB), 1)
            d = (ri - ci).astype(f32)
            dm_ref[hh] = jnp.where(d >= 0, jnp.exp(lg_f * jnp.maximum(d, 0.0)), jnp.exp(lg_b * jnp.maximum(-d, 0.0)))

        cross = cross * jnp.exp(lg_f * (idx + 1.0))
        kz = (kr * jnp.exp(lg_f * (TB - 1.0 - idx))).astype(bf16)
        st_ref[hh] = st * jnp.exp(lg_f * TB) + lax.dot_general(kz, v, (((0,), (0,)), ((), ())),
                                                               preferred_element_type=f32)
        sc = lax.dot_general(qr.astype(bf16), kr.astype(bf16), (((1,), (1,)), ((), ())), preferred_element_type=f32)
        sc = (sc * dm_ref[hh]).astype(bf16)
        o_ref[:, sl] = jnp.dot(sc, v, preferred_element_type=f32) + cross


def _retention(h, seq_first, seq_last, pos_blk, decay, t_max):
    n = h.shape[0]
    TB = SEQ_BLOCK
    nblk = n // TB
    H, dh = RET_HEADS, RET_DH
    cos, sin_lo, sin_hi = _rope_tables(jnp.arange(t_max), dh, RET_THETA, dh)
    sin = sin_lo + sin_hi
    dec = jnp.broadcast_to(decay.astype(jnp.float32).T[:, :, None, None], (H, 2, 8, LANE))

    HW = H * dh

    def specs(reverse):
        blk = (lambda s: nblk - 1 - s) if reverse else (lambda s: s)
        tok = lambda off: pl.BlockSpec((TB, HW), lambda s, f, p: (blk(s), off // HW))
        tab = pl.BlockSpec((TB, LANE), lambda s, f, p: (p[blk(s)], 0))
        sp = [tok(DQ_OFF), tok(DK_OFF), tok(DV_OFF), tab, tab,
              pl.BlockSpec((H, 2, 8, LANE), lambda s, f, p: (0, 0, 0, 0))]
        return sp, tok, pl.BlockSpec((TB, HW), lambda s, f, p: (blk(s), 0))

    sp, _, out = specs(False)
    o1 = pl.pallas_call(
        functools.partial(_ret_kernel, reverse=False, nblk=nblk), name="ret_fwd",
        grid_spec=pltpu.PrefetchScalarGridSpec(
            num_scalar_prefetch=2, grid=(nblk,), in_specs=sp, out_specs=out,
            scratch_shapes=[pltpu.VMEM((H, dh, dh), jnp.float32), pltpu.VMEM((H, TB, TB), jnp.float32)]),
        out_shape=jax.ShapeDtypeStruct((n, H * dh), jnp.float32),
        compiler_params=_cparams("arbitrary"),
    )(seq_first, pos_blk, h, h, h, cos, sin, dec)
    sp, tok, out = specs(True)
    sp = sp + [tok(DG_OFF), out]
    return pl.pallas_call(
        functools.partial(_ret_kernel, reverse=True, nblk=nblk), name="ret_bwd",
        grid_spec=pltpu.PrefetchScalarGridSpec(
            num_scalar_prefetch=2, grid=(nblk,), in_specs=sp, out_specs=out,
            scratch_shapes=[pltpu.VMEM((H, dh, dh), jnp.float32)]),
        out_shape=jax.ShapeDtypeStruct((n, H * dh), jnp.bfloat16),
        compiler_params=_cparams("arbitrary"),
    )(seq_last, pos_blk, h, h, h, cos, sin, dec, h, o1)


SWA_Q = 128
SWA_HALF = 64
assert all(w // (2 * d) == SWA_HALF for w, d in SWA_GROUPS)
NEG_BIG = -1e30


def _swa_kernel(hl_ref, hr_ref, q_ref, kl_ref, kc_ref, kr_ref, vl_ref, vc_ref, vr_ref,
                qcos_ref, qs1_ref, qs2_ref, kcos_ref, ks1_ref, ks2_ref, o_ref, lse_ref):
    f32, bf16 = jnp.float32, jnp.bfloat16
    j = pl.program_id(1)
    Q, HW = SWA_Q, SWA_HALF
    W = Q + 2 * HW
    ri = lax.broadcasted_iota(jnp.int32, (Q, W), 0)
    ci = lax.broadcasted_iota(jnp.int32, (Q, W), 1)
    rel = ci - HW - ri
    ok = (rel <= HW) & (rel >= -HW)
    ok = ok & ((ci >= HW) | (hl_ref[j] == 1)) & ((ci < HW + Q) | (hr_ref[j] == 1))
    lane = lax.broadcasted_iota(jnp.int32, (Q, LANE), 1)

    def rot(x, c_ref, s1_ref, s2_ref):
        return (x * c_ref[...] + pltpu.roll(x, LANE - ROPE_DIM // 2, axis=1) * s1_ref[...]
                + pltpu.roll(x, ROPE_DIM // 2, axis=1) * s2_ref[...])

    lse_tile = jnp.zeros((Q, LANE), f32)
    for hh in range(SWA_HEADS):
        sl = slice(hh * SWA_DH, (hh + 1) * SWA_DH)
        q = rot(q_ref[:, sl].astype(f32), qcos_ref, qs1_ref, qs2_ref) * (SWA_DH ** -0.5)
        k = jnp.concatenate([kl_ref[:, sl], kc_ref[:, sl], kr_ref[:, sl]], axis=0).astype(f32)
        k = rot(k, kcos_ref, ks1_ref, ks2_ref)
        v = jnp.concatenate([vl_ref[:, sl], vc_ref[:, sl], vr_ref[:, sl]], axis=0)
        s = lax.dot_general(q.astype(bf16), k.astype(bf16), (((1,), (1,)), ((), ())), preferred_element_type=f32)
        s = jnp.where(ok, s, NEG_BIG)
        m = jnp.max(s, -1, keepdims=True)
        p = jnp.exp(s - m)
        den = jnp.sum(p, -1, keepdims=True)
        o = jnp.dot(p.astype(bf16), v, preferred_element_type=f32) / den
        o_ref[:, sl] = o
        lse_tile = jnp.where(lane == hh, m + jnp.log(den), lse_tile)
    lse_ref[...] = lse_tile


def _swa_merge_kernel(*refs):
    ng = len(SWA_GROUPS)
    o_refs, l_refs, y_ref = refs[:ng], refs[ng:2 * ng], refs[2 * ng]
    scratch = refs[2 * ng + 1:]
    assert SWA_DH == LANE
    os_, ls_ = [], []
    si = 0
    for gi, (_, dil) in enumerate(SWA_GROUPS):
        if dil == 1:
            os_.append(lambda hh, r=o_refs[gi]: r[0, :, hh * LANE:(hh + 1) * LANE])
            ls_.append(l_refs[gi].at[0])
            continue
        o_s, l_s = scratch[si], scratch[si + 1]
        si += 2
        rows = l_s.shape[0] // dil
        for r in range(dil):
            for hh in range(SWA_HEADS):
                o_s[hh, pl.ds(r, rows, stride=dil), :] = o_refs[gi][r, :, hh * LANE:(hh + 1) * LANE]
            l_s[pl.ds(r, rows, stride=dil), :] = l_refs[gi][r]
        os_.append(lambda hh, s=o_s: s[hh])
        ls_.append(l_s)
    for hh in range(SWA_HEADS):
        ls = [r[:, hh:hh + 1] for r in ls_]
        m = functools.reduce(jnp.maximum, ls)
        ws = [jnp.exp(l - m) for l in ls]
        tot = functools.reduce(lambda a, b: a + b, ws)
        y = functools.reduce(lambda a, b: a + b, [w * o(hh) for w, o in zip(ws, os_)])
        y_ref[:, hh * LANE:(hh + 1) * LANE] = (y / tot).astype(y_ref.dtype)


def _proj_strided_kernel(a_ref, b_ref, o_ref, acc_ref, *, dil):
    acc = jnp.dot(a_ref[...], b_ref[...], preferred_element_type=jnp.float32)
    if dil == 1:
        o_ref[0] = acc.astype(o_ref.dtype)
        return
    rows = acc_ref.shape[1] // dil
    for s in range(acc_ref.shape[0]):
        acc_ref[s] = acc[:, s * LANE:(s + 1) * LANE]
        for r in range(dil):
            o_ref[r, :, s * LANE:(s + 1) * LANE] = acc_ref[s, pl.ds(r, rows, stride=dil), :].astype(o_ref.dtype)


def _proj_strided(a, b, dil, bm=1024, bn=512):
    n, K = a.shape
    N = b.shape[1]
    return pl.pallas_call(
        functools.partial(_proj_strided_kernel, dil=dil), name=f"swa_proj_dil{dil}", grid=(n // bm, N // bn),
        in_specs=[pl.BlockSpec((bm, K), lambda i, j: (i, 0)), pl.BlockSpec((K, bn), lambda i, j: (0, j))],
        out_specs=pl.BlockSpec((dil, bm // dil, bn), lambda i, j: (0, i, j)),
        out_shape=jax.ShapeDtypeStruct((dil, n // dil, N), jnp.bfloat16),
        scratch_shapes=[pltpu.VMEM((bn // LANE, bm, LANE), jnp.float32)],
        compiler_params=_cparams("parallel", "arbitrary"),
    )(a, b)


def _swa(xb, w_groups, seqs):
    n = xb.shape[0]
    Q, HW = SWA_Q, SWA_HALF
    GW = SWA_HEADS * SWA_DH
    outs, lses = [], []
    for gi, (window, dil) in enumerate(SWA_GROUPS):
        hs = _proj_strided(xb, w_groups[gi], dil)
        rows = n // dil
        nblk = rows // Q
        has_l, has_r = [], []
        for num, T in seqs:
            per = T // dil // Q
            assert per * Q * dil == T
            has_l += ([0] + [1] * (per - 1)) * num
            has_r += ([1] * (per - 1) + [0]) * num
        has_l = jnp.asarray(np.array(has_l, np.int32))
        has_r = jnp.asarray(np.array(has_r, np.int32))
        qpos = jnp.arange(Q) * dil
        kpos = (jnp.arange(Q + 2 * HW) - HW) * dil
        qt = _rope_tables(qpos, ROPE_DIM, ROPE_THETA, SWA_DH)
        kt = _rope_tables(kpos, ROPE_DIM, ROPE_THETA, SWA_DH)
        last64 = rows // HW - 1

        def center(c):
            return pl.BlockSpec((None, Q, GW), lambda r, j, hl, hr: (r, j, c))

        def left(c):
            return pl.BlockSpec((None, HW, GW), lambda r, j, hl, hr: (r, jnp.maximum(2 * j - 1, 0), c))

        def right(c):
            return pl.BlockSpec((None, HW, GW), lambda r, j, hl, hr: (r, jnp.minimum(2 * j + 2, last64), c))

        qtab = pl.BlockSpec((Q, SWA_DH), lambda r, j, hl, hr: (0, 0))
        ktab = pl.BlockSpec((Q + 2 * HW, SWA_DH), lambda r, j, hl, hr: (0, 0))
        o_g, lse_g = pl.pallas_call(
            _swa_kernel, name=f"swa_dil{dil}",
            grid_spec=pltpu.PrefetchScalarGridSpec(
                num_scalar_prefetch=2, grid=(dil, nblk),
                in_specs=[center(0), left(1), center(1), right(1), left(2), center(2), right(2),
                          qtab, qtab, qtab, ktab, ktab, ktab],
                out_specs=[pl.BlockSpec((None, Q, GW), lambda r, j, hl, hr: (r, j, 0)),
                           pl.BlockSpec((None, Q, LANE), lambda r, j, hl, hr: (r, j, 0))]),
            out_shape=[jax.ShapeDtypeStruct((dil, rows, GW), jnp.float32),
                       jax.ShapeDtypeStruct((dil, rows, LANE), jnp.float32)],
            compiler_params=_cparams("parallel", "parallel"),
        )(has_l, has_r, hs, hs, hs, hs, hs, hs, hs, *qt, *kt)
        outs.append(o_g)
        lses.append(lse_g)
    bm = 1024
    dils = [d for _, d in SWA_GROUPS]
    osp = [pl.BlockSpec((d, bm // d, GW), lambda i: (0, i, 0)) for d in dils]
    lsp = [pl.BlockSpec((d, bm // d, LANE), lambda i: (0, i, 0)) for d in dils]
    scratch = []
    for d in dils:
        if d > 1:
            scratch += [pltpu.VMEM((SWA_HEADS, bm, LANE), jnp.float32), pltpu.VMEM((bm, LANE), jnp.float32)]
    return pl.pallas_call(
        _swa_merge_kernel, name="swa_merge", grid=(n // bm,), in_specs=osp + lsp,
        out_specs=pl.BlockSpec((bm, GW), lambda i: (i, 0)),
        out_shape=jax.ShapeDtypeStruct((n, GW), jnp.bfloat16), scratch_shapes=scratch,
        compiler_params=_cparams("parallel"),
    )(*outs, *lses)


FFT_T2 = 128
FW = FNET_GROUPS * FNET_DG


def _dft_cos_sin(n_out, n_in, period):
    ang = (2.0 * np.pi / period) * np.mod(np.outer(np.arange(n_out), np.arange(n_in)), period)
    return np.cos(ang), np.sin(ang)


def _fnet_stage1_kernel(z_ref, m_ref, a_ref, *, cols):
    mat = m_ref[...]
    for c in range(cols):
        zr = z_ref[:, c * 2 * FW:c * 2 * FW + FW]
        zi = z_ref[:, c * 2 * FW + FW:(c + 1) * 2 * FW]
        a = jnp.dot(mat, jnp.concatenate([zr, zi], axis=0), preferred_element_type=jnp.float32)
        t1 = zr.shape[0]
        a_ref[:, c * 2 * FW:c * 2 * FW + FW] = a[:t1].astype(a_ref.dtype)
        a_ref[:, c * 2 * FW + FW:(c + 1) * 2 * FW] = a[t1:].astype(a_ref.dtype)


FNET_KB = 8


def _fnet_stage2_kernel(a_ref, g_ref, y_ref, *, scale):
    kw = g_ref.shape[1] // FNET_KB
    for j in range(FNET_KB):
        y = jnp.dot(g_ref[:, j * kw:(j + 1) * kw], a_ref[j], preferred_element_type=jnp.float32)
        y_ref[:, j * FW:(j + 1) * FW] = (y * scale).astype(y_ref.dtype)


def _fnet(h, seqs):
    n = h.shape[0]
    bf16 = jnp.bfloat16
    T2 = FFT_T2
    cc, ss = _dft_cos_sin(FNET_DG, FNET_DG, FNET_DG)
    eye = np.eye(FNET_GROUPS)
    wc = jnp.asarray(np.concatenate([np.kron(eye, cc), -np.kron(eye, ss)], axis=1), bf16)
    z = _matmul(h, wc, bf16, bn=2 * FW, a_col_blk=C_OFF // FW, name="fnet_channel_dft")
    zz = z.reshape(n // T2, T2 * 2 * FW)
    outs = []
    row0 = 0
    for num, T in seqs:
        T1 = T // T2
        assert T1 * T2 == T and T1 % 16 == 0
        c1, s1 = _dft_cos_sin(T1, T1, T1)
        mat = jnp.asarray(np.block([[c1, s1], [-s1, c1]]), bf16)
        cols = min(T2, max(1, 1024 // T1))
        rb0 = row0 // T1
        a = pl.pallas_call(
            functools.partial(_fnet_stage1_kernel, cols=cols), name=f"fnet_dft_t1_{T1}", grid=(num, T2 // cols),
            in_specs=[pl.BlockSpec((T1, cols * 2 * FW), lambda b, j: (rb0 + b, j)),
                      pl.BlockSpec((2 * T1, 2 * T1), lambda b, j: (0, 0))],
            out_specs=pl.BlockSpec((T1, cols * 2 * FW), lambda b, j: (b, j)),
            out_shape=jax.ShapeDtypeStruct((num * T1, T2 * 2 * FW), bf16),
            compiler_params=_cparams("parallel", "parallel"),
        )(zz, mat)
        ec, es = _dft_cos_sin(T, T2, T)
        g = jnp.asarray(np.stack([ec, es], axis=-1).reshape(T2, T1 * 2 * T2), bf16)
        y = pl.pallas_call(
            functools.partial(_fnet_stage2_kernel, scale=float(1.0 / np.sqrt(T * FNET_DG))), name=f"fnet_dft_t2_{T1}",
            grid=(num, T1 // FNET_KB),
            in_specs=[pl.BlockSpec((FNET_KB, 2 * T2, FW), lambda b, k1: (b * (T1 // FNET_KB) + k1, 0, 0)),
                      pl.BlockSpec((T2, FNET_KB * 2 * T2), lambda b, k1: (0, k1))],
            out_specs=pl.BlockSpec((T2, FNET_KB * FW), lambda b, k1: (b, k1)),
            out_shape=jax.ShapeDtypeStruct((num * T2, T1 * FW), bf16),
            compiler_params=_cparams("parallel", "parallel"),
        )(a.reshape(num * T1, 2 * T2, FW), g)
        outs.append(y.reshape(num * T, FW))
        row0 += num * T1
    return jnp.concatenate(outs, axis=0)


TOK_RADIX = 256


def _ffn_up_kernel(x_ref, w1_ref, w3_ref, h_ref, w1b_ref, w3b_ref):
    @pl.when(pl.program_id(2) == 0)
    def _():
        w1b_ref[...] = w1_ref[...].astype(w1b_ref.dtype)
        w3b_ref[...] = w3_ref[...].astype(w3b_ref.dtype)

    x = x_ref[...]
    h1 = jnp.dot(x, w1b_ref[...], preferred_element_type=jnp.float32)
    h3 = jnp.dot(x, w3b_ref[...], preferred_element_type=jnp.float32)
    h_ref[...] = (_silu(h1) * h3).astype(h_ref.dtype)


def _ffn_down_kernel(h_ref, w2_ref, g_ref, t_ref, o_ref, w2b_ref):
    @pl.when(pl.program_id(1) == 0)
    def _():
        w2b_ref[...] = w2_ref[...].astype(w2b_ref.dtype)

    D = w2_ref.shape[1]
    y = jnp.dot(h_ref[...], w2b_ref[...], preferred_element_type=jnp.float32)
    o_ref[:, :D] = (y * g_ref[...]).astype(o_ref.dtype)
    tok = t_ref[...]
    lane = lax.broadcasted_iota(jnp.int32, (tok.shape[0], LANE), 1)
    digits = jnp.where(lane == 0, tok // TOK_RADIX, jnp.where(lane == 1, tok % TOK_RADIX, 0))
    o_ref[:, D:] = digits.astype(jnp.float32).astype(o_ref.dtype)


def _expert_ffn(xe, w1, w3, w2, gate, tok, layer, bm=1024, bf=512):
    E, R, D = xe.shape
    F = w1.shape[-1]
    wspec = pl.BlockSpec((None, None, D, bf), lambda e, f, i: (layer, e, 0, f))
    hmid = pl.pallas_call(
        _ffn_up_kernel, name="expert_ffn_up", grid=(E, F // bf, R // bm),
        in_specs=[pl.BlockSpec((None, bm, D), lambda e, f, i: (e, i, 0)), wspec, wspec],
        out_specs=pl.BlockSpec((None, bm, bf), lambda e, f, i: (e, i, f)),
        out_shape=jax.ShapeDtypeStruct((E, R, F), jnp.bfloat16),
        scratch_shapes=[pltpu.VMEM((D, bf), jnp.bfloat16), pltpu.VMEM((D, bf), jnp.bfloat16)],
        compiler_params=_cparams("arbitrary", "arbitrary", "arbitrary"),
    )(xe, w1, w3)
    col = pl.BlockSpec((None, bm, 1), lambda e, i: (e, i, 0))
    return pl.pallas_call(
        _ffn_down_kernel, name="expert_ffn_down", grid=(E, R // bm),
        in_specs=[pl.BlockSpec((None, bm, F), lambda e, i: (e, i, 0)),
                  pl.BlockSpec((None, None, F, D), lambda e, i: (layer, e, 0, 0), pipeline_mode=pl.Buffered(1)),
                  col, col],
        out_specs=pl.BlockSpec((None, bm, D + LANE), lambda e, i: (e, i, 0)),
        out_shape=jax.ShapeDtypeStruct((E, R, D + LANE), jnp.bfloat16),
        scratch_shapes=[pltpu.VMEM((F, D), jnp.bfloat16)],
        compiler_params=_cparams("arbitrary", "arbitrary"),
    )(hmid, w2, gate, tok)


COMB_BM = 128
COMB_G = 16
COMB_K = 256


def _combine_kernel(lo_ref, x_ref, g_ref, b_ref, ye_hbm, *rest, nblk_all, blk_off, want_bf16):
    if want_bf16:
        xf_ref, xb_ref, buf_ref, acc_ref, sem = rest
    else:
        xf_ref, buf_ref, acc_ref, sem = rest
    f32, bf16 = jnp.float32, jnp.bfloat16
    E = ye_hbm.shape[0]
    D = x_ref.shape[1]
    BM, G, KC = COMB_BM, COMB_G, COMB_K
    i = pl.program_id(0)
    nsteps = pl.num_programs(0)
    blk = i + blk_off
    slot = i % 2

    @pl.when(i == 0)
    def _():
        buf_ref[...] = jnp.zeros_like(buf_ref)

    def granule_copy(sl, e, src_row, dst_row):
        return pltpu.make_async_copy(ye_hbm.at[e, pl.ds(src_row, G)], buf_ref.at[sl, pl.ds(dst_row, G)], sem.at[sl])

    def gather_block(b, sl, start):
        off = jnp.int32(0)
        for e in range(E):
            lo = lo_ref[e * (nblk_all + 1) + b]
            hi = lo_ref[e * (nblk_all + 1) + b + 1]
            lo_al = (lo // G) * G
            ng = jnp.where(hi > lo, (hi - lo_al + G - 1) // G, 0)
            if start:
                def issue(g, carry, e=e, lo_al=lo_al, off=off):
                    granule_copy(sl, e, pl.multiple_of(lo_al + g * G, G), pl.multiple_of(off + g * G, G)).start()
                    return carry

                lax.fori_loop(0, ng, issue, 0)
            off = off + ng * G
        return off

    @pl.when(i == 0)
    def _():
        gather_block(blk, slot, True)

    @pl.when(i + 1 < nsteps)
    def _():
        gather_block(blk + 1, 1 - slot, True)

    off = gather_block(blk, slot, False)

    def wait_one(g, carry):
        granule_copy(slot, 0, 0, 0).wait()
        return carry

    lax.fori_loop(0, off // G, wait_one, 0)

    acc_ref[...] = jnp.zeros_like(acc_ref)
    t0 = (blk * BM).astype(f32)
    lane_tok = lax.broadcasted_iota(jnp.int32, (KC, BM), 1).astype(f32)
    row = lax.broadcasted_iota(jnp.int32, (KC, 1), 0)

    def chunk(c, carry):
        win = buf_ref[slot, pl.ds(pl.multiple_of(c * KC, KC), KC), :]
        tok = win[:, D:D + 1].astype(f32) * TOK_RADIX + win[:, D + 1:D + 2].astype(f32)
        tok = jnp.where(row + c * KC < off, tok, -1.0)
        onehot_t = (tok - t0 == lane_tok).astype(bf16)
        acc_ref[...] += lax.dot_general(onehot_t, win[:, :D], (((0,), (0,)), ((), ())), preferred_element_type=f32)
        return carry

    lax.fori_loop(0, (off + KC - 1) // KC, chunk, 0)
    y = _ln_rows(DN_ALPHA * x_ref[...] + acc_ref[...], g_ref[...], b_ref[...])
    xf_ref[...] = y
    if want_bf16:
        xb_ref[...] = y.astype(bf16)


def _combine_ln(x, ye, lo, g, b, row_off, rows, want_bf16):
    n, D = x.shape
    E, R, DY = ye.shape
    BM = COMB_BM
    nblk_all = n // BM
    ob = row_off // BM
    kmax = -(-E * (BM + COMB_G) // COMB_K) * COMB_K
    irow = pl.BlockSpec((BM, D), lambda i, lo: (i + ob, 0))
    orow = pl.BlockSpec((BM, D), lambda i, lo: (i, 0))
    vec = pl.BlockSpec((1, D), lambda i, lo: (0, 0))
    out_shape = [jax.ShapeDtypeStruct((rows, D), jnp.float32)]
    out_specs = [orow]
    if want_bf16:
        out_shape.append(jax.ShapeDtypeStruct((rows, D), jnp.bfloat16))
        out_specs.append(orow)
    return pl.pallas_call(
        functools.partial(_combine_kernel, nblk_all=nblk_all, blk_off=ob, want_bf16=want_bf16), name="moe_combine_ln",
        grid_spec=pltpu.PrefetchScalarGridSpec(
            num_scalar_prefetch=1, grid=(rows // BM,),
            in_specs=[irow, vec, vec, pl.BlockSpec(memory_space=pl.ANY)], out_specs=out_specs,
            scratch_shapes=[pltpu.VMEM((2, kmax, DY), jnp.bfloat16), pltpu.VMEM((BM, D), jnp.float32),
                            pltpu.SemaphoreType.DMA((2,))]),
        out_shape=out_shape,
        compiler_params=_cparams("arbitrary"),
    )(lo, x, g.reshape(1, D), b.reshape(1, D), ye)


def _expert_choice(xb, aff, seqs, w1, w3, w2, layer):
    n, D = xb.shape
    aff_t = aff[:, :N_EXPERTS].T
    gates, idxs = [], []
    start = 0
    for num, T in seqs:
        m = num * T
        cap = max(1, CAPACITY_FACTOR * m // N_EXPERTS)
        gate, idx = lax.top_k(aff_t[:, start:start + m], cap)
        gates.append(gate)
        idxs.append(idx + start)
        start += m
    idx, gate = lax.sort((jnp.concatenate(idxs, axis=1), jnp.concatenate(gates, axis=1)), dimension=1, num_keys=1)
    bounds = jnp.arange(n // COMB_BM + 1, dtype=jnp.int32) * COMB_BM
    lo = jax.vmap(lambda row: jnp.searchsorted(row, bounds, side="left"))(idx).astype(jnp.int32).reshape(-1)
    ye = _expert_ffn(xb[idx], w1, w3, w2, gate[..., None], idx[..., None], layer)
    return ye, lo


def _pack_w_in(w_in):
    sizes = (GLA_HEADS * GLA_DK, GLA_HEADS * GLA_DK, GLA_HEADS * GLA_DV, GLA_HEADS * GLA_DV, 2 * GLA_RANK,
             SWA_NH * SWA_DH, SWA_NH * SWA_DH, SWA_NH * SWA_DH, FNET_GROUPS * FNET_DG,
             RET_HEADS * RET_DH, RET_HEADS * RET_DH, RET_HEADS * RET_DH, RET_HEADS * RET_DH)
    offs = np.concatenate([[0], np.cumsum(sizes)])
    part = lambda i: w_in[..., offs[i]:offs[i + 1]]
    a_q, a_k = part(0), part(1)
    cols = []
    for hh in range(GLA_HEADS):
        cols += [a_q[..., hh * GLA_DK:(hh + 1) * GLA_DK], a_k[..., hh * GLA_DK:(hh + 1) * GLA_DK]]
    cols += [part(2), part(3), part(4),
             jnp.zeros(w_in.shape[:2] + (C_OFF - LR_OFF - 2 * GLA_RANK,), w_in.dtype)]
    cols += [part(i) for i in range(8, 13)]
    out = jnp.concatenate(cols, axis=-1)
    assert out.shape[-1] == IN_COLS_P
    GW = SWA_HEADS * SWA_DH
    swa = [jnp.concatenate([part(i)[..., g * GW:(g + 1) * GW] for i in (5, 6, 7)], axis=-1)
           for g in range(len(SWA_GROUPS))]
    return out, swa


def _block_tables(seqs, blk):
    first, last, pos = [], [], []
    for num, T in seqs:
        per = T // blk
        assert per * blk == T
        for _ in range(num):
            first += [1] + [0] * (per - 1)
            last += [0] * (per - 1) + [1]
            pos += list(range(per))
    mk = lambda a: jnp.asarray(np.array(a, np.int32))
    return mk(first), mk(last), mk(pos)


def kernel(x_prompt, x_sample, ln_in_g, ln_in_b, w_in, gla_w_up, gla_b_up, ret_decay, w_gate, b_gate, w_branch,
           w_out, ln1_g, ln1_b, w_router, w_e1, w_e3, w_e2, ln2_g, ln2_b):
    bf16 = jnp.bfloat16
    D = D_MODEL
    seqs = (x_prompt.shape[:2], x_sample.shape[:2])
    n_p, n_s = x_prompt.shape[0] * x_prompt.shape[1], x_sample.shape[0] * x_sample.shape[1]
    n = n_p + n_s
    t_max = max(T for _, T in seqs)
    seq_first, seq_last, pos_blk = _block_tables(seqs, SEQ_BLOCK)

    w_in_p, w_swa = _pack_w_in(w_in.astype(bf16))
    wo = w_out.astype(bf16)
    wr = jnp.pad(w_router, ((0, 0), (0, 0), (0, LANE - N_EXPERTS))).astype(bf16)
    wg, wb, we1, we3, we2 = w_gate, w_branch, w_e1, w_e3, w_e2

    xf, xb = _ln_in(x_prompt.reshape(n_p, D), x_sample.reshape(n_s, D), ln_in_g, ln_in_b)
    for l in range(DEPTH):
        h = _matmul(xb, w_in_p[l], bf16, bm=1024, bn=IN_BN, name="in_proj")
        wup, bup = _pack_gla_up(gla_w_up[l], gla_b_up[l])
        ys = (_gla(h, seq_first, seq_last, wup, bup),
              _swa(xb, [w[l] for w in w_swa], seqs),
              _fnet(h, seqs),
              _retention(h, seq_first, seq_last, pos_blk, ret_decay[l], t_max))
        merged = _merge(xb, wg, b_gate[l], ys, wb, l)
        x1f, x1b, aff = _outproj(merged, wo[l], xf, ln1_g[l], ln1_b[l], wr[l])
        ye, lo = _expert_choice(x1b, aff, seqs, we1, we3, we2, l)
        if l + 1 < DEPTH:
            xf, xb = _combine_ln(x1f, ye, lo, ln2_g[l], ln2_b[l], 0, n, True)
        else:
            y_p, = _combine_ln(x1f, ye, lo, ln2_g[l], ln2_b[l], 0, n_p, False)
            y_s, = _combine_ln(x1f, ye, lo, ln2_g[l], ln2_b[l], n_p, n_s, False)
    return (y_p.reshape(x_prompt.shape), y_s.reshape(x_sample.shape))
```

```python
import functools

import jax
import jax.numpy as jnp
import numpy as np
from jax import lax
from jax.experimental import pallas as pl
from jax.experimental.pallas import tpu as pltpu

D_MODEL = 2048
DEPTH = 2
GLA_HEADS, GLA_DK, GLA_DV, GLA_RANK, GLA_TAU, GLA_CHUNK = 4, 64, 128, 16, 16.0, 64
SWA_GROUPS = ((128, 1), (512, 4), (2048, 16))
SWA_HEADS, SWA_DH = 4, 128
SWA_NH = SWA_HEADS * len(SWA_GROUPS)
SWA_BLOCK = 64
ROPE_THETA, ROPE_DIM = 500000.0, SWA_DH // 4
FNET_GROUPS, FNET_DG = 4, 128
RET_HEADS, RET_DH, RET_THETA = 4, 128, 10000.0
N_BRANCH, BRANCH_W = 4, 512
N_EXPERTS, EXPERT_FF, CAPACITY_FACTOR = 16, 2048, 2
DN_ALPHA = (2 * DEPTH) ** 0.25
LN_EPS = 1e-5

VMEM_LIMIT_BYTES = 48 * 1024 * 1024
LANE = 128

QK_OFF = 0
AV_OFF = QK_OFF + GLA_HEADS * LANE
AG_OFF = AV_OFF + GLA_HEADS * GLA_DV
LR_OFF = AG_OFF + GLA_HEADS * GLA_DV
C_OFF = LR_OFF + 4 * LANE
DQ_OFF = C_OFF + FNET_GROUPS * FNET_DG
DK_OFF = DQ_OFF + RET_HEADS * RET_DH
DV_OFF = DK_OFF + RET_HEADS * RET_DH
DG_OFF = DV_OFF + RET_HEADS * RET_DH
IN_USED = DG_OFF + RET_HEADS * RET_DH
IN_BN = 1536
IN_COLS_P = -(-IN_USED // IN_BN) * IN_BN
assert IN_COLS_P == IN_USED

SEQ_BLOCK = 512


def _cparams(*sem):
    return pltpu.CompilerParams(dimension_semantics=sem, vmem_limit_bytes=VMEM_LIMIT_BYTES)


def _mm_kernel(a_ref, b_ref, o_ref):
    o_ref[...] = jnp.dot(a_ref[...], b_ref[...], preferred_element_type=jnp.float32).astype(o_ref.dtype)


def _matmul(a, b, out_dtype, bm=1024, bn=512, a_col_blk=0, name="matmul"):
    batched = a.ndim == 3
    M = a.shape[-2]
    K, N = b.shape[-2:]
    bm = min(bm, M)
    bn = min(bn, N)
    assert M % bm == 0 and N % bn == 0, (M, N, bm, bn)
    if batched:
        E = a.shape[0]
        grid = (E, M // bm, N // bn)
        in_specs = [pl.BlockSpec((None, bm, K), lambda e, i, j: (e, i, 0)),
                    pl.BlockSpec((None, K, bn), lambda e, i, j: (e, 0, j))]
        out_specs = pl.BlockSpec((None, bm, bn), lambda e, i, j: (e, i, j))
        out_shape = jax.ShapeDtypeStruct((E, M, N), out_dtype)
        sem = ("parallel", "parallel", "arbitrary")
    else:
        grid = (M // bm, N // bn)
        in_specs = [pl.BlockSpec((bm, K), lambda i, j: (i, a_col_blk)),
                    pl.BlockSpec((K, bn), lambda i, j: (0, j))]
        out_specs = pl.BlockSpec((bm, bn), lambda i, j: (i, j))
        out_shape = jax.ShapeDtypeStruct((M, N), out_dtype)
        sem = ("parallel", "arbitrary")
    return pl.pallas_call(
        _mm_kernel, name=name, grid=grid, in_specs=in_specs, out_specs=out_specs, out_shape=out_shape,
        compiler_params=_cparams(*sem),
    )(a, b)


def _mm_w32_kernel(a_ref, w_ref, o_ref, wb_ref):
    @pl.when(pl.program_id(1) == 0)
    def _():
        wb_ref[...] = w_ref[...].astype(wb_ref.dtype)

    o_ref[...] = jnp.dot(a_ref[...], wb_ref[...], preferred_element_type=jnp.float32).astype(o_ref.dtype)


def _matmul_w32(a, w, out_dtype, bm=1024, bn=768, name="matmul_w32"):
    M, K = a.shape
    N = w.shape[1]
    assert M % bm == 0 and N % bn == 0, (M, N, bm, bn)
    return pl.pallas_call(
        _mm_w32_kernel, name=name, grid=(N // bn, M // bm),
        in_specs=[pl.BlockSpec((bm, K), lambda j, i: (i, 0)), pl.BlockSpec((K, bn), lambda j, i: (0, j))],
        out_specs=pl.BlockSpec((bm, bn), lambda j, i: (i, j)),
        out_shape=jax.ShapeDtypeStruct((M, N), out_dtype),
        scratch_shapes=[pltpu.VMEM((K, bn), jnp.bfloat16)],
        compiler_params=_cparams("arbitrary", "arbitrary"),
    )(a, w)


def _ln_rows(x, g, b):
    mu = jnp.mean(x, -1, keepdims=True)
    xc = x - mu
    var = jnp.mean(xc * xc, -1, keepdims=True)
    return xc * lax.rsqrt(var + LN_EPS) * g + b


def _ln_in_kernel(xa_ref, xc_ref, g_ref, b_ref, xf_ref, xb_ref, *, na_blocks):
    i = pl.program_id(0)

    def emit(x_ref):
        y = _ln_rows(x_ref[...], g_ref[...], b_ref[...])
        xf_ref[...] = y
        xb_ref[...] = y.astype(jnp.bfloat16)

    pl.when(i < na_blocks)(lambda: emit(xa_ref))
    pl.when(i >= na_blocks)(lambda: emit(xc_ref))


def _ln_in(xa, xc, g, b, bm=512):
    (na, D), nc = xa.shape, xc.shape[0]
    nab, ncb = na // bm, nc // bm
    row = pl.BlockSpec((bm, D), lambda i: (i, 0))
    vec = pl.BlockSpec((1, D), lambda i: (0, 0))
    return pl.pallas_call(
        functools.partial(_ln_in_kernel, na_blocks=nab), name="ln_in", grid=(nab + ncb,),
        in_specs=[pl.BlockSpec((bm, D), lambda i: (jnp.minimum(i, nab - 1), 0)),
                  pl.BlockSpec((bm, D), lambda i: (jnp.maximum(i - nab, 0), 0)), vec, vec],
        out_specs=[row, row],
        out_shape=[jax.ShapeDtypeStruct((na + nc, D), jnp.float32), jax.ShapeDtypeStruct((na + nc, D), jnp.bfloat16)],
        compiler_params=_cparams("arbitrary"))(xa, xc, g.reshape(1, D), b.reshape(1, D))


def _merge_kernel(xb_ref, wg_ref, bg_ref, ya_ref, yb_ref, yc_ref, yd_ref, wb_ref, o_ref, wgb_ref, wbb_ref):
    @pl.when(pl.program_id(1) == 0)
    def _():
        wgb_ref[...] = wg_ref[...].astype(wgb_ref.dtype)
        wbb_ref[...] = wb_ref[...].astype(wbb_ref.dtype)

    xb = xb_ref[...]
    acc = None
    for i, y_ref in enumerate((ya_ref, yb_ref, yc_ref, yd_ref)):
        gate = jax.nn.sigmoid(jnp.dot(xb, wgb_ref[i], preferred_element_type=jnp.float32) + bg_ref[i])
        term = gate * jnp.dot(y_ref[...], wbb_ref[i], preferred_element_type=jnp.float32)
        acc = term if acc is None else acc + term
    o_ref[...] = acc.astype(o_ref.dtype)


def _merge(xb, wg, bg, ys, wb, layer, bm=1024, bn=256):
    n, D = xb.shape
    W = ys[0].shape[1]
    y_spec = pl.BlockSpec((bm, W), lambda j, i: (i, 0))
    return pl.pallas_call(
        _merge_kernel, name="branch_merge", grid=(D // bn, n // bm),
        in_specs=[pl.BlockSpec((bm, D), lambda j, i: (i, 0)),
                  pl.BlockSpec((None, N_BRANCH, D, bn), lambda j, i: (layer, 0, 0, j)),
                  pl.BlockSpec((N_BRANCH, 1, bn), lambda j, i: (0, 0, j)),
                  y_spec, y_spec, y_spec, y_spec,
                  pl.BlockSpec((None, N_BRANCH, W, bn), lambda j, i: (layer, 0, 0, j))],
        out_specs=pl.BlockSpec((bm, bn), lambda j, i: (i, j)),
        out_shape=jax.ShapeDtypeStruct((n, D), jnp.bfloat16),
        scratch_shapes=[pltpu.VMEM((N_BRANCH, D, bn), jnp.bfloat16), pltpu.VMEM((N_BRANCH, W, bn), jnp.bfloat16)],
        compiler_params=_cparams("arbitrary", "arbitrary"),
    )(xb, wg, bg.reshape(N_BRANCH, 1, D), *ys, wb)


OUTPROJ_SUB = 128


def _outproj_kernel(m_ref, w_ref, x_ref, g_ref, b_ref, wr_ref, xf_ref, xb_ref, aff_ref):
    for s in range(m_ref.shape[0] // OUTPROJ_SUB):
        rows = pl.ds(s * OUTPROJ_SUB, OUTPROJ_SUB)
        r = jnp.dot(m_ref[rows, :], w_ref[...], preferred_element_type=jnp.float32)
        y = _ln_rows(DN_ALPHA * x_ref[rows, :] + r, g_ref[...], b_ref[...])
        yb = y.astype(jnp.bfloat16)
        xf_ref[rows, :] = y
        xb_ref[rows, :] = yb
        logits = jnp.dot(yb, wr_ref[...], preferred_element_type=jnp.float32)
        lane = lax.broadcasted_iota(jnp.int32, logits.shape, 1)
        logits = jnp.where(lane < N_EXPERTS, logits, -jnp.inf)
        e = jnp.exp(logits - jnp.max(logits, -1, keepdims=True))
        aff_ref[rows, :] = e / jnp.sum(e, -1, keepdims=True)


def _outproj(merged, w_out, x, g, b, w_router_p, bm=512):
    n, D = x.shape
    row = pl.BlockSpec((bm, D), lambda i: (i, 0))
    vec = pl.BlockSpec((1, D), lambda i: (0, 0))
    return pl.pallas_call(
        _outproj_kernel, name="outproj_ln_router", grid=(n // bm,),
        in_specs=[row, pl.BlockSpec((D, D), lambda i: (0, 0)), row, vec, vec,
                  pl.BlockSpec((D, LANE), lambda i: (0, 0))],
        out_specs=[row, row, pl.BlockSpec((bm, LANE), lambda i: (i, 0))],
        out_shape=[jax.ShapeDtypeStruct((n, D), jnp.float32), jax.ShapeDtypeStruct((n, D), jnp.bfloat16),
                   jax.ShapeDtypeStruct((n, LANE), jnp.float32)],
        compiler_params=_cparams("parallel"),
    )(merged, w_out, x, g.reshape(1, D), b.reshape(1, D), w_router_p)


def _log_sigmoid(z):
    return jnp.minimum(z, 0.0) - jnp.log(1.0 + jnp.exp(-jnp.abs(z)))


def _headnorm(o):
    mu = jnp.mean(o, -1, keepdims=True)
    oc = o - mu
    var = jnp.mean(oc * oc, -1, keepdims=True)
    return oc * lax.rsqrt(var + LN_EPS)


def _silu(x):
    return x * jax.nn.sigmoid(x)


def _chunk_cumsum(g, chunk, inclusive):
    row = lax.broadcasted_iota(jnp.int32, g.shape, 0) % chunk
    b = g
    s = 1
    while s < chunk:
        b = b + jnp.where(row >= s, pltpu.roll(b, s, axis=0), 0.0)
        s *= 2
    return b if inclusive else b - g


def _gla_kernel(flag_ref, qk_ref, v_ref, lr_ref, wup_ref, bup_ref, *rest, reverse, nblk):
    s_idx = pl.program_id(0)
    blk = nblk - 1 - s_idx if reverse else s_idx

    @pl.when(flag_ref[blk] == 1)
    def _():
        rest[-1][...] = jnp.zeros_like(rest[-1])

    for hh in range(GLA_HEADS):
        _gla_head(hh, qk_ref, v_ref, lr_ref, wup_ref, bup_ref, *rest, reverse=reverse)


def _gla_head(hh, qk_ref, v_ref, lr_ref, wup_ref, bup_ref, *rest, reverse):
    if reverse:
        gate_ref, of_ref, o_ref, st_ref = rest
    else:
        o_ref, st_ref = rest
    f32, bf16 = jnp.float32, jnp.bfloat16
    C = GLA_CHUNK
    TB = qk_ref.shape[0]
    nc = TB // C
    sl = slice(hh * LANE, (hh + 1) * LANE)

    qk = qk_ref[:, sl].astype(f32)
    q = qk[:, :GLA_DK] * (GLA_DK ** -0.5)
    k = qk[:, GLA_DK:]
    z = jnp.dot(lr_ref[...], wup_ref[hh], preferred_element_type=f32) + bup_ref[hh]
    g = _log_sigmoid(z) * (1.0 / GLA_TAU)
    cs = _chunk_cumsum(g, C, inclusive=not reverse)
    cs3 = cs.reshape(nc, C, GLA_DK)
    mid = cs3[:, C // 2:C // 2 + 1, :]
    if reverse:
        tot3 = cs3[:, C - 1:C, :] + g.reshape(nc, C, GLA_DK)[:, C - 1:C, :]
        pq3, pk3 = tot3 - cs3, cs3
        qm3, km3 = mid - cs3, cs3 - mid
    else:
        tot3 = cs3[:, C - 1:C, :]
        pq3, pk3 = cs3, tot3 - cs3
        qm3, km3 = cs3 - mid, mid - cs3
    q3 = q.reshape(nc, C, GLA_DK)
    k3 = k.reshape(nc, C, GLA_DK)
    qe = (q3 * jnp.exp(pq3)).astype(bf16)
    kd = (k3 * jnp.exp(pk3)).astype(bf16)
    qm = (q3 * jnp.exp(qm3)).astype(bf16)
    km = (k3 * jnp.exp(km3)).astype(bf16)
    etot = jnp.exp(tot3)
    v = v_ref[:, sl]
    ri = lax.broadcasted_iota(jnp.int32, (C, C), 0)
    ci = lax.broadcasted_iota(jnp.int32, (C, C), 1)
    keep = (ci > ri) if reverse else (ci <= ri)

    st = st_ref[hh]
    outs = [None] * nc
    for c in (range(nc - 1, -1, -1) if reverse else range(nc)):
        vc = v[c * C:(c + 1) * C, :]
        att = lax.dot_general(qm[c], km[c], (((1,), (1,)), ((), ())), preferred_element_type=f32)
        att = jnp.where(keep, att, 0.0).astype(bf16)
        o_c = jnp.dot(att, vc, preferred_element_type=f32)
        o_c = o_c + lax.dot_general(qe[c], st.astype(bf16), (((1,), (1,)), ((), ())), preferred_element_type=f32)
        upd = lax.dot_general(vc, kd[c], (((0,), (0,)), ((), ())), preferred_element_type=f32)
        st = st * etot[c] + upd
        outs[c] = o_c
    st_ref[hh] = st
    o = jnp.concatenate(outs, axis=0)
    if reverse:
        o = _headnorm(o + of_ref[:, sl])
        o_ref[:, sl] = (o * _silu(gate_ref[:, sl].astype(f32))).astype(o_ref.dtype)
    else:
        o_ref[:, sl] = o


def _gla(h, seq_first, seq_last, wup, bup):
    n = h.shape[0]
    TB = SEQ_BLOCK
    nblk = n // TB
    H = GLA_HEADS

    def col(off):
        return off // LANE

    HW = H * LANE

    def specs(reverse):
        blk = (lambda s: nblk - 1 - s) if reverse else (lambda s: s)
        sp = [pl.BlockSpec((TB, HW), lambda s, f: (blk(s), QK_OFF // HW)),
              pl.BlockSpec((TB, HW), lambda s, f: (blk(s), AV_OFF // HW)),
              pl.BlockSpec((TB, LANE), lambda s, f: (blk(s), col(LR_OFF))),
              pl.BlockSpec((None, H, LANE, GLA_DK), lambda s, f: (1 if reverse else 0, 0, 0, 0)),
              pl.BlockSpec((None, H, 1, GLA_DK), lambda s, f: (1 if reverse else 0, 0, 0, 0))]
        out = pl.BlockSpec((TB, HW), lambda s, f: (blk(s), 0))
        return sp, out, blk

    scratch = [pltpu.VMEM((H, GLA_DV, GLA_DK), jnp.float32)]
    sp, out, _ = specs(False)
    o_f = pl.pallas_call(
        functools.partial(_gla_kernel, reverse=False, nblk=nblk), name="gla_fwd",
        grid_spec=pltpu.PrefetchScalarGridSpec(num_scalar_prefetch=1, grid=(nblk,), in_specs=sp, out_specs=out,
                                               scratch_shapes=scratch),
        out_shape=jax.ShapeDtypeStruct((n, H * GLA_DV), jnp.float32),
        compiler_params=_cparams("arbitrary"),
    )(seq_first, h, h, h, wup, bup)
    sp, out, blk = specs(True)
    sp = sp + [pl.BlockSpec((TB, HW), lambda s, f: (blk(s), AG_OFF // HW)), out]
    return pl.pallas_call(
        functools.partial(_gla_kernel, reverse=True, nblk=nblk), name="gla_bwd",
        grid_spec=pltpu.PrefetchScalarGridSpec(num_scalar_prefetch=1, grid=(nblk,), in_specs=sp, out_specs=out,
                                               scratch_shapes=scratch),
        out_shape=jax.ShapeDtypeStruct((n, H * GLA_DV), jnp.bfloat16),
        compiler_params=_cparams("arbitrary"),
    )(seq_last, h, h, h, wup, bup, h, o_f)


def _pack_gla_up(w_up, b_up):
    H, dk, r = GLA_HEADS, GLA_DK, GLA_RANK
    w = w_up.reshape(2, r, H, dk).transpose(0, 2, 1, 3)
    wp = jnp.zeros((2, H, LANE, dk), jnp.float32)
    wp = wp.at[0, :, 0:r].set(w[0]).at[1, :, r:2 * r].set(w[1])
    return wp.astype(jnp.bfloat16), b_up.reshape(2, H, 1, dk).astype(jnp.float32)


def _rope_tables(pos, dim, theta, width):
    half = dim // 2
    T = pos.shape[0]
    inv = 1.0 / (theta ** (jnp.arange(half, dtype=jnp.float32) / half))
    ang = pos.astype(jnp.float32)[:, None] * inv[None, :]
    c, s = jnp.cos(ang), jnp.sin(ang)
    rest = width - dim
    cos = jnp.concatenate([c, c, jnp.ones((T, rest), jnp.float32)], axis=1)
    sin_lo = jnp.concatenate([-s, jnp.zeros((T, half + rest), jnp.float32)], axis=1)
    sin_hi = jnp.concatenate([jnp.zeros((T, half), jnp.float32), s, jnp.zeros((T, rest), jnp.float32)], axis=1)
    return cos, sin_lo, sin_hi


def _ret_kernel(flag_ref, pos_ref, q_ref, k_ref, v_ref, cos_ref, sin_ref, dec_ref, *rest, reverse, nblk):
    s_idx = pl.program_id(0)
    blk = nblk - 1 - s_idx if reverse else s_idx

    @pl.when(flag_ref[blk] == 1)
    def _():
        st_ref = rest[-1] if reverse else rest[-2]
        st_ref[...] = jnp.zeros_like(st_ref)

    for hh in range(RET_HEADS):
        _ret_head(hh, s_idx, q_ref, k_ref, v_ref, cos_ref, sin_ref, dec_ref, *rest, reverse=reverse)


def _ret_head(hh, s_idx, q_ref, k_ref, v_ref, cos_ref, sin_ref, dec_ref, *rest, reverse):
    if reverse:
        gate_ref, o1_ref, o_ref, st_ref = rest
    else:
        o_ref, st_ref, dm_ref = rest
    f32, bf16 = jnp.float32, jnp.bfloat16
    TB = q_ref.shape[0]
    sl = slice(hh * RET_DH, (hh + 1) * RET_DH)

    dec = dec_ref[hh]
    lg = -(jnp.maximum(dec, 0.0) + jnp.log1p(jnp.exp(-jnp.abs(dec))))
    lg_f, lg_b = lg[0, 0:1, 0:1], lg[1, 0:1, 0:1]

    cos, sin = cos_ref[...], sin_ref[...]

    def rot(x):
        return x * cos + pltpu.roll(x, RET_DH // 2, axis=1) * sin

    qr = rot(q_ref[:, sl].astype(f32))
    kr = rot(k_ref[:, sl].astype(f32)) * (RET_DH ** -0.5)
    v = v_ref[:, sl]
    idx = lax.broadcasted_iota(jnp.int32, (TB, 1), 0).astype(f32)
    st = st_ref[hh]
    cross = jnp.dot(qr.astype(bf16), st.astype(bf16), preferred_element_type=f32)
    if reverse:
        cross = cross * jnp.exp(lg_b * (TB - idx))
        kz = (kr * jnp.exp(lg_b * idx)).astype(bf16)
        st_ref[hh] = st * jnp.exp(lg_b * TB) + lax.dot_general(kz, v, (((0,), (0,)), ((), ())),
                                                               preferred_element_type=f32)
        o = _headnorm(o1_ref[:, sl] + cross)
        o_ref[:, sl] = (o * _silu(gate_ref[:, sl].astype(f32))).astype(o_ref.dtype)
    else:
        @pl.when(s_idx == 0)
        def _():
            ri = lax.broadcasted_iota(jnp.int32, (TB, TB), 0)
            ci = lax.broadcasted_iota(jnp.int32, (TB, TB), 1)
            d = (ri - ci).astype(f32)
            dm_ref[hh] = jnp.where(d >= 0, jnp.exp(lg_f * jnp.maximum(d, 0.0)), jnp.exp(lg_b * jnp.maximum(-d, 0.0)))

        cross = cross * jnp.exp(lg_f * (idx + 1.0))
        kz = (kr * jnp.exp(lg_f * (TB - 1.0 - idx))).astype(bf16)
        st_ref[hh] = st * jnp.exp(lg_f * TB) + lax.dot_general(kz, v, (((0,), (0,)), ((), ())),
                                                               preferred_element_type=f32)
        sc = lax.dot_general(qr.astype(bf16), kr.astype(bf16), (((1,), (1,)), ((), ())), preferred_element_type=f32)
        sc = (sc * dm_ref[hh]).astype(bf16)
        o_ref[:, sl] = jnp.dot(sc, v, preferred_element_type=f32) + cross


def _retention(h, seq_first, seq_last, pos_blk, decay, t_max):
    n = h.shape[0]
    TB = SEQ_BLOCK
    nblk = n // TB
    H, dh = RET_HEADS, RET_DH
    cos, sin_lo, sin_hi = _rope_tables(jnp.arange(t_max), dh, RET_THETA, dh)
    sin = sin_lo + sin_hi
    dec = jnp.broadcast_to(decay.astype(jnp.float32).T[:, :, None, None], (H, 2, 8, LANE))

    HW = H * dh

    def specs(reverse):
        blk = (lambda s: nblk - 1 - s) if reverse else (lambda s: s)
        tok = lambda off: pl.BlockSpec((TB, HW), lambda s, f, p: (blk(s), off // HW))
        tab = pl.BlockSpec((TB, LANE), lambda s, f, p: (p[blk(s)], 0))
        sp = [tok(DQ_OFF), tok(DK_OFF), tok(DV_OFF), tab, tab,
              pl.BlockSpec((H, 2, 8, LANE), lambda s, f, p: (0, 0, 0, 0))]
        return sp, tok, pl.BlockSpec((TB, HW), lambda s, f, p: (blk(s), 0))

    sp, _, out = specs(False)
    o1 = pl.pallas_call(
        functools.partial(_ret_kernel, reverse=False, nblk=nblk), name="ret_fwd",
        grid_spec=pltpu.PrefetchScalarGridSpec(
            num_scalar_prefetch=2, grid=(nblk,), in_specs=sp, out_specs=out,
            scratch_shapes=[pltpu.VMEM((H, dh, dh), jnp.float32), pltpu.VMEM((H, TB, TB), jnp.float32)]),
        out_shape=jax.ShapeDtypeStruct((n, H * dh), jnp.float32),
        compiler_params=_cparams("arbitrary"),
    )(seq_first, pos_blk, h, h, h, cos, sin, dec)
    sp, tok, out = specs(True)
    sp = sp + [tok(DG_OFF), out]
    return pl.pallas_call(
        functools.partial(_ret_kernel, reverse=True, nblk=nblk), name="ret_bwd",
        grid_spec=pltpu.PrefetchScalarGridSpec(
            num_scalar_prefetch=2, grid=(nblk,), in_specs=sp, out_specs=out,
            scratch_shapes=[pltpu.VMEM((H, dh, dh), jnp.float32)]),
        out_shape=jax.ShapeDtypeStruct((n, H * dh), jnp.bfloat16),
        compiler_params=_cparams("arbitrary"),
    )(seq_last, pos_blk, h, h, h, cos, sin, dec, h, o1)


SWA_Q = 128
SWA_HALF = 64
assert all(w // (2 * d) == SWA_HALF for w, d in SWA_GROUPS)
NEG_BIG = -1e30


def _swa_kernel(hl_ref, hr_ref, q_ref, kl_ref, kc_ref, kr_ref, vl_ref, vc_ref, vr_ref,
                qcos_ref, qs1_ref, qs2_ref, kcos_ref, ks1_ref, ks2_ref, o_ref, lse_ref):
    f32, bf16 = jnp.float32, jnp.bfloat16
    j = pl.program_id(1)
    Q, HW = SWA_Q, SWA_HALF
    W = Q + 2 * HW
    ri = lax.broadcasted_iota(jnp.int32, (Q, W), 0)
    ci = lax.broadcasted_iota(jnp.int32, (Q, W), 1)
    rel = ci - HW - ri
    ok = (rel <= HW) & (rel >= -HW)
    ok = ok & ((ci >= HW) | (hl_ref[j] == 1)) & ((ci < HW + Q) | (hr_ref[j] == 1))
    lane = lax.broadcasted_iota(jnp.int32, (Q, LANE), 1)

    def rot(x, c_ref, s1_ref, s2_ref):
        return (x * c_ref[...] + pltpu.roll(x, LANE - ROPE_DIM // 2, axis=1) * s1_ref[...]
                + pltpu.roll(x, ROPE_DIM // 2, axis=1) * s2_ref[...])

    lse_tile = jnp.zeros((Q, LANE), f32)
    for hh in range(SWA_HEADS):
        sl = slice(hh * SWA_DH, (hh + 1) * SWA_DH)
        q = rot(q_ref[:, sl].astype(f32), qcos_ref, qs1_ref, qs2_ref) * (SWA_DH ** -0.5)
        k = jnp.concatenate([kl_ref[:, sl], kc_ref[:, sl], kr_ref[:, sl]], axis=0).astype(f32)
        k = rot(k, kcos_ref, ks1_ref, ks2_ref)
        v = jnp.concatenate([vl_ref[:, sl], vc_ref[:, sl], vr_ref[:, sl]], axis=0)
        s = lax.dot_general(q.astype(bf16), k.astype(bf16), (((1,), (1,)), ((), ())), preferred_element_type=f32)
        s = jnp.where(ok, s, NEG_BIG)
        m = jnp.max(s, -1, keepdims=True)
        p = jnp.exp(s - m)
        den = jnp.sum(p, -1, keepdims=True)
        o = jnp.dot(p.astype(bf16), v, preferred_element_type=f32) / den
        o_ref[:, sl] = o
        lse_tile = jnp.where(lane == hh, m + jnp.log(den), lse_tile)
    lse_ref[...] = lse_tile


def _swa_merge_kernel(*refs):
    ng = len(SWA_GROUPS)
    o_refs, l_refs, y_ref = refs[:ng], refs[ng:2 * ng], refs[2 * ng]
    scratch = refs[2 * ng + 1:]
    assert SWA_DH == LANE
    os_, ls_ = [], []
    si = 0
    for gi, (_, dil) in enumerate(SWA_GROUPS):
        if dil == 1:
            os_.append(lambda hh, r=o_refs[gi]: r[0, :, hh * LANE:(hh + 1) * LANE])
            ls_.append(l_refs[gi].at[0])
            continue
        o_s, l_s = scratch[si], scratch[si + 1]
        si += 2
        rows = l_s.shape[0] // dil
        for r in range(dil):
            for hh in range(SWA_HEADS):
                o_s[hh, pl.ds(r, rows, stride=dil), :] = o_refs[gi][r, :, hh * LANE:(hh + 1) * LANE]
            l_s[pl.ds(r, rows, stride=dil), :] = l_refs[gi][r]
        os_.append(lambda hh, s=o_s: s[hh])
        ls_.append(l_s)
    for hh in range(SWA_HEADS):
        ls = [r[:, hh:hh + 1] for r in ls_]
        m = functools.reduce(jnp.maximum, ls)
        ws = [jnp.exp(l - m) for l in ls]
        tot = functools.reduce(lambda a, b: a + b, ws)
        y = functools.reduce(lambda a, b: a + b, [w * o(hh) for w, o in zip(ws, os_)])
        y_ref[:, hh * LANE:(hh + 1) * LANE] = (y / tot).astype(y_ref.dtype)


def _proj_strided_kernel(a_ref, b_ref, o_ref, acc_ref, *, dil):
    acc = jnp.dot(a_ref[...], b_ref[...], preferred_element_type=jnp.float32)
    if dil == 1:
        o_ref[0] = acc.astype(o_ref.dtype)
        return
    rows = acc_ref.shape[1] // dil
    for s in range(acc_ref.shape[0]):
        acc_ref[s] = acc[:, s * LANE:(s + 1) * LANE]
        for r in range(dil):
            o_ref[r, :, s * LANE:(s + 1) * LANE] = acc_ref[s, pl.ds(r, rows, stride=dil), :].astype(o_ref.dtype)


def _proj_strided(a, b, dil, bm=1024, bn=512):
    n, K = a.shape
    N = b.shape[1]
    return pl.pallas_call(
        functools.partial(_proj_strided_kernel, dil=dil), name=f"swa_proj_dil{dil}", grid=(n // bm, N // bn),
        in_specs=[pl.BlockSpec((bm, K), lambda i, j: (i, 0)), pl.BlockSpec((K, bn), lambda i, j: (0, j))],
        out_specs=pl.BlockSpec((dil, bm // dil, bn), lambda i, j: (0, i, j)),
        out_shape=jax.ShapeDtypeStruct((dil, n // dil, N), jnp.bfloat16),
        scratch_shapes=[pltpu.VMEM((bn // LANE, bm, LANE), jnp.float32)],
        compiler_params=_cparams("parallel", "arbitrary"),
    )(a, b)


def _swa(xb, w_groups, seqs):
    n = xb.shape[0]
    Q, HW = SWA_Q, SWA_HALF
    GW = SWA_HEADS * SWA_DH
    outs, lses = [], []
    for gi, (window, dil) in enumerate(SWA_GROUPS):
        hs = _proj_strided(xb, w_groups[gi], dil)
        rows = n // dil
        nblk = rows // Q
        has_l, has_r = [], []
        for num, T in seqs:
            per = T // dil // Q
            assert per * Q * dil == T
            has_l += ([0] + [1] * (per - 1)) * num
            has_r += ([1] * (per - 1) + [0]) * num
        has_l = jnp.asarray(np.array(has_l, np.int32))
        has_r = jnp.asarray(np.array(has_r, np.int32))
        qpos = jnp.arange(Q) * dil
        kpos = (jnp.arange(Q + 2 * HW) - HW) * dil
        qt = _rope_tables(qpos, ROPE_DIM, ROPE_THETA, SWA_DH)
        kt = _rope_tables(kpos, ROPE_DIM, ROPE_THETA, SWA_DH)
        last64 = rows // HW - 1

        def center(c):
            return pl.BlockSpec((None, Q, GW), lambda r, j, hl, hr: (r, j, c))

        def left(c):
            return pl.BlockSpec((None, HW, GW), lambda r, j, hl, hr: (r, jnp.maximum(2 * j - 1, 0), c))

        def right(c):
            return pl.BlockSpec((None, HW, GW), lambda r, j, hl, hr: (r, jnp.minimum(2 * j + 2, last64), c))

        qtab = pl.BlockSpec((Q, SWA_DH), lambda r, j, hl, hr: (0, 0))
        ktab = pl.BlockSpec((Q + 2 * HW, SWA_DH), lambda r, j, hl, hr: (0, 0))
        o_g, lse_g = pl.pallas_call(
            _swa_kernel, name=f"swa_dil{dil}",
            grid_spec=pltpu.PrefetchScalarGridSpec(
                num_scalar_prefetch=2, grid=(dil, nblk),
                in_specs=[center(0), left(1), center(1), right(1), left(2), center(2), right(2),
                          qtab, qtab, qtab, ktab, ktab, ktab],
                out_specs=[pl.BlockSpec((None, Q, GW), lambda r, j, hl, hr: (r, j, 0)),
                           pl.BlockSpec((None, Q, LANE), lambda r, j, hl, hr: (r, j, 0))]),
            out_shape=[jax.ShapeDtypeStruct((dil, rows, GW), jnp.float32),
                       jax.ShapeDtypeStruct((dil, rows, LANE), jnp.float32)],
            compiler_params=_cparams("parallel", "parallel"),
        )(has_l, has_r, hs, hs, hs, hs, hs, hs, hs, *qt, *kt)
        outs.append(o_g)
        lses.append(lse_g)
    bm = 1024
    dils = [d for _, d in SWA_GROUPS]
    osp = [pl.BlockSpec((d, bm // d, GW), lambda i: (0, i, 0)) for d in dils]
    lsp = [pl.BlockSpec((d, bm // d, LANE), lambda i: (0, i, 0)) for d in dils]
    scratch = []
    for d in dils:
        if d > 1:
            scratch += [pltpu.VMEM((SWA_HEADS, bm, LANE), jnp.float32), pltpu.VMEM((bm, LANE), jnp.float32)]
    return pl.pallas_call(
        _swa_merge_kernel, name="swa_merge", grid=(n // bm,), in_specs=osp + lsp,
        out_specs=pl.BlockSpec((bm, GW), lambda i: (i, 0)),
        out_shape=jax.ShapeDtypeStruct((n, GW), jnp.bfloat16), scratch_shapes=scratch,
        compiler_params=_cparams("parallel"),
    )(*outs, *lses)


FFT_T2 = 128
FW = FNET_GROUPS * FNET_DG


def _dft_cos_sin(n_out, n_in, period):
    ang = (2.0 * np.pi / period) * np.mod(np.outer(np.arange(n_out), np.arange(n_in)), period)
    return np.cos(ang), np.sin(ang)


def _fnet_stage1_kernel(z_ref, m_ref, a_ref, *, cols):
    mat = m_ref[...]
    for c in range(cols):
        zr = z_ref[:, c * 2 * FW:c * 2 * FW + FW]
        zi = z_ref[:, c * 2 * FW + FW:(c + 1) * 2 * FW]
        a = jnp.dot(mat, jnp.concatenate([zr, zi], axis=0), preferred_element_type=jnp.float32)
        t1 = zr.shape[0]
        a_ref[:, c * 2 * FW:c * 2 * FW + FW] = a[:t1].astype(a_ref.dtype)
        a_ref[:, c * 2 * FW + FW:(c + 1) * 2 * FW] = a[t1:].astype(a_ref.dtype)


FNET_KB = 8


def _fnet_stage2_kernel(a_ref, g_ref, y_ref, *, scale):
    kw = g_ref.shape[1] // FNET_KB
    for j in range(FNET_KB):
        y = jnp.dot(g_ref[:, j * kw:(j + 1) * kw], a_ref[j], preferred_element_type=jnp.float32)
        y_ref[:, j * FW:(j + 1) * FW] = (y * scale).astype(y_ref.dtype)


def _fnet(h, seqs):
    n = h.shape[0]
    bf16 = jnp.bfloat16
    T2 = FFT_T2
    cc, ss = _dft_cos_sin(FNET_DG, FNET_DG, FNET_DG)
    eye = np.eye(FNET_GROUPS)
    wc = jnp.asarray(np.concatenate([np.kron(eye, cc), -np.kron(eye, ss)], axis=1), bf16)
    z = _matmul(h, wc, bf16, bn=2 * FW, a_col_blk=C_OFF // FW, name="fnet_channel_dft")
    zz = z.reshape(n // T2, T2 * 2 * FW)
    outs = []
    row0 = 0
    for num, T in seqs:
        T1 = T // T2
        assert T1 * T2 == T and T1 % 16 == 0
        c1, s1 = _dft_cos_sin(T1, T1, T1)
        mat = jnp.asarray(np.block([[c1, s1], [-s1, c1]]), bf16)
        cols = min(T2, max(1, 1024 // T1))
        rb0 = row0 // T1
        a = pl.pallas_call(
            functools.partial(_fnet_stage1_kernel, cols=cols), name=f"fnet_dft_t1_{T1}", grid=(num, T2 // cols),
            in_specs=[pl.BlockSpec((T1, cols * 2 * FW), lambda b, j: (rb0 + b, j)),
                      pl.BlockSpec((2 * T1, 2 * T1), lambda b, j: (0, 0))],
            out_specs=pl.BlockSpec((T1, cols * 2 * FW), lambda b, j: (b, j)),
            out_shape=jax.ShapeDtypeStruct((num * T1, T2 * 2 * FW), bf16),
            compiler_params=_cparams("parallel", "parallel"),
        )(zz, mat)
        ec, es = _dft_cos_sin(T, T2, T)
        g = jnp.asarray(np.stack([ec, es], axis=-1).reshape(T2, T1 * 2 * T2), bf16)
        y = pl.pallas_call(
            functools.partial(_fnet_stage2_kernel, scale=float(1.0 / np.sqrt(T * FNET_DG))), name=f"fnet_dft_t2_{T1}",
            grid=(num, T1 // FNET_KB),
            in_specs=[pl.BlockSpec((FNET_KB, 2 * T2, FW), lambda b, k1: (b * (T1 // FNET_KB) + k1, 0, 0)),
                      pl.BlockSpec((T2, FNET_KB * 2 * T2), lambda b, k1: (0, k1))],
            out_specs=pl.BlockSpec((T2, FNET_KB * FW), lambda b, k1: (b, k1)),
            out_shape=jax.ShapeDtypeStruct((num * T2, T1 * FW), bf16),
            compiler_params=_cparams("parallel", "parallel"),
        )(a.reshape(num * T1, 2 * T2, FW), g)
        outs.append(y.reshape(num * T, FW))
        row0 += num * T1
    return jnp.concatenate(outs, axis=0)


TOK_RADIX = 256


def _ffn_up_kernel(x_ref, w1_ref, w3_ref, h_ref, w1b_ref, w3b_ref):
    @pl.when(pl.program_id(2) == 0)
    def _():
        w1b_ref[...] = w1_ref[...].astype(w1b_ref.dtype)
        w3b_ref[...] = w3_ref[...].astype(w3b_ref.dtype)

    x = x_ref[...]
    h1 = jnp.dot(x, w1b_ref[...], preferred_element_type=jnp.float32)
    h3 = jnp.dot(x, w3b_ref[...], preferred_element_type=jnp.float32)
    h_ref[...] = (_silu(h1) * h3).astype(h_ref.dtype)


def _ffn_down_kernel(h_ref, w2_ref, g_ref, t_ref, o_ref, w2b_ref):
    @pl.when(pl.program_id(1) == 0)
    def _():
        w2b_ref[...] = w2_ref[...].astype(w2b_ref.dtype)

    D = w2_ref.shape[1]
    y = jnp.dot(h_ref[...], w2b_ref[...], preferred_element_type=jnp.float32)
    o_ref[:, :D] = (y * g_ref[...]).astype(o_ref.dtype)
    tok = t_ref[...]
    lane = lax.broadcasted_iota(jnp.int32, (tok.shape[0], LANE), 1)
    digits = jnp.where(lane == 0, tok // TOK_RADIX, jnp.where(lane == 1, tok % TOK_RADIX, 0))
    o_ref[:, D:] = digits.astype(jnp.float32).astype(o_ref.dtype)


def _expert_ffn(xe, w1, w3, w2, gate, tok, layer, bm=1024, bf=512):
    E, R, D = xe.shape
    F = w1.shape[-1]
    wspec = pl.BlockSpec((None, None, D, bf), lambda e, f, i: (layer, e, 0, f))
    hmid = pl.pallas_call(
        _ffn_up_kernel, name="expert_ffn_up", grid=(E, F // bf, R // bm),
        in_specs=[pl.BlockSpec((None, bm, D), lambda e, f, i: (e, i, 0)), wspec, wspec],
        out_specs=pl.BlockSpec((None, bm, bf), lambda e, f, i: (e, i, f)),
        out_shape=jax.ShapeDtypeStruct((E, R, F), jnp.bfloat16),
        scratch_shapes=[pltpu.VMEM((D, bf), jnp.bfloat16), pltpu.VMEM((D, bf), jnp.bfloat16)],
        compiler_params=_cparams("arbitrary", "arbitrary", "arbitrary"),
    )(xe, w1, w3)
    col = pl.BlockSpec((None, bm, 1), lambda e, i: (e, i, 0))
    return pl.pallas_call(
        _ffn_down_kernel, name="expert_ffn_down", grid=(E, R // bm),
        in_specs=[pl.BlockSpec((None, bm, F), lambda e, i: (e, i, 0)),
                  pl.BlockSpec((None, None, F, D), lambda e, i: (layer, e, 0, 0), pipeline_mode=pl.Buffered(1)),
                  col, col],
        out_specs=pl.BlockSpec((None, bm, D + LANE), lambda e, i: (e, i, 0)),
        out_shape=jax.ShapeDtypeStruct((E, R, D + LANE), jnp.bfloat16),
        scratch_shapes=[pltpu.VMEM((F, D), jnp.bfloat16)],
        compiler_params=_cparams("arbitrary", "arbitrary"),
    )(hmid, w2, gate, tok)


COMB_BM = 128
COMB_G = 16
COMB_K = 256


def _combine_kernel(lo_ref, x_ref, g_ref, b_ref, ye_hbm, *rest, nblk_all, blk_off, want_bf16):
    if want_bf16:
        xf_ref, xb_ref, buf_ref, acc_ref, sem = rest
    else:
        xf_ref, buf_ref, acc_ref, sem = rest
    f32, bf16 = jnp.float32, jnp.bfloat16
    E = ye_hbm.shape[0]
    D = x_ref.shape[1]
    BM, G, KC = COMB_BM, COMB_G, COMB_K
    i = pl.program_id(0)
    nsteps = pl.num_programs(0)
    blk = i + blk_off
    slot = i % 2

    @pl.when(i == 0)
    def _():
        buf_ref[...] = jnp.zeros_like(buf_ref)

    def granule_copy(sl, e, src_row, dst_row):
        return pltpu.make_async_copy(ye_hbm.at[e, pl.ds(src_row, G)], buf_ref.at[sl, pl.ds(dst_row, G)], sem.at[sl])

    def gather_block(b, sl, start):
        off = jnp.int32(0)
        for e in range(E):
            lo = lo_ref[e * (nblk_all + 1) + b]
            hi = lo_ref[e * (nblk_all + 1) + b + 1]
            lo_al = (lo // G) * G
            ng = jnp.where(hi > lo, (hi - lo_al + G - 1) // G, 0)
            if start:
                def issue(g, carry, e=e, lo_al=lo_al, off=off):
                    granule_copy(sl, e, pl.multiple_of(lo_al + g * G, G), pl.multiple_of(off + g * G, G)).start()
                    return carry

                lax.fori_loop(0, ng, issue, 0)
            off = off + ng * G
        return off

    @pl.when(i == 0)
    def _():
        gather_block(blk, slot, True)

    @pl.when(i + 1 < nsteps)
    def _():
        gather_block(blk + 1, 1 - slot, True)

    off = gather_block(blk, slot, False)

    def wait_one(g, carry):
        granule_copy(slot, 0, 0, 0).wait()
        return carry

    lax.fori_loop(0, off // G, wait_one, 0)

    acc_ref[...] = jnp.zeros_like(acc_ref)
    t0 = (blk * BM).astype(f32)
    lane_tok = lax.broadcasted_iota(jnp.int32, (KC, BM), 1).astype(f32)
    row = lax.broadcasted_iota(jnp.int32, (KC, 1), 0)

    def chunk(c, carry):
        win = buf_ref[slot, pl.ds(pl.multiple_of(c * KC, KC), KC), :]
        tok = win[:, D:D + 1].astype(f32) * TOK_RADIX + win[:, D + 1:D + 2].astype(f32)
        tok = jnp.where(row + c * KC < off, tok, -1.0)
        onehot_t = (tok - t0 == lane_tok).astype(bf16)
        acc_ref[...] += lax.dot_general(onehot_t, win[:, :D], (((0,), (0,)), ((), ())), preferred_element_type=f32)
        return carry

    lax.fori_loop(0, (off + KC - 1) // KC, chunk, 0)
    y = _ln_rows(DN_ALPHA * x_ref[...] + acc_ref[...], g_ref[...], b_ref[...])
    xf_ref[...] = y
    if want_bf16:
        xb_ref[...] = y.astype(bf16)


def _combine_ln(x, ye, lo, g, b, row_off, rows, want_bf16):
    n, D = x.shape
    E, R, DY = ye.shape
    BM = COMB_BM
    nblk_all = n // BM
    ob = row_off // BM
    kmax = -(-E * (BM + COMB_G) // COMB_K) * COMB_K
    irow = pl.BlockSpec((BM, D), lambda i, lo: (i + ob, 0))
    orow = pl.BlockSpec((BM, D), lambda i, lo: (i, 0))
    vec = pl.BlockSpec((1, D), lambda i, lo: (0, 0))
    out_shape = [jax.ShapeDtypeStruct((rows, D), jnp.float32)]
    out_specs = [orow]
    if want_bf16:
        out_shape.append(jax.ShapeDtypeStruct((rows, D), jnp.bfloat16))
        out_specs.append(orow)
    return pl.pallas_call(
        functools.partial(_combine_kernel, nblk_all=nblk_all, blk_off=ob, want_bf16=want_bf16), name="moe_combine_ln",
        grid_spec=pltpu.PrefetchScalarGridSpec(
            num_scalar_prefetch=1, grid=(rows // BM,),
            in_specs=[irow, vec, vec, pl.BlockSpec(memory_space=pl.ANY)], out_specs=out_specs,
            scratch_shapes=[pltpu.VMEM((2, kmax, DY), jnp.bfloat16), pltpu.VMEM((BM, D), jnp.float32),
                            pltpu.SemaphoreType.DMA((2,))]),
        out_shape=out_shape,
        compiler_params=_cparams("arbitrary"),
    )(lo, x, g.reshape(1, D), b.reshape(1, D), ye)


def _route_kernel(a_ref, idx_ref, gate_ref, cprev_ref, *, cap, tok_off):
    f32, bf16, i32 = jnp.float32, jnp.bfloat16, jnp.int32
    a = a_ref[...]
    R = a.shape[0]
    bits = pltpu.bitcast(a, i32)

    def refine(i, prefix):
        cand = prefix | jnp.left_shift(jnp.int32(1), 30 - i)
        cnt = jnp.sum((bits >= cand).astype(i32), keepdims=True)
        return jnp.where(cnt >= cap, cand, prefix)

    thr = lax.fori_loop(0, 31, refine, jnp.zeros((1, 1), i32))
    gt, eq = bits > thr, bits == thr
    need = cap - jnp.sum(gt.astype(i32), keepdims=True)

    li = lax.broadcasted_iota(i32, (LANE, LANE), 0)
    lj = lax.broadcasted_iota(i32, (LANE, LANE), 1)
    upto = (li <= lj).astype(bf16)
    ri = lax.broadcasted_iota(i32, (R, R), 0)
    rj = lax.broadcasted_iota(i32, (R, R), 1)
    before = (rj < ri).astype(bf16)

    def counts(mask):
        inc = jnp.dot(mask.astype(bf16), upto, preferred_element_type=f32)
        tot = jnp.broadcast_to(inc[:, LANE - 1:LANE], (R, LANE)).astype(bf16)
        return inc, jnp.dot(before, tot, preferred_element_type=f32)

    eq_inc, eq_prev = counts(eq)
    sel = gt | (eq & (eq_prev + eq_inc - 1.0 < need.astype(f32)))
    cl, cprev = counts(sel)
    cprev_ref[...] = cprev[:, :1].astype(i32)

    selb = sel.astype(bf16)
    c_upto = jnp.sum(lax.dot_general(selb, (ri <= rj).astype(bf16), (((0,), (0,)), ((), ())),
                                     preferred_element_type=f32), axis=0, keepdims=True)
    c_before = jnp.sum(lax.dot_general(selb, (ri < rj).astype(bf16), (((0,), (0,)), ((), ())),
                                       preferred_element_type=f32), axis=0, keepdims=True)
    slot = lax.broadcasted_iota(i32, (cap, 1), 0).astype(f32)
    onehot = ((c_before <= slot) & (slot < c_upto)).astype(bf16)
    lane = lax.broadcasted_iota(i32, (R, LANE), 1)
    cp = cprev.astype(i32)
    aux = jnp.where(lane == 0, cp // TOK_RADIX, jnp.where(lane == 1, cp % TOK_RADIX,
                    jnp.where(lane == 2, lax.broadcasted_iota(i32, (R, LANE), 0), 0)))
    m_aux = jnp.dot(onehot, aux.astype(f32).astype(bf16), preferred_element_type=f32)
    base = m_aux[:, 0:1] * TOK_RADIX + m_aux[:, 1:2]
    chunk = m_aux[:, 2:3]
    m_cl = jnp.dot(onehot, cl.astype(bf16), preferred_element_type=f32)
    pos = jnp.sum((m_cl <= slot - base).astype(f32), axis=1, keepdims=True)
    idx_ref[...] = (chunk * LANE + pos).astype(i32) + tok_off
    hi = a.astype(bf16)
    r1 = a - hi.astype(f32)
    mid = r1.astype(bf16)
    low = (r1 - mid.astype(f32)).astype(bf16)
    m_a = (jnp.dot(onehot, hi, preferred_element_type=f32) + jnp.dot(onehot, mid, preferred_element_type=f32)
           + jnp.dot(onehot, low, preferred_element_type=f32))
    lane_c = lax.broadcasted_iota(i32, (cap, LANE), 1).astype(f32)
    gate_ref[...] = jnp.sum(jnp.where(lane_c == pos, m_a, 0.0), axis=1, keepdims=True)


def _route(aff_t, cap, tok_off):
    E, R, _ = aff_t.shape
    return pl.pallas_call(
        functools.partial(_route_kernel, cap=cap, tok_off=tok_off), name="moe_route", grid=(E,),
        in_specs=[pl.BlockSpec((None, R, LANE), lambda e: (e, 0, 0))],
        out_specs=[pl.BlockSpec((None, cap, 1), lambda e: (e, 0, 0)), pl.BlockSpec((None, cap, 1), lambda e: (e, 0, 0)),
                   pl.BlockSpec((None, R, 1), lambda e: (e, 0, 0))],
        out_shape=[jax.ShapeDtypeStruct((E, cap, 1), jnp.int32), jax.ShapeDtypeStruct((E, cap, 1), jnp.float32),
                   jax.ShapeDtypeStruct((E, R, 1), jnp.int32)],
        compiler_params=_cparams("parallel"),
    )(aff_t)


def _expert_choice(xb, aff, seqs, w1, w3, w2, layer):
    assert COMB_BM == LANE
    aff_t = aff[:, :N_EXPERTS].T
    gates, idxs, los = [], [], []
    start = slot0 = 0
    for num, T in seqs:
        m = num * T
        cap = max(1, CAPACITY_FACTOR * m // N_EXPERTS)
        idx, gate, cprev = _route(aff_t[:, start:start + m].reshape(N_EXPERTS, m // LANE, LANE), cap, start)
        idxs.append(idx)
        gates.append(gate)
        los.append(cprev[..., 0] + slot0)
        start += m
        slot0 += cap
    idx, gate = jnp.concatenate(idxs, axis=1), jnp.concatenate(gates, axis=1)
    lo = jnp.concatenate(los + [jnp.full((N_EXPERTS, 1), slot0, jnp.int32)], axis=1).reshape(-1)
    ye = _expert_ffn(xb[idx[..., 0]], w1, w3, w2, gate, idx, layer)
    return ye, lo


def _pack_w_in(w_in):
    sizes = (GLA_HEADS * GLA_DK, GLA_HEADS * GLA_DK, GLA_HEADS * GLA_DV, GLA_HEADS * GLA_DV, 2 * GLA_RANK,
             SWA_NH * SWA_DH, SWA_NH * SWA_DH, SWA_NH * SWA_DH, FNET_GROUPS * FNET_DG,
             RET_HEADS * RET_DH, RET_HEADS * RET_DH, RET_HEADS * RET_DH, RET_HEADS * RET_DH)
    offs = np.concatenate([[0], np.cumsum(sizes)])
    part = lambda i: w_in[..., offs[i]:offs[i + 1]]
    a_q, a_k = part(0), part(1)
    cols = []
    for hh in range(GLA_HEADS):
        cols += [a_q[..., hh * GLA_DK:(hh + 1) * GLA_DK], a_k[..., hh * GLA_DK:(hh + 1) * GLA_DK]]
    cols += [part(2), part(3), part(4),
             jnp.zeros(w_in.shape[:2] + (C_OFF - LR_OFF - 2 * GLA_RANK,), w_in.dtype)]
    cols += [part(i) for i in range(8, 13)]
    out = jnp.concatenate(cols, axis=-1)
    assert out.shape[-1] == IN_COLS_P
    GW = SWA_HEADS * SWA_DH
    swa = [jnp.concatenate([part(i)[..., g * GW:(g + 1) * GW] for i in (5, 6, 7)], axis=-1)
           for g in range(len(SWA_GROUPS))]
    return out, swa


def _block_tables(seqs, blk):
    first, last, pos = [], [], []
    for num, T in seqs:
        per = T // blk
        assert per * blk == T
        for _ in range(num):
            first += [1] + [0] * (per - 1)
            last += [0] * (per - 1) + [1]
            pos += list(range(per))
    mk = lambda a: jnp.asarray(np.array(a, np.int32))
    return mk(first), mk(last), mk(pos)


def kernel(x_prompt, x_sample, ln_in_g, ln_in_b, w_in, gla_w_up, gla_b_up, ret_decay, w_gate, b_gate, w_branch,
           w_out, ln1_g, ln1_b, w_router, w_e1, w_e3, w_e2, ln2_g, ln2_b):
    bf16 = jnp.bfloat16
    D = D_MODEL
    seqs = (x_prompt.shape[:2], x_sample.shape[:2])
    n_p, n_s = x_prompt.shape[0] * x_prompt.shape[1], x_sample.shape[0] * x_sample.shape[1]
    n = n_p + n_s
    t_max = max(T for _, T in seqs)
    seq_first, seq_last, pos_blk = _block_tables(seqs, SEQ_BLOCK)

    w_in_p, w_swa = _pack_w_in(w_in.astype(bf16))
    wo = w_out.astype(bf16)
    wr = jnp.pad(w_router, ((0, 0), (0, 0), (0, LANE - N_EXPERTS))).astype(bf16)
    wg, wb, we1, we3, we2 = w_gate, w_branch, w_e1, w_e3, w_e2

    xf, xb = _ln_in(x_prompt.reshape(n_p, D), x_sample.reshape(n_s, D), ln_in_g, ln_in_b)
    for l in range(DEPTH):
        h = _matmul(xb, w_in_p[l], bf16, bm=1024, bn=IN_BN, name="in_proj")
        wup, bup = _pack_gla_up(gla_w_up[l], gla_b_up[l])
        ys = (_gla(h, seq_first, seq_last, wup, bup),
              _swa(xb, [w[l] for w in w_swa], seqs),
              _fnet(h, seqs),
              _retention(h, seq_first, seq_last, pos_blk, ret_decay[l], t_max))
        merged = _merge(xb, wg, b_gate[l], ys, wb, l)
        x1f, x1b, aff = _outproj(merged, wo[l], xf, ln1_g[l], ln1_b[l], wr[l])
        ye, lo = _expert_choice(x1b, aff, seqs, we1, we3, we2, l)
        if l + 1 < DEPTH:
            xf, xb = _combine_ln(x1f, ye, lo, ln2_g[l], ln2_b[l], 0, n, True)
        else:
            y_p, = _combine_ln(x1f, ye, lo, ln2_g[l], ln2_b[l], 0, n_p, False)
            y_s, = _combine_ln(x1f, ye, lo, ln2_g[l], ln2_b[l], n_p, n_s, False)
    return (y_p.reshape(x_prompt.shape), y_s.reshape(x_sample.shape))
```

```python
import functools

import jax
import jax.numpy as jnp
import numpy as np
from jax import lax
from jax.experimental import pallas as pl
from jax.experimental.pallas import tpu as pltpu

D_MODEL = 2048
DEPTH = 2
GLA_HEADS, GLA_DK, GLA_DV, GLA_RANK, GLA_TAU, GLA_CHUNK = 4, 64, 128, 16, 16.0, 64
SWA_GROUPS = ((128, 1), (512, 4), (2048, 16))
SWA_HEADS, SWA_DH = 4, 128
SWA_NH = SWA_HEADS * len(SWA_GROUPS)
SWA_BLOCK = 64
ROPE_THETA, ROPE_DIM = 500000.0, SWA_DH // 4
FNET_GROUPS, FNET_DG = 4, 128
RET_HEADS, RET_DH, RET_THETA = 4, 128, 10000.0
N_BRANCH, BRANCH_W = 4, 512
N_EXPERTS, EXPERT_FF, CAPACITY_FACTOR = 16, 2048, 2
DN_ALPHA = (2 * DEPTH) ** 0.25
LN_EPS = 1e-5

VMEM_LIMIT_BYTES = 48 * 1024 * 1024
LANE = 128

QK_OFF = 0
AV_OFF = QK_OFF + GLA_HEADS * LANE
AG_OFF = AV_OFF + GLA_HEADS * GLA_DV
LR_OFF = AG_OFF + GLA_HEADS * GLA_DV
C_OFF = LR_OFF + 4 * LANE
DQ_OFF = C_OFF + FNET_GROUPS * FNET_DG
DK_OFF = DQ_OFF + RET_HEADS * RET_DH
DV_OFF = DK_OFF + RET_HEADS * RET_DH
DG_OFF = DV_OFF + RET_HEADS * RET_DH
IN_USED = DG_OFF + RET_HEADS * RET_DH
IN_BN = 1536
IN_COLS_P = -(-IN_USED // IN_BN) * IN_BN
assert IN_COLS_P == IN_USED

SEQ_BLOCK = 512


def _cparams(*sem):
    return pltpu.CompilerParams(dimension_semantics=sem, vmem_limit_bytes=VMEM_LIMIT_BYTES)


def _mm_kernel(a_ref, b_ref, o_ref):
    o_ref[...] = jnp.dot(a_ref[...], b_ref[...], preferred_element_type=jnp.float32).astype(o_ref.dtype)


def _matmul(a, b, out_dtype, bm=1024, bn=512, a_col_blk=0, name="matmul"):
    batched = a.ndim == 3
    M = a.shape[-2]
    K, N = b.shape[-2:]
    bm = min(bm, M)
    bn = min(bn, N)
    assert M % bm == 0 and N % bn == 0, (M, N, bm, bn)
    if batched:
        E = a.shape[0]
        grid = (E, M // bm, N // bn)
        in_specs = [pl.BlockSpec((None, bm, K), lambda e, i, j: (e, i, 0)),
                    pl.BlockSpec((None, K, bn), lambda e, i, j: (e, 0, j))]
        out_specs = pl.BlockSpec((None, bm, bn), lambda e, i, j: (e, i, j))
        out_shape = jax.ShapeDtypeStruct((E, M, N), out_dtype)
        sem = ("parallel", "parallel", "arbitrary")
    else:
        grid = (M // bm, N // bn)
        in_specs = [pl.BlockSpec((bm, K), lambda i, j: (i, a_col_blk)),
                    pl.BlockSpec((K, bn), lambda i, j: (0, j))]
        out_specs = pl.BlockSpec((bm, bn), lambda i, j: (i, j))
        out_shape = jax.ShapeDtypeStruct((M, N), out_dtype)
        sem = ("parallel", "arbitrary")
    return pl.pallas_call(
        _mm_kernel, name=name, grid=grid, in_specs=in_specs, out_specs=out_specs, out_shape=out_shape,
        compiler_params=_cparams(*sem),
    )(a, b)


def _mm_w32_kernel(a_ref, w_ref, o_ref, wb_ref):
    @pl.when(pl.program_id(1) == 0)
    def _():
        wb_ref[...] = w_ref[...].astype(wb_ref.dtype)

    o_ref[...] = jnp.dot(a_ref[...], wb_ref[...], preferred_element_type=jnp.float32).astype(o_ref.dtype)


def _matmul_w32(a, w, out_dtype, bm=1024, bn=768, name="matmul_w32"):
    M, K = a.shape
    N = w.shape[1]
    assert M % bm == 0 and N % bn == 0, (M, N, bm, bn)
    return pl.pallas_call(
        _mm_w32_kernel, name=name, grid=(N // bn, M // bm),
        in_specs=[pl.BlockSpec((bm, K), lambda j, i: (i, 0)), pl.BlockSpec((K, bn), lambda j, i: (0, j))],
        out_specs=pl.BlockSpec((bm, bn), lambda j, i: (i, j)),
        out_shape=jax.ShapeDtypeStruct((M, N), out_dtype),
        scratch_shapes=[pltpu.VMEM((K, bn), jnp.bfloat16)],
        compiler_params=_cparams("arbitrary", "arbitrary"),
    )(a, w)


def _ln_rows(x, g, b):
    mu = jnp.mean(x, -1, keepdims=True)
    xc = x - mu
    var = jnp.mean(xc * xc, -1, keepdims=True)
    return xc * lax.rsqrt(var + LN_EPS) * g + b


def _ln_in_kernel(xa_ref, xc_ref, g_ref, b_ref, xf_ref, xb_ref, *, na_blocks):
    i = pl.program_id(0)

    def emit(x_ref):
        y = _ln_rows(x_ref[...], g_ref[...], b_ref[...])
        xf_ref[...] = y
        xb_ref[...] = y.astype(jnp.bfloat16)

    pl.when(i < na_blocks)(lambda: emit(xa_ref))
    pl.when(i >= na_blocks)(lambda: emit(xc_ref))


def _ln_in(xa, xc, g, b, bm=512):
    (na, D), nc = xa.shape, xc.shape[0]
    nab, ncb = na // bm, nc // bm
    row = pl.BlockSpec((bm, D), lambda i: (i, 0))
    vec = pl.BlockSpec((1, D), lambda i: (0, 0))
    return pl.pallas_call(
        functools.partial(_ln_in_kernel, na_blocks=nab), name="ln_in", grid=(nab + ncb,),
        in_specs=[pl.BlockSpec((bm, D), lambda i: (jnp.minimum(i, nab - 1), 0)),
                  pl.BlockSpec((bm, D), lambda i: (jnp.maximum(i - nab, 0), 0)), vec, vec],
        out_specs=[row, row],
        out_shape=[jax.ShapeDtypeStruct((na + nc, D), jnp.float32), jax.ShapeDtypeStruct((na + nc, D), jnp.bfloat16)],
        compiler_params=_cparams("arbitrary"))(xa, xc, g.reshape(1, D), b.reshape(1, D))


def _merge_kernel(xb_ref, wg_ref, bg_ref, ya_ref, yb_ref, yc_ref, yd_ref, wb_ref, o_ref, wgb_ref, wbb_ref):
    @pl.when(pl.program_id(1) == 0)
    def _():
        wgb_ref[...] = wg_ref[...].astype(wgb_ref.dtype)
        wbb_ref[...] = wb_ref[...].astype(wbb_ref.dtype)

    xb = xb_ref[...]
    acc = None
    for i, y_ref in enumerate((ya_ref, yb_ref, yc_ref, yd_ref)):
        gate = jax.nn.sigmoid(jnp.dot(xb, wgb_ref[i], preferred_element_type=jnp.float32) + bg_ref[i])
        term = gate * jnp.dot(y_ref[...], wbb_ref[i], preferred_element_type=jnp.float32)
        acc = term if acc is None else acc + term
    o_ref[...] = acc.astype(o_ref.dtype)


def _merge(xb, wg, bg, ys, wb, layer, bm=1024, bn=256):
    n, D = xb.shape
    W = ys[0].shape[1]
    y_spec = pl.BlockSpec((bm, W), lambda j, i: (i, 0))
    return pl.pallas_call(
        _merge_kernel, name="branch_merge", grid=(D // bn, n // bm),
        in_specs=[pl.BlockSpec((bm, D), lambda j, i: (i, 0)),
                  pl.BlockSpec((None, N_BRANCH, D, bn), lambda j, i: (layer, 0, 0, j)),
                  pl.BlockSpec((N_BRANCH, 1, bn), lambda j, i: (0, 0, j)),
                  y_spec, y_spec, y_spec, y_spec,
                  pl.BlockSpec((None, N_BRANCH, W, bn), lambda j, i: (layer, 0, 0, j))],
        out_specs=pl.BlockSpec((bm, bn), lambda j, i: (i, j)),
        out_shape=jax.ShapeDtypeStruct((n, D), jnp.bfloat16),
        scratch_shapes=[pltpu.VMEM((N_BRANCH, D, bn), jnp.bfloat16), pltpu.VMEM((N_BRANCH, W, bn), jnp.bfloat16)],
        compiler_params=_cparams("arbitrary", "arbitrary"),
    )(xb, wg, bg.reshape(N_BRANCH, 1, D), *ys, wb)


OUTPROJ_SUB = 128


TOK_RADIX = 256
SIDE_MID, SIDE_LOW, SIDE_TOK = N_EXPERTS, 2 * N_EXPERTS, 3 * N_EXPERTS


def _outproj_kernel(m_ref, w_ref, x_ref, g_ref, b_ref, wr_ref, xf_ref, xb_ref, aff_ref):
    f32, bf16 = jnp.float32, jnp.bfloat16
    D = x_ref.shape[1]
    for s in range(m_ref.shape[0] // OUTPROJ_SUB):
        rows = pl.ds(s * OUTPROJ_SUB, OUTPROJ_SUB)
        r = jnp.dot(m_ref[rows, :], w_ref[...], preferred_element_type=f32)
        y = _ln_rows(DN_ALPHA * x_ref[rows, :] + r, g_ref[...], b_ref[...])
        yb = y.astype(bf16)
        xf_ref[rows, :] = y
        xb_ref[rows, :D] = yb
        logits = jnp.dot(yb, wr_ref[...], preferred_element_type=f32)
        lane = lax.broadcasted_iota(jnp.int32, logits.shape, 1)
        logits = jnp.where(lane < N_EXPERTS, logits, -jnp.inf)
        e = jnp.exp(logits - jnp.max(logits, -1, keepdims=True))
        aff = e / jnp.sum(e, -1, keepdims=True)
        aff_ref[rows, :] = aff
        hi = aff.astype(bf16).astype(f32)
        mid = (aff - hi).astype(bf16).astype(f32)
        low = (aff - hi - mid).astype(bf16).astype(f32)
        tok = (pl.program_id(0) * m_ref.shape[0] + s * OUTPROJ_SUB
               + lax.broadcasted_iota(jnp.int32, logits.shape, 0))
        side = hi + pltpu.roll(mid, SIDE_MID, axis=1) + pltpu.roll(low, SIDE_LOW, axis=1)
        side = jnp.where(lane == SIDE_TOK, (tok // TOK_RADIX).astype(f32),
                         jnp.where(lane == SIDE_TOK + 1, (tok % TOK_RADIX).astype(f32), side))
        xb_ref[rows, D:] = side.astype(bf16)


def _outproj(merged, w_out, x, g, b, w_router_p, bm=512):
    n, D = x.shape
    row = pl.BlockSpec((bm, D), lambda i: (i, 0))
    vec = pl.BlockSpec((1, D), lambda i: (0, 0))
    return pl.pallas_call(
        _outproj_kernel, name="outproj_ln_router", grid=(n // bm,),
        in_specs=[row, pl.BlockSpec((D, D), lambda i: (0, 0)), row, vec, vec,
                  pl.BlockSpec((D, LANE), lambda i: (0, 0))],
        out_specs=[row, pl.BlockSpec((bm, D + LANE), lambda i: (i, 0)), pl.BlockSpec((bm, LANE), lambda i: (i, 0))],
        out_shape=[jax.ShapeDtypeStruct((n, D), jnp.float32), jax.ShapeDtypeStruct((n, D + LANE), jnp.bfloat16),
                   jax.ShapeDtypeStruct((n, LANE), jnp.float32)],
        compiler_params=_cparams("parallel"),
    )(merged, w_out, x, g.reshape(1, D), b.reshape(1, D), w_router_p)


def _aff_t_kernel(a_ref, o_ref):
    for s in range(a_ref.shape[0] // LANE):
        o_ref[:, s, :] = a_ref[s * LANE:(s + 1) * LANE, :].T[:N_EXPERTS, :]


def _aff_chunks(aff, bm=1024):
    n = aff.shape[0]
    return pl.pallas_call(
        _aff_t_kernel, name="aff_chunks", grid=(n // bm,),
        in_specs=[pl.BlockSpec((bm, LANE), lambda i: (i, 0))],
        out_specs=pl.BlockSpec((N_EXPERTS, bm // LANE, LANE), lambda i: (0, i, 0)),
        out_shape=jax.ShapeDtypeStruct((N_EXPERTS, n // LANE, LANE), jnp.float32),
        compiler_params=_cparams("parallel"),
    )(aff)


def _log_sigmoid(z):
    return jnp.minimum(z, 0.0) - jnp.log(1.0 + jnp.exp(-jnp.abs(z)))


def _headnorm(o):
    mu = jnp.mean(o, -1, keepdims=True)
    oc = o - mu
    var = jnp.mean(oc * oc, -1, keepdims=True)
    return oc * lax.rsqrt(var + LN_EPS)


def _silu(x):
    return x * jax.nn.sigmoid(x)


def _chunk_cumsum(g, chunk, inclusive):
    row = lax.broadcasted_iota(jnp.int32, g.shape, 0) % chunk
    b = g
    s = 1
    while s < chunk:
        b = b + jnp.where(row >= s, pltpu.roll(b, s, axis=0), 0.0)
        s *= 2
    return b if inclusive else b - g


def _gla_kernel(flag_ref, qk_ref, v_ref, lr_ref, wup_ref, bup_ref, *rest, reverse, nblk):
    s_idx = pl.program_id(0)
    blk = nblk - 1 - s_idx if reverse else s_idx

    @pl.when(flag_ref[blk] == 1)
    def _():
        rest[-1][...] = jnp.zeros_like(rest[-1])

    for hh in range(GLA_HEADS):
        _gla_head(hh, qk_ref, v_ref, lr_ref, wup_ref, bup_ref, *rest, reverse=reverse)


def _gla_head(hh, qk_ref, v_ref, lr_ref, wup_ref, bup_ref, *rest, reverse):
    if reverse:
        gate_ref, of_ref, o_ref, st_ref = rest
    else:
        o_ref, st_ref = rest
    f32, bf16 = jnp.float32, jnp.bfloat16
    C = GLA_CHUNK
    TB = qk_ref.shape[0]
    nc = TB // C
    sl = slice(hh * LANE, (hh + 1) * LANE)

    qk = qk_ref[:, sl].astype(f32)
    q = qk[:, :GLA_DK] * (GLA_DK ** -0.5)
    k = qk[:, GLA_DK:]
    z = jnp.dot(lr_ref[...], wup_ref[hh], preferred_element_type=f32) + bup_ref[hh]
    g = _log_sigmoid(z) * (1.0 / GLA_TAU)
    cs = _chunk_cumsum(g, C, inclusive=not reverse)
    cs3 = cs.reshape(nc, C, GLA_DK)
    mid = cs3[:, C // 2:C // 2 + 1, :]
    if reverse:
        tot3 = cs3[:, C - 1:C, :] + g.reshape(nc, C, GLA_DK)[:, C - 1:C, :]
        pq3, pk3 = tot3 - cs3, cs3
        qm3, km3 = mid - cs3, cs3 - mid
    else:
        tot3 = cs3[:, C - 1:C, :]
        pq3, pk3 = cs3, tot3 - cs3
        qm3, km3 = cs3 - mid, mid - cs3
    q3 = q.reshape(nc, C, GLA_DK)
    k3 = k.reshape(nc, C, GLA_DK)
    qe = (q3 * jnp.exp(pq3)).astype(bf16)
    kd = (k3 * jnp.exp(pk3)).astype(bf16)
    qm = (q3 * jnp.exp(qm3)).astype(bf16)
    km = (k3 * jnp.exp(km3)).astype(bf16)
    etot = jnp.exp(tot3)
    v = v_ref[:, sl]
    ri = lax.broadcasted_iota(jnp.int32, (C, C), 0)
    ci = lax.broadcasted_iota(jnp.int32, (C, C), 1)
    keep = (ci > ri) if reverse else (ci <= ri)

    st = st_ref[hh]
    outs = [None] * nc
    for c in (range(nc - 1, -1, -1) if reverse else range(nc)):
        vc = v[c * C:(c + 1) * C, :]
        att = lax.dot_general(qm[c], km[c], (((1,), (1,)), ((), ())), preferred_element_type=f32)
        att = jnp.where(keep, att, 0.0).astype(bf16)
        o_c = jnp.dot(att, vc, preferred_element_type=f32)
        o_c = o_c + lax.dot_general(qe[c], st.astype(bf16), (((1,), (1,)), ((), ())), preferred_element_type=f32)
        upd = lax.dot_general(vc, kd[c], (((0,), (0,)), ((), ())), preferred_element_type=f32)
        st = st * etot[c] + upd
        outs[c] = o_c
    st_ref[hh] = st
    o = jnp.concatenate(outs, axis=0)
    if reverse:
        o = _headnorm(o + of_ref[:, sl])
        o_ref[:, sl] = (o * _silu(gate_ref[:, sl].astype(f32))).astype(o_ref.dtype)
    else:
        o_ref[:, sl] = o


def _gla(h, seq_first, seq_last, wup, bup):
    n = h.shape[0]
    TB = SEQ_BLOCK
    nblk = n // TB
    H = GLA_HEADS

    def col(off):
        return off // LANE

    HW = H * LANE

    def specs(reverse):
        blk = (lambda s: nblk - 1 - s) if reverse else (lambda s: s)
        sp = [pl.BlockSpec((TB, HW), lambda s, f: (blk(s), QK_OFF // HW)),
              pl.BlockSpec((TB, HW), lambda s, f: (blk(s), AV_OFF // HW)),
              pl.BlockSpec((TB, LANE), lambda s, f: (blk(s), col(LR_OFF))),
              pl.BlockSpec((None, H, LANE, GLA_DK), lambda s, f: (1 if reverse else 0, 0, 0, 0)),
              pl.BlockSpec((None, H, 1, GLA_DK), lambda s, f: (1 if reverse else 0, 0, 0, 0))]
        out = pl.BlockSpec((TB, HW), lambda s, f: (blk(s), 0))
        return sp, out, blk

    scratch = [pltpu.VMEM((H, GLA_DV, GLA_DK), jnp.float32)]
    sp, out, _ = specs(False)
    o_f = pl.pallas_call(
        functools.partial(_gla_kernel, reverse=False, nblk=nblk), name="gla_fwd",
        grid_spec=pltpu.PrefetchScalarGridSpec(num_scalar_prefetch=1, grid=(nblk,), in_specs=sp, out_specs=out,
                                               scratch_shapes=scratch),
        out_shape=jax.ShapeDtypeStruct((n, H * GLA_DV), jnp.float32),
        compiler_params=_cparams("arbitrary"),
    )(seq_first, h, h, h, wup, bup)
    sp, out, blk = specs(True)
    sp = sp + [pl.BlockSpec((TB, HW), lambda s, f: (blk(s), AG_OFF // HW)), out]
    return pl.pallas_call(
        functools.partial(_gla_kernel, reverse=True, nblk=nblk), name="gla_bwd",
        grid_spec=pltpu.PrefetchScalarGridSpec(num_scalar_prefetch=1, grid=(nblk,), in_specs=sp, out_specs=out,
                                               scratch_shapes=scratch),
        out_shape=jax.ShapeDtypeStruct((n, H * GLA_DV), jnp.bfloat16),
        compiler_params=_cparams("arbitrary"),
    )(seq_last, h, h, h, wup, bup, h, o_f)


def _pack_gla_up(w_up, b_up):
    H, dk, r = GLA_HEADS, GLA_DK, GLA_RANK
    w = w_up.reshape(2, r, H, dk).transpose(0, 2, 1, 3)
    wp = jnp.zeros((2, H, LANE, dk), jnp.float32)
    wp = wp.at[0, :, 0:r].set(w[0]).at[1, :, r:2 * r].set(w[1])
    return wp.astype(jnp.bfloat16), b_up.reshape(2, H, 1, dk).astype(jnp.float32)


def _rope_tables(pos, dim, theta, width):
    half = dim // 2
    T = pos.shape[0]
    inv = 1.0 / (theta ** (jnp.arange(half, dtype=jnp.float32) / half))
    ang = pos.astype(jnp.float32)[:, None] * inv[None, :]
    c, s = jnp.cos(ang), jnp.sin(ang)
    rest = width - dim
    cos = jnp.concatenate([c, c, jnp.ones((T, rest), jnp.float32)], axis=1)
    sin_lo = jnp.concatenate([-s, jnp.zeros((T, half + rest), jnp.float32)], axis=1)
    sin_hi = jnp.concatenate([jnp.zeros((T, half), jnp.float32), s, jnp.zeros((T, rest), jnp.float32)], axis=1)
    return cos, sin_lo, sin_hi


def _ret_kernel(flag_ref, pos_ref, q_ref, k_ref, v_ref, cos_ref, sin_ref, dec_ref, *rest, reverse, nblk):
    s_idx = pl.program_id(0)
    blk = nblk - 1 - s_idx if reverse else s_idx

    @pl.when(flag_ref[blk] == 1)
    def _():
        st_ref = rest[-1] if reverse else rest[-2]
        st_ref[...] = jnp.zeros_like(st_ref)

    for hh in range(RET_HEADS):
        _ret_head(hh, s_idx, q_ref, k_ref, v_ref, cos_ref, sin_ref, dec_ref, *rest, reverse=reverse)


def _ret_head(hh, s_idx, q_ref, k_ref, v_ref, cos_ref, sin_ref, dec_ref, *rest, reverse):
    if reverse:
        gate_ref, o1_ref, o_ref, st_ref = rest
    else:
        o_ref, st_ref, dm_ref = rest
    f32, bf16 = jnp.float32, jnp.bfloat16
    TB = q_ref.shape[0]
    sl = slice(hh * RET_DH, (hh + 1) * RET_DH)

    dec = dec_ref[hh]
    lg = -(jnp.maximum(dec, 0.0) + jnp.log1p(jnp.exp(-jnp.abs(dec))))
    lg_f, lg_b = lg[0, 0:1, 0:1], lg[1, 0:1, 0:1]

    cos, sin = cos_ref[...], sin_ref[...]

    def rot(x):
        return x * cos + pltpu.roll(x, RET_DH // 2, axis=1) * sin

    qr = rot(q_ref[:, sl].astype(f32))
    kr = rot(k_ref[:, sl].astype(f32)) * (RET_DH ** -0.5)
    v = v_ref[:, sl]
    idx = lax.broadcasted_iota(jnp.int32, (TB, 1), 0).astype(f32)
    st = st_ref[hh]
    cross = jnp.dot(qr.astype(bf16), st.astype(bf16), preferred_element_type=f32)
    if reverse:
        cross = cross * jnp.exp(lg_b * (TB - idx))
        kz = (kr * jnp.exp(lg_b * idx)).astype(bf16)
        st_ref[hh] = st * jnp.exp(lg_b * TB) + lax.dot_general(kz, v, (((0,), (0,)), ((), ())),
                                                               preferred_element_type=f32)
        o = _headnorm(o1_ref[:, sl] + cross)
        o_ref[:, sl] = (o * _silu(gate_ref[:, sl].astype(f32))).astype(o_ref.dtype)
    else:
        @pl.when(s_idx == 0)
        def _():
            ri = lax.broadcasted_iota(jnp.int32, (TB, TB), 0)
            ci = lax.broadcasted_iota(jnp.int32, (TB, TB), 1)
            d = (ri - ci).astype(f32)
            dm_ref[hh] = jnp.where(d >= 0, jnp.exp(lg_f * jnp.maximum(d, 0.0)), jnp.exp(lg_b * jnp.maximum(-d, 0.0)))

        cross = cross * jnp.exp(lg_f * (idx + 1.0))
        kz = (kr * jnp.exp(lg_f * (TB - 1.0 - idx))).astype(bf16)
        st_ref[hh] = st * jnp.exp(lg_f * TB) + lax.dot_general(kz, v, (((0,), (0,)), ((), ())),
                                                               preferred_element_type=f32)
        sc = lax.dot_general(qr.astype(bf16), kr.astype(bf16), (((1,), (1,)), ((), ())), preferred_element_type=f32)
        sc = (sc * dm_ref[hh]).astype(bf16)
        o_ref[:, sl] = jnp.dot(sc, v, preferred_element_type=f32) + cross


def _retention(h, seq_first, seq_last, pos_blk, decay, t_max):
    n = h.shape[0]
    TB = SEQ_BLOCK
    nblk = n // TB
    H, dh = RET_HEADS, RET_DH
    cos, sin_lo, sin_hi = _rope_tables(jnp.arange(t_max), dh, RET_THETA, dh)
    sin = sin_lo + sin_hi
    dec = jnp.broadcast_to(decay.astype(jnp.float32).T[:, :, None, None], (H, 2, 8, LANE))

    HW = H * dh

    def specs(reverse):
        blk = (lambda s: nblk - 1 - s) if reverse else (lambda s: s)
        tok = lambda off: pl.BlockSpec((TB, HW), lambda s, f, p: (blk(s), off // HW))
        tab = pl.BlockSpec((TB, LANE), lambda s, f, p: (p[blk(s)], 0))
        sp = [tok(DQ_OFF), tok(DK_OFF), tok(DV_OFF), tab, tab,
              pl.BlockSpec((H, 2, 8, LANE), lambda s, f, p: (0, 0, 0, 0))]
        return sp, tok, pl.BlockSpec((TB, HW), lambda s, f, p: (blk(s), 0))

    sp, _, out = specs(False)
    o1 = pl.pallas_call(
        functools.partial(_ret_kernel, reverse=False, nblk=nblk), name="ret_fwd",
        grid_spec=pltpu.PrefetchScalarGridSpec(
            num_scalar_prefetch=2, grid=(nblk,), in_specs=sp, out_specs=out,
            scratch_shapes=[pltpu.VMEM((H, dh, dh), jnp.float32), pltpu.VMEM((H, TB, TB), jnp.float32)]),
        out_shape=jax.ShapeDtypeStruct((n, H * dh), jnp.float32),
        compiler_params=_cparams("arbitrary"),
    )(seq_first, pos_blk, h, h, h, cos, sin, dec)
    sp, tok, out = specs(True)
    sp = sp + [tok(DG_OFF), out]
    return pl.pallas_call(
        functools.partial(_ret_kernel, reverse=True, nblk=nblk), name="ret_bwd",
        grid_spec=pltpu.PrefetchScalarGridSpec(
            num_scalar_prefetch=2, grid=(nblk,), in_specs=sp, out_specs=out,
            scratch_shapes=[pltpu.VMEM((H, dh, dh), jnp.float32)]),
        out_shape=jax.ShapeDtypeStruct((n, H * dh), jnp.bfloat16),
        compiler_params=_cparams("arbitrary"),
    )(seq_last, pos_blk, h, h, h, cos, sin, dec, h, o1)


SWA_Q = 128
SWA_HALF = 64
assert all(w // (2 * d) == SWA_HALF for w, d in SWA_GROUPS)
NEG_BIG = -1e30


def _swa_kernel(hl_ref, hr_ref, q_ref, kl_ref, kc_ref, kr_ref, vl_ref, vc_ref, vr_ref,
                qcos_ref, qs1_ref, qs2_ref, kcos_ref, ks1_ref, ks2_ref, o_ref, lse_ref):
    f32, bf16 = jnp.float32, jnp.bfloat16
    j = pl.program_id(1)
    Q, HW = SWA_Q, SWA_HALF
    W = Q + 2 * HW
    ri = lax.broadcasted_iota(jnp.int32, (Q, W), 0)
    ci = lax.broadcasted_iota(jnp.int32, (Q, W), 1)
    rel = ci - HW - ri
    ok = (rel <= HW) & (rel >= -HW)
    ok = ok & ((ci >= HW) | (hl_ref[j] == 1)) & ((ci < HW + Q) | (hr_ref[j] == 1))
    lane = lax.broadcasted_iota(jnp.int32, (Q, LANE), 1)

    def rot(x, c_ref, s1_ref, s2_ref):
        return (x * c_ref[...] + pltpu.roll(x, LANE - ROPE_DIM // 2, axis=1) * s1_ref[...]
                + pltpu.roll(x, ROPE_DIM // 2, axis=1) * s2_ref[...])

    lse_tile = jnp.zeros((Q, LANE), f32)
    for hh in range(SWA_HEADS):
        sl = slice(hh * SWA_DH, (hh + 1) * SWA_DH)
        q = rot(q_ref[:, sl].astype(f32), qcos_ref, qs1_ref, qs2_ref) * (SWA_DH ** -0.5)
        k = jnp.concatenate([kl_ref[:, sl], kc_ref[:, sl], kr_ref[:, sl]], axis=0).astype(f32)
        k = rot(k, kcos_ref, ks1_ref, ks2_ref)
        v = jnp.concatenate([vl_ref[:, sl], vc_ref[:, sl], vr_ref[:, sl]], axis=0)
        s = lax.dot_general(q.astype(bf16), k.astype(bf16), (((1,), (1,)), ((), ())), preferred_element_type=f32)
        s = jnp.where(ok, s, NEG_BIG)
        m = jnp.max(s, -1, keepdims=True)
        p = jnp.exp(s - m)
        den = jnp.sum(p, -1, keepdims=True)
        o = jnp.dot(p.astype(bf16), v, preferred_element_type=f32) / den
        o_ref[:, sl] = o
        lse_tile = jnp.where(lane == hh, m + jnp.log(den), lse_tile)
    lse_ref[...] = lse_tile


def _swa_merge_kernel(*refs):
    ng = len(SWA_GROUPS)
    o_refs, l_refs, y_ref = refs[:ng], refs[ng:2 * ng], refs[2 * ng]
    scratch = refs[2 * ng + 1:]
    assert SWA_DH == LANE
    os_, ls_ = [], []
    si = 0
    for gi, (_, dil) in enumerate(SWA_GROUPS):
        if dil == 1:
            os_.append(lambda hh, r=o_refs[gi]: r[0, :, hh * LANE:(hh + 1) * LANE])
            ls_.append(l_refs[gi].at[0])
            continue
        o_s, l_s = scratch[si], scratch[si + 1]
        si += 2
        rows = l_s.shape[0] // dil
        for r in range(dil):
            for hh in range(SWA_HEADS):
                o_s[hh, pl.ds(r, rows, stride=dil), :] = o_refs[gi][r, :, hh * LANE:(hh + 1) * LANE]
            l_s[pl.ds(r, rows, stride=dil), :] = l_refs[gi][r]
        os_.append(lambda hh, s=o_s: s[hh])
        ls_.append(l_s)
    for hh in range(SWA_HEADS):
        ls = [r[:, hh:hh + 1] for r in ls_]
        m = functools.reduce(jnp.maximum, ls)
        ws = [jnp.exp(l - m) for l in ls]
        tot = functools.reduce(lambda a, b: a + b, ws)
        y = functools.reduce(lambda a, b: a + b, [w * o(hh) for w, o in zip(ws, os_)])
        y_ref[:, hh * LANE:(hh + 1) * LANE] = (y / tot).astype(y_ref.dtype)


def _proj_strided_kernel(a_ref, b_ref, o_ref, acc_ref, *, dil):
    acc = jnp.dot(a_ref[...], b_ref[...], preferred_element_type=jnp.float32)
    if dil == 1:
        o_ref[0] = acc.astype(o_ref.dtype)
        return
    rows = acc_ref.shape[1] // dil
    for s in range(acc_ref.shape[0]):
        acc_ref[s] = acc[:, s * LANE:(s + 1) * LANE]
        for r in range(dil):
            o_ref[r, :, s * LANE:(s + 1) * LANE] = acc_ref[s, pl.ds(r, rows, stride=dil), :].astype(o_ref.dtype)


def _proj_strided(a, b, dil, bm=1024, bn=512):
    n, K = a.shape
    N = b.shape[1]
    return pl.pallas_call(
        functools.partial(_proj_strided_kernel, dil=dil), name=f"swa_proj_dil{dil}", grid=(n // bm, N // bn),
        in_specs=[pl.BlockSpec((bm, K), lambda i, j: (i, 0)), pl.BlockSpec((K, bn), lambda i, j: (0, j))],
        out_specs=pl.BlockSpec((dil, bm // dil, bn), lambda i, j: (0, i, j)),
        out_shape=jax.ShapeDtypeStruct((dil, n // dil, N), jnp.bfloat16),
        scratch_shapes=[pltpu.VMEM((bn // LANE, bm, LANE), jnp.float32)],
        compiler_params=_cparams("parallel", "arbitrary"),
    )(a, b)


def _swa(xb, w_groups, seqs):
    n = xb.shape[0]
    Q, HW = SWA_Q, SWA_HALF
    GW = SWA_HEADS * SWA_DH
    outs, lses = [], []
    for gi, (window, dil) in enumerate(SWA_GROUPS):
        hs = _proj_strided(xb, w_groups[gi], dil)
        rows = n // dil
        nblk = rows // Q
        has_l, has_r = [], []
        for num, T in seqs:
            per = T // dil // Q
            assert per * Q * dil == T
            has_l += ([0] + [1] * (per - 1)) * num
            has_r += ([1] * (per - 1) + [0]) * num
        has_l = jnp.asarray(np.array(has_l, np.int32))
        has_r = jnp.asarray(np.array(has_r, np.int32))
        qpos = jnp.arange(Q) * dil
        kpos = (jnp.arange(Q + 2 * HW) - HW) * dil
        qt = _rope_tables(qpos, ROPE_DIM, ROPE_THETA, SWA_DH)
        kt = _rope_tables(kpos, ROPE_DIM, ROPE_THETA, SWA_DH)
        last64 = rows // HW - 1

        def center(c):
            return pl.BlockSpec((None, Q, GW), lambda r, j, hl, hr: (r, j, c))

        def left(c):
            return pl.BlockSpec((None, HW, GW), lambda r, j, hl, hr: (r, jnp.maximum(2 * j - 1, 0), c))

        def right(c):
            return pl.BlockSpec((None, HW, GW), lambda r, j, hl, hr: (r, jnp.minimum(2 * j + 2, last64), c))

        qtab = pl.BlockSpec((Q, SWA_DH), lambda r, j, hl, hr: (0, 0))
        ktab = pl.BlockSpec((Q + 2 * HW, SWA_DH), lambda r, j, hl, hr: (0, 0))
        o_g, lse_g = pl.pallas_call(
            _swa_kernel, name=f"swa_dil{dil}",
            grid_spec=pltpu.PrefetchScalarGridSpec(
                num_scalar_prefetch=2, grid=(dil, nblk),
                in_specs=[center(0), left(1), center(1), right(1), left(2), center(2), right(2),
                          qtab, qtab, qtab, ktab, ktab, ktab],
                out_specs=[pl.BlockSpec((None, Q, GW), lambda r, j, hl, hr: (r, j, 0)),
                           pl.BlockSpec((None, Q, LANE), lambda r, j, hl, hr: (r, j, 0))]),
            out_shape=[jax.ShapeDtypeStruct((dil, rows, GW), jnp.float32),
                       jax.ShapeDtypeStruct((dil, rows, LANE), jnp.float32)],
            compiler_params=_cparams("parallel", "parallel"),
        )(has_l, has_r, hs, hs, hs, hs, hs, hs, hs, *qt, *kt)
        outs.append(o_g)
        lses.append(lse_g)
    bm = 1024
    dils = [d for _, d in SWA_GROUPS]
    osp = [pl.BlockSpec((d, bm // d, GW), lambda i: (0, i, 0)) for d in dils]
    lsp = [pl.BlockSpec((d, bm // d, LANE), lambda i: (0, i, 0)) for d in dils]
    scratch = []
    for d in dils:
        if d > 1:
            scratch += [pltpu.VMEM((SWA_HEADS, bm, LANE), jnp.float32), pltpu.VMEM((bm, LANE), jnp.float32)]
    return pl.pallas_call(
        _swa_merge_kernel, name="swa_merge", grid=(n // bm,), in_specs=osp + lsp,
        out_specs=pl.BlockSpec((bm, GW), lambda i: (i, 0)),
        out_shape=jax.ShapeDtypeStruct((n, GW), jnp.bfloat16), scratch_shapes=scratch,
        compiler_params=_cparams("parallel"),
    )(*outs, *lses)


FFT_T2 = 128
FW = FNET_GROUPS * FNET_DG


def _dft_cos_sin(n_out, n_in, period):
    ang = (2.0 * np.pi / period) * np.mod(np.outer(np.arange(n_out), np.arange(n_in)), period)
    return np.cos(ang), np.sin(ang)


def _fnet_stage1_kernel(z_ref, m_ref, a_ref, *, cols):
    mat = m_ref[...]
    for c in range(cols):
        zr = z_ref[:, c * 2 * FW:c * 2 * FW + FW]
        zi = z_ref[:, c * 2 * FW + FW:(c + 1) * 2 * FW]
        a = jnp.dot(mat, jnp.concatenate([zr, zi], axis=0), preferred_element_type=jnp.float32)
        t1 = zr.shape[0]
        a_ref[:, c * 2 * FW:c * 2 * FW + FW] = a[:t1].astype(a_ref.dtype)
        a_ref[:, c * 2 * FW + FW:(c + 1) * 2 * FW] = a[t1:].astype(a_ref.dtype)


FNET_KB = 8


def _fnet_stage2_kernel(a_ref, g_ref, y_ref, *, scale):
    kw = g_ref.shape[1] // FNET_KB
    for j in range(FNET_KB):
        y = jnp.dot(g_ref[:, j * kw:(j + 1) * kw], a_ref[j], preferred_element_type=jnp.float32)
        y_ref[:, j * FW:(j + 1) * FW] = (y * scale).astype(y_ref.dtype)


def _fnet(h, seqs):
    n = h.shape[0]
    bf16 = jnp.bfloat16
    T2 = FFT_T2
    cc, ss = _dft_cos_sin(FNET_DG, FNET_DG, FNET_DG)
    eye = np.eye(FNET_GROUPS)
    wc = jnp.asarray(np.concatenate([np.kron(eye, cc), -np.kron(eye, ss)], axis=1), bf16)
    z = _matmul(h, wc, bf16, bn=2 * FW, a_col_blk=C_OFF // FW, name="fnet_channel_dft")
    zz = z.reshape(n // T2, T2 * 2 * FW)
    outs = []
    row0 = 0
    for num, T in seqs:
        T1 = T // T2
        assert T1 * T2 == T and T1 % 16 == 0
        c1, s1 = _dft_cos_sin(T1, T1, T1)
        mat = jnp.asarray(np.block([[c1, s1], [-s1, c1]]), bf16)
        cols = min(T2, max(1, 1024 // T1))
        rb0 = row0 // T1
        a = pl.pallas_call(
            functools.partial(_fnet_stage1_kernel, cols=cols), name=f"fnet_dft_t1_{T1}", grid=(num, T2 // cols),
            in_specs=[pl.BlockSpec((T1, cols * 2 * FW), lambda b, j: (rb0 + b, j)),
                      pl.BlockSpec((2 * T1, 2 * T1), lambda b, j: (0, 0))],
            out_specs=pl.BlockSpec((T1, cols * 2 * FW), lambda b, j: (b, j)),
            out_shape=jax.ShapeDtypeStruct((num * T1, T2 * 2 * FW), bf16),
            compiler_params=_cparams("parallel", "parallel"),
        )(zz, mat)
        ec, es = _dft_cos_sin(T, T2, T)
        g = jnp.asarray(np.stack([ec, es], axis=-1).reshape(T2, T1 * 2 * T2), bf16)
        y = pl.pallas_call(
            functools.partial(_fnet_stage2_kernel, scale=float(1.0 / np.sqrt(T * FNET_DG))), name=f"fnet_dft_t2_{T1}",
            grid=(num, T1 // FNET_KB),
            in_specs=[pl.BlockSpec((FNET_KB, 2 * T2, FW), lambda b, k1: (b * (T1 // FNET_KB) + k1, 0, 0)),
                      pl.BlockSpec((T2, FNET_KB * 2 * T2), lambda b, k1: (0, k1))],
            out_specs=pl.BlockSpec((T2, FNET_KB * FW), lambda b, k1: (b, k1)),
            out_shape=jax.ShapeDtypeStruct((num * T2, T1 * FW), bf16),
            compiler_params=_cparams("parallel", "parallel"),
        )(a.reshape(num * T1, 2 * T2, FW), g)
        outs.append(y.reshape(num * T, FW))
        row0 += num * T1
    return jnp.concatenate(outs, axis=0)


def _ffn_up_kernel(x_ref, w1_ref, w3_ref, h_ref, w1b_ref, w3b_ref):
    @pl.when(pl.program_id(2) == 0)
    def _():
        w1b_ref[...] = w1_ref[...].astype(w1b_ref.dtype)
        w3b_ref[...] = w3_ref[...].astype(w3b_ref.dtype)

    x = x_ref[...]
    h1 = jnp.dot(x, w1b_ref[...], preferred_element_type=jnp.float32)
    h3 = jnp.dot(x, w3b_ref[...], preferred_element_type=jnp.float32)
    h_ref[...] = (_silu(h1) * h3).astype(h_ref.dtype)


def _ffn_down_kernel(h_ref, w2_ref, s_ref, o_ref, w2b_ref):
    e = pl.program_id(0)

    @pl.when(pl.program_id(1) == 0)
    def _():
        w2b_ref[...] = w2_ref[...].astype(w2b_ref.dtype)

    D = w2_ref.shape[1]
    side = s_ref[...]
    lane = lax.broadcasted_iota(jnp.int32, side.shape, 1)
    mine = (lane == e) | (lane == e + SIDE_MID) | (lane == e + SIDE_LOW)
    gate = jnp.sum(jnp.where(mine, side.astype(jnp.float32), 0.0), axis=1, keepdims=True)
    y = jnp.dot(h_ref[...], w2b_ref[...], preferred_element_type=jnp.float32)
    o_ref[:, :D] = (y * gate).astype(o_ref.dtype)
    o_ref[:, D:] = side


def _expert_ffn(xe, w1, w3, w2, layer, bm=1024, bf=512):
    E, R, DY = xe.shape
    D = DY - LANE
    F = w1.shape[-1]
    wspec = pl.BlockSpec((None, None, D, bf), lambda e, f, i: (layer, e, 0, f))
    hmid = pl.pallas_call(
        _ffn_up_kernel, name="expert_ffn_up", grid=(E, F // bf, R // bm),
        in_specs=[pl.BlockSpec((None, bm, D), lambda e, f, i: (e, i, 0)), wspec, wspec],
        out_specs=pl.BlockSpec((None, bm, bf), lambda e, f, i: (e, i, f)),
        out_shape=jax.ShapeDtypeStruct((E, R, F), jnp.bfloat16),
        scratch_shapes=[pltpu.VMEM((D, bf), jnp.bfloat16), pltpu.VMEM((D, bf), jnp.bfloat16)],
        compiler_params=_cparams("arbitrary", "arbitrary", "arbitrary"),
    )(xe, w1, w3)
    return pl.pallas_call(
        _ffn_down_kernel, name="expert_ffn_down", grid=(E, R // bm),
        in_specs=[pl.BlockSpec((None, bm, F), lambda e, i: (e, i, 0)),
                  pl.BlockSpec((None, None, F, D), lambda e, i: (layer, e, 0, 0), pipeline_mode=pl.Buffered(1)),
                  pl.BlockSpec((None, bm, LANE), lambda e, i: (e, i, D // LANE))],
        out_specs=pl.BlockSpec((None, bm, D + LANE), lambda e, i: (e, i, 0)),
        out_shape=jax.ShapeDtypeStruct((E, R, D + LANE), jnp.bfloat16),
        scratch_shapes=[pltpu.VMEM((F, D), jnp.bfloat16)],
        compiler_params=_cparams("arbitrary", "arbitrary"),
    )(hmid, w2, xe)


COMB_BM = 128
COMB_G = 16
COMB_K = 256


def _combine_kernel(lo_ref, x_ref, g_ref, b_ref, ye_hbm, *rest, nblk_all, blk_off, want_bf16):
    if want_bf16:
        xf_ref, xb_ref, buf_ref, acc_ref, sem = rest
    else:
        xf_ref, buf_ref, acc_ref, sem = rest
    f32, bf16 = jnp.float32, jnp.bfloat16
    E = ye_hbm.shape[0]
    D = x_ref.shape[1]
    BM, G, KC = COMB_BM, COMB_G, COMB_K
    i = pl.program_id(0)
    nsteps = pl.num_programs(0)
    blk = i + blk_off
    slot = i % 2

    @pl.when(i == 0)
    def _():
        buf_ref[...] = jnp.zeros_like(buf_ref)

    def granule_copy(sl, e, src_row, dst_row):
        return pltpu.make_async_copy(ye_hbm.at[e, pl.ds(src_row, G)], buf_ref.at[sl, pl.ds(dst_row, G)], sem.at[sl])

    def gather_block(b, sl, start):
        off = jnp.int32(0)
        for e in range(E):
            lo = lo_ref[e * (nblk_all + 1) + b]
            hi = lo_ref[e * (nblk_all + 1) + b + 1]
            lo_al = (lo // G) * G
            ng = jnp.where(hi > lo, (hi - lo_al + G - 1) // G, 0)
            if start:
                def issue(g, carry, e=e, lo_al=lo_al, off=off):
                    granule_copy(sl, e, pl.multiple_of(lo_al + g * G, G), pl.multiple_of(off + g * G, G)).start()
                    return carry

                lax.fori_loop(0, ng, issue, 0)
            off = off + ng * G
        return off

    @pl.when(i == 0)
    def _():
        gather_block(blk, slot, True)

    @pl.when(i + 1 < nsteps)
    def _():
        gather_block(blk + 1, 1 - slot, True)

    off = gather_block(blk, slot, False)

    def wait_one(g, carry):
        granule_copy(slot, 0, 0, 0).wait()
        return carry

    lax.fori_loop(0, off // G, wait_one, 0)

    acc_ref[...] = jnp.zeros_like(acc_ref)
    t0 = (blk * BM).astype(f32)
    lane_tok = lax.broadcasted_iota(jnp.int32, (KC, BM), 1).astype(f32)
    row = lax.broadcasted_iota(jnp.int32, (KC, 1), 0)

    def chunk(c, carry):
        win = buf_ref[slot, pl.ds(pl.multiple_of(c * KC, KC), KC), :]
        tok = (win[:, D + SIDE_TOK:D + SIDE_TOK + 1].astype(f32) * TOK_RADIX
               + win[:, D + SIDE_TOK + 1:D + SIDE_TOK + 2].astype(f32))
        tok = jnp.where(row + c * KC < off, tok, -1.0)
        onehot_t = (tok - t0 == lane_tok).astype(bf16)
        acc_ref[...] += lax.dot_general(onehot_t, win[:, :D], (((0,), (0,)), ((), ())), preferred_element_type=f32)
        return carry

    lax.fori_loop(0, (off + KC - 1) // KC, chunk, 0)
    y = _ln_rows(DN_ALPHA * x_ref[...] + acc_ref[...], g_ref[...], b_ref[...])
    xf_ref[...] = y
    if want_bf16:
        xb_ref[...] = y.astype(bf16)


def _combine_ln(x, ye, lo, g, b, row_off, rows, want_bf16):
    n, D = x.shape
    E, R, DY = ye.shape
    BM = COMB_BM
    nblk_all = n // BM
    ob = row_off // BM
    kmax = -(-E * (BM + COMB_G) // COMB_K) * COMB_K
    irow = pl.BlockSpec((BM, D), lambda i, lo: (i + ob, 0))
    orow = pl.BlockSpec((BM, D), lambda i, lo: (i, 0))
    vec = pl.BlockSpec((1, D), lambda i, lo: (0, 0))
    out_shape = [jax.ShapeDtypeStruct((rows, D), jnp.float32)]
    out_specs = [orow]
    if want_bf16:
        out_shape.append(jax.ShapeDtypeStruct((rows, D), jnp.bfloat16))
        out_specs.append(orow)
    return pl.pallas_call(
        functools.partial(_combine_kernel, nblk_all=nblk_all, blk_off=ob, want_bf16=want_bf16), name="moe_combine_ln",
        grid_spec=pltpu.PrefetchScalarGridSpec(
            num_scalar_prefetch=1, grid=(rows // BM,),
            in_specs=[irow, vec, vec, pl.BlockSpec(memory_space=pl.ANY)], out_specs=out_specs,
            scratch_shapes=[pltpu.VMEM((2, kmax, DY), jnp.bfloat16), pltpu.VMEM((BM, D), jnp.float32),
                            pltpu.SemaphoreType.DMA((2,))]),
        out_shape=out_shape,
        compiler_params=_cparams("arbitrary"),
    )(lo, x, g.reshape(1, D), b.reshape(1, D), ye)


def _route_kernel(all_ref, a_ref, idx_ref, cprev_ref, thr_ref, need_ref, *, cap, tok_off):
    f32, bf16, i32 = jnp.float32, jnp.bfloat16, jnp.int32
    e = pl.program_id(0)

    def total(x):
        return jnp.sum(jnp.sum(x, axis=2, keepdims=True), axis=1, keepdims=True)

    @pl.when(e == 0)
    def _():
        bits_all = pltpu.bitcast(all_ref[...], i32)

        def refine(i, prefix):
            cand = prefix | jnp.left_shift(jnp.int32(1), 30 - i)
            return jnp.where(total((bits_all >= cand).astype(i32)) >= cap, cand, prefix)

        thr_all = lax.fori_loop(0, 31, refine, jnp.zeros((all_ref.shape[0], 1, 1), i32))
        thr_ref[...] = jnp.broadcast_to(thr_all, thr_ref.shape)
        need_ref[...] = jnp.broadcast_to(cap - total((bits_all > thr_all).astype(i32)), need_ref.shape)

    a = a_ref[...]
    R = a.shape[0]
    bits = pltpu.bitcast(a, i32)
    thr = thr_ref[e][0:1, 0:1]
    need = need_ref[e][0:1, 0:1]
    gt, eq = bits > thr, bits == thr

    li = lax.broadcasted_iota(i32, (LANE, LANE), 0)
    lj = lax.broadcasted_iota(i32, (LANE, LANE), 1)
    upto = (li <= lj).astype(bf16)
    ri = lax.broadcasted_iota(i32, (R, R), 0)
    rj = lax.broadcasted_iota(i32, (R, R), 1)
    before = (rj < ri).astype(bf16)

    def counts(mask):
        inc = jnp.dot(mask.astype(bf16), upto, preferred_element_type=f32)
        tot = jnp.broadcast_to(inc[:, LANE - 1:LANE], (R, LANE)).astype(bf16)
        return inc, jnp.dot(before, tot, preferred_element_type=f32)

    eq_inc, eq_prev = counts(eq)
    sel = gt | (eq & (eq_prev + eq_inc - 1.0 < need.astype(f32)))
    cl, cprev = counts(sel)
    cprev_ref[...] = cprev[:, :1].astype(i32)

    selb = sel.astype(bf16)
    c_upto = jnp.sum(lax.dot_general(selb, (ri <= rj).astype(bf16), (((0,), (0,)), ((), ())),
                                     preferred_element_type=f32), axis=0, keepdims=True)
    c_before = jnp.sum(lax.dot_general(selb, (ri < rj).astype(bf16), (((0,), (0,)), ((), ())),
                                       preferred_element_type=f32), axis=0, keepdims=True)
    slot = lax.broadcasted_iota(i32, (cap, 1), 0).astype(f32)
    onehot = ((c_before <= slot) & (slot < c_upto)).astype(bf16)
    lane = lax.broadcasted_iota(i32, (R, LANE), 1)
    cp = cprev.astype(i32)
    aux = jnp.where(lane == 0, cp // TOK_RADIX, jnp.where(lane == 1, cp % TOK_RADIX,
                    jnp.where(lane == 2, lax.broadcasted_iota(i32, (R, LANE), 0), 0)))
    rhs = jnp.concatenate([cl.astype(bf16), aux.astype(f32).astype(bf16)], axis=1)
    m = jnp.dot(onehot, rhs, preferred_element_type=f32)
    base = m[:, LANE:LANE + 1] * TOK_RADIX + m[:, LANE + 1:LANE + 2]
    chunk = m[:, LANE + 2:LANE + 3]
    pos = jnp.sum((m[:, :LANE] <= slot - base).astype(f32), axis=1, keepdims=True)
    idx_ref[...] = (chunk * LANE + pos).astype(i32) + tok_off


def _route(aff_t, cap, tok_off):
    E, R, _ = aff_t.shape
    return pl.pallas_call(
        functools.partial(_route_kernel, cap=cap, tok_off=tok_off), name="moe_route", grid=(E,),
        in_specs=[pl.BlockSpec((E, R, LANE), lambda e: (0, 0, 0)), pl.BlockSpec((None, R, LANE), lambda e: (e, 0, 0))],
        out_specs=[pl.BlockSpec((None, cap, 1), lambda e: (e, 0, 0)), pl.BlockSpec((None, R, 1), lambda e: (e, 0, 0))],
        out_shape=[jax.ShapeDtypeStruct((E, cap, 1), jnp.int32), jax.ShapeDtypeStruct((E, R, 1), jnp.int32)],
        scratch_shapes=[pltpu.VMEM((E, 8, LANE), jnp.int32), pltpu.VMEM((E, 8, LANE), jnp.int32)],
        compiler_params=_cparams("arbitrary"),
    )(aff_t, aff_t)


def _expert_choice(xs, aff, seqs, w1, w3, w2, layer):
    assert COMB_BM == LANE
    aff_t = _aff_chunks(aff)
    idxs, los = [], []
    start = slot0 = 0
    for num, T in seqs:
        m = num * T
        cap = max(1, CAPACITY_FACTOR * m // N_EXPERTS)
        idx, cprev = _route(aff_t[:, start // LANE:(start + m) // LANE], cap, start)
        idxs.append(idx[..., 0])
        los.append(cprev[..., 0] + slot0)
        start += m
        slot0 += cap
    lo = jnp.concatenate(los + [jnp.full((N_EXPERTS, 1), slot0, jnp.int32)], axis=1).reshape(-1)
    ye = _expert_ffn(xs[jnp.concatenate(idxs, axis=1)], w1, w3, w2, layer)
    return ye, lo


def _pack_w_in(w_in):
    sizes = (GLA_HEADS * GLA_DK, GLA_HEADS * GLA_DK, GLA_HEADS * GLA_DV, GLA_HEADS * GLA_DV, 2 * GLA_RANK,
             SWA_NH * SWA_DH, SWA_NH * SWA_DH, SWA_NH * SWA_DH, FNET_GROUPS * FNET_DG,
             RET_HEADS * RET_DH, RET_HEADS * RET_DH, RET_HEADS * RET_DH, RET_HEADS * RET_DH)
    offs = np.concatenate([[0], np.cumsum(sizes)])
    part = lambda i: w_in[..., offs[i]:offs[i + 1]]
    a_q, a_k = part(0), part(1)
    cols = []
    for hh in range(GLA_HEADS):
        cols += [a_q[..., hh * GLA_DK:(hh + 1) * GLA_DK], a_k[..., hh * GLA_DK:(hh + 1) * GLA_DK]]
    cols += [part(2), part(3), part(4),
             jnp.zeros(w_in.shape[:2] + (C_OFF - LR_OFF - 2 * GLA_RANK,), w_in.dtype)]
    cols += [part(i) for i in range(8, 13)]
    out = jnp.concatenate(cols, axis=-1)
    assert out.shape[-1] == IN_COLS_P
    GW = SWA_HEADS * SWA_DH
    swa = [jnp.concatenate([part(i)[..., g * GW:(g + 1) * GW] for i in (5, 6, 7)], axis=-1)
           for g in range(len(SWA_GROUPS))]
    return out, swa


def _block_tables(seqs, blk):
    first, last, pos = [], [], []
    for num, T in seqs:
        per = T // blk
        assert per * blk == T
        for _ in range(num):
            first += [1] + [0] * (per - 1)
            last += [0] * (per - 1) + [1]
            pos += list(range(per))
    mk = lambda a: jnp.asarray(np.array(a, np.int32))
    return mk(first), mk(last), mk(pos)


def kernel(x_prompt, x_sample, ln_in_g, ln_in_b, w_in, gla_w_up, gla_b_up, ret_decay, w_gate, b_gate, w_branch,
           w_out, ln1_g, ln1_b, w_router, w_e1, w_e3, w_e2, ln2_g, ln2_b):
    bf16 = jnp.bfloat16
    D = D_MODEL
    seqs = (x_prompt.shape[:2], x_sample.shape[:2])
    n_p, n_s = x_prompt.shape[0] * x_prompt.shape[1], x_sample.shape[0] * x_sample.shape[1]
    n = n_p + n_s
    t_max = max(T for _, T in seqs)
    seq_first, seq_last, pos_blk = _block_tables(seqs, SEQ_BLOCK)

    w_in_p, w_swa = _pack_w_in(w_in.astype(bf16))
    wo = w_out.astype(bf16)
    wr = jnp.pad(w_router, ((0, 0), (0, 0), (0, LANE - N_EXPERTS))).astype(bf16)
    wg, wb, we1, we3, we2 = w_gate, w_branch, w_e1, w_e3, w_e2

    xf, xb = _ln_in(x_prompt.reshape(n_p, D), x_sample.reshape(n_s, D), ln_in_g, ln_in_b)
    for l in range(DEPTH):
        h = _matmul(xb, w_in_p[l], bf16, bm=1024, bn=IN_BN, name="in_proj")
        wup, bup = _pack_gla_up(gla_w_up[l], gla_b_up[l])
        ys = (_gla(h, seq_first, seq_last, wup, bup),
              _swa(xb, [w[l] for w in w_swa], seqs),
              _fnet(h, seqs),
              _retention(h, seq_first, seq_last, pos_blk, ret_decay[l], t_max))
        merged = _merge(xb, wg, b_gate[l], ys, wb, l)
        x1f, x1s, aff = _outproj(merged, wo[l], xf, ln1_g[l], ln1_b[l], wr[l])
        ye, lo = _expert_choice(x1s, aff, seqs, we1, we3, we2, l)
        if l + 1 < DEPTH:
            xf, xb = _combine_ln(x1f, ye, lo, ln2_g[l], ln2_b[l], 0, n, True)
        else:
            y_p, = _combine_ln(x1f, ye, lo, ln2_g[l], ln2_b[l], 0, n_p, False)
            y_s, = _combine_ln(x1f, ye, lo, ln2_g[l], ln2_b[l], n_p, n_s, False)
    return (y_p.reshape(x_prompt.shape), y_s.reshape(x_sample.shape))
```

```python
import functools

import jax
import jax.numpy as jnp
import numpy as np
from jax import lax
from jax.experimental import pallas as pl
from jax.experimental.pallas import tpu as pltpu

D_MODEL = 2048
DEPTH = 2
GLA_HEADS, GLA_DK, GLA_DV, GLA_RANK, GLA_TAU, GLA_CHUNK = 4, 64, 128, 16, 16.0, 64
SWA_GROUPS = ((128, 1), (512, 4), (2048, 16))
SWA_HEADS, SWA_DH = 4, 128
SWA_NH = SWA_HEADS * len(SWA_GROUPS)
SWA_BLOCK = 64
ROPE_THETA, ROPE_DIM = 500000.0, SWA_DH // 4
FNET_GROUPS, FNET_DG = 4, 128
RET_HEADS, RET_DH, RET_THETA = 4, 128, 10000.0
N_BRANCH, BRANCH_W = 4, 512
N_EXPERTS, EXPERT_FF, CAPACITY_FACTOR = 16, 2048, 2
DN_ALPHA = (2 * DEPTH) ** 0.25
LN_EPS = 1e-5

VMEM_LIMIT_BYTES = 48 * 1024 * 1024
LANE = 128

QK_OFF = 0
AV_OFF = QK_OFF + GLA_HEADS * LANE
AG_OFF = AV_OFF + GLA_HEADS * GLA_DV
LR_OFF = AG_OFF + GLA_HEADS * GLA_DV
C_OFF = LR_OFF + 4 * LANE
DQ_OFF = C_OFF + FNET_GROUPS * FNET_DG
DK_OFF = DQ_OFF + RET_HEADS * RET_DH
DV_OFF = DK_OFF + RET_HEADS * RET_DH
DG_OFF = DV_OFF + RET_HEADS * RET_DH
IN_USED = DG_OFF + RET_HEADS * RET_DH
IN_BN = 1536
IN_COLS_P = -(-IN_USED // IN_BN) * IN_BN
assert IN_COLS_P == IN_USED

SEQ_BLOCK = 512


def _cparams(*sem):
    return pltpu.CompilerParams(dimension_semantics=sem, vmem_limit_bytes=VMEM_LIMIT_BYTES)


def _mm_kernel(a_ref, b_ref, o_ref):
    o_ref[...] = jnp.dot(a_ref[...], b_ref[...], preferred_element_type=jnp.float32).astype(o_ref.dtype)


def _matmul(a, b, out_dtype, bm=1024, bn=512, a_col_blk=0, name="matmul"):
    batched = a.ndim == 3
    M = a.shape[-2]
    K, N = b.shape[-2:]
    bm = min(bm, M)
    bn = min(bn, N)
    assert M % bm == 0 and N % bn == 0, (M, N, bm, bn)
    if batched:
        E = a.shape[0]
        grid = (E, M // bm, N // bn)
        in_specs = [pl.BlockSpec((None, bm, K), lambda e, i, j: (e, i, 0)),
                    pl.BlockSpec((None, K, bn), lambda e, i, j: (e, 0, j))]
        out_specs = pl.BlockSpec((None, bm, bn), lambda e, i, j: (e, i, j))
        out_shape = jax.ShapeDtypeStruct((E, M, N), out_dtype)
        sem = ("parallel", "parallel", "arbitrary")
    else:
        grid = (M // bm, N // bn)
        in_specs = [pl.BlockSpec((bm, K), lambda i, j: (i, a_col_blk)),
                    pl.BlockSpec((K, bn), lambda i, j: (0, j))]
        out_specs = pl.BlockSpec((bm, bn), lambda i, j: (i, j))
        out_shape = jax.ShapeDtypeStruct((M, N), out_dtype)
        sem = ("parallel", "arbitrary")
    return pl.pallas_call(
        _mm_kernel, name=name, grid=grid, in_specs=in_specs, out_specs=out_specs, out_shape=out_shape,
        compiler_params=_cparams(*sem),
    )(a, b)


def _mm_w32_kernel(a_ref, w_ref, o_ref, wb_ref):
    @pl.when(pl.program_id(1) == 0)
    def _():
        wb_ref[...] = w_ref[...].astype(wb_ref.dtype)

    o_ref[...] = jnp.dot(a_ref[...], wb_ref[...], preferred_element_type=jnp.float32).astype(o_ref.dtype)


def _matmul_w32(a, w, out_dtype, bm=1024, bn=768, name="matmul_w32"):
    M, K = a.shape
    N = w.shape[1]
    assert M % bm == 0 and N % bn == 0, (M, N, bm, bn)
    return pl.pallas_call(
        _mm_w32_kernel, name=name, grid=(N // bn, M // bm),
        in_specs=[pl.BlockSpec((bm, K), lambda j, i: (i, 0)), pl.BlockSpec((K, bn), lambda j, i: (0, j))],
        out_specs=pl.BlockSpec((bm, bn), lambda j, i: (i, j)),
        out_shape=jax.ShapeDtypeStruct((M, N), out_dtype),
        scratch_shapes=[pltpu.VMEM((K, bn), jnp.bfloat16)],
        compiler_params=_cparams("arbitrary", "arbitrary"),
    )(a, w)


def _ln_rows(x, g, b):
    mu = jnp.mean(x, -1, keepdims=True)
    xc = x - mu
    var = jnp.mean(xc * xc, -1, keepdims=True)
    return xc * lax.rsqrt(var + LN_EPS) * g + b


def _ln_in_kernel(xa_ref, xc_ref, g_ref, b_ref, xf_ref, xb_ref, *, na_blocks):
    i = pl.program_id(0)

    def emit(x_ref):
        y = _ln_rows(x_ref[...], g_ref[...], b_ref[...])
        xf_ref[...] = y
        xb_ref[...] = y.astype(jnp.bfloat16)

    pl.when(i < na_blocks)(lambda: emit(xa_ref))
    pl.when(i >= na_blocks)(lambda: emit(xc_ref))


def _ln_in(xa, xc, g, b, bm=512):
    (na, D), nc = xa.shape, xc.shape[0]
    nab, ncb = na // bm, nc // bm
    row = pl.BlockSpec((bm, D), lambda i: (i, 0))
    vec = pl.BlockSpec((1, D), lambda i: (0, 0))
    return pl.pallas_call(
        functools.partial(_ln_in_kernel, na_blocks=nab), name="ln_in", grid=(nab + ncb,),
        in_specs=[pl.BlockSpec((bm, D), lambda i: (jnp.minimum(i, nab - 1), 0)),
                  pl.BlockSpec((bm, D), lambda i: (jnp.maximum(i - nab, 0), 0)), vec, vec],
        out_specs=[row, row],
        out_shape=[jax.ShapeDtypeStruct((na + nc, D), jnp.float32), jax.ShapeDtypeStruct((na + nc, D), jnp.bfloat16)],
        compiler_params=_cparams("arbitrary"))(xa, xc, g.reshape(1, D), b.reshape(1, D))


def _merge_kernel(xb_ref, wg_ref, bg_ref, ya_ref, yb_ref, yc_ref, yd_ref, wb_ref, o_ref, wgb_ref, wbb_ref):
    @pl.when(pl.program_id(1) == 0)
    def _():
        wgb_ref[...] = wg_ref[...].astype(wgb_ref.dtype)
        wbb_ref[...] = wb_ref[...].astype(wbb_ref.dtype)

    xb = xb_ref[...]
    acc = None
    for i, y_ref in enumerate((ya_ref, yb_ref, yc_ref, yd_ref)):
        gate = jax.nn.sigmoid(jnp.dot(xb, wgb_ref[i], preferred_element_type=jnp.float32) + bg_ref[i])
        term = gate * jnp.dot(y_ref[...], wbb_ref[i], preferred_element_type=jnp.float32)
        acc = term if acc is None else acc + term
    o_ref[...] = acc.astype(o_ref.dtype)


def _merge(xb, wg, bg, ys, wb, layer, bm=1024, bn=256):
    n, D = xb.shape
    W = ys[0].shape[1]
    y_spec = pl.BlockSpec((bm, W), lambda j, i: (i, 0))
    return pl.pallas_call(
        _merge_kernel, name="branch_merge", grid=(D // bn, n // bm),
        in_specs=[pl.BlockSpec((bm, D), lambda j, i: (i, 0)),
                  pl.BlockSpec((None, N_BRANCH, D, bn), lambda j, i: (layer, 0, 0, j)),
                  pl.BlockSpec((N_BRANCH, 1, bn), lambda j, i: (0, 0, j)),
                  y_spec, y_spec, y_spec, y_spec,
                  pl.BlockSpec((None, N_BRANCH, W, bn), lambda j, i: (layer, 0, 0, j))],
        out_specs=pl.BlockSpec((bm, bn), lambda j, i: (i, j)),
        out_shape=jax.ShapeDtypeStruct((n, D), jnp.bfloat16),
        scratch_shapes=[pltpu.VMEM((N_BRANCH, D, bn), jnp.bfloat16), pltpu.VMEM((N_BRANCH, W, bn), jnp.bfloat16)],
        compiler_params=_cparams("arbitrary", "arbitrary"),
    )(xb, wg, bg.reshape(N_BRANCH, 1, D), *ys, wb)


OUTPROJ_SUB = 256


TOK_RADIX = 256
SIDE_MID, SIDE_LOW, SIDE_TOK = N_EXPERTS, 2 * N_EXPERTS, 3 * N_EXPERTS


def _outproj_kernel(m_ref, w_ref, x_ref, g_ref, b_ref, wr_ref, xf_ref, xb_ref, aff_ref):
    f32, bf16 = jnp.float32, jnp.bfloat16
    D = x_ref.shape[1]
    for s in range(m_ref.shape[0] // OUTPROJ_SUB):
        rows = pl.ds(s * OUTPROJ_SUB, OUTPROJ_SUB)
        r = jnp.dot(m_ref[rows, :], w_ref[...], preferred_element_type=f32)
        y = _ln_rows(DN_ALPHA * x_ref[rows, :] + r, g_ref[...], b_ref[...])
        yb = y.astype(bf16)
        xf_ref[rows, :] = y
        xb_ref[rows, :D] = yb
        logits = jnp.dot(yb, wr_ref[...], preferred_element_type=f32)
        lane = lax.broadcasted_iota(jnp.int32, logits.shape, 1)
        logits = jnp.where(lane < N_EXPERTS, logits, -jnp.inf)
        e = jnp.exp(logits - jnp.max(logits, -1, keepdims=True))
        aff = e / jnp.sum(e, -1, keepdims=True)
        aff_ref[rows, :] = aff
        hi = aff.astype(bf16).astype(f32)
        mid = (aff - hi).astype(bf16).astype(f32)
        low = (aff - hi - mid).astype(bf16).astype(f32)
        tok = (pl.program_id(0) * m_ref.shape[0] + s * OUTPROJ_SUB
               + lax.broadcasted_iota(jnp.int32, logits.shape, 0))
        side = hi + pltpu.roll(mid, SIDE_MID, axis=1) + pltpu.roll(low, SIDE_LOW, axis=1)
        side = jnp.where(lane == SIDE_TOK, (tok // TOK_RADIX).astype(f32),
                         jnp.where(lane == SIDE_TOK + 1, (tok % TOK_RADIX).astype(f32), side))
        xb_ref[rows, D:] = side.astype(bf16)


def _outproj(merged, w_out, x, g, b, w_router_p, bm=512):
    n, D = x.shape
    row = pl.BlockSpec((bm, D), lambda i: (i, 0))
    vec = pl.BlockSpec((1, D), lambda i: (0, 0))
    return pl.pallas_call(
        _outproj_kernel, name="outproj_ln_router", grid=(n // bm,),
        in_specs=[row, pl.BlockSpec((D, D), lambda i: (0, 0)), row, vec, vec,
                  pl.BlockSpec((D, LANE), lambda i: (0, 0))],
        out_specs=[row, pl.BlockSpec((bm, D + LANE), lambda i: (i, 0)), pl.BlockSpec((bm, LANE), lambda i: (i, 0))],
        out_shape=[jax.ShapeDtypeStruct((n, D), jnp.float32), jax.ShapeDtypeStruct((n, D + LANE), jnp.bfloat16),
                   jax.ShapeDtypeStruct((n, LANE), jnp.float32)],
        compiler_params=_cparams("parallel"),
    )(merged, w_out, x, g.reshape(1, D), b.reshape(1, D), w_router_p)


def _aff_t_kernel(a_ref, o_ref):
    for s in range(a_ref.shape[0] // LANE):
        o_ref[:, s, :] = a_ref[s * LANE:(s + 1) * LANE, :].T[:N_EXPERTS, :]


def _aff_chunks(aff, bm=1024):
    n = aff.shape[0]
    return pl.pallas_call(
        _aff_t_kernel, name="aff_chunks", grid=(n // bm,),
        in_specs=[pl.BlockSpec((bm, LANE), lambda i: (i, 0))],
        out_specs=pl.BlockSpec((N_EXPERTS, bm // LANE, LANE), lambda i: (0, i, 0)),
        out_shape=jax.ShapeDtypeStruct((N_EXPERTS, n // LANE, LANE), jnp.float32),
        compiler_params=_cparams("parallel"),
    )(aff)


def _log_sigmoid(z):
    return jnp.minimum(z, 0.0) - jnp.log(1.0 + jnp.exp(-jnp.abs(z)))


def _headnorm(o):
    mu = jnp.mean(o, -1, keepdims=True)
    oc = o - mu
    var = jnp.mean(oc * oc, -1, keepdims=True)
    return oc * lax.rsqrt(var + LN_EPS)


def _silu(x):
    return x * jax.nn.sigmoid(x)


def _chunk_cumsum(g, chunk, inclusive):
    row = lax.broadcasted_iota(jnp.int32, g.shape, 0) % chunk
    b = g
    s = 1
    while s < chunk:
        b = b + jnp.where(row >= s, pltpu.roll(b, s, axis=0), 0.0)
        s *= 2
    return b if inclusive else b - g


def _gla_kernel(flag_ref, qk_ref, v_ref, lr_ref, wup_ref, bup_ref, *rest, reverse, nblk):
    s_idx = pl.program_id(0)
    blk = nblk - 1 - s_idx if reverse else s_idx

    @pl.when(flag_ref[blk] == 1)
    def _():
        rest[-1][...] = jnp.zeros_like(rest[-1])

    for hh in range(GLA_HEADS):
        _gla_head(hh, qk_ref, v_ref, lr_ref, wup_ref, bup_ref, *rest, reverse=reverse)


def _gla_head(hh, qk_ref, v_ref, lr_ref, wup_ref, bup_ref, *rest, reverse):
    if reverse:
        gate_ref, of_ref, o_ref, st_ref = rest
    else:
        o_ref, st_ref = rest
    f32, bf16 = jnp.float32, jnp.bfloat16
    C = GLA_CHUNK
    TB = qk_ref.shape[0]
    nc = TB // C
    sl = slice(hh * LANE, (hh + 1) * LANE)

    is_q = lax.broadcasted_iota(jnp.int32, (1, LANE), 1) < GLA_DK
    qk = qk_ref[:, sl].astype(f32) * jnp.where(is_q, GLA_DK ** -0.5, 1.0)
    z = jnp.dot(lr_ref[...], wup_ref[hh], preferred_element_type=f32) + bup_ref[hh]
    g = _log_sigmoid(z) * (1.0 / GLA_TAU)
    cs = _chunk_cumsum(g, C, inclusive=not reverse)
    cs3 = cs.reshape(nc, C, LANE)
    mid = cs3[:, C // 2:C // 2 + 1, :]
    if reverse:
        tot3 = cs3[:, C - 1:C, :] + g.reshape(nc, C, LANE)[:, C - 1:C, :]
        edge3 = jnp.where(is_q, tot3 - cs3, cs3)
        mid3 = jnp.where(is_q, mid - cs3, cs3 - mid)
    else:
        tot3 = cs3[:, C - 1:C, :]
        edge3 = jnp.where(is_q, cs3, tot3 - cs3)
        mid3 = jnp.where(is_q, cs3 - mid, mid - cs3)
    x_edge = qk * jnp.exp(edge3).reshape(TB, LANE)
    x_mid = qk * jnp.exp(mid3).reshape(TB, LANE)
    qe = jnp.where(is_q, x_edge, 0.0).astype(bf16).reshape(nc, C, LANE)
    kd = jnp.where(is_q, pltpu.roll(x_edge, GLA_DK, axis=1), 0.0).astype(bf16).reshape(nc, C, LANE)
    qm = jnp.where(is_q, x_mid, 0.0).astype(bf16).reshape(nc, C, LANE)
    km = pltpu.roll(x_mid, GLA_DK, axis=1).astype(bf16).reshape(nc, C, LANE)
    etot = jnp.exp(tot3)
    v = v_ref[:, sl]
    ri = lax.broadcasted_iota(jnp.int32, (C, C), 0)
    ci = lax.broadcasted_iota(jnp.int32, (C, C), 1)
    keep = (ci > ri) if reverse else (ci <= ri)

    st = st_ref[hh]
    outs = [None] * nc
    for c in (range(nc - 1, -1, -1) if reverse else range(nc)):
        vc = v[c * C:(c + 1) * C, :]
        att = lax.dot_general(qm[c], km[c], (((1,), (1,)), ((), ())), preferred_element_type=f32)
        att = jnp.where(keep, att, 0.0).astype(bf16)
        o_c = jnp.dot(att, vc, preferred_element_type=f32)
        o_c = o_c + lax.dot_general(qe[c], st.astype(bf16), (((1,), (1,)), ((), ())), preferred_element_type=f32)
        upd = lax.dot_general(vc, kd[c], (((0,), (0,)), ((), ())), preferred_element_type=f32)
        st = st * etot[c] + upd
        outs[c] = o_c
    st_ref[hh] = st
    o = jnp.concatenate(outs, axis=0)
    if reverse:
        o = _headnorm(o + of_ref[:, sl])
        o_ref[:, sl] = (o * _silu(gate_ref[:, sl].astype(f32))).astype(o_ref.dtype)
    else:
        o_ref[:, sl] = o


def _gla(h, seq_first, seq_last, wup, bup):
    n = h.shape[0]
    TB = SEQ_BLOCK
    nblk = n // TB
    H = GLA_HEADS

    def col(off):
        return off // LANE

    HW = H * LANE

    def specs(reverse):
        blk = (lambda s: nblk - 1 - s) if reverse else (lambda s: s)
        sp = [pl.BlockSpec((TB, HW), lambda s, f: (blk(s), QK_OFF // HW)),
              pl.BlockSpec((TB, HW), lambda s, f: (blk(s), AV_OFF // HW)),
              pl.BlockSpec((TB, LANE), lambda s, f: (blk(s), col(LR_OFF))),
              pl.BlockSpec((None, H, LANE, LANE), lambda s, f: (1 if reverse else 0, 0, 0, 0)),
              pl.BlockSpec((None, H, 1, LANE), lambda s, f: (1 if reverse else 0, 0, 0, 0))]
        out = pl.BlockSpec((TB, HW), lambda s, f: (blk(s), 0))
        return sp, out, blk

    scratch = [pltpu.VMEM((H, GLA_DV, LANE), jnp.float32)]
    sp, out, _ = specs(False)
    o_f = pl.pallas_call(
        functools.partial(_gla_kernel, reverse=False, nblk=nblk), name="gla_fwd",
        grid_spec=pltpu.PrefetchScalarGridSpec(num_scalar_prefetch=1, grid=(nblk,), in_specs=sp, out_specs=out,
                                               scratch_shapes=scratch),
        out_shape=jax.ShapeDtypeStruct((n, H * GLA_DV), jnp.float32),
        compiler_params=_cparams("arbitrary"),
    )(seq_first, h, h, h, wup, bup)
    sp, out, blk = specs(True)
    sp = sp + [pl.BlockSpec((TB, HW), lambda s, f: (blk(s), AG_OFF // HW)), out]
    return pl.pallas_call(
        functools.partial(_gla_kernel, reverse=True, nblk=nblk), name="gla_bwd",
        grid_spec=pltpu.PrefetchScalarGridSpec(num_scalar_prefetch=1, grid=(nblk,), in_specs=sp, out_specs=out,
                                               scratch_shapes=scratch),
        out_shape=jax.ShapeDtypeStruct((n, H * GLA_DV), jnp.bfloat16),
        compiler_params=_cparams("arbitrary"),
    )(seq_last, h, h, h, wup, bup, h, o_f)


def _pack_gla_up(w_up, b_up):
    H, dk, r = GLA_HEADS, GLA_DK, GLA_RANK
    assert 2 * dk == LANE
    w = w_up.reshape(2, r, H, dk).transpose(0, 2, 1, 3)
    wp = jnp.zeros((2, H, LANE, dk), jnp.float32)
    wp = wp.at[0, :, 0:r].set(w[0]).at[1, :, r:2 * r].set(w[1])
    bp = b_up.reshape(2, H, 1, dk).astype(jnp.float32)
    return jnp.concatenate([wp, wp], axis=-1).astype(jnp.bfloat16), jnp.concatenate([bp, bp], axis=-1)


def _rope_tables(pos, dim, theta, width):
    half = dim // 2
    T = pos.shape[0]
    inv = 1.0 / (theta ** (jnp.arange(half, dtype=jnp.float32) / half))
    ang = pos.astype(jnp.float32)[:, None] * inv[None, :]
    c, s = jnp.cos(ang), jnp.sin(ang)
    rest = width - dim
    cos = jnp.concatenate([c, c, jnp.ones((T, rest), jnp.float32)], axis=1)
    sin_lo = jnp.concatenate([-s, jnp.zeros((T, half + rest), jnp.float32)], axis=1)
    sin_hi = jnp.concatenate([jnp.zeros((T, half), jnp.float32), s, jnp.zeros((T, rest), jnp.float32)], axis=1)
    return cos, sin_lo, sin_hi


def _ret_kernel(flag_ref, pos_ref, q_ref, k_ref, v_ref, cos_ref, sin_ref, dec_ref, *rest, reverse, nblk):
    s_idx = pl.program_id(0)
    blk = nblk - 1 - s_idx if reverse else s_idx

    @pl.when(flag_ref[blk] == 1)
    def _():
        st_ref = rest[-1] if reverse else rest[-2]
        st_ref[...] = jnp.zeros_like(st_ref)

    for hh in range(RET_HEADS):
        _ret_head(hh, s_idx, q_ref, k_ref, v_ref, cos_ref, sin_ref, dec_ref, *rest, reverse=reverse)


def _ret_head(hh, s_idx, q_ref, k_ref, v_ref, cos_ref, sin_ref, dec_ref, *rest, reverse):
    if reverse:
        gate_ref, o1_ref, o_ref, st_ref = rest
    else:
        o_ref, st_ref, dm_ref = rest
    f32, bf16 = jnp.float32, jnp.bfloat16
    TB = q_ref.shape[0]
    sl = slice(hh * RET_DH, (hh + 1) * RET_DH)

    dec = dec_ref[hh]
    lg = -(jnp.maximum(dec, 0.0) + jnp.log1p(jnp.exp(-jnp.abs(dec))))
    lg_f, lg_b = lg[0, 0:1, 0:1], lg[1, 0:1, 0:1]

    cos, sin = cos_ref[...], sin_ref[...]

    def rot(x):
        return x * cos + pltpu.roll(x, RET_DH // 2, axis=1) * sin

    qr = rot(q_ref[:, sl].astype(f32))
    kr = rot(k_ref[:, sl].astype(f32)) * (RET_DH ** -0.5)
    v = v_ref[:, sl]
    idx = lax.broadcasted_iota(jnp.int32, (TB, 1), 0).astype(f32)
    st = st_ref[hh]
    cross = jnp.dot(qr.astype(bf16), st.astype(bf16), preferred_element_type=f32)
    if reverse:
        cross = cross * jnp.exp(lg_b * (TB - idx))
        kz = (kr * jnp.exp(lg_b * idx)).astype(bf16)
        st_ref[hh] = st * jnp.exp(lg_b * TB) + lax.dot_general(kz, v, (((0,), (0,)), ((), ())),
                                                               preferred_element_type=f32)
        o = _headnorm(o1_ref[:, sl] + cross)
        o_ref[:, sl] = (o * _silu(gate_ref[:, sl].astype(f32))).astype(o_ref.dtype)
    else:
        @pl.when(s_idx == 0)
        def _():
            ri = lax.broadcasted_iota(jnp.int32, (TB, TB), 0)
            ci = lax.broadcasted_iota(jnp.int32, (TB, TB), 1)
            d = (ri - ci).astype(f32)
            dm_ref[hh] = jnp.where(d >= 0, jnp.exp(lg_f * jnp.maximum(d, 0.0)), jnp.exp(lg_b * jnp.maximum(-d, 0.0)))

        cross = cross * jnp.exp(lg_f * (idx + 1.0))
        kz = (kr * jnp.exp(lg_f * (TB - 1.0 - idx))).astype(bf16)
        st_ref[hh] = st * jnp.exp(lg_f * TB) + lax.dot_general(kz, v, (((0,), (0,)), ((), ())),
                                                               preferred_element_type=f32)
        sc = lax.dot_general(qr.astype(bf16), kr.astype(bf16), (((1,), (1,)), ((), ())), preferred_element_type=f32)
        sc = (sc * dm_ref[hh]).astype(bf16)
        o_ref[:, sl] = jnp.dot(sc, v, preferred_element_type=f32) + cross


def _retention(h, seq_first, seq_last, pos_blk, decay, t_max):
    n = h.shape[0]
    TB = SEQ_BLOCK
    nblk = n // TB
    H, dh = RET_HEADS, RET_DH
    cos, sin_lo, sin_hi = _rope_tables(jnp.arange(t_max), dh, RET_THETA, dh)
    sin = sin_lo + sin_hi
    dec = jnp.broadcast_to(decay.astype(jnp.float32).T[:, :, None, None], (H, 2, 8, LANE))

    HW = H * dh

    def specs(reverse):
        blk = (lambda s: nblk - 1 - s) if reverse else (lambda s: s)
        tok = lambda off: pl.BlockSpec((TB, HW), lambda s, f, p: (blk(s), off // HW))
        tab = pl.BlockSpec((TB, LANE), lambda s, f, p: (p[blk(s)], 0))
        sp = [tok(DQ_OFF), tok(DK_OFF), tok(DV_OFF), tab, tab,
              pl.BlockSpec((H, 2, 8, LANE), lambda s, f, p: (0, 0, 0, 0))]
        return sp, tok, pl.BlockSpec((TB, HW), lambda s, f, p: (blk(s), 0))

    sp, _, out = specs(False)
    o1 = pl.pallas_call(
        functools.partial(_ret_kernel, reverse=False, nblk=nblk), name="ret_fwd",
        grid_spec=pltpu.PrefetchScalarGridSpec(
            num_scalar_prefetch=2, grid=(nblk,), in_specs=sp, out_specs=out,
            scratch_shapes=[pltpu.VMEM((H, dh, dh), jnp.float32), pltpu.VMEM((H, TB, TB), jnp.float32)]),
        out_shape=jax.ShapeDtypeStruct((n, H * dh), jnp.float32),
        compiler_params=_cparams("arbitrary"),
    )(seq_first, pos_blk, h, h, h, cos, sin, dec)
    sp, tok, out = specs(True)
    sp = sp + [tok(DG_OFF), out]
    return pl.pallas_call(
        functools.partial(_ret_kernel, reverse=True, nblk=nblk), name="ret_bwd",
        grid_spec=pltpu.PrefetchScalarGridSpec(
            num_scalar_prefetch=2, grid=(nblk,), in_specs=sp, out_specs=out,
            scratch_shapes=[pltpu.VMEM((H, dh, dh), jnp.float32)]),
        out_shape=jax.ShapeDtypeStruct((n, H * dh), jnp.bfloat16),
        compiler_params=_cparams("arbitrary"),
    )(seq_last, pos_blk, h, h, h, cos, sin, dec, h, o1)


SWA_Q = 128
SWA_HALF = 64
assert all(w // (2 * d) == SWA_HALF for w, d in SWA_GROUPS)
NEG_BIG = -1e30


def _swa_kernel(hl_ref, hr_ref, q_ref, kl_ref, kc_ref, kr_ref, vl_ref, vc_ref, vr_ref,
                qcos_ref, qs1_ref, qs2_ref, kcos_ref, ks1_ref, ks2_ref, o_ref, lse_ref):
    f32, bf16 = jnp.float32, jnp.bfloat16
    j = pl.program_id(1)
    Q, HW = SWA_Q, SWA_HALF
    W = Q + 2 * HW
    ri = lax.broadcasted_iota(jnp.int32, (Q, W), 0)
    ci = lax.broadcasted_iota(jnp.int32, (Q, W), 1)
    rel = ci - HW - ri
    ok = (rel <= HW) & (rel >= -HW)
    ok = ok & ((ci >= HW) | (hl_ref[j] == 1)) & ((ci < HW + Q) | (hr_ref[j] == 1))
    lane = lax.broadcasted_iota(jnp.int32, (Q, LANE), 1)

    def rot(x, c_ref, s1_ref, s2_ref):
        return (x * c_ref[...] + pltpu.roll(x, LANE - ROPE_DIM // 2, axis=1) * s1_ref[...]
                + pltpu.roll(x, ROPE_DIM // 2, axis=1) * s2_ref[...])

    lse_tile = jnp.zeros((Q, LANE), f32)
    for hh in range(SWA_HEADS):
        sl = slice(hh * SWA_DH, (hh + 1) * SWA_DH)
        q = rot(q_ref[:, sl].astype(f32), qcos_ref, qs1_ref, qs2_ref) * (SWA_DH ** -0.5)
        k = jnp.concatenate([kl_ref[:, sl], kc_ref[:, sl], kr_ref[:, sl]], axis=0).astype(f32)
        k = rot(k, kcos_ref, ks1_ref, ks2_ref)
        v = jnp.concatenate([vl_ref[:, sl], vc_ref[:, sl], vr_ref[:, sl]], axis=0)
        s = lax.dot_general(q.astype(bf16), k.astype(bf16), (((1,), (1,)), ((), ())), preferred_element_type=f32)
        s = jnp.where(ok, s, NEG_BIG)
        m = jnp.max(s, -1, keepdims=True)
        p = jnp.exp(s - m)
        den = jnp.sum(p, -1, keepdims=True)
        o = jnp.dot(p.astype(bf16), v, preferred_element_type=f32) / den
        o_ref[:, sl] = o
        lse_tile = jnp.where(lane == hh, m + jnp.log(den), lse_tile)
    lse_ref[...] = lse_tile


def _swa_merge_kernel(*refs):
    ng = len(SWA_GROUPS)
    o_refs, l_refs, y_ref = refs[:ng], refs[ng:2 * ng], refs[2 * ng]
    scratch = refs[2 * ng + 1:]
    assert SWA_DH == LANE
    os_, ls_ = [], []
    si = 0
    for gi, (_, dil) in enumerate(SWA_GROUPS):
        if dil == 1:
            os_.append(lambda hh, r=o_refs[gi]: r[0, :, hh * LANE:(hh + 1) * LANE])
            ls_.append(l_refs[gi].at[0])
            continue
        o_s, l_s = scratch[si], scratch[si + 1]
        si += 2
        rows = l_s.shape[0] // dil
        for r in range(dil):
            for hh in range(SWA_HEADS):
                o_s[hh, pl.ds(r, rows, stride=dil), :] = o_refs[gi][r, :, hh * LANE:(hh + 1) * LANE]
            l_s[pl.ds(r, rows, stride=dil), :] = l_refs[gi][r]
        os_.append(lambda hh, s=o_s: s[hh])
        ls_.append(l_s)
    for hh in range(SWA_HEADS):
        ls = [r[:, hh:hh + 1] for r in ls_]
        m = functools.reduce(jnp.maximum, ls)
        ws = [jnp.exp(l - m) for l in ls]
        tot = functools.reduce(lambda a, b: a + b, ws)
        y = functools.reduce(lambda a, b: a + b, [w * o(hh) for w, o in zip(ws, os_)])
        y_ref[:, hh * LANE:(hh + 1) * LANE] = (y / tot).astype(y_ref.dtype)


def _proj_strided_kernel(a_ref, b_ref, o_ref, acc_ref, *, dil):
    acc = jnp.dot(a_ref[...], b_ref[...], preferred_element_type=jnp.float32)
    if dil == 1:
        o_ref[0] = acc.astype(o_ref.dtype)
        return
    rows = acc_ref.shape[1] // dil
    for s in range(acc_ref.shape[0]):
        acc_ref[s] = acc[:, s * LANE:(s + 1) * LANE]
        for r in range(dil):
            o_ref[r, :, s * LANE:(s + 1) * LANE] = acc_ref[s, pl.ds(r, rows, stride=dil), :].astype(o_ref.dtype)


def _proj_strided(a, b, dil, bm=1024, bn=512):
    n, K = a.shape
    N = b.shape[1]
    return pl.pallas_call(
        functools.partial(_proj_strided_kernel, dil=dil), name=f"swa_proj_dil{dil}", grid=(n // bm, N // bn),
        in_specs=[pl.BlockSpec((bm, K), lambda i, j: (i, 0)), pl.BlockSpec((K, bn), lambda i, j: (0, j))],
        out_specs=pl.BlockSpec((dil, bm // dil, bn), lambda i, j: (0, i, j)),
        out_shape=jax.ShapeDtypeStruct((dil, n // dil, N), jnp.bfloat16),
        scratch_shapes=[pltpu.VMEM((bn // LANE, bm, LANE), jnp.float32)],
        compiler_params=_cparams("parallel", "arbitrary"),
    )(a, b)


def _swa(xb, w_groups, seqs):
    n = xb.shape[0]
    Q, HW = SWA_Q, SWA_HALF
    GW = SWA_HEADS * SWA_DH
    outs, lses = [], []
    for gi, (window, dil) in enumerate(SWA_GROUPS):
        hs = _proj_strided(xb, w_groups[gi], dil)
        rows = n // dil
        nblk = rows // Q
        has_l, has_r = [], []
        for num, T in seqs:
            per = T // dil // Q
            assert per * Q * dil == T
            has_l += ([0] + [1] * (per - 1)) * num
            has_r += ([1] * (per - 1) + [0]) * num
        has_l = jnp.asarray(np.array(has_l, np.int32))
        has_r = jnp.asarray(np.array(has_r, np.int32))
        qpos = jnp.arange(Q) * dil
        kpos = (jnp.arange(Q + 2 * HW) - HW) * dil
        qt = _rope_tables(qpos, ROPE_DIM, ROPE_THETA, SWA_DH)
        kt = _rope_tables(kpos, ROPE_DIM, ROPE_THETA, SWA_DH)
        last64 = rows // HW - 1

        def center(c):
            return pl.BlockSpec((None, Q, GW), lambda r, j, hl, hr: (r, j, c))

        def left(c):
            return pl.BlockSpec((None, HW, GW), lambda r, j, hl, hr: (r, jnp.maximum(2 * j - 1, 0), c))

        def right(c):
            return pl.BlockSpec((None, HW, GW), lambda r, j, hl, hr: (r, jnp.minimum(2 * j + 2, last64), c))

        qtab = pl.BlockSpec((Q, SWA_DH), lambda r, j, hl, hr: (0, 0))
        ktab = pl.BlockSpec((Q + 2 * HW, SWA_DH), lambda r, j, hl, hr: (0, 0))
        o_g, lse_g = pl.pallas_call(
            _swa_kernel, name=f"swa_dil{dil}",
            grid_spec=pltpu.PrefetchScalarGridSpec(
                num_scalar_prefetch=2, grid=(dil, nblk),
                in_specs=[center(0), left(1), center(1), right(1), left(2), center(2), right(2),
                          qtab, qtab, qtab, ktab, ktab, ktab],
                out_specs=[pl.BlockSpec((None, Q, GW), lambda r, j, hl, hr: (r, j, 0)),
                           pl.BlockSpec((None, Q, LANE), lambda r, j, hl, hr: (r, j, 0))]),
            out_shape=[jax.ShapeDtypeStruct((dil, rows, GW), jnp.float32),
                       jax.ShapeDtypeStruct((dil, rows, LANE), jnp.float32)],
            compiler_params=_cparams("parallel", "parallel"),
        )(has_l, has_r, hs, hs, hs, hs, hs, hs, hs, *qt, *kt)
        outs.append(o_g)
        lses.append(lse_g)
    bm = 1024
    dils = [d for _, d in SWA_GROUPS]
    osp = [pl.BlockSpec((d, bm // d, GW), lambda i: (0, i, 0)) for d in dils]
    lsp = [pl.BlockSpec((d, bm // d, LANE), lambda i: (0, i, 0)) for d in dils]
    scratch = []
    for d in dils:
        if d > 1:
            scratch += [pltpu.VMEM((SWA_HEADS, bm, LANE), jnp.float32), pltpu.VMEM((bm, LANE), jnp.float32)]
    return pl.pallas_call(
        _swa_merge_kernel, name="swa_merge", grid=(n // bm,), in_specs=osp + lsp,
        out_specs=pl.BlockSpec((bm, GW), lambda i: (i, 0)),
        out_shape=jax.ShapeDtypeStruct((n, GW), jnp.bfloat16), scratch_shapes=scratch,
        compiler_params=_cparams("parallel"),
    )(*outs, *lses)


FFT_T2 = 128
FW = FNET_GROUPS * FNET_DG


def _dft_cos_sin(n_out, n_in, period):
    ang = (2.0 * np.pi / period) * np.mod(np.outer(np.arange(n_out), np.arange(n_in)), period)
    return np.cos(ang), np.sin(ang)


def _fnet_stage1_kernel(z_ref, m_ref, a_ref, *, cols):
    mat = m_ref[...]
    for c in range(cols):
        zr = z_ref[:, c * 2 * FW:c * 2 * FW + FW]
        zi = z_ref[:, c * 2 * FW + FW:(c + 1) * 2 * FW]
        a = jnp.dot(mat, jnp.concatenate([zr, zi], axis=0), preferred_element_type=jnp.float32)
        t1 = zr.shape[0]
        a_ref[:, c * 2 * FW:c * 2 * FW + FW] = a[:t1].astype(a_ref.dtype)
        a_ref[:, c * 2 * FW + FW:(c + 1) * 2 * FW] = a[t1:].astype(a_ref.dtype)


FNET_KB = 8


def _fnet_stage2_kernel(a_ref, g_ref, y_ref, *, scale):
    kw = g_ref.shape[1] // FNET_KB
    for j in range(FNET_KB):
        y = jnp.dot(g_ref[:, j * kw:(j + 1) * kw], a_ref[j], preferred_element_type=jnp.float32)
        y_ref[:, j * FW:(j + 1) * FW] = (y * scale).astype(y_ref.dtype)


def _fnet(h, seqs):
    n = h.shape[0]
    bf16 = jnp.bfloat16
    T2 = FFT_T2
    cc, ss = _dft_cos_sin(FNET_DG, FNET_DG, FNET_DG)
    eye = np.eye(FNET_GROUPS)
    wc = jnp.asarray(np.concatenate([np.kron(eye, cc), -np.kron(eye, ss)], axis=1), bf16)
    z = _matmul(h, wc, bf16, bn=2 * FW, a_col_blk=C_OFF // FW, name="fnet_channel_dft")
    zz = z.reshape(n // T2, T2 * 2 * FW)
    outs = []
    row0 = 0
    for num, T in seqs:
        T1 = T // T2
        assert T1 * T2 == T and T1 % 16 == 0
        c1, s1 = _dft_cos_sin(T1, T1, T1)
        mat = jnp.asarray(np.block([[c1, s1], [-s1, c1]]), bf16)
        cols = min(T2, max(1, 1024 // T1))
        rb0 = row0 // T1
        a = pl.pallas_call(
            functools.partial(_fnet_stage1_kernel, cols=cols), name=f"fnet_dft_t1_{T1}", grid=(num, T2 // cols),
            in_specs=[pl.BlockSpec((T1, cols * 2 * FW), lambda b, j: (rb0 + b, j)),
                      pl.BlockSpec((2 * T1, 2 * T1), lambda b, j: (0, 0))],
            out_specs=pl.BlockSpec((T1, cols * 2 * FW), lambda b, j: (b, j)),
            out_shape=jax.ShapeDtypeStruct((num * T1, T2 * 2 * FW), bf16),
            compiler_params=_cparams("parallel", "parallel"),
        )(zz, mat)
        ec, es = _dft_cos_sin(T, T2, T)
        g = jnp.asarray(np.stack([ec, es], axis=-1).reshape(T2, T1 * 2 * T2), bf16)
        y = pl.pallas_call(
            functools.partial(_fnet_stage2_kernel, scale=float(1.0 / np.sqrt(T * FNET_DG))), name=f"fnet_dft_t2_{T1}",
            grid=(num, T1 // FNET_KB),
            in_specs=[pl.BlockSpec((FNET_KB, 2 * T2, FW), lambda b, k1: (b * (T1 // FNET_KB) + k1, 0, 0)),
                      pl.BlockSpec((T2, FNET_KB * 2 * T2), lambda b, k1: (0, k1))],
            out_specs=pl.BlockSpec((T2, FNET_KB * FW), lambda b, k1: (b, k1)),
            out_shape=jax.ShapeDtypeStruct((num * T2, T1 * FW), bf16),
            compiler_params=_cparams("parallel", "parallel"),
        )(a.reshape(num * T1, 2 * T2, FW), g)
        outs.append(y.reshape(num * T, FW))
        row0 += num * T1
    return jnp.concatenate(outs, axis=0)


def _ffn_up_kernel(x_ref, w1_ref, w3_ref, h_ref, w1b_ref, w3b_ref):
    @pl.when(pl.program_id(2) == 0)
    def _():
        w1b_ref[...] = w1_ref[...].astype(w1b_ref.dtype)
        w3b_ref[...] = w3_ref[...].astype(w3b_ref.dtype)

    x = x_ref[...]
    h1 = jnp.dot(x, w1b_ref[...], preferred_element_type=jnp.float32)
    h3 = jnp.dot(x, w3b_ref[...], preferred_element_type=jnp.float32)
    h_ref[...] = (_silu(h1) * h3).astype(h_ref.dtype)


def _ffn_down_kernel(h_ref, w2_ref, s_ref, o_ref, w2b_ref):
    e = pl.program_id(0)

    @pl.when(pl.program_id(1) == 0)
    def _():
        w2b_ref[...] = w2_ref[...].astype(w2b_ref.dtype)

    D = w2_ref.shape[1]
    side = s_ref[...]
    lane = lax.broadcasted_iota(jnp.int32, side.shape, 1)
    mine = (lane == e) | (lane == e + SIDE_MID) | (lane == e + SIDE_LOW)
    gate = jnp.sum(jnp.where(mine, side.astype(jnp.float32), 0.0), axis=1, keepdims=True)
    y = jnp.dot(h_ref[...], w2b_ref[...], preferred_element_type=jnp.float32)
    o_ref[:, :D] = (y * gate).astype(o_ref.dtype)
    o_ref[:, D:] = side


def _expert_ffn(xe, w1, w3, w2, layer, bm=1024, bf=512):
    E, R, DY = xe.shape
    D = DY - LANE
    F = w1.shape[-1]
    wspec = pl.BlockSpec((None, None, D, bf), lambda e, f, i: (layer, e, 0, f))
    hmid = pl.pallas_call(
        _ffn_up_kernel, name="expert_ffn_up", grid=(E, F // bf, R // bm),
        in_specs=[pl.BlockSpec((None, bm, D), lambda e, f, i: (e, i, 0)), wspec, wspec],
        out_specs=pl.BlockSpec((None, bm, bf), lambda e, f, i: (e, i, f)),
        out_shape=jax.ShapeDtypeStruct((E, R, F), jnp.bfloat16),
        scratch_shapes=[pltpu.VMEM((D, bf), jnp.bfloat16), pltpu.VMEM((D, bf), jnp.bfloat16)],
        compiler_params=_cparams("arbitrary", "arbitrary", "arbitrary"),
    )(xe, w1, w3)
    return pl.pallas_call(
        _ffn_down_kernel, name="expert_ffn_down", grid=(E, R // bm),
        in_specs=[pl.BlockSpec((None, bm, F), lambda e, i: (e, i, 0)),
                  pl.BlockSpec((None, None, F, D), lambda e, i: (layer, e, 0, 0), pipeline_mode=pl.Buffered(1)),
                  pl.BlockSpec((None, bm, LANE), lambda e, i: (e, i, D // LANE))],
        out_specs=pl.BlockSpec((None, bm, D + LANE), lambda e, i: (e, i, 0)),
        out_shape=jax.ShapeDtypeStruct((E, R, D + LANE), jnp.bfloat16),
        scratch_shapes=[pltpu.VMEM((F, D), jnp.bfloat16)],
        compiler_params=_cparams("arbitrary", "arbitrary"),
    )(hmid, w2, xe)


COMB_BM = 128
COMB_G = 16
COMB_K = 256


def _combine_kernel(lo_ref, x_ref, g_ref, b_ref, ye_hbm, *rest, nblk_all, blk_off, want_bf16):
    if want_bf16:
        xf_ref, xb_ref, buf_ref, acc_ref, sem = rest
    else:
        xf_ref, buf_ref, acc_ref, sem = rest
    f32, bf16 = jnp.float32, jnp.bfloat16
    E = ye_hbm.shape[0]
    D = x_ref.shape[1]
    BM, G, KC = COMB_BM, COMB_G, COMB_K
    i = pl.program_id(0)
    nsteps = pl.num_programs(0)
    blk = i + blk_off
    slot = i % 2

    @pl.when(i == 0)
    def _():
        buf_ref[...] = jnp.zeros_like(buf_ref)

    def granule_copy(sl, e, src_row, dst_row):
        return pltpu.make_async_copy(ye_hbm.at[e, pl.ds(src_row, G)], buf_ref.at[sl, pl.ds(dst_row, G)], sem.at[sl])

    def gather_block(b, sl, start):
        off = jnp.int32(0)
        for e in range(E):
            lo = lo_ref[e * (nblk_all + 1) + b]
            hi = lo_ref[e * (nblk_all + 1) + b + 1]
            lo_al = (lo // G) * G
            ng = jnp.where(hi > lo, (hi - lo_al + G - 1) // G, 0)
            if start:
                def issue(g, carry, e=e, lo_al=lo_al, off=off):
                    granule_copy(sl, e, pl.multiple_of(lo_al + g * G, G), pl.multiple_of(off + g * G, G)).start()
                    return carry

                lax.fori_loop(0, ng, issue, 0)
            off = off + ng * G
        return off

    @pl.when(i == 0)
    def _():
        gather_block(blk, slot, True)

    @pl.when(i + 1 < nsteps)
    def _():
        gather_block(blk + 1, 1 - slot, True)

    off = gather_block(blk, slot, False)

    def wait_one(g, carry):
        granule_copy(slot, 0, 0, 0).wait()
        return carry

    lax.fori_loop(0, off // G, wait_one, 0)

    acc_ref[...] = jnp.zeros_like(acc_ref)
    t0 = (blk * BM).astype(f32)
    lane_tok = lax.broadcasted_iota(jnp.int32, (KC, BM), 1).astype(f32)
    row = lax.broadcasted_iota(jnp.int32, (KC, 1), 0)

    def chunk(c, carry):
        win = buf_ref[slot, pl.ds(pl.multiple_of(c * KC, KC), KC), :]
        tok = (win[:, D + SIDE_TOK:D + SIDE_TOK + 1].astype(f32) * TOK_RADIX
               + win[:, D + SIDE_TOK + 1:D + SIDE_TOK + 2].astype(f32))
        tok = jnp.where(row + c * KC < off, tok, -1.0)
        onehot_t = (tok - t0 == lane_tok).astype(bf16)
        acc_ref[...] += lax.dot_general(onehot_t, win[:, :D], (((0,), (0,)), ((), ())), preferred_element_type=f32)
        return carry

    lax.fori_loop(0, (off + KC - 1) // KC, chunk, 0)
    y = _ln_rows(DN_ALPHA * x_ref[...] + acc_ref[...], g_ref[...], b_ref[...])
    xf_ref[...] = y
    if want_bf16:
        xb_ref[...] = y.astype(bf16)


def _combine_ln(x, ye, lo, g, b, row_off, rows, want_bf16):
    n, D = x.shape
    E, R, DY = ye.shape
    BM = COMB_BM
    nblk_all = n // BM
    ob = row_off // BM
    kmax = -(-E * (BM + COMB_G) // COMB_K) * COMB_K
    irow = pl.BlockSpec((BM, D), lambda i, lo: (i + ob, 0))
    orow = pl.BlockSpec((BM, D), lambda i, lo: (i, 0))
    vec = pl.BlockSpec((1, D), lambda i, lo: (0, 0))
    out_shape = [jax.ShapeDtypeStruct((rows, D), jnp.float32)]
    out_specs = [orow]
    if want_bf16:
        out_shape.append(jax.ShapeDtypeStruct((rows, D), jnp.bfloat16))
        out_specs.append(orow)
    return pl.pallas_call(
        functools.partial(_combine_kernel, nblk_all=nblk_all, blk_off=ob, want_bf16=want_bf16), name="moe_combine_ln",
        grid_spec=pltpu.PrefetchScalarGridSpec(
            num_scalar_prefetch=1, grid=(rows // BM,),
            in_specs=[irow, vec, vec, pl.BlockSpec(memory_space=pl.ANY)], out_specs=out_specs,
            scratch_shapes=[pltpu.VMEM((2, kmax, DY), jnp.bfloat16), pltpu.VMEM((BM, D), jnp.float32),
                            pltpu.SemaphoreType.DMA((2,))]),
        out_shape=out_shape,
        compiler_params=_cparams("arbitrary"),
    )(lo, x, g.reshape(1, D), b.reshape(1, D), ye)


def _route_kernel(all_ref, a_ref, idx_ref, cprev_ref, thr_ref, need_ref, *, cap, tok_off):
    f32, bf16, i32 = jnp.float32, jnp.bfloat16, jnp.int32
    e = pl.program_id(0)

    def total(x):
        return jnp.sum(jnp.sum(x, axis=2, keepdims=True), axis=1, keepdims=True)

    @pl.when(e == 0)
    def _():
        bits_all = pltpu.bitcast(all_ref[...], i32)

        def refine(i, prefix):
            cand = prefix | jnp.left_shift(jnp.int32(1), 30 - i)
            return jnp.where(total((bits_all >= cand).astype(i32)) >= cap, cand, prefix)

        thr_all = lax.fori_loop(0, 31, refine, jnp.zeros((all_ref.shape[0], 1, 1), i32))
        thr_ref[...] = jnp.broadcast_to(thr_all, thr_ref.shape)
        need_ref[...] = jnp.broadcast_to(cap - total((bits_all > thr_all).astype(i32)), need_ref.shape)

    a = a_ref[...]
    R = a.shape[0]
    bits = pltpu.bitcast(a, i32)
    thr = thr_ref[e][0:1, 0:1]
    need = need_ref[e][0:1, 0:1]
    gt, eq = bits > thr, bits == thr

    li = lax.broadcasted_iota(i32, (LANE, LANE), 0)
    lj = lax.broadcasted_iota(i32, (LANE, LANE), 1)
    upto = (li <= lj).astype(bf16)
    ri = lax.broadcasted_iota(i32, (R, R), 0)
    rj = lax.broadcasted_iota(i32, (R, R), 1)
    before = (rj < ri).astype(bf16)

    def counts(mask):
        inc = jnp.dot(mask.astype(bf16), upto, preferred_element_type=f32)
        tot = jnp.broadcast_to(inc[:, LANE - 1:LANE], (R, LANE)).astype(bf16)
        return inc, jnp.dot(before, tot, preferred_element_type=f32)

    eq_inc, eq_prev = counts(eq)
    sel = gt | (eq & (eq_prev + eq_inc - 1.0 < need.astype(f32)))
    cl, cprev = counts(sel)
    cprev_ref[...] = cprev[:, :1].astype(i32)

    selb = sel.astype(bf16)
    c_upto = jnp.sum(lax.dot_general(selb, (ri <= rj).astype(bf16), (((0,), (0,)), ((), ())),
                                     preferred_element_type=f32), axis=0, keepdims=True)
    c_before = jnp.sum(lax.dot_general(selb, (ri < rj).astype(bf16), (((0,), (0,)), ((), ())),
                                       preferred_element_type=f32), axis=0, keepdims=True)
    slot = lax.broadcasted_iota(i32, (cap, 1), 0).astype(f32)
    onehot = ((c_before <= slot) & (slot < c_upto)).astype(bf16)
    lane = lax.broadcasted_iota(i32, (R, LANE), 1)
    cp = cprev.astype(i32)
    aux = jnp.where(lane == 0, cp // TOK_RADIX, jnp.where(lane == 1, cp % TOK_RADIX,
                    jnp.where(lane == 2, lax.broadcasted_iota(i32, (R, LANE), 0), 0)))
    rhs = jnp.concatenate([cl.astype(bf16), aux.astype(f32).astype(bf16)], axis=1)
    m = jnp.dot(onehot, rhs, preferred_element_type=f32)
    base = m[:, LANE:LANE + 1] * TOK_RADIX + m[:, LANE + 1:LANE + 2]
    chunk = m[:, LANE + 2:LANE + 3]
    pos = jnp.sum((m[:, :LANE] <= slot - base).astype(f32), axis=1, keepdims=True)
    idx_ref[...] = (chunk * LANE + pos).astype(i32) + tok_off


def _route(aff_t, cap, tok_off):
    E, R, _ = aff_t.shape
    return pl.pallas_call(
        functools.partial(_route_kernel, cap=cap, tok_off=tok_off), name="moe_route", grid=(E,),
        in_specs=[pl.BlockSpec((E, R, LANE), lambda e: (0, 0, 0)), pl.BlockSpec((None, R, LANE), lambda e: (e, 0, 0))],
        out_specs=[pl.BlockSpec((None, cap, 1), lambda e: (e, 0, 0)), pl.BlockSpec((None, R, 1), lambda e: (e, 0, 0))],
        out_shape=[jax.ShapeDtypeStruct((E, cap, 1), jnp.int32), jax.ShapeDtypeStruct((E, R, 1), jnp.int32)],
        scratch_shapes=[pltpu.VMEM((E, 8, LANE), jnp.int32), pltpu.VMEM((E, 8, LANE), jnp.int32)],
        compiler_params=_cparams("arbitrary"),
    )(aff_t, aff_t)


def _expert_choice(xs, aff, seqs, w1, w3, w2, layer):
    assert COMB_BM == LANE
    aff_t = _aff_chunks(aff)
    idxs, los = [], []
    start = slot0 = 0
    for num, T in seqs:
        m = num * T
        cap = max(1, CAPACITY_FACTOR * m // N_EXPERTS)
        idx, cprev = _route(aff_t[:, start // LANE:(start + m) // LANE], cap, start)
        idxs.append(idx[..., 0])
        los.append(cprev[..., 0] + slot0)
        start += m
        slot0 += cap
    lo = jnp.concatenate(los + [jnp.full((N_EXPERTS, 1), slot0, jnp.int32)], axis=1).reshape(-1)
    ye = _expert_ffn(xs[jnp.concatenate(idxs, axis=1)], w1, w3, w2, layer)
    return ye, lo


def _pack_w_in(w_in):
    sizes = (GLA_HEADS * GLA_DK, GLA_HEADS * GLA_DK, GLA_HEADS * GLA_DV, GLA_HEADS * GLA_DV, 2 * GLA_RANK,
             SWA_NH * SWA_DH, SWA_NH * SWA_DH, SWA_NH * SWA_DH, FNET_GROUPS * FNET_DG,
             RET_HEADS * RET_DH, RET_HEADS * RET_DH, RET_HEADS * RET_DH, RET_HEADS * RET_DH)
    offs = np.concatenate([[0], np.cumsum(sizes)])
    part = lambda i: w_in[..., offs[i]:offs[i + 1]]
    a_q, a_k = part(0), part(1)
    cols = []
    for hh in range(GLA_HEADS):
        cols += [a_q[..., hh * GLA_DK:(hh + 1) * GLA_DK], a_k[..., hh * GLA_DK:(hh + 1) * GLA_DK]]
    cols += [part(2), part(3), part(4),
             jnp.zeros(w_in.shape[:2] + (C_OFF - LR_OFF - 2 * GLA_RANK,), w_in.dtype)]
    cols += [part(i) for i in range(8, 13)]
    out = jnp.concatenate(cols, axis=-1)
    assert out.shape[-1] == IN_COLS_P
    GW = SWA_HEADS * SWA_DH
    swa = [jnp.concatenate([part(i)[..., g * GW:(g + 1) * GW] for i in (5, 6, 7)], axis=-1)
           for g in range(len(SWA_GROUPS))]
    return out, swa


def _block_tables(seqs, blk):
    first, last, pos = [], [], []
    for num, T in seqs:
        per = T // blk
        assert per * blk == T
        for _ in range(num):
            first += [1] + [0] * (per - 1)
            last += [0] * (per - 1) + [1]
            pos += list(range(per))
    mk = lambda a: jnp.asarray(np.array(a, np.int32))
    return mk(first), mk(last), mk(pos)


def kernel(x_prompt, x_sample, ln_in_g, ln_in_b, w_in, gla_w_up, gla_b_up, ret_decay, w_gate, b_gate, w_branch,
           w_out, ln1_g, ln1_b, w_router, w_e1, w_e3, w_e2, ln2_g, ln2_b):
    bf16 = jnp.bfloat16
    D = D_MODEL
    seqs = (x_prompt.shape[:2], x_sample.shape[:2])
    n_p, n_s = x_prompt.shape[0] * x_prompt.shape[1], x_sample.shape[0] * x_sample.shape[1]
    n = n_p + n_s
    t_max = max(T for _, T in seqs)
    seq_first, seq_last, pos_blk = _block_tables(seqs, SEQ_BLOCK)

    w_in_p, w_swa = _pack_w_in(w_in.astype(bf16))
    wo = w_out.astype(bf16)
    wr = jnp.pad(w_router, ((0, 0), (0, 0), (0, LANE - N_EXPERTS))).astype(bf16)
    wg, wb, we1, we3, we2 = w_gate, w_branch, w_e1, w_e3, w_e2

    xf, xb = _ln_in(x_prompt.reshape(n_p, D), x_sample.reshape(n_s, D), ln_in_g, ln_in_b)
    for l in range(DEPTH):
        h = _matmul(xb, w_in_p[l], bf16, bm=1024, bn=IN_BN, name="in_proj")
        wup, bup = _pack_gla_up(gla_w_up[l], gla_b_up[l])
        ys = (_gla(h, seq_first, seq_last, wup, bup),
              _swa(xb, [w[l] for w in w_swa], seqs),
              _fnet(h, seqs),
              _retention(h, seq_first, seq_last, pos_blk, ret_decay[l], t_max))
        merged = _merge(xb, wg, b_gate[l], ys, wb, l)
        x1f, x1s, aff = _outproj(merged, wo[l], xf, ln1_g[l], ln1_b[l], wr[l])
        ye, lo = _expert_choice(x1s, aff, seqs, we1, we3, we2, l)
        if l + 1 < DEPTH:
            xf, xb = _combine_ln(x1f, ye, lo, ln2_g[l], ln2_b[l], 0, n, True)
        else:
            y_p, = _combine_ln(x1f, ye, lo, ln2_g[l], ln2_b[l], 0, n_p, False)
            y_s, = _combine_ln(x1f, ye, lo, ln2_g[l], ln2_b[l], n_p, n_s, False)
    return (y_p.reshape(x_prompt.shape), y_s.reshape(x_sample.shape))
```

```python
import functools

import jax
import jax.numpy as jnp
import numpy as np
from jax import lax
from jax.experimental import pallas as pl
from jax.experimental.pallas import tpu as pltpu

D_MODEL = 2048
DEPTH = 2
GLA_HEADS, GLA_DK, GLA_DV, GLA_RANK, GLA_TAU, GLA_CHUNK = 4, 64, 128, 16, 16.0, 64
SWA_GROUPS = ((128, 1), (512, 4), (2048, 16))
SWA_HEADS, SWA_DH = 4, 128
SWA_NH = SWA_HEADS * len(SWA_GROUPS)
SWA_BLOCK = 64
ROPE_THETA, ROPE_DIM = 500000.0, SWA_DH // 4
FNET_GROUPS, FNET_DG = 4, 128
RET_HEADS, RET_DH, RET_THETA = 4, 128, 10000.0
N_BRANCH, BRANCH_W = 4, 512
N_EXPERTS, EXPERT_FF, CAPACITY_FACTOR = 16, 2048, 2
DN_ALPHA = (2 * DEPTH) ** 0.25
LN_EPS = 1e-5

VMEM_LIMIT_BYTES = 48 * 1024 * 1024
LANE = 128

QK_OFF = 0
AV_OFF = QK_OFF + GLA_HEADS * LANE
AG_OFF = AV_OFF + GLA_HEADS * GLA_DV
LR_OFF = AG_OFF + GLA_HEADS * GLA_DV
C_OFF = LR_OFF + 4 * LANE
DQ_OFF = C_OFF + FNET_GROUPS * FNET_DG
DK_OFF = DQ_OFF + RET_HEADS * RET_DH
DV_OFF = DK_OFF + RET_HEADS * RET_DH
DG_OFF = DV_OFF + RET_HEADS * RET_DH
IN_USED = DG_OFF + RET_HEADS * RET_DH
IN_BN = 1536
IN_COLS_P = -(-IN_USED // IN_BN) * IN_BN
assert IN_COLS_P == IN_USED

SEQ_BLOCK = 512


def _cparams(*sem):
    return pltpu.CompilerParams(dimension_semantics=sem, vmem_limit_bytes=VMEM_LIMIT_BYTES)


def _mm_kernel(a_ref, b_ref, o_ref):
    o_ref[...] = jnp.dot(a_ref[...], b_ref[...], preferred_element_type=jnp.float32).astype(o_ref.dtype)


def _matmul(a, b, out_dtype, bm=1024, bn=512, a_col_blk=0, name="matmul"):
    batched = a.ndim == 3
    M = a.shape[-2]
    K, N = b.shape[-2:]
    bm = min(bm, M)
    bn = min(bn, N)
    assert M % bm == 0 and N % bn == 0, (M, N, bm, bn)
    if batched:
        E = a.shape[0]
        grid = (E, M // bm, N // bn)
        in_specs = [pl.BlockSpec((None, bm, K), lambda e, i, j: (e, i, 0)),
                    pl.BlockSpec((None, K, bn), lambda e, i, j: (e, 0, j))]
        out_specs = pl.BlockSpec((None, bm, bn), lambda e, i, j: (e, i, j))
        out_shape = jax.ShapeDtypeStruct((E, M, N), out_dtype)
        sem = ("parallel", "parallel", "arbitrary")
    else:
        grid = (M // bm, N // bn)
        in_specs = [pl.BlockSpec((bm, K), lambda i, j: (i, a_col_blk)),
                    pl.BlockSpec((K, bn), lambda i, j: (0, j))]
        out_specs = pl.BlockSpec((bm, bn), lambda i, j: (i, j))
        out_shape = jax.ShapeDtypeStruct((M, N), out_dtype)
        sem = ("parallel", "arbitrary")
    return pl.pallas_call(
        _mm_kernel, name=name, grid=grid, in_specs=in_specs, out_specs=out_specs, out_shape=out_shape,
        compiler_params=_cparams(*sem),
    )(a, b)


def _mm_w32_kernel(a_ref, w_ref, o_ref, wb_ref):
    @pl.when(pl.program_id(1) == 0)
    def _():
        wb_ref[...] = w_ref[...].astype(wb_ref.dtype)

    o_ref[...] = jnp.dot(a_ref[...], wb_ref[...], preferred_element_type=jnp.float32).astype(o_ref.dtype)


def _matmul_w32(a, w, out_dtype, bm=1024, bn=768, name="matmul_w32"):
    M, K = a.shape
    N = w.shape[1]
    assert M % bm == 0 and N % bn == 0, (M, N, bm, bn)
    return pl.pallas_call(
        _mm_w32_kernel, name=name, grid=(N // bn, M // bm),
        in_specs=[pl.BlockSpec((bm, K), lambda j, i: (i, 0)), pl.BlockSpec((K, bn), lambda j, i: (0, j))],
        out_specs=pl.BlockSpec((bm, bn), lambda j, i: (i, j)),
        out_shape=jax.ShapeDtypeStruct((M, N), out_dtype),
        scratch_shapes=[pltpu.VMEM((K, bn), jnp.bfloat16)],
        compiler_params=_cparams("arbitrary", "arbitrary"),
    )(a, w)


def _ln_rows(x, g, b):
    mu = jnp.mean(x, -1, keepdims=True)
    xc = x - mu
    var = jnp.mean(xc * xc, -1, keepdims=True)
    return xc * lax.rsqrt(var + LN_EPS) * g + b


def _ln_in_kernel(xa_ref, xc_ref, g_ref, b_ref, xf_ref, xb_ref, *, na_blocks):
    i = pl.program_id(0)

    def emit(x_ref):
        y = _ln_rows(x_ref[...], g_ref[...], b_ref[...])
        xf_ref[...] = y
        xb_ref[...] = y.astype(jnp.bfloat16)

    pl.when(i < na_blocks)(lambda: emit(xa_ref))
    pl.when(i >= na_blocks)(lambda: emit(xc_ref))


def _ln_in(xa, xc, g, b, bm=512):
    (na, D), nc = xa.shape, xc.shape[0]
    nab, ncb = na // bm, nc // bm
    row = pl.BlockSpec((bm, D), lambda i: (i, 0))
    vec = pl.BlockSpec((1, D), lambda i: (0, 0))
    return pl.pallas_call(
        functools.partial(_ln_in_kernel, na_blocks=nab), name="ln_in", grid=(nab + ncb,),
        in_specs=[pl.BlockSpec((bm, D), lambda i: (jnp.minimum(i, nab - 1), 0)),
                  pl.BlockSpec((bm, D), lambda i: (jnp.maximum(i - nab, 0), 0)), vec, vec],
        out_specs=[row, row],
        out_shape=[jax.ShapeDtypeStruct((na + nc, D), jnp.float32), jax.ShapeDtypeStruct((na + nc, D), jnp.bfloat16)],
        compiler_params=_cparams("arbitrary"))(xa, xc, g.reshape(1, D), b.reshape(1, D))


def _merge_kernel(xb_ref, wg_ref, bg_ref, ya_ref, yb_ref, yc_ref, yd_ref, wb_ref, o_ref, wgb_ref, wbb_ref):
    @pl.when(pl.program_id(1) == 0)
    def _():
        wgb_ref[...] = wg_ref[...].astype(wgb_ref.dtype)
        wbb_ref[...] = wb_ref[...].astype(wbb_ref.dtype)

    xb = xb_ref[...]
    acc = None
    for i, y_ref in enumerate((ya_ref, yb_ref, yc_ref, yd_ref)):
        gate = jax.nn.sigmoid(jnp.dot(xb, wgb_ref[i], preferred_element_type=jnp.float32) + bg_ref[i])
        term = gate * jnp.dot(y_ref[...], wbb_ref[i], preferred_element_type=jnp.float32)
        acc = term if acc is None else acc + term
    o_ref[...] = acc.astype(o_ref.dtype)


def _merge(xb, wg, bg, ys, wb, layer, bm=1024, bn=256):
    n, D = xb.shape
    W = ys[0].shape[1]
    y_spec = pl.BlockSpec((bm, W), lambda j, i: (i, 0))
    return pl.pallas_call(
        _merge_kernel, name="branch_merge", grid=(D // bn, n // bm),
        in_specs=[pl.BlockSpec((bm, D), lambda j, i: (i, 0)),
                  pl.BlockSpec((None, N_BRANCH, D, bn), lambda j, i: (layer, 0, 0, j)),
                  pl.BlockSpec((N_BRANCH, 1, bn), lambda j, i: (0, 0, j)),
                  y_spec, y_spec, y_spec, y_spec,
                  pl.BlockSpec((None, N_BRANCH, W, bn), lambda j, i: (layer, 0, 0, j))],
        out_specs=pl.BlockSpec((bm, bn), lambda j, i: (i, j)),
        out_shape=jax.ShapeDtypeStruct((n, D), jnp.bfloat16),
        scratch_shapes=[pltpu.VMEM((N_BRANCH, D, bn), jnp.bfloat16), pltpu.VMEM((N_BRANCH, W, bn), jnp.bfloat16)],
        compiler_params=_cparams("arbitrary", "arbitrary"),
    )(xb, wg, bg.reshape(N_BRANCH, 1, D), *ys, wb)


OUTPROJ_SUB = 256


TOK_RADIX = 256
SIDE_MID, SIDE_LOW, SIDE_TOK = N_EXPERTS, 2 * N_EXPERTS, 3 * N_EXPERTS


def _outproj_kernel(m_ref, w_ref, x_ref, g_ref, b_ref, wr_ref, xf_ref, xb_ref, aff_ref):
    f32, bf16 = jnp.float32, jnp.bfloat16
    D = x_ref.shape[1]
    for s in range(m_ref.shape[0] // OUTPROJ_SUB):
        rows = pl.ds(s * OUTPROJ_SUB, OUTPROJ_SUB)
        r = jnp.dot(m_ref[rows, :], w_ref[...], preferred_element_type=f32)
        y = _ln_rows(DN_ALPHA * x_ref[rows, :] + r, g_ref[...], b_ref[...])
        yb = y.astype(bf16)
        xf_ref[rows, :] = y
        xb_ref[rows, :D] = yb
        logits = jnp.dot(yb, wr_ref[...], preferred_element_type=f32)
        lane = lax.broadcasted_iota(jnp.int32, logits.shape, 1)
        logits = jnp.where(lane < N_EXPERTS, logits, -jnp.inf)
        e = jnp.exp(logits - jnp.max(logits, -1, keepdims=True))
        aff = e / jnp.sum(e, -1, keepdims=True)
        aff_ref[rows, :] = aff
        hi = aff.astype(bf16).astype(f32)
        mid = (aff - hi).astype(bf16).astype(f32)
        low = (aff - hi - mid).astype(bf16).astype(f32)
        tok = (pl.program_id(0) * m_ref.shape[0] + s * OUTPROJ_SUB
               + lax.broadcasted_iota(jnp.int32, logits.shape, 0))
        side = hi + pltpu.roll(mid, SIDE_MID, axis=1) + pltpu.roll(low, SIDE_LOW, axis=1)
        side = jnp.where(lane == SIDE_TOK, (tok // TOK_RADIX).astype(f32),
                         jnp.where(lane == SIDE_TOK + 1, (tok % TOK_RADIX).astype(f32), side))
        xb_ref[rows, D:] = side.astype(bf16)


def _outproj(merged, w_out, x, g, b, w_router_p, bm=512):
    n, D = x.shape
    row = pl.BlockSpec((bm, D), lambda i: (i, 0))
    vec = pl.BlockSpec((1, D), lambda i: (0, 0))
    return pl.pallas_call(
        _outproj_kernel, name="outproj_ln_router", grid=(n // bm,),
        in_specs=[row, pl.BlockSpec((D, D), lambda i: (0, 0)), row, vec, vec,
                  pl.BlockSpec((D, LANE), lambda i: (0, 0))],
        out_specs=[row, pl.BlockSpec((bm, D + LANE), lambda i: (i, 0)), pl.BlockSpec((bm, LANE), lambda i: (i, 0))],
        out_shape=[jax.ShapeDtypeStruct((n, D), jnp.float32), jax.ShapeDtypeStruct((n, D + LANE), jnp.bfloat16),
                   jax.ShapeDtypeStruct((n, LANE), jnp.float32)],
        compiler_params=_cparams("parallel"),
    )(merged, w_out, x, g.reshape(1, D), b.reshape(1, D), w_router_p)


def _aff_t_kernel(a_ref, o_ref):
    for s in range(a_ref.shape[0] // LANE):
        o_ref[:, s, :] = a_ref[s * LANE:(s + 1) * LANE, :].T[:N_EXPERTS, :]


def _aff_chunks(aff, bm=1024):
    n = aff.shape[0]
    return pl.pallas_call(
        _aff_t_kernel, name="aff_chunks", grid=(n // bm,),
        in_specs=[pl.BlockSpec((bm, LANE), lambda i: (i, 0))],
        out_specs=pl.BlockSpec((N_EXPERTS, bm // LANE, LANE), lambda i: (0, i, 0)),
        out_shape=jax.ShapeDtypeStruct((N_EXPERTS, n // LANE, LANE), jnp.float32),
        compiler_params=_cparams("parallel"),
    )(aff)


def _log_sigmoid(z):
    return jnp.minimum(z, 0.0) - jnp.log(1.0 + jnp.exp(-jnp.abs(z)))


def _headnorm(o):
    mu = jnp.mean(o, -1, keepdims=True)
    oc = o - mu
    var = jnp.mean(oc * oc, -1, keepdims=True)
    return oc * lax.rsqrt(var + LN_EPS)


def _silu(x):
    return x * jax.nn.sigmoid(x)


def _chunk_cumsum(g, chunk, inclusive):
    row = lax.broadcasted_iota(jnp.int32, g.shape, 0) % chunk
    b = g
    s = 1
    while s < chunk:
        b = b + jnp.where(row >= s, pltpu.roll(b, s, axis=0), 0.0)
        s *= 2
    return b if inclusive else b - g


def _gla_kernel(flag_ref, qk_ref, v_ref, lr_ref, wup_ref, bup_ref, *rest, reverse, nblk):
    s_idx = pl.program_id(0)
    blk = nblk - 1 - s_idx if reverse else s_idx

    @pl.when(flag_ref[blk] == 1)
    def _():
        rest[-1][...] = jnp.zeros_like(rest[-1])

    for hh in range(GLA_HEADS):
        _gla_head(hh, qk_ref, v_ref, lr_ref, wup_ref, bup_ref, *rest, reverse=reverse)


def _gla_head(hh, qk_ref, v_ref, lr_ref, wup_ref, bup_ref, *rest, reverse):
    if reverse:
        gate_ref, of_ref, o_ref, st_ref = rest
    else:
        o_ref, st_ref = rest
    f32, bf16 = jnp.float32, jnp.bfloat16
    C = GLA_CHUNK
    TB = qk_ref.shape[0]
    nc = TB // C
    sl = slice(hh * LANE, (hh + 1) * LANE)

    is_q = lax.broadcasted_iota(jnp.int32, (1, LANE), 1) < GLA_DK
    qk = qk_ref[:, sl].astype(f32) * jnp.where(is_q, GLA_DK ** -0.5, 1.0)
    z = jnp.dot(lr_ref[...], wup_ref[hh], preferred_element_type=f32) + bup_ref[hh]
    g = _log_sigmoid(z) * (1.0 / GLA_TAU)
    cs = _chunk_cumsum(g, C, inclusive=not reverse)
    cs3 = cs.reshape(nc, C, LANE)
    mid = cs3[:, C // 2:C // 2 + 1, :]
    if reverse:
        tot3 = cs3[:, C - 1:C, :] + g.reshape(nc, C, LANE)[:, C - 1:C, :]
        edge3 = jnp.where(is_q, tot3 - cs3, cs3)
        mid3 = jnp.where(is_q, mid - cs3, cs3 - mid)
    else:
        tot3 = cs3[:, C - 1:C, :]
        edge3 = jnp.where(is_q, cs3, tot3 - cs3)
        mid3 = jnp.where(is_q, cs3 - mid, mid - cs3)
    x_edge = qk * jnp.exp(edge3).reshape(TB, LANE)
    x_mid = qk * jnp.exp(mid3).reshape(TB, LANE)
    qe = jnp.where(is_q, x_edge, 0.0).astype(bf16).reshape(nc, C, LANE)
    kd = jnp.where(is_q, pltpu.roll(x_edge, GLA_DK, axis=1), 0.0).astype(bf16).reshape(nc, C, LANE)
    qm = jnp.where(is_q, x_mid, 0.0).astype(bf16).reshape(nc, C, LANE)
    km = pltpu.roll(x_mid, GLA_DK, axis=1).astype(bf16).reshape(nc, C, LANE)
    etot = jnp.exp(tot3)
    v = v_ref[:, sl]
    ri = lax.broadcasted_iota(jnp.int32, (C, C), 0)
    ci = lax.broadcasted_iota(jnp.int32, (C, C), 1)
    keep = (ci > ri) if reverse else (ci <= ri)

    st = st_ref[hh]
    outs = [None] * nc
    for c in (range(nc - 1, -1, -1) if reverse else range(nc)):
        vc = v[c * C:(c + 1) * C, :]
        att = lax.dot_general(qm[c], km[c], (((1,), (1,)), ((), ())), preferred_element_type=f32)
        att = jnp.where(keep, att, 0.0).astype(bf16)
        o_c = jnp.dot(att, vc, preferred_element_type=f32)
        o_c = o_c + lax.dot_general(qe[c], st.astype(bf16), (((1,), (1,)), ((), ())), preferred_element_type=f32)
        upd = lax.dot_general(vc, kd[c], (((0,), (0,)), ((), ())), preferred_element_type=f32)
        st = st * etot[c] + upd
        outs[c] = o_c
    st_ref[hh] = st
    o = jnp.concatenate(outs, axis=0)
    if reverse:
        o = _headnorm(o + of_ref[:, sl])
        o_ref[:, sl] = (o * _silu(gate_ref[:, sl].astype(f32))).astype(o_ref.dtype)
    else:
        o_ref[:, sl] = o


def _gla(h, seq_first, seq_last, wup, bup):
    n = h.shape[0]
    TB = SEQ_BLOCK
    nblk = n // TB
    H = GLA_HEADS

    def col(off):
        return off // LANE

    HW = H * LANE

    def specs(reverse):
        blk = (lambda s: nblk - 1 - s) if reverse else (lambda s: s)
        sp = [pl.BlockSpec((TB, HW), lambda s, f: (blk(s), QK_OFF // HW)),
              pl.BlockSpec((TB, HW), lambda s, f: (blk(s), AV_OFF // HW)),
              pl.BlockSpec((TB, LANE), lambda s, f: (blk(s), col(LR_OFF))),
              pl.BlockSpec((None, H, LANE, LANE), lambda s, f: (1 if reverse else 0, 0, 0, 0)),
              pl.BlockSpec((None, H, 1, LANE), lambda s, f: (1 if reverse else 0, 0, 0, 0))]
        out = pl.BlockSpec((TB, HW), lambda s, f: (blk(s), 0))
        return sp, out, blk

    scratch = [pltpu.VMEM((H, GLA_DV, LANE), jnp.float32)]
    sp, out, _ = specs(False)
    o_f = pl.pallas_call(
        functools.partial(_gla_kernel, reverse=False, nblk=nblk), name="gla_fwd",
        grid_spec=pltpu.PrefetchScalarGridSpec(num_scalar_prefetch=1, grid=(nblk,), in_specs=sp, out_specs=out,
                                               scratch_shapes=scratch),
        out_shape=jax.ShapeDtypeStruct((n, H * GLA_DV), jnp.float32),
        compiler_params=_cparams("arbitrary"),
    )(seq_first, h, h, h, wup, bup)
    sp, out, blk = specs(True)
    sp = sp + [pl.BlockSpec((TB, HW), lambda s, f: (blk(s), AG_OFF // HW)), out]
    return pl.pallas_call(
        functools.partial(_gla_kernel, reverse=True, nblk=nblk), name="gla_bwd",
        grid_spec=pltpu.PrefetchScalarGridSpec(num_scalar_prefetch=1, grid=(nblk,), in_specs=sp, out_specs=out,
                                               scratch_shapes=scratch),
        out_shape=jax.ShapeDtypeStruct((n, H * GLA_DV), jnp.bfloat16),
        compiler_params=_cparams("arbitrary"),
    )(seq_last, h, h, h, wup, bup, h, o_f)


def _pack_gla_up(w_up, b_up):
    H, dk, r = GLA_HEADS, GLA_DK, GLA_RANK
    assert 2 * dk == LANE
    w = w_up.reshape(2, r, H, dk).transpose(0, 2, 1, 3)
    wp = jnp.zeros((2, H, LANE, dk), jnp.float32)
    wp = wp.at[0, :, 0:r].set(w[0]).at[1, :, r:2 * r].set(w[1])
    bp = b_up.reshape(2, H, 1, dk).astype(jnp.float32)
    return jnp.concatenate([wp, wp], axis=-1).astype(jnp.bfloat16), jnp.concatenate([bp, bp], axis=-1)


def _rope_tables(pos, dim, theta, width):
    half = dim // 2
    T = pos.shape[0]
    inv = 1.0 / (theta ** (jnp.arange(half, dtype=jnp.float32) / half))
    ang = pos.astype(jnp.float32)[:, None] * inv[None, :]
    c, s = jnp.cos(ang), jnp.sin(ang)
    rest = width - dim
    cos = jnp.concatenate([c, c, jnp.ones((T, rest), jnp.float32)], axis=1)
    sin_lo = jnp.concatenate([-s, jnp.zeros((T, half + rest), jnp.float32)], axis=1)
    sin_hi = jnp.concatenate([jnp.zeros((T, half), jnp.float32), s, jnp.zeros((T, rest), jnp.float32)], axis=1)
    return cos, sin_lo, sin_hi


def _ret_kernel(flag_ref, pos_ref, q_ref, k_ref, v_ref, cos_ref, sin_ref, dec_ref, *rest, reverse, nblk):
    s_idx = pl.program_id(0)
    blk = nblk - 1 - s_idx if reverse else s_idx

    @pl.when(flag_ref[blk] == 1)
    def _():
        st_ref = rest[-1] if reverse else rest[-2]
        st_ref[...] = jnp.zeros_like(st_ref)

    for hh in range(RET_HEADS):
        _ret_head(hh, s_idx, q_ref, k_ref, v_ref, cos_ref, sin_ref, dec_ref, *rest, reverse=reverse)


def _ret_head(hh, s_idx, q_ref, k_ref, v_ref, cos_ref, sin_ref, dec_ref, *rest, reverse):
    if reverse:
        gate_ref, o1_ref, o_ref, st_ref = rest
    else:
        o_ref, st_ref, dm_ref = rest
    f32, bf16 = jnp.float32, jnp.bfloat16
    TB = q_ref.shape[0]
    sl = slice(hh * RET_DH, (hh + 1) * RET_DH)

    dec = dec_ref[hh]
    lg = -(jnp.maximum(dec, 0.0) + jnp.log1p(jnp.exp(-jnp.abs(dec))))
    lg_f, lg_b = lg[0, 0:1, 0:1], lg[1, 0:1, 0:1]

    cos, sin = cos_ref[...], sin_ref[...]

    def rot(x):
        return x * cos + pltpu.roll(x, RET_DH // 2, axis=1) * sin

    qr = rot(q_ref[:, sl].astype(f32))
    kr = rot(k_ref[:, sl].astype(f32)) * (RET_DH ** -0.5)
    v = v_ref[:, sl]
    idx = lax.broadcasted_iota(jnp.int32, (TB, 1), 0).astype(f32)
    st = st_ref[hh]
    cross = jnp.dot(qr.astype(bf16), st.astype(bf16), preferred_element_type=f32)
    if reverse:
        cross = cross * jnp.exp(lg_b * (TB - idx))
        kz = (kr * jnp.exp(lg_b * idx)).astype(bf16)
        st_ref[hh] = st * jnp.exp(lg_b * TB) + lax.dot_general(kz, v, (((0,), (0,)), ((), ())),
                                                               preferred_element_type=f32)
        o = _headnorm(o1_ref[:, sl] + cross)
        o_ref[:, sl] = (o * _silu(gate_ref[:, sl].astype(f32))).astype(o_ref.dtype)
    else:
        @pl.when(s_idx == 0)
        def _():
            ri = lax.broadcasted_iota(jnp.int32, (TB, TB), 0)
            ci = lax.broadcasted_iota(jnp.int32, (TB, TB), 1)
            d = (ri - ci).astype(f32)
            dm_ref[hh] = jnp.where(d >= 0, jnp.exp(lg_f * jnp.maximum(d, 0.0)), jnp.exp(lg_b * jnp.maximum(-d, 0.0)))

        cross = cross * jnp.exp(lg_f * (idx + 1.0))
        kz = (kr * jnp.exp(lg_f * (TB - 1.0 - idx))).astype(bf16)
        st_ref[hh] = st * jnp.exp(lg_f * TB) + lax.dot_general(kz, v, (((0,), (0,)), ((), ())),
                                                               preferred_element_type=f32)
        sc = lax.dot_general(qr.astype(bf16), kr.astype(bf16), (((1,), (1,)), ((), ())), preferred_element_type=f32)
        sc = (sc * dm_ref[hh]).astype(bf16)
        o_ref[:, sl] = jnp.dot(sc, v, preferred_element_type=f32) + cross


def _retention(h, seq_first, seq_last, pos_blk, decay, t_max):
    n = h.shape[0]
    TB = SEQ_BLOCK
    nblk = n // TB
    H, dh = RET_HEADS, RET_DH
    cos, sin_lo, sin_hi = _rope_tables(jnp.arange(t_max), dh, RET_THETA, dh)
    sin = sin_lo + sin_hi
    dec = jnp.broadcast_to(decay.astype(jnp.float32).T[:, :, None, None], (H, 2, 8, LANE))

    HW = H * dh

    def specs(reverse):
        blk = (lambda s: nblk - 1 - s) if reverse else (lambda s: s)
        tok = lambda off: pl.BlockSpec((TB, HW), lambda s, f, p: (blk(s), off // HW))
        tab = pl.BlockSpec((TB, LANE), lambda s, f, p: (p[blk(s)], 0))
        sp = [tok(DQ_OFF), tok(DK_OFF), tok(DV_OFF), tab, tab,
              pl.BlockSpec((H, 2, 8, LANE), lambda s, f, p: (0, 0, 0, 0))]
        return sp, tok, pl.BlockSpec((TB, HW), lambda s, f, p: (blk(s), 0))

    sp, _, out = specs(False)
    o1 = pl.pallas_call(
        functools.partial(_ret_kernel, reverse=False, nblk=nblk), name="ret_fwd",
        grid_spec=pltpu.PrefetchScalarGridSpec(
            num_scalar_prefetch=2, grid=(nblk,), in_specs=sp, out_specs=out,
            scratch_shapes=[pltpu.VMEM((H, dh, dh), jnp.float32), pltpu.VMEM((H, TB, TB), jnp.float32)]),
        out_shape=jax.ShapeDtypeStruct((n, H * dh), jnp.float32),
        compiler_params=_cparams("arbitrary"),
    )(seq_first, pos_blk, h, h, h, cos, sin, dec)
    sp, tok, out = specs(True)
    sp = sp + [tok(DG_OFF), out]
    return pl.pallas_call(
        functools.partial(_ret_kernel, reverse=True, nblk=nblk), name="ret_bwd",
        grid_spec=pltpu.PrefetchScalarGridSpec(
            num_scalar_prefetch=2, grid=(nblk,), in_specs=sp, out_specs=out,
            scratch_shapes=[pltpu.VMEM((H, dh, dh), jnp.float32)]),
        out_shape=jax.ShapeDtypeStruct((n, H * dh), jnp.bfloat16),
        compiler_params=_cparams("arbitrary"),
    )(seq_last, pos_blk, h, h, h, cos, sin, dec, h, o1)


SWA_Q = 128
SWA_SUBS = 2
SWA_HALF = 64
assert all(w // (2 * d) == SWA_HALF for w, d in SWA_GROUPS)
NEG_BIG = -1e30


def _swa_kernel(hl_ref, mid_ref, hr_ref, q_ref, kl_ref, kc_ref, kr_ref, vl_ref, vc_ref, vr_ref,
                qcos_ref, qs1_ref, qs2_ref, kcos_ref, ks1_ref, ks2_ref, o_ref, lse_ref):
    f32, bf16 = jnp.float32, jnp.bfloat16
    j = pl.program_id(1)
    Q, HW = SWA_Q, SWA_HALF
    W = Q + 2 * HW
    ri = lax.broadcasted_iota(jnp.int32, (Q, W), 0)
    ci = lax.broadcasted_iota(jnp.int32, (Q, W), 1)
    rel = ci - HW - ri
    band = (rel <= HW) & (rel >= -HW)
    joins = [hl_ref[j] == 1] + [mid_ref[j] == 1] * (SWA_SUBS - 1) + [hr_ref[j] == 1]
    oks = [band & ((ci >= HW) | joins[u]) & ((ci < HW + Q) | joins[u + 1]) for u in range(SWA_SUBS)]
    lane = lax.broadcasted_iota(jnp.int32, (Q, LANE), 1)

    def rot(x, c_ref, s1_ref, s2_ref):
        return (x * c_ref[...] + pltpu.roll(x, LANE - ROPE_DIM // 2, axis=1) * s1_ref[...]
                + pltpu.roll(x, ROPE_DIM // 2, axis=1) * s2_ref[...])

    lse_tiles = [jnp.zeros((Q, LANE), f32) for _ in range(SWA_SUBS)]
    for hh in range(SWA_HEADS):
        sl = slice(hh * SWA_DH, (hh + 1) * SWA_DH)
        q = (rot(q_ref[:, sl].astype(f32), qcos_ref, qs1_ref, qs2_ref) * (SWA_DH ** -0.5)).astype(bf16)
        k = jnp.concatenate([kl_ref[:, sl], kc_ref[:, sl], kr_ref[:, sl]], axis=0).astype(f32)
        k = rot(k, kcos_ref, ks1_ref, ks2_ref).astype(bf16)
        v = jnp.concatenate([vl_ref[:, sl], vc_ref[:, sl], vr_ref[:, sl]], axis=0)
        for u in range(SWA_SUBS):
            s = lax.dot_general(q[u * Q:(u + 1) * Q], k[u * Q:u * Q + W], (((1,), (1,)), ((), ())),
                                preferred_element_type=f32)
            s = jnp.where(oks[u], s, NEG_BIG)
            m = jnp.max(s, -1, keepdims=True)
            p = jnp.exp(s - m)
            den = jnp.sum(p, -1, keepdims=True)
            o = jnp.dot(p.astype(bf16), v[u * Q:u * Q + W], preferred_element_type=f32) / den
            o_ref[u * Q:(u + 1) * Q, sl] = o
            lse_tiles[u] = jnp.where(lane == hh, m + jnp.log(den), lse_tiles[u])
    for u in range(SWA_SUBS):
        lse_ref[u * Q:(u + 1) * Q, :] = lse_tiles[u]


def _swa_merge_kernel(*refs):
    ng = len(SWA_GROUPS)
    o_refs, l_refs, y_ref = refs[:ng], refs[ng:2 * ng], refs[2 * ng]
    scratch = refs[2 * ng + 1:]
    assert SWA_DH == LANE
    os_, ls_ = [], []
    si = 0
    for gi, (_, dil) in enumerate(SWA_GROUPS):
        if dil == 1:
            os_.append(lambda hh, r=o_refs[gi]: r[0, :, hh * LANE:(hh + 1) * LANE])
            ls_.append(l_refs[gi].at[0])
            continue
        o_s, l_s = scratch[si], scratch[si + 1]
        si += 2
        rows = l_s.shape[0] // dil
        for r in range(dil):
            for hh in range(SWA_HEADS):
                o_s[hh, pl.ds(r, rows, stride=dil), :] = o_refs[gi][r, :, hh * LANE:(hh + 1) * LANE]
            l_s[pl.ds(r, rows, stride=dil), :] = l_refs[gi][r]
        os_.append(lambda hh, s=o_s: s[hh])
        ls_.append(l_s)
    for hh in range(SWA_HEADS):
        ls = [r[:, hh:hh + 1] for r in ls_]
        m = functools.reduce(jnp.maximum, ls)
        ws = [jnp.exp(l - m) for l in ls]
        tot = functools.reduce(lambda a, b: a + b, ws)
        y = functools.reduce(lambda a, b: a + b, [w * o(hh) for w, o in zip(ws, os_)])
        y_ref[:, hh * LANE:(hh + 1) * LANE] = (y / tot).astype(y_ref.dtype)


def _proj_strided_kernel(a_ref, b_ref, o_ref, acc_ref, *, dil):
    acc = jnp.dot(a_ref[...], b_ref[...], preferred_element_type=jnp.float32)
    if dil == 1:
        o_ref[0] = acc.astype(o_ref.dtype)
        return
    rows = acc_ref.shape[1] // dil
    for s in range(acc_ref.shape[0]):
        acc_ref[s] = acc[:, s * LANE:(s + 1) * LANE]
        for r in range(dil):
            o_ref[r, :, s * LANE:(s + 1) * LANE] = acc_ref[s, pl.ds(r, rows, stride=dil), :].astype(o_ref.dtype)


def _proj_strided(a, b, dil, bm=1024, bn=512):
    n, K = a.shape
    N = b.shape[1]
    return pl.pallas_call(
        functools.partial(_proj_strided_kernel, dil=dil), name=f"swa_proj_dil{dil}", grid=(n // bm, N // bn),
        in_specs=[pl.BlockSpec((bm, K), lambda i, j: (i, 0)), pl.BlockSpec((K, bn), lambda i, j: (0, j))],
        out_specs=pl.BlockSpec((dil, bm // dil, bn), lambda i, j: (0, i, j)),
        out_shape=jax.ShapeDtypeStruct((dil, n // dil, N), jnp.bfloat16),
        scratch_shapes=[pltpu.VMEM((bn // LANE, bm, LANE), jnp.float32)],
        compiler_params=_cparams("parallel", "arbitrary"),
    )(a, b)


def _swa(xb, w_groups, seqs):
    n = xb.shape[0]
    Q, HW = SWA_Q, SWA_HALF
    GW = SWA_HEADS * SWA_DH
    outs, lses = [], []
    for gi, (window, dil) in enumerate(SWA_GROUPS):
        hs = _proj_strided(xb, w_groups[gi], dil)
        rows = n // dil
        QB = SWA_SUBS * Q
        nblk = rows // QB
        has_l, has_r = [], []
        for num, T in seqs:
            per = T // dil // Q
            assert per * Q * dil == T
            has_l += ([0] + [1] * (per - 1)) * num
            has_r += ([1] * (per - 1) + [0]) * num
        assert len(has_l) == nblk * SWA_SUBS and SWA_SUBS == 2
        flags = [jnp.asarray(np.array(a, np.int32)) for a in (has_l[0::2], has_r[0::2], has_r[1::2])]
        qpos = jnp.arange(QB) * dil
        kpos = (jnp.arange(QB + 2 * HW) - HW) * dil
        qt = _rope_tables(qpos, ROPE_DIM, ROPE_THETA, SWA_DH)
        kt = _rope_tables(kpos, ROPE_DIM, ROPE_THETA, SWA_DH)
        per64 = QB // HW
        last64 = rows // HW - 1

        def center(c):
            return pl.BlockSpec((None, QB, GW), lambda r, j, *_: (r, j, c))

        def left(c):
            return pl.BlockSpec((None, HW, GW), lambda r, j, *_: (r, jnp.maximum(per64 * j - 1, 0), c))

        def right(c):
            return pl.BlockSpec((None, HW, GW), lambda r, j, *_: (r, jnp.minimum(per64 * (j + 1), last64), c))

        qtab = pl.BlockSpec((QB, SWA_DH), lambda r, j, *_: (0, 0))
        ktab = pl.BlockSpec((QB + 2 * HW, SWA_DH), lambda r, j, *_: (0, 0))
        o_g, lse_g = pl.pallas_call(
            _swa_kernel, name=f"swa_dil{dil}",
            grid_spec=pltpu.PrefetchScalarGridSpec(
                num_scalar_prefetch=3, grid=(dil, nblk),
                in_specs=[center(0), left(1), center(1), right(1), left(2), center(2), right(2),
                          qtab, qtab, qtab, ktab, ktab, ktab],
                out_specs=[pl.BlockSpec((None, QB, GW), lambda r, j, *_: (r, j, 0)),
                           pl.BlockSpec((None, QB, LANE), lambda r, j, *_: (r, j, 0))]),
            out_shape=[jax.ShapeDtypeStruct((dil, rows, GW), jnp.float32),
                       jax.ShapeDtypeStruct((dil, rows, LANE), jnp.float32)],
            compiler_params=_cparams("parallel", "parallel"),
        )(*flags, hs, hs, hs, hs, hs, hs, hs, *qt, *kt)
        outs.append(o_g)
        lses.append(lse_g)
    bm = 1024
    dils = [d for _, d in SWA_GROUPS]
    osp = [pl.BlockSpec((d, bm // d, GW), lambda i: (0, i, 0)) for d in dils]
    lsp = [pl.BlockSpec((d, bm // d, LANE), lambda i: (0, i, 0)) for d in dils]
    scratch = []
    for d in dils:
        if d > 1:
            scratch += [pltpu.VMEM((SWA_HEADS, bm, LANE), jnp.float32), pltpu.VMEM((bm, LANE), jnp.float32)]
    return pl.pallas_call(
        _swa_merge_kernel, name="swa_merge", grid=(n // bm,), in_specs=osp + lsp,
        out_specs=pl.BlockSpec((bm, GW), lambda i: (i, 0)),
        out_shape=jax.ShapeDtypeStruct((n, GW), jnp.bfloat16), scratch_shapes=scratch,
        compiler_params=_cparams("parallel"),
    )(*outs, *lses)


FFT_T2 = 128
FW = FNET_GROUPS * FNET_DG


def _dft_cos_sin(n_out, n_in, period):
    ang = (2.0 * np.pi / period) * np.mod(np.outer(np.arange(n_out), np.arange(n_in)), period)
    return np.cos(ang), np.sin(ang)


def _fnet_stage1_kernel(z_ref, m_ref, a_ref, *, cols):
    mat = m_ref[...]
    for c in range(cols):
        zr = z_ref[:, c * 2 * FW:c * 2 * FW + FW]
        zi = z_ref[:, c * 2 * FW + FW:(c + 1) * 2 * FW]
        a = jnp.dot(mat, jnp.concatenate([zr, zi], axis=0), preferred_element_type=jnp.float32)
        t1 = zr.shape[0]
        a_ref[:, c * 2 * FW:c * 2 * FW + FW] = a[:t1].astype(a_ref.dtype)
        a_ref[:, c * 2 * FW + FW:(c + 1) * 2 * FW] = a[t1:].astype(a_ref.dtype)


FNET_KB = 8


def _fnet_stage2_kernel(a_ref, g_ref, y_ref, *, scale):
    kw = g_ref.shape[1] // FNET_KB
    for j in range(FNET_KB):
        y = jnp.dot(g_ref[:, j * kw:(j + 1) * kw], a_ref[j], preferred_element_type=jnp.float32)
        y_ref[:, j * FW:(j + 1) * FW] = (y * scale).astype(y_ref.dtype)


def _fnet(h, seqs):
    n = h.shape[0]
    bf16 = jnp.bfloat16
    T2 = FFT_T2
    cc, ss = _dft_cos_sin(FNET_DG, FNET_DG, FNET_DG)
    eye = np.eye(FNET_GROUPS)
    wc = jnp.asarray(np.concatenate([np.kron(eye, cc), -np.kron(eye, ss)], axis=1), bf16)
    z = _matmul(h, wc, bf16, bn=2 * FW, a_col_blk=C_OFF // FW, name="fnet_channel_dft")
    zz = z.reshape(n // T2, T2 * 2 * FW)
    outs = []
    row0 = 0
    for num, T in seqs:
        T1 = T // T2
        assert T1 * T2 == T and T1 % 16 == 0
        c1, s1 = _dft_cos_sin(T1, T1, T1)
        mat = jnp.asarray(np.block([[c1, s1], [-s1, c1]]), bf16)
        cols = min(T2, max(1, 1024 // T1))
        rb0 = row0 // T1
        a = pl.pallas_call(
            functools.partial(_fnet_stage1_kernel, cols=cols), name=f"fnet_dft_t1_{T1}", grid=(num, T2 // cols),
            in_specs=[pl.BlockSpec((T1, cols * 2 * FW), lambda b, j: (rb0 + b, j)),
                      pl.BlockSpec((2 * T1, 2 * T1), lambda b, j: (0, 0))],
            out_specs=pl.BlockSpec((T1, cols * 2 * FW), lambda b, j: (b, j)),
            out_shape=jax.ShapeDtypeStruct((num * T1, T2 * 2 * FW), bf16),
            compiler_params=_cparams("parallel", "parallel"),
        )(zz, mat)
        ec, es = _dft_cos_sin(T, T2, T)
        g = jnp.asarray(np.stack([ec, es], axis=-1).reshape(T2, T1 * 2 * T2), bf16)
        y = pl.pallas_call(
            functools.partial(_fnet_stage2_kernel, scale=float(1.0 / np.sqrt(T * FNET_DG))), name=f"fnet_dft_t2_{T1}",
            grid=(num, T1 // FNET_KB),
            in_specs=[pl.BlockSpec((FNET_KB, 2 * T2, FW), lambda b, k1: (b * (T1 // FNET_KB) + k1, 0, 0)),
                      pl.BlockSpec((T2, FNET_KB * 2 * T2), lambda b, k1: (0, k1))],
            out_specs=pl.BlockSpec((T2, FNET_KB * FW), lambda b, k1: (b, k1)),
            out_shape=jax.ShapeDtypeStruct((num * T2, T1 * FW), bf16),
            compiler_params=_cparams("parallel", "parallel"),
        )(a.reshape(num * T1, 2 * T2, FW), g)
        outs.append(y.reshape(num * T, FW))
        row0 += num * T1
    return jnp.concatenate(outs, axis=0)


def _ffn_up_kernel(x_ref, w1_ref, w3_ref, h_ref, w1b_ref, w3b_ref):
    @pl.when(pl.program_id(2) == 0)
    def _():
        w1b_ref[...] = w1_ref[...].astype(w1b_ref.dtype)
        w3b_ref[...] = w3_ref[...].astype(w3b_ref.dtype)

    x = x_ref[...]
    h1 = jnp.dot(x, w1b_ref[...], preferred_element_type=jnp.float32)
    h3 = jnp.dot(x, w3b_ref[...], preferred_element_type=jnp.float32)
    h_ref[...] = (_silu(h1) * h3).astype(h_ref.dtype)


def _ffn_down_kernel(h_ref, w2_ref, s_ref, o_ref, w2b_ref):
    e = pl.program_id(0)

    @pl.when(pl.program_id(1) == 0)
    def _():
        w2b_ref[...] = w2_ref[...].astype(w2b_ref.dtype)

    D = w2_ref.shape[1]
    side = s_ref[...]
    lane = lax.broadcasted_iota(jnp.int32, side.shape, 1)
    mine = (lane == e) | (lane == e + SIDE_MID) | (lane == e + SIDE_LOW)
    gate = jnp.sum(jnp.where(mine, side.astype(jnp.float32), 0.0), axis=1, keepdims=True)
    y = jnp.dot(h_ref[...], w2b_ref[...], preferred_element_type=jnp.float32)
    o_ref[:, :D] = (y * gate).astype(o_ref.dtype)
    o_ref[:, D:] = side


def _expert_ffn(xe, w1, w3, w2, layer, bm=1024, bf=512):
    E, R, DY = xe.shape
    D = DY - LANE
    F = w1.shape[-1]
    wspec = pl.BlockSpec((None, None, D, bf), lambda e, f, i: (layer, e, 0, f))
    hmid = pl.pallas_call(
        _ffn_up_kernel, name="expert_ffn_up", grid=(E, F // bf, R // bm),
        in_specs=[pl.BlockSpec((None, bm, D), lambda e, f, i: (e, i, 0)), wspec, wspec],
        out_specs=pl.BlockSpec((None, bm, bf), lambda e, f, i: (e, i, f)),
        out_shape=jax.ShapeDtypeStruct((E, R, F), jnp.bfloat16),
        scratch_shapes=[pltpu.VMEM((D, bf), jnp.bfloat16), pltpu.VMEM((D, bf), jnp.bfloat16)],
        compiler_params=_cparams("arbitrary", "arbitrary", "arbitrary"),
    )(xe, w1, w3)
    return pl.pallas_call(
        _ffn_down_kernel, name="expert_ffn_down", grid=(E, R // bm),
        in_specs=[pl.BlockSpec((None, bm, F), lambda e, i: (e, i, 0)),
                  pl.BlockSpec((None, None, F, D), lambda e, i: (layer, e, 0, 0), pipeline_mode=pl.Buffered(1)),
                  pl.BlockSpec((None, bm, LANE), lambda e, i: (e, i, D // LANE))],
        out_specs=pl.BlockSpec((None, bm, D + LANE), lambda e, i: (e, i, 0)),
        out_shape=jax.ShapeDtypeStruct((E, R, D + LANE), jnp.bfloat16),
        scratch_shapes=[pltpu.VMEM((F, D), jnp.bfloat16)],
        compiler_params=_cparams("arbitrary", "arbitrary"),
    )(hmid, w2, xe)


COMB_BM = 128
COMB_G = 16
COMB_K = 256


def _combine_kernel(lo_ref, x_ref, g_ref, b_ref, ye_hbm, *rest, nblk_all, blk_off, want_bf16):
    if want_bf16:
        xf_ref, xb_ref, buf_ref, acc_ref, sem = rest
    else:
        xf_ref, buf_ref, acc_ref, sem = rest
    f32, bf16 = jnp.float32, jnp.bfloat16
    E = ye_hbm.shape[0]
    D = x_ref.shape[1]
    BM, G, KC = COMB_BM, COMB_G, COMB_K
    i = pl.program_id(0)
    nsteps = pl.num_programs(0)
    blk = i + blk_off
    slot = i % 2

    @pl.when(i == 0)
    def _():
        buf_ref[...] = jnp.zeros_like(buf_ref)

    def granule_copy(sl, e, src_row, dst_row):
        return pltpu.make_async_copy(ye_hbm.at[e, pl.ds(src_row, G)], buf_ref.at[sl, pl.ds(dst_row, G)], sem.at[sl])

    def gather_block(b, sl, start):
        off = jnp.int32(0)
        for e in range(E):
            lo = lo_ref[e * (nblk_all + 1) + b]
            hi = lo_ref[e * (nblk_all + 1) + b + 1]
            lo_al = (lo // G) * G
            ng = jnp.where(hi > lo, (hi - lo_al + G - 1) // G, 0)
            if start:
                def issue(g, carry, e=e, lo_al=lo_al, off=off):
                    granule_copy(sl, e, pl.multiple_of(lo_al + g * G, G), pl.multiple_of(off + g * G, G)).start()
                    return carry

                lax.fori_loop(0, ng, issue, 0)
            off = off + ng * G
        return off

    @pl.when(i == 0)
    def _():
        gather_block(blk, slot, True)

    @pl.when(i + 1 < nsteps)
    def _():
        gather_block(blk + 1, 1 - slot, True)

    off = gather_block(blk, slot, False)

    def wait_one(g, carry):
        granule_copy(slot, 0, 0, 0).wait()
        return carry

    lax.fori_loop(0, off // G, wait_one, 0)

    acc_ref[...] = jnp.zeros_like(acc_ref)
    t0 = (blk * BM).astype(f32)
    lane_tok = lax.broadcasted_iota(jnp.int32, (KC, BM), 1).astype(f32)
    row = lax.broadcasted_iota(jnp.int32, (KC, 1), 0)

    def chunk(c, carry):
        win = buf_ref[slot, pl.ds(pl.multiple_of(c * KC, KC), KC), :]
        tok = (win[:, D + SIDE_TOK:D + SIDE_TOK + 1].astype(f32) * TOK_RADIX
               + win[:, D + SIDE_TOK + 1:D + SIDE_TOK + 2].astype(f32))
        tok = jnp.where(row + c * KC < off, tok, -1.0)
        onehot_t = (tok - t0 == lane_tok).astype(bf16)
        acc_ref[...] += lax.dot_general(onehot_t, win[:, :D], (((0,), (0,)), ((), ())), preferred_element_type=f32)
        return carry

    lax.fori_loop(0, (off + KC - 1) // KC, chunk, 0)
    y = _ln_rows(DN_ALPHA * x_ref[...] + acc_ref[...], g_ref[...], b_ref[...])
    xf_ref[...] = y
    if want_bf16:
        xb_ref[...] = y.astype(bf16)


def _combine_ln(x, ye, lo, g, b, row_off, rows, want_bf16):
    n, D = x.shape
    E, R, DY = ye.shape
    BM = COMB_BM
    nblk_all = n // BM
    ob = row_off // BM
    kmax = -(-E * (BM + COMB_G) // COMB_K) * COMB_K
    irow = pl.BlockSpec((BM, D), lambda i, lo: (i + ob, 0))
    orow = pl.BlockSpec((BM, D), lambda i, lo: (i, 0))
    vec = pl.BlockSpec((1, D), lambda i, lo: (0, 0))
    out_shape = [jax.ShapeDtypeStruct((rows, D), jnp.float32)]
    out_specs = [orow]
    if want_bf16:
        out_shape.append(jax.ShapeDtypeStruct((rows, D), jnp.bfloat16))
        out_specs.append(orow)
    return pl.pallas_call(
        functools.partial(_combine_kernel, nblk_all=nblk_all, blk_off=ob, want_bf16=want_bf16), name="moe_combine_ln",
        grid_spec=pltpu.PrefetchScalarGridSpec(
            num_scalar_prefetch=1, grid=(rows // BM,),
            in_specs=[irow, vec, vec, pl.BlockSpec(memory_space=pl.ANY)], out_specs=out_specs,
            scratch_shapes=[pltpu.VMEM((2, kmax, DY), jnp.bfloat16), pltpu.VMEM((BM, D), jnp.float32),
                            pltpu.SemaphoreType.DMA((2,))]),
        out_shape=out_shape,
        compiler_params=_cparams("arbitrary"),
    )(lo, x, g.reshape(1, D), b.reshape(1, D), ye)


def _route_kernel(all_ref, a_ref, idx_ref, cprev_ref, thr_ref, need_ref, *, cap, tok_off):
    f32, bf16, i32 = jnp.float32, jnp.bfloat16, jnp.int32
    e = pl.program_id(0)

    def total(x):
        return jnp.sum(jnp.sum(x, axis=2, keepdims=True), axis=1, keepdims=True)

    @pl.when(e == 0)
    def _():
        bits_all = pltpu.bitcast(all_ref[...], i32)

        def refine(i, prefix):
            cand = prefix | jnp.left_shift(jnp.int32(1), 30 - i)
            return jnp.where(total((bits_all >= cand).astype(i32)) >= cap, cand, prefix)

        thr_all = lax.fori_loop(0, 31, refine, jnp.zeros((all_ref.shape[0], 1, 1), i32))
        thr_ref[...] = jnp.broadcast_to(thr_all, thr_ref.shape)
        need_ref[...] = jnp.broadcast_to(cap - total((bits_all > thr_all).astype(i32)), need_ref.shape)

    a = a_ref[...]
    R = a.shape[0]
    bits = pltpu.bitcast(a, i32)
    thr = thr_ref[e][0:1, 0:1]
    need = need_ref[e][0:1, 0:1]
    gt, eq = bits > thr, bits == thr

    li = lax.broadcasted_iota(i32, (LANE, LANE), 0)
    lj = lax.broadcasted_iota(i32, (LANE, LANE), 1)
    upto = (li <= lj).astype(bf16)
    ri = lax.broadcasted_iota(i32, (R, R), 0)
    rj = lax.broadcasted_iota(i32, (R, R), 1)
    before = (rj < ri).astype(bf16)

    def counts(mask):
        inc = jnp.dot(mask.astype(bf16), upto, preferred_element_type=f32)
        tot = jnp.broadcast_to(inc[:, LANE - 1:LANE], (R, LANE)).astype(bf16)
        return inc, jnp.dot(before, tot, preferred_element_type=f32)

    eq_inc, eq_prev = counts(eq)
    sel = gt | (eq & (eq_prev + eq_inc - 1.0 < need.astype(f32)))
    cl, cprev = counts(sel)
    cprev_ref[...] = cprev[:, :1].astype(i32)

    selb = sel.astype(bf16)
    c_upto = jnp.sum(lax.dot_general(selb, (ri <= rj).astype(bf16), (((0,), (0,)), ((), ())),
                                     preferred_element_type=f32), axis=0, keepdims=True)
    c_before = jnp.sum(lax.dot_general(selb, (ri < rj).astype(bf16), (((0,), (0,)), ((), ())),
                                       preferred_element_type=f32), axis=0, keepdims=True)
    slot = lax.broadcasted_iota(i32, (cap, 1), 0).astype(f32)
    onehot = ((c_before <= slot) & (slot < c_upto)).astype(bf16)
    lane = lax.broadcasted_iota(i32, (R, LANE), 1)
    cp = cprev.astype(i32)
    aux = jnp.where(lane == 0, cp // TOK_RADIX, jnp.where(lane == 1, cp % TOK_RADIX,
                    jnp.where(lane == 2, lax.broadcasted_iota(i32, (R, LANE), 0), 0)))
    rhs = jnp.concatenate([cl.astype(bf16), aux.astype(f32).astype(bf16)], axis=1)
    m = jnp.dot(onehot, rhs, preferred_element_type=f32)
    base = m[:, LANE:LANE + 1] * TOK_RADIX + m[:, LANE + 1:LANE + 2]
    chunk = m[:, LANE + 2:LANE + 3]
    pos = jnp.sum((m[:, :LANE] <= slot - base).astype(f32), axis=1, keepdims=True)
    idx_ref[...] = (chunk * LANE + pos).astype(i32) + tok_off


def _route(aff_t, cap, tok_off):
    E, R, _ = aff_t.shape
    return pl.pallas_call(
        functools.partial(_route_kernel, cap=cap, tok_off=tok_off), name="moe_route", grid=(E,),
        in_specs=[pl.BlockSpec((E, R, LANE), lambda e: (0, 0, 0)), pl.BlockSpec((None, R, LANE), lambda e: (e, 0, 0))],
        out_specs=[pl.BlockSpec((None, cap, 1), lambda e: (e, 0, 0)), pl.BlockSpec((None, R, 1), lambda e: (e, 0, 0))],
        out_shape=[jax.ShapeDtypeStruct((E, cap, 1), jnp.int32), jax.ShapeDtypeStruct((E, R, 1), jnp.int32)],
        scratch_shapes=[pltpu.VMEM((E, 8, LANE), jnp.int32), pltpu.VMEM((E, 8, LANE), jnp.int32)],
        compiler_params=_cparams("arbitrary"),
    )(aff_t, aff_t)


def _expert_choice(xs, aff, seqs, w1, w3, w2, layer):
    assert COMB_BM == LANE
    aff_t = _aff_chunks(aff)
    idxs, los = [], []
    start = slot0 = 0
    for num, T in seqs:
        m = num * T
        cap = max(1, CAPACITY_FACTOR * m // N_EXPERTS)
        idx, cprev = _route(aff_t[:, start // LANE:(start + m) // LANE], cap, start)
        idxs.append(idx[..., 0])
        los.append(cprev[..., 0] + slot0)
        start += m
        slot0 += cap
    lo = jnp.concatenate(los + [jnp.full((N_EXPERTS, 1), slot0, jnp.int32)], axis=1).reshape(-1)
    ye = _expert_ffn(xs[jnp.concatenate(idxs, axis=1)], w1, w3, w2, layer)
    return ye, lo


def _pack_w_in(w_in):
    sizes = (GLA_HEADS * GLA_DK, GLA_HEADS * GLA_DK, GLA_HEADS * GLA_DV, GLA_HEADS * GLA_DV, 2 * GLA_RANK,
             SWA_NH * SWA_DH, SWA_NH * SWA_DH, SWA_NH * SWA_DH, FNET_GROUPS * FNET_DG,
             RET_HEADS * RET_DH, RET_HEADS * RET_DH, RET_HEADS * RET_DH, RET_HEADS * RET_DH)
    offs = np.concatenate([[0], np.cumsum(sizes)])
    part = lambda i: w_in[..., offs[i]:offs[i + 1]]
    a_q, a_k = part(0), part(1)
    cols = []
    for hh in range(GLA_HEADS):
        cols += [a_q[..., hh * GLA_DK:(hh + 1) * GLA_DK], a_k[..., hh * GLA_DK:(hh + 1) * GLA_DK]]
    cols += [part(2), part(3), part(4),
             jnp.zeros(w_in.shape[:2] + (C_OFF - LR_OFF - 2 * GLA_RANK,), w_in.dtype)]
    cols += [part(i) for i in range(8, 13)]
    out = jnp.concatenate(cols, axis=-1)
    assert out.shape[-1] == IN_COLS_P
    GW = SWA_HEADS * SWA_DH
    swa = [jnp.concatenate([part(i)[..., g * GW:(g + 1) * GW] for i in (5, 6, 7)], axis=-1)
           for g in range(len(SWA_GROUPS))]
    return out, swa


def _block_tables(seqs, blk):
    first, last, pos = [], [], []
    for num, T in seqs:
        per = T // blk
        assert per * blk == T
        for _ in range(num):
            first += [1] + [0] * (per - 1)
            last += [0] * (per - 1) + [1]
            pos += list(range(per))
    mk = lambda a: jnp.asarray(np.array(a, np.int32))
    return mk(first), mk(last), mk(pos)


def kernel(x_prompt, x_sample, ln_in_g, ln_in_b, w_in, gla_w_up, gla_b_up, ret_decay, w_gate, b_gate, w_branch,
           w_out, ln1_g, ln1_b, w_router, w_e1, w_e3, w_e2, ln2_g, ln2_b):
    bf16 = jnp.bfloat16
    D = D_MODEL
    seqs = (x_prompt.shape[:2], x_sample.shape[:2])
    n_p, n_s = x_prompt.shape[0] * x_prompt.shape[1], x_sample.shape[0] * x_sample.shape[1]
    n = n_p + n_s
    t_max = max(T for _, T in seqs)
    seq_first, seq_last, pos_blk = _block_tables(seqs, SEQ_BLOCK)

    w_in_p, w_swa = _pack_w_in(w_in.astype(bf16))
    wo = w_out.astype(bf16)
    wr = jnp.pad(w_router, ((0, 0), (0, 0), (0, LANE - N_EXPERTS))).astype(bf16)
    wg, wb, we1, we3, we2 = w_gate, w_branch, w_e1, w_e3, w_e2

    xf, xb = _ln_in(x_prompt.reshape(n_p, D), x_sample.reshape(n_s, D), ln_in_g, ln_in_b)
    for l in range(DEPTH):
        h = _matmul(xb, w_in_p[l], bf16, bm=1024, bn=IN_BN, name="in_proj")
        wup, bup = _pack_gla_up(gla_w_up[l], gla_b_up[l])
        ys = (_gla(h, seq_first, seq_last, wup, bup),
              _swa(xb, [w[l] for w in w_swa], seqs),
              _fnet(h, seqs),
              _retention(h, seq_first, seq_last, pos_blk, ret_decay[l], t_max))
        merged = _merge(xb, wg, b_gate[l], ys, wb, l)
        x1f, x1s, aff = _outproj(merged, wo[l], xf, ln1_g[l], ln1_b[l], wr[l])
        ye, lo = _expert_choice(x1s, aff, seqs, we1, we3, we2, l)
        if l + 1 < DEPTH:
            xf, xb = _combine_ln(x1f, ye, lo, ln2_g[l], ln2_b[l], 0, n, True)
        else:
            y_p, = _combine_ln(x1f, ye, lo, ln2_g[l], ln2_b[l], 0, n_p, False)
            y_s, = _combine_ln(x1f, ye, lo, ln2_g[l], ln2_b[l], n_p, n_s, False)
    return (y_p.reshape(x_prompt.shape), y_s.reshape(x_sample.shape))
```

```python
import functools

import jax
import jax.numpy as jnp
import numpy as np
from jax import lax
from jax.experimental import pallas as pl
from jax.experimental.pallas import tpu as pltpu

D_MODEL = 2048
DEPTH = 2
GLA_HEADS, GLA_DK, GLA_DV, GLA_RANK, GLA_TAU, GLA_CHUNK = 4, 64, 128, 16, 16.0, 64
SWA_GROUPS = ((128, 1), (512, 4), (2048, 16))
SWA_HEADS, SWA_DH = 4, 128
SWA_NH = SWA_HEADS * len(SWA_GROUPS)
SWA_BLOCK = 64
ROPE_THETA, ROPE_DIM = 500000.0, SWA_DH // 4
FNET_GROUPS, FNET_DG = 4, 128
RET_HEADS, RET_DH, RET_THETA = 4, 128, 10000.0
N_BRANCH, BRANCH_W = 4, 512
N_EXPERTS, EXPERT_FF, CAPACITY_FACTOR = 16, 2048, 2
DN_ALPHA = (2 * DEPTH) ** 0.25
LN_EPS = 1e-5

VMEM_LIMIT_BYTES = 48 * 1024 * 1024
LANE = 128

QK_OFF = 0
AV_OFF = QK_OFF + GLA_HEADS * LANE
AG_OFF = AV_OFF + GLA_HEADS * GLA_DV
LR_OFF = AG_OFF + GLA_HEADS * GLA_DV
C_OFF = LR_OFF + 4 * LANE
DQ_OFF = C_OFF + FNET_GROUPS * FNET_DG
DK_OFF = DQ_OFF + RET_HEADS * RET_DH
DV_OFF = DK_OFF + RET_HEADS * RET_DH
DG_OFF = DV_OFF + RET_HEADS * RET_DH
IN_USED = DG_OFF + RET_HEADS * RET_DH
IN_BN = 1536
IN_COLS_P = -(-IN_USED // IN_BN) * IN_BN
assert IN_COLS_P == IN_USED

SEQ_BLOCK = 512


def _cparams(*sem):
    return pltpu.CompilerParams(dimension_semantics=sem, vmem_limit_bytes=VMEM_LIMIT_BYTES)


def _mm_kernel(a_ref, b_ref, o_ref):
    o_ref[...] = jnp.dot(a_ref[...], b_ref[...], preferred_element_type=jnp.float32).astype(o_ref.dtype)


def _matmul(a, b, out_dtype, bm=1024, bn=512, a_col_blk=0, name="matmul"):
    batched = a.ndim == 3
    M = a.shape[-2]
    K, N = b.shape[-2:]
    bm = min(bm, M)
    bn = min(bn, N)
    assert M % bm == 0 and N % bn == 0, (M, N, bm, bn)
    if batched:
        E = a.shape[0]
        grid = (E, M // bm, N // bn)
        in_specs = [pl.BlockSpec((None, bm, K), lambda e, i, j: (e, i, 0)),
                    pl.BlockSpec((None, K, bn), lambda e, i, j: (e, 0, j))]
        out_specs = pl.BlockSpec((None, bm, bn), lambda e, i, j: (e, i, j))
        out_shape = jax.ShapeDtypeStruct((E, M, N), out_dtype)
        sem = ("parallel", "parallel", "arbitrary")
    else:
        grid = (M // bm, N // bn)
        in_specs = [pl.BlockSpec((bm, K), lambda i, j: (i, a_col_blk)),
                    pl.BlockSpec((K, bn), lambda i, j: (0, j))]
        out_specs = pl.BlockSpec((bm, bn), lambda i, j: (i, j))
        out_shape = jax.ShapeDtypeStruct((M, N), out_dtype)
        sem = ("parallel", "arbitrary")
    return pl.pallas_call(
        _mm_kernel, name=name, grid=grid, in_specs=in_specs, out_specs=out_specs, out_shape=out_shape,
        compiler_params=_cparams(*sem),
    )(a, b)


def _mm_w32_kernel(a_ref, w_ref, o_ref, wb_ref):
    @pl.when(pl.program_id(1) == 0)
    def _():
        wb_ref[...] = w_ref[...].astype(wb_ref.dtype)

    o_ref[...] = jnp.dot(a_ref[...], wb_ref[...], preferred_element_type=jnp.float32).astype(o_ref.dtype)


def _matmul_w32(a, w, out_dtype, bm=1024, bn=768, name="matmul_w32"):
    M, K = a.shape
    N = w.shape[1]
    assert M % bm == 0 and N % bn == 0, (M, N, bm, bn)
    return pl.pallas_call(
        _mm_w32_kernel, name=name, grid=(N // bn, M // bm),
        in_specs=[pl.BlockSpec((bm, K), lambda j, i: (i, 0)), pl.BlockSpec((K, bn), lambda j, i: (0, j))],
        out_specs=pl.BlockSpec((bm, bn), lambda j, i: (i, j)),
        out_shape=jax.ShapeDtypeStruct((M, N), out_dtype),
        scratch_shapes=[pltpu.VMEM((K, bn), jnp.bfloat16)],
        compiler_params=_cparams("arbitrary", "arbitrary"),
    )(a, w)


def _ln_rows(x, g, b):
    mu = jnp.mean(x, -1, keepdims=True)
    xc = x - mu
    var = jnp.mean(xc * xc, -1, keepdims=True)
    return xc * lax.rsqrt(var + LN_EPS) * g + b


def _ln_in_kernel(xa_ref, xc_ref, g_ref, b_ref, xf_ref, xb_ref, *, na_blocks):
    i = pl.program_id(0)

    def emit(x_ref):
        y = _ln_rows(x_ref[...], g_ref[...], b_ref[...])
        xf_ref[...] = y
        xb_ref[...] = y.astype(jnp.bfloat16)

    pl.when(i < na_blocks)(lambda: emit(xa_ref))
    pl.when(i >= na_blocks)(lambda: emit(xc_ref))


def _ln_in(xa, xc, g, b, bm=512):
    (na, D), nc = xa.shape, xc.shape[0]
    nab, ncb = na // bm, nc // bm
    row = pl.BlockSpec((bm, D), lambda i: (i, 0))
    vec = pl.BlockSpec((1, D), lambda i: (0, 0))
    return pl.pallas_call(
        functools.partial(_ln_in_kernel, na_blocks=nab), name="ln_in", grid=(nab + ncb,),
        in_specs=[pl.BlockSpec((bm, D), lambda i: (jnp.minimum(i, nab - 1), 0)),
                  pl.BlockSpec((bm, D), lambda i: (jnp.maximum(i - nab, 0), 0)), vec, vec],
        out_specs=[row, row],
        out_shape=[jax.ShapeDtypeStruct((na + nc, D), jnp.float32), jax.ShapeDtypeStruct((na + nc, D), jnp.bfloat16)],
        compiler_params=_cparams("arbitrary"))(xa, xc, g.reshape(1, D), b.reshape(1, D))


def _merge_kernel(xb_ref, wg_ref, bg_ref, ya_ref, yb_ref, yc_ref, yd_ref, wb_ref, o_ref, wgb_ref, wbb_ref):
    @pl.when(pl.program_id(1) == 0)
    def _():
        wgb_ref[...] = wg_ref[...].astype(wgb_ref.dtype)
        wbb_ref[...] = wb_ref[...].astype(wbb_ref.dtype)

    xb = xb_ref[...]
    acc = None
    for i, y_ref in enumerate((ya_ref, yb_ref, yc_ref, yd_ref)):
        gate = jax.nn.sigmoid(jnp.dot(xb, wgb_ref[i], preferred_element_type=jnp.float32) + bg_ref[i])
        term = gate * jnp.dot(y_ref[...], wbb_ref[i], preferred_element_type=jnp.float32)
        acc = term if acc is None else acc + term
    o_ref[...] = acc.astype(o_ref.dtype)


def _merge(xb, wg, bg, ys, wb, layer, bm=1024, bn=256):
    n, D = xb.shape
    W = ys[0].shape[1]
    y_spec = pl.BlockSpec((bm, W), lambda j, i: (i, 0))
    return pl.pallas_call(
        _merge_kernel, name="branch_merge", grid=(D // bn, n // bm),
        in_specs=[pl.BlockSpec((bm, D), lambda j, i: (i, 0)),
                  pl.BlockSpec((None, N_BRANCH, D, bn), lambda j, i: (layer, 0, 0, j)),
                  pl.BlockSpec((N_BRANCH, 1, bn), lambda j, i: (0, 0, j)),
                  y_spec, y_spec, y_spec, y_spec,
                  pl.BlockSpec((None, N_BRANCH, W, bn), lambda j, i: (layer, 0, 0, j))],
        out_specs=pl.BlockSpec((bm, bn), lambda j, i: (i, j)),
        out_shape=jax.ShapeDtypeStruct((n, D), jnp.bfloat16),
        scratch_shapes=[pltpu.VMEM((N_BRANCH, D, bn), jnp.bfloat16), pltpu.VMEM((N_BRANCH, W, bn), jnp.bfloat16)],
        compiler_params=_cparams("arbitrary", "arbitrary"),
    )(xb, wg, bg.reshape(N_BRANCH, 1, D), *ys, wb)


OUTPROJ_SUB = 256


TOK_RADIX = 256
SIDE_MID, SIDE_LOW, SIDE_TOK = N_EXPERTS, 2 * N_EXPERTS, 3 * N_EXPERTS


def _outproj_kernel(m_ref, w_ref, x_ref, g_ref, b_ref, wr_ref, xf_ref, xb_ref, aff_ref):
    f32, bf16 = jnp.float32, jnp.bfloat16
    D = x_ref.shape[1]
    for s in range(m_ref.shape[0] // OUTPROJ_SUB):
        rows = pl.ds(s * OUTPROJ_SUB, OUTPROJ_SUB)
        r = jnp.dot(m_ref[rows, :], w_ref[...], preferred_element_type=f32)
        y = _ln_rows(DN_ALPHA * x_ref[rows, :] + r, g_ref[...], b_ref[...])
        yb = y.astype(bf16)
        xf_ref[rows, :] = y
        xb_ref[rows, :D] = yb
        logits = jnp.dot(yb, wr_ref[...], preferred_element_type=f32)
        lane = lax.broadcasted_iota(jnp.int32, logits.shape, 1)
        logits = jnp.where(lane < N_EXPERTS, logits, -jnp.inf)
        e = jnp.exp(logits - jnp.max(logits, -1, keepdims=True))
        aff = e / jnp.sum(e, -1, keepdims=True)
        aff_ref[rows, :] = aff
        hi = aff.astype(bf16).astype(f32)
        mid = (aff - hi).astype(bf16).astype(f32)
        low = (aff - hi - mid).astype(bf16).astype(f32)
        tok = (pl.program_id(0) * m_ref.shape[0] + s * OUTPROJ_SUB
               + lax.broadcasted_iota(jnp.int32, logits.shape, 0))
        side = hi + pltpu.roll(mid, SIDE_MID, axis=1) + pltpu.roll(low, SIDE_LOW, axis=1)
        side = jnp.where(lane == SIDE_TOK, (tok // TOK_RADIX).astype(f32),
                         jnp.where(lane == SIDE_TOK + 1, (tok % TOK_RADIX).astype(f32), side))
        xb_ref[rows, D:] = side.astype(bf16)


def _outproj(merged, w_out, x, g, b, w_router_p, bm=512):
    n, D = x.shape
    row = pl.BlockSpec((bm, D), lambda i: (i, 0))
    vec = pl.BlockSpec((1, D), lambda i: (0, 0))
    return pl.pallas_call(
        _outproj_kernel, name="outproj_ln_router", grid=(n // bm,),
        in_specs=[row, pl.BlockSpec((D, D), lambda i: (0, 0)), row, vec, vec,
                  pl.BlockSpec((D, LANE), lambda i: (0, 0))],
        out_specs=[row, pl.BlockSpec((bm, D + LANE), lambda i: (i, 0)), pl.BlockSpec((bm, LANE), lambda i: (i, 0))],
        out_shape=[jax.ShapeDtypeStruct((n, D), jnp.float32), jax.ShapeDtypeStruct((n, D + LANE), jnp.bfloat16),
                   jax.ShapeDtypeStruct((n, LANE), jnp.float32)],
        compiler_params=_cparams("parallel"),
    )(merged, w_out, x, g.reshape(1, D), b.reshape(1, D), w_router_p)


def _aff_t_kernel(a_ref, o_ref):
    for s in range(a_ref.shape[0] // LANE):
        o_ref[:, s, :] = a_ref[s * LANE:(s + 1) * LANE, :].T[:N_EXPERTS, :]


def _aff_chunks(aff, bm=1024):
    n = aff.shape[0]
    return pl.pallas_call(
        _aff_t_kernel, name="aff_chunks", grid=(n // bm,),
        in_specs=[pl.BlockSpec((bm, LANE), lambda i: (i, 0))],
        out_specs=pl.BlockSpec((N_EXPERTS, bm // LANE, LANE), lambda i: (0, i, 0)),
        out_shape=jax.ShapeDtypeStruct((N_EXPERTS, n // LANE, LANE), jnp.float32),
        compiler_params=_cparams("parallel"),
    )(aff)


def _log_sigmoid(z):
    return jnp.minimum(z, 0.0) - jnp.log(1.0 + jnp.exp(-jnp.abs(z)))


def _headnorm(o):
    mu = jnp.mean(o, -1, keepdims=True)
    oc = o - mu
    var = jnp.mean(oc * oc, -1, keepdims=True)
    return oc * lax.rsqrt(var + LN_EPS)


def _silu(x):
    return x * jax.nn.sigmoid(x)


def _chunk_cumsum(g, chunk, inclusive):
    row = lax.broadcasted_iota(jnp.int32, g.shape, 0) % chunk
    b = g
    s = 1
    while s < chunk:
        b = b + jnp.where(row >= s, pltpu.roll(b, s, axis=0), 0.0)
        s *= 2
    return b if inclusive else b - g


def _gla_kernel(flag_ref, qk_ref, v_ref, lr_ref, wup_ref, bup_ref, *rest, reverse, nblk):
    s_idx = pl.program_id(0)
    blk = nblk - 1 - s_idx if reverse else s_idx

    @pl.when(flag_ref[blk] == 1)
    def _():
        rest[-1][...] = jnp.zeros_like(rest[-1])

    for hh in range(GLA_HEADS):
        _gla_head(hh, qk_ref, v_ref, lr_ref, wup_ref, bup_ref, *rest, reverse=reverse)


def _gla_head(hh, qk_ref, v_ref, lr_ref, wup_ref, bup_ref, *rest, reverse):
    if reverse:
        gate_ref, of_ref, o_ref, st_ref = rest
    else:
        o_ref, st_ref = rest
    f32, bf16 = jnp.float32, jnp.bfloat16
    C = GLA_CHUNK
    TB = qk_ref.shape[0]
    nc = TB // C
    sl = slice(hh * LANE, (hh + 1) * LANE)

    is_q = lax.broadcasted_iota(jnp.int32, (1, LANE), 1) < GLA_DK
    qk = qk_ref[:, sl].astype(f32) * jnp.where(is_q, GLA_DK ** -0.5, 1.0)
    z = jnp.dot(lr_ref[...], wup_ref[hh], preferred_element_type=f32) + bup_ref[hh]
    g = _log_sigmoid(z) * (1.0 / GLA_TAU)
    cs = _chunk_cumsum(g, C, inclusive=not reverse)
    cs3 = cs.reshape(nc, C, LANE)
    mid = cs3[:, C // 2:C // 2 + 1, :]
    if reverse:
        tot3 = cs3[:, C - 1:C, :] + g.reshape(nc, C, LANE)[:, C - 1:C, :]
        edge3 = jnp.where(is_q, tot3 - cs3, cs3)
        mid3 = jnp.where(is_q, mid - cs3, cs3 - mid)
    else:
        tot3 = cs3[:, C - 1:C, :]
        edge3 = jnp.where(is_q, cs3, tot3 - cs3)
        mid3 = jnp.where(is_q, cs3 - mid, mid - cs3)
    x_edge = qk * jnp.exp(edge3).reshape(TB, LANE)
    x_mid = qk * jnp.exp(mid3).reshape(TB, LANE)
    qe = jnp.where(is_q, x_edge, 0.0).astype(bf16).reshape(nc, C, LANE)
    kd = jnp.where(is_q, pltpu.roll(x_edge, GLA_DK, axis=1), 0.0).astype(bf16).reshape(nc, C, LANE)
    qm = jnp.where(is_q, x_mid, 0.0).astype(bf16).reshape(nc, C, LANE)
    km = pltpu.roll(x_mid, GLA_DK, axis=1).astype(bf16).reshape(nc, C, LANE)
    etot = jnp.exp(tot3)
    v = v_ref[:, sl]
    ri = lax.broadcasted_iota(jnp.int32, (C, C), 0)
    ci = lax.broadcasted_iota(jnp.int32, (C, C), 1)
    keep = (ci > ri) if reverse else (ci <= ri)

    st = st_ref[hh]
    outs = [None] * nc
    for c in (range(nc - 1, -1, -1) if reverse else range(nc)):
        vc = v[c * C:(c + 1) * C, :]
        att = lax.dot_general(qm[c], km[c], (((1,), (1,)), ((), ())), preferred_element_type=f32)
        att = jnp.where(keep, att, 0.0).astype(bf16)
        o_c = jnp.dot(att, vc, preferred_element_type=f32)
        o_c = o_c + lax.dot_general(qe[c], st.astype(bf16), (((1,), (1,)), ((), ())), preferred_element_type=f32)
        upd = lax.dot_general(vc, kd[c], (((0,), (0,)), ((), ())), preferred_element_type=f32)
        st = st * etot[c] + upd
        outs[c] = o_c
    st_ref[hh] = st
    o = jnp.concatenate(outs, axis=0)
    if reverse:
        o = _headnorm(o + of_ref[:, sl])
        o_ref[:, sl] = (o * _silu(gate_ref[:, sl].astype(f32))).astype(o_ref.dtype)
    else:
        o_ref[:, sl] = o


def _gla(h, seq_first, seq_last, wup, bup):
    n = h.shape[0]
    TB = SEQ_BLOCK
    nblk = n // TB
    H = GLA_HEADS

    def col(off):
        return off // LANE

    HW = H * LANE

    def specs(reverse):
        blk = (lambda s: nblk - 1 - s) if reverse else (lambda s: s)
        sp = [pl.BlockSpec((TB, HW), lambda s, f: (blk(s), QK_OFF // HW)),
              pl.BlockSpec((TB, HW), lambda s, f: (blk(s), AV_OFF // HW)),
              pl.BlockSpec((TB, LANE), lambda s, f: (blk(s), col(LR_OFF))),
              pl.BlockSpec((None, H, LANE, LANE), lambda s, f: (1 if reverse else 0, 0, 0, 0)),
              pl.BlockSpec((None, H, 1, LANE), lambda s, f: (1 if reverse else 0, 0, 0, 0))]
        out = pl.BlockSpec((TB, HW), lambda s, f: (blk(s), 0))
        return sp, out, blk

    scratch = [pltpu.VMEM((H, GLA_DV, LANE), jnp.float32)]
    sp, out, _ = specs(False)
    o_f = pl.pallas_call(
        functools.partial(_gla_kernel, reverse=False, nblk=nblk), name="gla_fwd",
        grid_spec=pltpu.PrefetchScalarGridSpec(num_scalar_prefetch=1, grid=(nblk,), in_specs=sp, out_specs=out,
                                               scratch_shapes=scratch),
        out_shape=jax.ShapeDtypeStruct((n, H * GLA_DV), jnp.float32),
        compiler_params=_cparams("arbitrary"),
    )(seq_first, h, h, h, wup, bup)
    sp, out, blk = specs(True)
    sp = sp + [pl.BlockSpec((TB, HW), lambda s, f: (blk(s), AG_OFF // HW)), out]
    return pl.pallas_call(
        functools.partial(_gla_kernel, reverse=True, nblk=nblk), name="gla_bwd",
        grid_spec=pltpu.PrefetchScalarGridSpec(num_scalar_prefetch=1, grid=(nblk,), in_specs=sp, out_specs=out,
                                               scratch_shapes=scratch),
        out_shape=jax.ShapeDtypeStruct((n, H * GLA_DV), jnp.bfloat16),
        compiler_params=_cparams("arbitrary"),
    )(seq_last, h, h, h, wup, bup, h, o_f)


def _pack_gla_up(w_up, b_up):
    H, dk, r = GLA_HEADS, GLA_DK, GLA_RANK
    assert 2 * dk == LANE
    w = w_up.reshape(2, r, H, dk).transpose(0, 2, 1, 3)
    wp = jnp.zeros((2, H, LANE, dk), jnp.float32)
    wp = wp.at[0, :, 0:r].set(w[0]).at[1, :, r:2 * r].set(w[1])
    bp = b_up.reshape(2, H, 1, dk).astype(jnp.float32)
    return jnp.concatenate([wp, wp], axis=-1).astype(jnp.bfloat16), jnp.concatenate([bp, bp], axis=-1)


def _rope_tables(pos, dim, theta, width):
    half = dim // 2
    T = pos.shape[0]
    inv = 1.0 / (theta ** (jnp.arange(half, dtype=jnp.float32) / half))
    ang = pos.astype(jnp.float32)[:, None] * inv[None, :]
    c, s = jnp.cos(ang), jnp.sin(ang)
    rest = width - dim
    cos = jnp.concatenate([c, c, jnp.ones((T, rest), jnp.float32)], axis=1)
    sin_lo = jnp.concatenate([-s, jnp.zeros((T, half + rest), jnp.float32)], axis=1)
    sin_hi = jnp.concatenate([jnp.zeros((T, half), jnp.float32), s, jnp.zeros((T, rest), jnp.float32)], axis=1)
    return cos, sin_lo, sin_hi


def _ret_kernel(flag_ref, pos_ref, q_ref, k_ref, v_ref, cos_ref, sin_ref, dec_ref, *rest, reverse, nblk):
    s_idx = pl.program_id(0)
    blk = nblk - 1 - s_idx if reverse else s_idx

    @pl.when(flag_ref[blk] == 1)
    def _():
        st_ref = rest[-1] if reverse else rest[-2]
        st_ref[...] = jnp.zeros_like(st_ref)

    for hh in range(RET_HEADS):
        _ret_head(hh, s_idx, q_ref, k_ref, v_ref, cos_ref, sin_ref, dec_ref, *rest, reverse=reverse)


def _ret_head(hh, s_idx, q_ref, k_ref, v_ref, cos_ref, sin_ref, dec_ref, *rest, reverse):
    if reverse:
        gate_ref, o1_ref, o_ref, st_ref = rest
    else:
        o_ref, st_ref, dm_ref = rest
    f32, bf16 = jnp.float32, jnp.bfloat16
    TB = q_ref.shape[0]
    sl = slice(hh * RET_DH, (hh + 1) * RET_DH)

    dec = dec_ref[hh]
    lg = -(jnp.maximum(dec, 0.0) + jnp.log1p(jnp.exp(-jnp.abs(dec))))
    lg_f, lg_b = lg[0, 0:1, 0:1], lg[1, 0:1, 0:1]

    cos, sin = cos_ref[...], sin_ref[...]

    def rot(x):
        return x * cos + pltpu.roll(x, RET_DH // 2, axis=1) * sin

    qr = rot(q_ref[:, sl].astype(f32))
    kr = rot(k_ref[:, sl].astype(f32)) * (RET_DH ** -0.5)
    v = v_ref[:, sl]
    idx = lax.broadcasted_iota(jnp.int32, (TB, 1), 0).astype(f32)
    st = st_ref[hh]
    cross = jnp.dot(qr.astype(bf16), st.astype(bf16), preferred_element_type=f32)
    if reverse:
        cross = cross * jnp.exp(lg_b * (TB - idx))
        kz = (kr * jnp.exp(lg_b * idx)).astype(bf16)
        st_ref[hh] = st * jnp.exp(lg_b * TB) + lax.dot_general(kz, v, (((0,), (0,)), ((), ())),
                                                               preferred_element_type=f32)
        o = _headnorm(o1_ref[:, sl] + cross)
        o_ref[:, sl] = (o * _silu(gate_ref[:, sl].astype(f32))).astype(o_ref.dtype)
    else:
        @pl.when(s_idx == 0)
        def _():
            ri = lax.broadcasted_iota(jnp.int32, (TB, TB), 0)
            ci = lax.broadcasted_iota(jnp.int32, (TB, TB), 1)
            d = (ri - ci).astype(f32)
            dm_ref[hh] = jnp.where(d >= 0, jnp.exp(lg_f * jnp.maximum(d, 0.0)), jnp.exp(lg_b * jnp.maximum(-d, 0.0)))

        cross = cross * jnp.exp(lg_f * (idx + 1.0))
        kz = (kr * jnp.exp(lg_f * (TB - 1.0 - idx))).astype(bf16)
        st_ref[hh] = st * jnp.exp(lg_f * TB) + lax.dot_general(kz, v, (((0,), (0,)), ((), ())),
                                                               preferred_element_type=f32)
        sc = lax.dot_general(qr.astype(bf16), kr.astype(bf16), (((1,), (1,)), ((), ())), preferred_element_type=f32)
        sc = (sc * dm_ref[hh]).astype(bf16)
        o_ref[:, sl] = jnp.dot(sc, v, preferred_element_type=f32) + cross


def _retention(h, seq_first, seq_last, pos_blk, decay, t_max):
    n = h.shape[0]
    TB = SEQ_BLOCK
    nblk = n // TB
    H, dh = RET_HEADS, RET_DH
    cos, sin_lo, sin_hi = _rope_tables(jnp.arange(t_max), dh, RET_THETA, dh)
    sin = sin_lo + sin_hi
    dec = jnp.broadcast_to(decay.astype(jnp.float32).T[:, :, None, None], (H, 2, 8, LANE))

    HW = H * dh

    def specs(reverse):
        blk = (lambda s: nblk - 1 - s) if reverse else (lambda s: s)
        tok = lambda off: pl.BlockSpec((TB, HW), lambda s, f, p: (blk(s), off // HW))
        tab = pl.BlockSpec((TB, LANE), lambda s, f, p: (p[blk(s)], 0))
        sp = [tok(DQ_OFF), tok(DK_OFF), tok(DV_OFF), tab, tab,
              pl.BlockSpec((H, 2, 8, LANE), lambda s, f, p: (0, 0, 0, 0))]
        return sp, tok, pl.BlockSpec((TB, HW), lambda s, f, p: (blk(s), 0))

    sp, _, out = specs(False)
    o1 = pl.pallas_call(
        functools.partial(_ret_kernel, reverse=False, nblk=nblk), name="ret_fwd",
        grid_spec=pltpu.PrefetchScalarGridSpec(
            num_scalar_prefetch=2, grid=(nblk,), in_specs=sp, out_specs=out,
            scratch_shapes=[pltpu.VMEM((H, dh, dh), jnp.float32), pltpu.VMEM((H, TB, TB), jnp.float32)]),
        out_shape=jax.ShapeDtypeStruct((n, H * dh), jnp.float32),
        compiler_params=_cparams("arbitrary"),
    )(seq_first, pos_blk, h, h, h, cos, sin, dec)
    sp, tok, out = specs(True)
    sp = sp + [tok(DG_OFF), out]
    return pl.pallas_call(
        functools.partial(_ret_kernel, reverse=True, nblk=nblk), name="ret_bwd",
        grid_spec=pltpu.PrefetchScalarGridSpec(
            num_scalar_prefetch=2, grid=(nblk,), in_specs=sp, out_specs=out,
            scratch_shapes=[pltpu.VMEM((H, dh, dh), jnp.float32)]),
        out_shape=jax.ShapeDtypeStruct((n, H * dh), jnp.bfloat16),
        compiler_params=_cparams("arbitrary"),
    )(seq_last, pos_blk, h, h, h, cos, sin, dec, h, o1)


SWA_Q = 128
SWA_SUBS = 4
SWA_HALF = 64
assert all(w // (2 * d) == SWA_HALF for w, d in SWA_GROUPS)
NEG_BIG = -1e30


def _swa_kernel(join_ref, q_ref, kl_ref, kc_ref, kr_ref, vl_ref, vc_ref, vr_ref,
                qcos_ref, qs1_ref, qs2_ref, kcos_ref, ks1_ref, ks2_ref, o_ref, lse_ref):
    f32, bf16 = jnp.float32, jnp.bfloat16
    j = pl.program_id(1)
    Q, HW = SWA_Q, SWA_HALF
    W = Q + 2 * HW
    ri = lax.broadcasted_iota(jnp.int32, (Q, W), 0)
    ci = lax.broadcasted_iota(jnp.int32, (Q, W), 1)
    rel = ci - HW - ri
    band = (rel <= HW) & (rel >= -HW)
    joins = [join_ref[j * (SWA_SUBS + 1) + u] == 1 for u in range(SWA_SUBS + 1)]
    oks = [band & ((ci >= HW) | joins[u]) & ((ci < HW + Q) | joins[u + 1]) for u in range(SWA_SUBS)]
    lane = lax.broadcasted_iota(jnp.int32, (Q, LANE), 1)

    def rot(x, c_ref, s1_ref, s2_ref):
        return (x * c_ref[...] + pltpu.roll(x, LANE - ROPE_DIM // 2, axis=1) * s1_ref[...]
                + pltpu.roll(x, ROPE_DIM // 2, axis=1) * s2_ref[...])

    lse_tiles = [jnp.zeros((Q, LANE), f32) for _ in range(SWA_SUBS)]
    for hh in range(SWA_HEADS):
        sl = slice(hh * SWA_DH, (hh + 1) * SWA_DH)
        q = (rot(q_ref[:, sl].astype(f32), qcos_ref, qs1_ref, qs2_ref) * (SWA_DH ** -0.5)).astype(bf16)
        k = jnp.concatenate([kl_ref[:, sl], kc_ref[:, sl], kr_ref[:, sl]], axis=0).astype(f32)
        k = rot(k, kcos_ref, ks1_ref, ks2_ref).astype(bf16)
        v = jnp.concatenate([vl_ref[:, sl], vc_ref[:, sl], vr_ref[:, sl]], axis=0)
        for u in range(SWA_SUBS):
            s = lax.dot_general(q[u * Q:(u + 1) * Q], k[u * Q:u * Q + W], (((1,), (1,)), ((), ())),
                                preferred_element_type=f32)
            s = jnp.where(oks[u], s, NEG_BIG)
            m = jnp.max(s, -1, keepdims=True)
            p = jnp.exp(s - m)
            den = jnp.sum(p, -1, keepdims=True)
            o = jnp.dot(p.astype(bf16), v[u * Q:u * Q + W], preferred_element_type=f32) / den
            o_ref[u * Q:(u + 1) * Q, sl] = o
            lse_tiles[u] = jnp.where(lane == hh, m + jnp.log(den), lse_tiles[u])
    for u in range(SWA_SUBS):
        lse_ref[u * Q:(u + 1) * Q, :] = lse_tiles[u]


def _swa_merge_kernel(*refs):
    ng = len(SWA_GROUPS)
    o_refs, l_refs, y_ref = refs[:ng], refs[ng:2 * ng], refs[2 * ng]
    scratch = refs[2 * ng + 1:]
    assert SWA_DH == LANE
    os_, ls_ = [], []
    si = 0
    for gi, (_, dil) in enumerate(SWA_GROUPS):
        if dil == 1:
            os_.append(lambda hh, r=o_refs[gi]: r[0, :, hh * LANE:(hh + 1) * LANE])
            ls_.append(l_refs[gi].at[0])
            continue
        o_s, l_s = scratch[si], scratch[si + 1]
        si += 2
        rows = l_s.shape[0] // dil
        for r in range(dil):
            for hh in range(SWA_HEADS):
                o_s[hh, pl.ds(r, rows, stride=dil), :] = o_refs[gi][r, :, hh * LANE:(hh + 1) * LANE]
            l_s[pl.ds(r, rows, stride=dil), :] = l_refs[gi][r]
        os_.append(lambda hh, s=o_s: s[hh])
        ls_.append(l_s)
    for hh in range(SWA_HEADS):
        ls = [r[:, hh:hh + 1] for r in ls_]
        m = functools.reduce(jnp.maximum, ls)
        ws = [jnp.exp(l - m) for l in ls]
        tot = functools.reduce(lambda a, b: a + b, ws)
        y = functools.reduce(lambda a, b: a + b, [w * o(hh) for w, o in zip(ws, os_)])
        y_ref[:, hh * LANE:(hh + 1) * LANE] = (y / tot).astype(y_ref.dtype)


def _proj_strided_kernel(a_ref, b_ref, o_ref, acc_ref, *, dil):
    acc = jnp.dot(a_ref[...], b_ref[...], preferred_element_type=jnp.float32)
    if dil == 1:
        o_ref[0] = acc.astype(o_ref.dtype)
        return
    rows = acc_ref.shape[1] // dil
    for s in range(acc_ref.shape[0]):
        acc_ref[s] = acc[:, s * LANE:(s + 1) * LANE]
        for r in range(dil):
            o_ref[r, :, s * LANE:(s + 1) * LANE] = acc_ref[s, pl.ds(r, rows, stride=dil), :].astype(o_ref.dtype)


def _proj_strided(a, b, dil, bm=1024, bn=512):
    n, K = a.shape
    N = b.shape[1]
    return pl.pallas_call(
        functools.partial(_proj_strided_kernel, dil=dil), name=f"swa_proj_dil{dil}", grid=(n // bm, N // bn),
        in_specs=[pl.BlockSpec((bm, K), lambda i, j: (i, 0)), pl.BlockSpec((K, bn), lambda i, j: (0, j))],
        out_specs=pl.BlockSpec((dil, bm // dil, bn), lambda i, j: (0, i, j)),
        out_shape=jax.ShapeDtypeStruct((dil, n // dil, N), jnp.bfloat16),
        scratch_shapes=[pltpu.VMEM((bn // LANE, bm, LANE), jnp.float32)],
        compiler_params=_cparams("parallel", "arbitrary"),
    )(a, b)


def _swa(xb, w_groups, seqs):
    n = xb.shape[0]
    Q, HW = SWA_Q, SWA_HALF
    GW = SWA_HEADS * SWA_DH
    outs, lses = [], []
    for gi, (window, dil) in enumerate(SWA_GROUPS):
        hs = _proj_strided(xb, w_groups[gi], dil)
        rows = n // dil
        QB = SWA_SUBS * Q
        nblk = rows // QB
        has_l, has_r = [], []
        for num, T in seqs:
            per = T // dil // Q
            assert per * Q * dil == T
            has_l += ([0] + [1] * (per - 1)) * num
            has_r += ([1] * (per - 1) + [0]) * num
        assert len(has_l) == nblk * SWA_SUBS
        joins = np.array(has_l, np.int32).reshape(nblk, SWA_SUBS)
        joins = np.concatenate([joins, np.array(has_r, np.int32).reshape(nblk, SWA_SUBS)[:, -1:]], axis=1)
        joins = jnp.asarray(joins.reshape(-1))
        qpos = jnp.arange(QB) * dil
        kpos = (jnp.arange(QB + 2 * HW) - HW) * dil
        qt = _rope_tables(qpos, ROPE_DIM, ROPE_THETA, SWA_DH)
        kt = _rope_tables(kpos, ROPE_DIM, ROPE_THETA, SWA_DH)
        per64 = QB // HW
        last64 = rows // HW - 1

        def center(c):
            return pl.BlockSpec((None, QB, GW), lambda r, j, *_: (r, j, c))

        def left(c):
            return pl.BlockSpec((None, HW, GW), lambda r, j, *_: (r, jnp.maximum(per64 * j - 1, 0), c))

        def right(c):
            return pl.BlockSpec((None, HW, GW), lambda r, j, *_: (r, jnp.minimum(per64 * (j + 1), last64), c))

        qtab = pl.BlockSpec((QB, SWA_DH), lambda r, j, *_: (0, 0))
        ktab = pl.BlockSpec((QB + 2 * HW, SWA_DH), lambda r, j, *_: (0, 0))
        o_g, lse_g = pl.pallas_call(
            _swa_kernel, name=f"swa_dil{dil}",
            grid_spec=pltpu.PrefetchScalarGridSpec(
                num_scalar_prefetch=1, grid=(dil, nblk),
                in_specs=[center(0), left(1), center(1), right(1), left(2), center(2), right(2),
                          qtab, qtab, qtab, ktab, ktab, ktab],
                out_specs=[pl.BlockSpec((None, QB, GW), lambda r, j, *_: (r, j, 0)),
                           pl.BlockSpec((None, QB, LANE), lambda r, j, *_: (r, j, 0))]),
            out_shape=[jax.ShapeDtypeStruct((dil, rows, GW), jnp.float32),
                       jax.ShapeDtypeStruct((dil, rows, LANE), jnp.float32)],
            compiler_params=_cparams("parallel", "parallel"),
        )(joins, hs, hs, hs, hs, hs, hs, hs, *qt, *kt)
        outs.append(o_g)
        lses.append(lse_g)
    bm = 1024
    dils = [d for _, d in SWA_GROUPS]
    osp = [pl.BlockSpec((d, bm // d, GW), lambda i: (0, i, 0)) for d in dils]
    lsp = [pl.BlockSpec((d, bm // d, LANE), lambda i: (0, i, 0)) for d in dils]
    scratch = []
    for d in dils:
        if d > 1:
            scratch += [pltpu.VMEM((SWA_HEADS, bm, LANE), jnp.float32), pltpu.VMEM((bm, LANE), jnp.float32)]
    return pl.pallas_call(
        _swa_merge_kernel, name="swa_merge", grid=(n // bm,), in_specs=osp + lsp,
        out_specs=pl.BlockSpec((bm, GW), lambda i: (i, 0)),
        out_shape=jax.ShapeDtypeStruct((n, GW), jnp.bfloat16), scratch_shapes=scratch,
        compiler_params=_cparams("parallel"),
    )(*outs, *lses)


FFT_T2 = 128
FW = FNET_GROUPS * FNET_DG


def _dft_cos_sin(n_out, n_in, period):
    ang = (2.0 * np.pi / period) * np.mod(np.outer(np.arange(n_out), np.arange(n_in)), period)
    return np.cos(ang), np.sin(ang)


def _fnet_stage1_kernel(z_ref, m_ref, a_ref, *, cols):
    mat = m_ref[...]
    for c in range(cols):
        zr = z_ref[:, c * 2 * FW:c * 2 * FW + FW]
        zi = z_ref[:, c * 2 * FW + FW:(c + 1) * 2 * FW]
        a = jnp.dot(mat, jnp.concatenate([zr, zi], axis=0), preferred_element_type=jnp.float32)
        t1 = zr.shape[0]
        a_ref[:, c * 2 * FW:c * 2 * FW + FW] = a[:t1].astype(a_ref.dtype)
        a_ref[:, c * 2 * FW + FW:(c + 1) * 2 * FW] = a[t1:].astype(a_ref.dtype)


FNET_KB = 8


def _fnet_stage2_kernel(a_ref, g_ref, y_ref, *, scale):
    kw = g_ref.shape[1] // FNET_KB
    for j in range(FNET_KB):
        y = jnp.dot(g_ref[:, j * kw:(j + 1) * kw], a_ref[j], preferred_element_type=jnp.float32)
        y_ref[:, j * FW:(j + 1) * FW] = (y * scale).astype(y_ref.dtype)


def _fnet(h, seqs):
    n = h.shape[0]
    bf16 = jnp.bfloat16
    T2 = FFT_T2
    cc, ss = _dft_cos_sin(FNET_DG, FNET_DG, FNET_DG)
    eye = np.eye(FNET_GROUPS)
    wc = jnp.asarray(np.concatenate([np.kron(eye, cc), -np.kron(eye, ss)], axis=1), bf16)
    z = _matmul(h, wc, bf16, bn=2 * FW, a_col_blk=C_OFF // FW, name="fnet_channel_dft")
    zz = z.reshape(n // T2, T2 * 2 * FW)
    outs = []
    row0 = 0
    for num, T in seqs:
        T1 = T // T2
        assert T1 * T2 == T and T1 % 16 == 0
        c1, s1 = _dft_cos_sin(T1, T1, T1)
        mat = jnp.asarray(np.block([[c1, s1], [-s1, c1]]), bf16)
        cols = min(T2, max(1, 1024 // T1))
        rb0 = row0 // T1
        a = pl.pallas_call(
            functools.partial(_fnet_stage1_kernel, cols=cols), name=f"fnet_dft_t1_{T1}", grid=(num, T2 // cols),
            in_specs=[pl.BlockSpec((T1, cols * 2 * FW), lambda b, j: (rb0 + b, j)),
                      pl.BlockSpec((2 * T1, 2 * T1), lambda b, j: (0, 0))],
            out_specs=pl.BlockSpec((T1, cols * 2 * FW), lambda b, j: (b, j)),
            out_shape=jax.ShapeDtypeStruct((num * T1, T2 * 2 * FW), bf16),
            compiler_params=_cparams("parallel", "parallel"),
        )(zz, mat)
        ec, es = _dft_cos_sin(T, T2, T)
        g = jnp.asarray(np.stack([ec, es], axis=-1).reshape(T2, T1 * 2 * T2), bf16)
        y = pl.pallas_call(
            functools.partial(_fnet_stage2_kernel, scale=float(1.0 / np.sqrt(T * FNET_DG))), name=f"fnet_dft_t2_{T1}",
            grid=(num, T1 // FNET_KB),
            in_specs=[pl.BlockSpec((FNET_KB, 2 * T2, FW), lambda b, k1: (b * (T1 // FNET_KB) + k1, 0, 0)),
                      pl.BlockSpec((T2, FNET_KB * 2 * T2), lambda b, k1: (0, k1))],
            out_specs=pl.BlockSpec((T2, FNET_KB * FW), lambda b, k1: (b, k1)),
            out_shape=jax.ShapeDtypeStruct((num * T2, T1 * FW), bf16),
            compiler_params=_cparams("parallel", "parallel"),
        )(a.reshape(num * T1, 2 * T2, FW), g)
        outs.append(y.reshape(num * T, FW))
        row0 += num * T1
    return jnp.concatenate(outs, axis=0)


def _ffn_up_kernel(x_ref, w1_ref, w3_ref, h_ref, w1b_ref, w3b_ref):
    @pl.when(pl.program_id(2) == 0)
    def _():
        w1b_ref[...] = w1_ref[...].astype(w1b_ref.dtype)
        w3b_ref[...] = w3_ref[...].astype(w3b_ref.dtype)

    x = x_ref[...]
    h1 = jnp.dot(x, w1b_ref[...], preferred_element_type=jnp.float32)
    h3 = jnp.dot(x, w3b_ref[...], preferred_element_type=jnp.float32)
    h_ref[...] = (_silu(h1) * h3).astype(h_ref.dtype)


def _ffn_down_kernel(h_ref, w2_ref, s_ref, o_ref, w2b_ref):
    e = pl.program_id(0)

    @pl.when(pl.program_id(1) == 0)
    def _():
        w2b_ref[...] = w2_ref[...].astype(w2b_ref.dtype)

    D = w2_ref.shape[1]
    side = s_ref[...]
    lane = lax.broadcasted_iota(jnp.int32, side.shape, 1)
    mine = (lane == e) | (lane == e + SIDE_MID) | (lane == e + SIDE_LOW)
    gate = jnp.sum(jnp.where(mine, side.astype(jnp.float32), 0.0), axis=1, keepdims=True)
    y = jnp.dot(h_ref[...], w2b_ref[...], preferred_element_type=jnp.float32)
    o_ref[:, :D] = (y * gate).astype(o_ref.dtype)
    o_ref[:, D:] = side


def _expert_ffn(xe, w1, w3, w2, layer, bm=1024, bf=512):
    E, R, DY = xe.shape
    D = DY - LANE
    F = w1.shape[-1]
    wspec = pl.BlockSpec((None, None, D, bf), lambda e, f, i: (layer, e, 0, f))
    hmid = pl.pallas_call(
        _ffn_up_kernel, name="expert_ffn_up", grid=(E, F // bf, R // bm),
        in_specs=[pl.BlockSpec((None, bm, D), lambda e, f, i: (e, i, 0)), wspec, wspec],
        out_specs=pl.BlockSpec((None, bm, bf), lambda e, f, i: (e, i, f)),
        out_shape=jax.ShapeDtypeStruct((E, R, F), jnp.bfloat16),
        scratch_shapes=[pltpu.VMEM((D, bf), jnp.bfloat16), pltpu.VMEM((D, bf), jnp.bfloat16)],
        compiler_params=_cparams("arbitrary", "arbitrary", "arbitrary"),
    )(xe, w1, w3)
    return pl.pallas_call(
        _ffn_down_kernel, name="expert_ffn_down", grid=(E, R // bm),
        in_specs=[pl.BlockSpec((None, bm, F), lambda e, i: (e, i, 0)),
                  pl.BlockSpec((None, None, F, D), lambda e, i: (layer, e, 0, 0), pipeline_mode=pl.Buffered(1)),
                  pl.BlockSpec((None, bm, LANE), lambda e, i: (e, i, D // LANE))],
        out_specs=pl.BlockSpec((None, bm, D + LANE), lambda e, i: (e, i, 0)),
        out_shape=jax.ShapeDtypeStruct((E, R, D + LANE), jnp.bfloat16),
        scratch_shapes=[pltpu.VMEM((F, D), jnp.bfloat16)],
        compiler_params=_cparams("arbitrary", "arbitrary"),
    )(hmid, w2, xe)


COMB_BM = 128
COMB_G = 16
COMB_K = 256


def _combine_kernel(lo_ref, x_ref, g_ref, b_ref, ye_hbm, *rest, nblk_all, blk_off, want_bf16):
    if want_bf16:
        xf_ref, xb_ref, buf_ref, acc_ref, sem = rest
    else:
        xf_ref, buf_ref, acc_ref, sem = rest
    f32, bf16 = jnp.float32, jnp.bfloat16
    E = ye_hbm.shape[0]
    D = x_ref.shape[1]
    BM, G, KC = COMB_BM, COMB_G, COMB_K
    i = pl.program_id(0)
    nsteps = pl.num_programs(0)
    blk = i + blk_off
    slot = i % 2

    @pl.when(i == 0)
    def _():
        buf_ref[...] = jnp.zeros_like(buf_ref)

    def granule_copy(sl, e, src_row, dst_row):
        return pltpu.make_async_copy(ye_hbm.at[e, pl.ds(src_row, G)], buf_ref.at[sl, pl.ds(dst_row, G)], sem.at[sl])

    def gather_block(b, sl, start):
        off = jnp.int32(0)
        for e in range(E):
            lo = lo_ref[e * (nblk_all + 1) + b]
            hi = lo_ref[e * (nblk_all + 1) + b + 1]
            lo_al = (lo // G) * G
            ng = jnp.where(hi > lo, (hi - lo_al + G - 1) // G, 0)
            if start:
                def issue(g, carry, e=e, lo_al=lo_al, off=off):
                    granule_copy(sl, e, pl.multiple_of(lo_al + g * G, G), pl.multiple_of(off + g * G, G)).start()
                    return carry

                lax.fori_loop(0, ng, issue, 0)
            off = off + ng * G
        return off

    @pl.when(i == 0)
    def _():
        gather_block(blk, slot, True)

    @pl.when(i + 1 < nsteps)
    def _():
        gather_block(blk + 1, 1 - slot, True)

    off = gather_block(blk, slot, False)

    def wait_one(g, carry):
        granule_copy(slot, 0, 0, 0).wait()
        return carry

    lax.fori_loop(0, off // G, wait_one, 0)

    acc_ref[...] = jnp.zeros_like(acc_ref)
    t0 = (blk * BM).astype(f32)
    lane_tok = lax.broadcasted_iota(jnp.int32, (KC, BM), 1).astype(f32)
    row = lax.broadcasted_iota(jnp.int32, (KC, 1), 0)

    def chunk(c, carry):
        win = buf_ref[slot, pl.ds(pl.multiple_of(c * KC, KC), KC), :]
        tok = (win[:, D + SIDE_TOK:D + SIDE_TOK + 1].astype(f32) * TOK_RADIX
               + win[:, D + SIDE_TOK + 1:D + SIDE_TOK + 2].astype(f32))
        tok = jnp.where(row + c * KC < off, tok, -1.0)
        onehot_t = (tok - t0 == lane_tok).astype(bf16)
        acc_ref[...] += lax.dot_general(onehot_t, win[:, :D], (((0,), (0,)), ((), ())), preferred_element_type=f32)
        return carry

    lax.fori_loop(0, (off + KC - 1) // KC, chunk, 0)
    y = _ln_rows(DN_ALPHA * x_ref[...] + acc_ref[...], g_ref[...], b_ref[...])
    xf_ref[...] = y
    if want_bf16:
        xb_ref[...] = y.astype(bf16)


def _combine_ln(x, ye, lo, g, b, row_off, rows, want_bf16):
    n, D = x.shape
    E, R, DY = ye.shape
    BM = COMB_BM
    nblk_all = n // BM
    ob = row_off // BM
    kmax = -(-E * (BM + COMB_G) // COMB_K) * COMB_K
    irow = pl.BlockSpec((BM, D), lambda i, lo: (i + ob, 0))
    orow = pl.BlockSpec((BM, D), lambda i, lo: (i, 0))
    vec = pl.BlockSpec((1, D), lambda i, lo: (0, 0))
    out_shape = [jax.ShapeDtypeStruct((rows, D), jnp.float32)]
    out_specs = [orow]
    if want_bf16:
        out_shape.append(jax.ShapeDtypeStruct((rows, D), jnp.bfloat16))
        out_specs.append(orow)
    return pl.pallas_call(
        functools.partial(_combine_kernel, nblk_all=nblk_all, blk_off=ob, want_bf16=want_bf16), name="moe_combine_ln",
        grid_spec=pltpu.PrefetchScalarGridSpec(
            num_scalar_prefetch=1, grid=(rows // BM,),
            in_specs=[irow, vec, vec, pl.BlockSpec(memory_space=pl.ANY)], out_specs=out_specs,
            scratch_shapes=[pltpu.VMEM((2, kmax, DY), jnp.bfloat16), pltpu.VMEM((BM, D), jnp.float32),
                            pltpu.SemaphoreType.DMA((2,))]),
        out_shape=out_shape,
        compiler_params=_cparams("arbitrary"),
    )(lo, x, g.reshape(1, D), b.reshape(1, D), ye)


def _route_kernel(all_ref, a_ref, idx_ref, cprev_ref, thr_ref, need_ref, *, cap, tok_off):
    f32, bf16, i32 = jnp.float32, jnp.bfloat16, jnp.int32
    e = pl.program_id(0)

    def total(x):
        return jnp.sum(jnp.sum(x, axis=2, keepdims=True), axis=1, keepdims=True)

    @pl.when(e == 0)
    def _():
        bits_all = pltpu.bitcast(all_ref[...], i32)

        def refine(i, prefix):
            cand = prefix | jnp.left_shift(jnp.int32(1), 30 - i)
            return jnp.where(total((bits_all >= cand).astype(i32)) >= cap, cand, prefix)

        thr_all = lax.fori_loop(0, 31, refine, jnp.zeros((all_ref.shape[0], 1, 1), i32))
        thr_ref[...] = jnp.broadcast_to(thr_all, thr_ref.shape)
        need_ref[...] = jnp.broadcast_to(cap - total((bits_all > thr_all).astype(i32)), need_ref.shape)

    a = a_ref[...]
    R = a.shape[0]
    bits = pltpu.bitcast(a, i32)
    thr = thr_ref[e][0:1, 0:1]
    need = need_ref[e][0:1, 0:1]
    gt, eq = bits > thr, bits == thr

    li = lax.broadcasted_iota(i32, (LANE, LANE), 0)
    lj = lax.broadcasted_iota(i32, (LANE, LANE), 1)
    upto = (li <= lj).astype(bf16)
    ri = lax.broadcasted_iota(i32, (R, R), 0)
    rj = lax.broadcasted_iota(i32, (R, R), 1)
    before = (rj < ri).astype(bf16)

    def counts(mask):
        inc = jnp.dot(mask.astype(bf16), upto, preferred_element_type=f32)
        tot = jnp.broadcast_to(inc[:, LANE - 1:LANE], (R, LANE)).astype(bf16)
        return inc, jnp.dot(before, tot, preferred_element_type=f32)

    eq_inc, eq_prev = counts(eq)
    sel = gt | (eq & (eq_prev + eq_inc - 1.0 < need.astype(f32)))
    cl, cprev = counts(sel)
    cprev_ref[...] = cprev[:, :1].astype(i32)

    selb = sel.astype(bf16)
    c_upto = jnp.sum(lax.dot_general(selb, (ri <= rj).astype(bf16), (((0,), (0,)), ((), ())),
                                     preferred_element_type=f32), axis=0, keepdims=True)
    c_before = jnp.sum(lax.dot_general(selb, (ri < rj).astype(bf16), (((0,), (0,)), ((), ())),
                                       preferred_element_type=f32), axis=0, keepdims=True)
    slot = lax.broadcasted_iota(i32, (cap, 1), 0).astype(f32)
    onehot = ((c_before <= slot) & (slot < c_upto)).astype(bf16)
    lane = lax.broadcasted_iota(i32, (R, LANE), 1)
    cp = cprev.astype(i32)
    aux = jnp.where(lane == 0, cp // TOK_RADIX, jnp.where(lane == 1, cp % TOK_RADIX,
                    jnp.where(lane == 2, lax.broadcasted_iota(i32, (R, LANE), 0), 0)))
    rhs = jnp.concatenate([cl.astype(bf16), aux.astype(f32).astype(bf16)], axis=1)
    m = jnp.dot(onehot, rhs, preferred_element_type=f32)
    base = m[:, LANE:LANE + 1] * TOK_RADIX + m[:, LANE + 1:LANE + 2]
    chunk = m[:, LANE + 2:LANE + 3]
    pos = jnp.sum((m[:, :LANE] <= slot - base).astype(f32), axis=1, keepdims=True)
    idx_ref[...] = (chunk * LANE + pos).astype(i32) + tok_off


def _route(aff_t, cap, tok_off):
    E, R, _ = aff_t.shape
    return pl.pallas_call(
        functools.partial(_route_kernel, cap=cap, tok_off=tok_off), name="moe_route", grid=(E,),
        in_specs=[pl.BlockSpec((E, R, LANE), lambda e: (0, 0, 0)), pl.BlockSpec((None, R, LANE), lambda e: (e, 0, 0))],
        out_specs=[pl.BlockSpec((None, cap, 1), lambda e: (e, 0, 0)), pl.BlockSpec((None, R, 1), lambda e: (e, 0, 0))],
        out_shape=[jax.ShapeDtypeStruct((E, cap, 1), jnp.int32), jax.ShapeDtypeStruct((E, R, 1), jnp.int32)],
        scratch_shapes=[pltpu.VMEM((E, 8, LANE), jnp.int32), pltpu.VMEM((E, 8, LANE), jnp.int32)],
        compiler_params=_cparams("arbitrary"),
    )(aff_t, aff_t)


def _expert_choice(xs, aff, seqs, w1, w3, w2, layer):
    assert COMB_BM == LANE
    aff_t = _aff_chunks(aff)
    idxs, los = [], []
    start = slot0 = 0
    for num, T in seqs:
        m = num * T
        cap = max(1, CAPACITY_FACTOR * m // N_EXPERTS)
        idx, cprev = _route(aff_t[:, start // LANE:(start + m) // LANE], cap, start)
        idxs.append(idx[..., 0])
        los.append(cprev[..., 0] + slot0)
        start += m
        slot0 += cap
    lo = jnp.concatenate(los + [jnp.full((N_EXPERTS, 1), slot0, jnp.int32)], axis=1).reshape(-1)
    ye = _expert_ffn(xs[jnp.concatenate(idxs, axis=1)], w1, w3, w2, layer)
    return ye, lo


def _pack_w_in(w_in):
    sizes = (GLA_HEADS * GLA_DK, GLA_HEADS * GLA_DK, GLA_HEADS * GLA_DV, GLA_HEADS * GLA_DV, 2 * GLA_RANK,
             SWA_NH * SWA_DH, SWA_NH * SWA_DH, SWA_NH * SWA_DH, FNET_GROUPS * FNET_DG,
             RET_HEADS * RET_DH, RET_HEADS * RET_DH, RET_HEADS * RET_DH, RET_HEADS * RET_DH)
    offs = np.concatenate([[0], np.cumsum(sizes)])
    part = lambda i: w_in[..., offs[i]:offs[i + 1]]
    a_q, a_k = part(0), part(1)
    cols = []
    for hh in range(GLA_HEADS):
        cols += [a_q[..., hh * GLA_DK:(hh + 1) * GLA_DK], a_k[..., hh * GLA_DK:(hh + 1) * GLA_DK]]
    cols += [part(2), part(3), part(4),
             jnp.zeros(w_in.shape[:2] + (C_OFF - LR_OFF - 2 * GLA_RANK,), w_in.dtype)]
    cols += [part(i) for i in range(8, 13)]
    out = jnp.concatenate(cols, axis=-1)
    assert out.shape[-1] == IN_COLS_P
    GW = SWA_HEADS * SWA_DH
    swa = [jnp.concatenate([part(i)[..., g * GW:(g + 1) * GW] for i in (5, 6, 7)], axis=-1)
           for g in range(len(SWA_GROUPS))]
    return out, swa


def _block_tables(seqs, blk):
    first, last, pos = [], [], []
    for num, T in seqs:
        per = T // blk
        assert per * blk == T
        for _ in range(num):
            first += [1] + [0] * (per - 1)
            last += [0] * (per - 1) + [1]
            pos += list(range(per))
    mk = lambda a: jnp.asarray(np.array(a, np.int32))
    return mk(first), mk(last), mk(pos)


def kernel(x_prompt, x_sample, ln_in_g, ln_in_b, w_in, gla_w_up, gla_b_up, ret_decay, w_gate, b_gate, w_branch,
           w_out, ln1_g, ln1_b, w_router, w_e1, w_e3, w_e2, ln2_g, ln2_b):
    bf16 = jnp.bfloat16
    D = D_MODEL
    seqs = (x_prompt.shape[:2], x_sample.shape[:2])
    n_p, n_s = x_prompt.shape[0] * x_prompt.shape[1], x_sample.shape[0] * x_sample.shape[1]
    n = n_p + n_s
    t_max = max(T for _, T in seqs)
    seq_first, seq_last, pos_blk = _block_tables(seqs, SEQ_BLOCK)

    w_in_p, w_swa = _pack_w_in(w_in.astype(bf16))
    wo = w_out.astype(bf16)
    wr = jnp.pad(w_router, ((0, 0), (0, 0), (0, LANE - N_EXPERTS))).astype(bf16)
    wg, wb, we1, we3, we2 = w_gate, w_branch, w_e1, w_e3, w_e2

    xf, xb = _ln_in(x_prompt.reshape(n_p, D), x_sample.reshape(n_s, D), ln_in_g, ln_in_b)
    for l in range(DEPTH):
        h = _matmul(xb, w_in_p[l], bf16, bm=1024, bn=IN_BN, name="in_proj")
        wup, bup = _pack_gla_up(gla_w_up[l], gla_b_up[l])
        ys = (_gla(h, seq_first, seq_last, wup, bup),
              _swa(xb, [w[l] for w in w_swa], seqs),
              _fnet(h, seqs),
              _retention(h, seq_first, seq_last, pos_blk, ret_decay[l], t_max))
        merged = _merge(xb, wg, b_gate[l], ys, wb, l)
        x1f, x1s, aff = _outproj(merged, wo[l], xf, ln1_g[l], ln1_b[l], wr[l])
        ye, lo = _expert_choice(x1s, aff, seqs, we1, we3, we2, l)
        if l + 1 < DEPTH:
            xf, xb = _combine_ln(x1f, ye, lo, ln2_g[l], ln2_b[l], 0, n, True)
        else:
            y_p, = _combine_ln(x1f, ye, lo, ln2_g[l], ln2_b[l], 0, n_p, False)
            y_s, = _combine_ln(x1f, ye, lo, ln2_g[l], ln2_b[l], n_p, n_s, False)
    return (y_p.reshape(x_prompt.shape), y_s.reshape(x_sample.shape))
```

```python
import functools

import jax
import jax.numpy as jnp
import numpy as np
from jax import lax
from jax.experimental import pallas as pl
from jax.experimental.pallas import tpu as pltpu

D_MODEL = 2048
DEPTH = 2
GLA_HEADS, GLA_DK, GLA_DV, GLA_RANK, GLA_TAU, GLA_CHUNK = 4, 64, 128, 16, 16.0, 64
SWA_GROUPS = ((128, 1), (512, 4), (2048, 16))
SWA_HEADS, SWA_DH = 4, 128
SWA_NH = SWA_HEADS * len(SWA_GROUPS)
SWA_BLOCK = 64
ROPE_THETA, ROPE_DIM = 500000.0, SWA_DH // 4
FNET_GROUPS, FNET_DG = 4, 128
RET_HEADS, RET_DH, RET_THETA = 4, 128, 10000.0
N_BRANCH, BRANCH_W = 4, 512
N_EXPERTS, EXPERT_FF, CAPACITY_FACTOR = 16, 2048, 2
DN_ALPHA = (2 * DEPTH) ** 0.25
LN_EPS = 1e-5

VMEM_LIMIT_BYTES = 48 * 1024 * 1024
LANE = 128

QK_OFF = 0
AV_OFF = QK_OFF + GLA_HEADS * LANE
AG_OFF = AV_OFF + GLA_HEADS * GLA_DV
LR_OFF = AG_OFF + GLA_HEADS * GLA_DV
C_OFF = LR_OFF + 4 * LANE
DQ_OFF = C_OFF + FNET_GROUPS * FNET_DG
DK_OFF = DQ_OFF + RET_HEADS * RET_DH
DV_OFF = DK_OFF + RET_HEADS * RET_DH
DG_OFF = DV_OFF + RET_HEADS * RET_DH
IN_USED = DG_OFF + RET_HEADS * RET_DH
IN_BN = 1536
IN_COLS_P = -(-IN_USED // IN_BN) * IN_BN
assert IN_COLS_P == IN_USED

SEQ_BLOCK = 512


def _cparams(*sem):
    return pltpu.CompilerParams(dimension_semantics=sem, vmem_limit_bytes=VMEM_LIMIT_BYTES)


def _mm_kernel(a_ref, b_ref, o_ref):
    o_ref[...] = jnp.dot(a_ref[...], b_ref[...], preferred_element_type=jnp.float32).astype(o_ref.dtype)


def _matmul(a, b, out_dtype, bm=1024, bn=512, a_col_blk=0, name="matmul"):
    batched = a.ndim == 3
    M = a.shape[-2]
    K, N = b.shape[-2:]
    bm = min(bm, M)
    bn = min(bn, N)
    assert M % bm == 0 and N % bn == 0, (M, N, bm, bn)
    if batched:
        E = a.shape[0]
        grid = (E, M // bm, N // bn)
        in_specs = [pl.BlockSpec((None, bm, K), lambda e, i, j: (e, i, 0)),
                    pl.BlockSpec((None, K, bn), lambda e, i, j: (e, 0, j))]
        out_specs = pl.BlockSpec((None, bm, bn), lambda e, i, j: (e, i, j))
        out_shape = jax.ShapeDtypeStruct((E, M, N), out_dtype)
        sem = ("parallel", "parallel", "arbitrary")
    else:
        grid = (M // bm, N // bn)
        in_specs = [pl.BlockSpec((bm, K), lambda i, j: (i, a_col_blk)),
                    pl.BlockSpec((K, bn), lambda i, j: (0, j))]
        out_specs = pl.BlockSpec((bm, bn), lambda i, j: (i, j))
        out_shape = jax.ShapeDtypeStruct((M, N), out_dtype)
        sem = ("parallel", "arbitrary")
    return pl.pallas_call(
        _mm_kernel, name=name, grid=grid, in_specs=in_specs, out_specs=out_specs, out_shape=out_shape,
        compiler_params=_cparams(*sem),
    )(a, b)


def _mm_w32_kernel(a_ref, w_ref, o_ref, wb_ref):
    @pl.when(pl.program_id(1) == 0)
    def _():
        wb_ref[...] = w_ref[...].astype(wb_ref.dtype)

    o_ref[...] = jnp.dot(a_ref[...], wb_ref[...], preferred_element_type=jnp.float32).astype(o_ref.dtype)


def _matmul_w32(a, w, out_dtype, bm=1024, bn=768, name="matmul_w32"):
    M, K = a.shape
    N = w.shape[1]
    assert M % bm == 0 and N % bn == 0, (M, N, bm, bn)
    return pl.pallas_call(
        _mm_w32_kernel, name=name, grid=(N // bn, M // bm),
        in_specs=[pl.BlockSpec((bm, K), lambda j, i: (i, 0)), pl.BlockSpec((K, bn), lambda j, i: (0, j))],
        out_specs=pl.BlockSpec((bm, bn), lambda j, i: (i, j)),
        out_shape=jax.ShapeDtypeStruct((M, N), out_dtype),
        scratch_shapes=[pltpu.VMEM((K, bn), jnp.bfloat16)],
        compiler_params=_cparams("arbitrary", "arbitrary"),
    )(a, w)


def _ln_rows(x, g, b):
    mu = jnp.mean(x, -1, keepdims=True)
    xc = x - mu
    var = jnp.mean(xc * xc, -1, keepdims=True)
    return xc * lax.rsqrt(var + LN_EPS) * g + b


def _ln_in_kernel(xa_ref, xc_ref, g_ref, b_ref, xf_ref, xb_ref, *, na_blocks):
    i = pl.program_id(0)

    def emit(x_ref):
        y = _ln_rows(x_ref[...], g_ref[...], b_ref[...])
        xf_ref[...] = y
        xb_ref[...] = y.astype(jnp.bfloat16)

    pl.when(i < na_blocks)(lambda: emit(xa_ref))
    pl.when(i >= na_blocks)(lambda: emit(xc_ref))


def _ln_in(xa, xc, g, b, bm=512):
    (na, D), nc = xa.shape, xc.shape[0]
    nab, ncb = na // bm, nc // bm
    row = pl.BlockSpec((bm, D), lambda i: (i, 0))
    vec = pl.BlockSpec((1, D), lambda i: (0, 0))
    return pl.pallas_call(
        functools.partial(_ln_in_kernel, na_blocks=nab), name="ln_in", grid=(nab + ncb,),
        in_specs=[pl.BlockSpec((bm, D), lambda i: (jnp.minimum(i, nab - 1), 0)),
                  pl.BlockSpec((bm, D), lambda i: (jnp.maximum(i - nab, 0), 0)), vec, vec],
        out_specs=[row, row],
        out_shape=[jax.ShapeDtypeStruct((na + nc, D), jnp.float32), jax.ShapeDtypeStruct((na + nc, D), jnp.bfloat16)],
        compiler_params=_cparams("arbitrary"))(xa, xc, g.reshape(1, D), b.reshape(1, D))


def _merge_kernel(xb_ref, wg_ref, bg_ref, ya_ref, yb_ref, yc_ref, yd_ref, wb_ref, o_ref, wgb_ref, wbb_ref):
    @pl.when(pl.program_id(1) == 0)
    def _():
        wgb_ref[...] = wg_ref[...].astype(wgb_ref.dtype)
        wbb_ref[...] = wb_ref[...].astype(wbb_ref.dtype)

    xb = xb_ref[...]
    acc = None
    for i, y_ref in enumerate((ya_ref, yb_ref, yc_ref, yd_ref)):
        gate = jax.nn.sigmoid(jnp.dot(xb, wgb_ref[i], preferred_element_type=jnp.float32) + bg_ref[i])
        term = gate * jnp.dot(y_ref[...], wbb_ref[i], preferred_element_type=jnp.float32)
        acc = term if acc is None else acc + term
    o_ref[...] = acc.astype(o_ref.dtype)


def _merge(xb, wg, bg, ys, wb, layer, bm=1024, bn=256):
    n, D = xb.shape
    W = ys[0].shape[1]
    y_spec = pl.BlockSpec((bm, W), lambda j, i: (i, 0))
    return pl.pallas_call(
        _merge_kernel, name="branch_merge", grid=(D // bn, n // bm),
        in_specs=[pl.BlockSpec((bm, D), lambda j, i: (i, 0)),
                  pl.BlockSpec((None, N_BRANCH, D, bn), lambda j, i: (layer, 0, 0, j)),
                  pl.BlockSpec((N_BRANCH, 1, bn), lambda j, i: (0, 0, j)),
                  y_spec, y_spec, y_spec, y_spec,
                  pl.BlockSpec((None, N_BRANCH, W, bn), lambda j, i: (layer, 0, 0, j))],
        out_specs=pl.BlockSpec((bm, bn), lambda j, i: (i, j)),
        out_shape=jax.ShapeDtypeStruct((n, D), jnp.bfloat16),
        scratch_shapes=[pltpu.VMEM((N_BRANCH, D, bn), jnp.bfloat16), pltpu.VMEM((N_BRANCH, W, bn), jnp.bfloat16)],
        compiler_params=_cparams("arbitrary", "arbitrary"),
    )(xb, wg, bg.reshape(N_BRANCH, 1, D), *ys, wb)


OUTPROJ_SUB = 256


TOK_RADIX = 256
SIDE_MID, SIDE_LOW, SIDE_TOK = N_EXPERTS, 2 * N_EXPERTS, 3 * N_EXPERTS


def _outproj_kernel(m_ref, w_ref, x_ref, g_ref, b_ref, wr_ref, xf_ref, xb_ref, aff_ref):
    f32, bf16 = jnp.float32, jnp.bfloat16
    D = x_ref.shape[1]
    for s in range(m_ref.shape[0] // OUTPROJ_SUB):
        rows = pl.ds(s * OUTPROJ_SUB, OUTPROJ_SUB)
        r = jnp.dot(m_ref[rows, :], w_ref[...], preferred_element_type=f32)
        y = _ln_rows(DN_ALPHA * x_ref[rows, :] + r, g_ref[...], b_ref[...])
        yb = y.astype(bf16)
        xf_ref[rows, :] = y
        xb_ref[rows, :D] = yb
        logits = jnp.dot(yb, wr_ref[...], preferred_element_type=f32)
        lane = lax.broadcasted_iota(jnp.int32, logits.shape, 1)
        logits = jnp.where(lane < N_EXPERTS, logits, -jnp.inf)
        e = jnp.exp(logits - jnp.max(logits, -1, keepdims=True))
        aff = e / jnp.sum(e, -1, keepdims=True)
        aff_ref[rows, :] = aff
        hi = aff.astype(bf16).astype(f32)
        mid = (aff - hi).astype(bf16).astype(f32)
        low = (aff - hi - mid).astype(bf16).astype(f32)
        tok = (pl.program_id(0) * m_ref.shape[0] + s * OUTPROJ_SUB
               + lax.broadcasted_iota(jnp.int32, logits.shape, 0))
        side = hi + pltpu.roll(mid, SIDE_MID, axis=1) + pltpu.roll(low, SIDE_LOW, axis=1)
        side = jnp.where(lane == SIDE_TOK, (tok // TOK_RADIX).astype(f32),
                         jnp.where(lane == SIDE_TOK + 1, (tok % TOK_RADIX).astype(f32), side))
        xb_ref[rows, D:] = side.astype(bf16)


def _outproj(merged, w_out, x, g, b, w_router_p, bm=512):
    n, D = x.shape
    row = pl.BlockSpec((bm, D), lambda i: (i, 0))
    vec = pl.BlockSpec((1, D), lambda i: (0, 0))
    return pl.pallas_call(
        _outproj_kernel, name="outproj_ln_router", grid=(n // bm,),
        in_specs=[row, pl.BlockSpec((D, D), lambda i: (0, 0)), row, vec, vec,
                  pl.BlockSpec((D, LANE), lambda i: (0, 0))],
        out_specs=[row, pl.BlockSpec((bm, D + LANE), lambda i: (i, 0)), pl.BlockSpec((bm, LANE), lambda i: (i, 0))],
        out_shape=[jax.ShapeDtypeStruct((n, D), jnp.float32), jax.ShapeDtypeStruct((n, D + LANE), jnp.bfloat16),
                   jax.ShapeDtypeStruct((n, LANE), jnp.float32)],
        compiler_params=_cparams("parallel"),
    )(merged, w_out, x, g.reshape(1, D), b.reshape(1, D), w_router_p)


def _aff_t_kernel(a_ref, o_ref):
    for s in range(a_ref.shape[0] // LANE):
        o_ref[:, s, :] = a_ref[s * LANE:(s + 1) * LANE, :].T[:N_EXPERTS, :]


def _aff_chunks(aff, bm=1024):
    n = aff.shape[0]
    return pl.pallas_call(
        _aff_t_kernel, name="aff_chunks", grid=(n // bm,),
        in_specs=[pl.BlockSpec((bm, LANE), lambda i: (i, 0))],
        out_specs=pl.BlockSpec((N_EXPERTS, bm // LANE, LANE), lambda i: (0, i, 0)),
        out_shape=jax.ShapeDtypeStruct((N_EXPERTS, n // LANE, LANE), jnp.float32),
        compiler_params=_cparams("parallel"),
    )(aff)


def _log_sigmoid(z):
    return jnp.minimum(z, 0.0) - jnp.log(1.0 + jnp.exp(-jnp.abs(z)))


def _headnorm(o):
    mu = jnp.mean(o, -1, keepdims=True)
    oc = o - mu
    var = jnp.mean(oc * oc, -1, keepdims=True)
    return oc * lax.rsqrt(var + LN_EPS)


def _silu(x):
    return x * jax.nn.sigmoid(x)


def _chunk_cumsum(g, chunk, inclusive):
    row = lax.broadcasted_iota(jnp.int32, g.shape, 0) % chunk
    b = g
    s = 1
    while s < chunk:
        b = b + jnp.where(row >= s, pltpu.roll(b, s, axis=0), 0.0)
        s *= 2
    return b if inclusive else b - g


def _gla_kernel(flag_ref, qk_ref, v_ref, lr_ref, wup_ref, bup_ref, *rest, reverse, nblk):
    s_idx = pl.program_id(0)
    blk = nblk - 1 - s_idx if reverse else s_idx

    @pl.when(flag_ref[blk] == 1)
    def _():
        rest[-1][...] = jnp.zeros_like(rest[-1])

    for hh in range(GLA_HEADS):
        _gla_head(hh, qk_ref, v_ref, lr_ref, wup_ref, bup_ref, *rest, reverse=reverse)


def _gla_head(hh, qk_ref, v_ref, lr_ref, wup_ref, bup_ref, *rest, reverse):
    if reverse:
        gate_ref, of_ref, o_ref, st_ref = rest
    else:
        o_ref, st_ref = rest
    f32, bf16 = jnp.float32, jnp.bfloat16
    C = GLA_CHUNK
    TB = qk_ref.shape[0]
    nc = TB // C
    sl = slice(hh * LANE, (hh + 1) * LANE)

    is_q = lax.broadcasted_iota(jnp.int32, (1, LANE), 1) < GLA_DK
    qk = qk_ref[:, sl].astype(f32) * jnp.where(is_q, GLA_DK ** -0.5, 1.0)
    z = jnp.dot(lr_ref[...], wup_ref[hh], preferred_element_type=f32) + bup_ref[hh]
    g = _log_sigmoid(z) * (1.0 / GLA_TAU)
    cs = _chunk_cumsum(g, C, inclusive=not reverse)
    cs3 = cs.reshape(nc, C, LANE)
    mid = cs3[:, C // 2:C // 2 + 1, :]
    if reverse:
        tot3 = cs3[:, C - 1:C, :] + g.reshape(nc, C, LANE)[:, C - 1:C, :]
        edge3 = jnp.where(is_q, tot3 - cs3, cs3)
        mid3 = jnp.where(is_q, mid - cs3, cs3 - mid)
    else:
        tot3 = cs3[:, C - 1:C, :]
        edge3 = jnp.where(is_q, cs3, tot3 - cs3)
        mid3 = jnp.where(is_q, cs3 - mid, mid - cs3)
    x_edge = qk * jnp.exp(edge3).reshape(TB, LANE)
    x_mid = qk * jnp.exp(mid3).reshape(TB, LANE)
    qe = jnp.where(is_q, x_edge, 0.0).astype(bf16).reshape(nc, C, LANE)
    kd = jnp.where(is_q, pltpu.roll(x_edge, GLA_DK, axis=1), 0.0).astype(bf16).reshape(nc, C, LANE)
    qm = jnp.where(is_q, x_mid, 0.0).astype(bf16).reshape(nc, C, LANE)
    km = pltpu.roll(x_mid, GLA_DK, axis=1).astype(bf16).reshape(nc, C, LANE)
    etot = jnp.exp(tot3)
    v = v_ref[:, sl]
    ri = lax.broadcasted_iota(jnp.int32, (C, C), 0)
    ci = lax.broadcasted_iota(jnp.int32, (C, C), 1)
    keep = (ci > ri) if reverse else (ci <= ri)

    st = st_ref[hh]
    outs = [None] * nc
    for c in (range(nc - 1, -1, -1) if reverse else range(nc)):
        vc = v[c * C:(c + 1) * C, :]
        att = lax.dot_general(qm[c], km[c], (((1,), (1,)), ((), ())), preferred_element_type=f32)
        att = jnp.where(keep, att, 0.0).astype(bf16)
        o_c = jnp.dot(att, vc, preferred_element_type=f32)
        o_c = o_c + lax.dot_general(qe[c], st.astype(bf16), (((1,), (1,)), ((), ())), preferred_element_type=f32)
        upd = lax.dot_general(vc, kd[c], (((0,), (0,)), ((), ())), preferred_element_type=f32)
        st = st * etot[c] + upd
        outs[c] = o_c
    st_ref[hh] = st
    o = jnp.concatenate(outs, axis=0)
    if reverse:
        o = _headnorm(o + of_ref[:, sl])
        o_ref[:, sl] = (o * _silu(gate_ref[:, sl].astype(f32))).astype(o_ref.dtype)
    else:
        o_ref[:, sl] = o


def _gla(h, seq_first, seq_last, wup, bup):
    n = h.shape[0]
    TB = SEQ_BLOCK
    nblk = n // TB
    H = GLA_HEADS

    def col(off):
        return off // LANE

    HW = H * LANE

    def specs(reverse):
        blk = (lambda s: nblk - 1 - s) if reverse else (lambda s: s)
        sp = [pl.BlockSpec((TB, HW), lambda s, f: (blk(s), QK_OFF // HW)),
              pl.BlockSpec((TB, HW), lambda s, f: (blk(s), AV_OFF // HW)),
              pl.BlockSpec((TB, LANE), lambda s, f: (blk(s), col(LR_OFF))),
              pl.BlockSpec((None, H, LANE, LANE), lambda s, f: (1 if reverse else 0, 0, 0, 0)),
              pl.BlockSpec((None, H, 1, LANE), lambda s, f: (1 if reverse else 0, 0, 0, 0))]
        out = pl.BlockSpec((TB, HW), lambda s, f: (blk(s), 0))
        return sp, out, blk

    scratch = [pltpu.VMEM((H, GLA_DV, LANE), jnp.float32)]
    sp, out, _ = specs(False)
    o_f = pl.pallas_call(
        functools.partial(_gla_kernel, reverse=False, nblk=nblk), name="gla_fwd",
        grid_spec=pltpu.PrefetchScalarGridSpec(num_scalar_prefetch=1, grid=(nblk,), in_specs=sp, out_specs=out,
                                               scratch_shapes=scratch),
        out_shape=jax.ShapeDtypeStruct((n, H * GLA_DV), jnp.float32),
        compiler_params=_cparams("arbitrary"),
    )(seq_first, h, h, h, wup, bup)
    sp, out, blk = specs(True)
    sp = sp + [pl.BlockSpec((TB, HW), lambda s, f: (blk(s), AG_OFF // HW)), out]
    return pl.pallas_call(
        functools.partial(_gla_kernel, reverse=True, nblk=nblk), name="gla_bwd",
        grid_spec=pltpu.PrefetchScalarGridSpec(num_scalar_prefetch=1, grid=(nblk,), in_specs=sp, out_specs=out,
                                               scratch_shapes=scratch),
        out_shape=jax.ShapeDtypeStruct((n, H * GLA_DV), jnp.bfloat16),
        compiler_params=_cparams("arbitrary"),
    )(seq_last, h, h, h, wup, bup, h, o_f)


def _pack_gla_up(w_up, b_up):
    H, dk, r = GLA_HEADS, GLA_DK, GLA_RANK
    assert 2 * dk == LANE
    w = w_up.reshape(2, r, H, dk).transpose(0, 2, 1, 3)
    wp = jnp.zeros((2, H, LANE, dk), jnp.float32)
    wp = wp.at[0, :, 0:r].set(w[0]).at[1, :, r:2 * r].set(w[1])
    bp = b_up.reshape(2, H, 1, dk).astype(jnp.float32)
    return jnp.concatenate([wp, wp], axis=-1).astype(jnp.bfloat16), jnp.concatenate([bp, bp], axis=-1)


def _rope_tables(pos, dim, theta, width):
    half = dim // 2
    pos = np.asarray(pos, np.float64)
    T = pos.shape[0]
    inv = 1.0 / (float(theta) ** (np.arange(half, dtype=np.float64) / half))
    ang = pos[:, None] * inv[None, :]
    c, s = np.cos(ang), np.sin(ang)
    rest = width - dim
    cos = np.concatenate([c, c, np.ones((T, rest))], axis=1)
    sin_lo = np.concatenate([-s, np.zeros((T, half + rest))], axis=1)
    sin_hi = np.concatenate([np.zeros((T, half)), s, np.zeros((T, rest))], axis=1)
    return tuple(jnp.asarray(a, jnp.float32) for a in (cos, sin_lo, sin_hi))


def _ret_kernel(flag_ref, pos_ref, q_ref, k_ref, v_ref, cos_ref, sin_ref, dec_ref, *rest, reverse, nblk):
    s_idx = pl.program_id(0)
    blk = nblk - 1 - s_idx if reverse else s_idx

    @pl.when(flag_ref[blk] == 1)
    def _():
        st_ref = rest[-1] if reverse else rest[-2]
        st_ref[...] = jnp.zeros_like(st_ref)

    for hh in range(RET_HEADS):
        _ret_head(hh, s_idx, q_ref, k_ref, v_ref, cos_ref, sin_ref, dec_ref, *rest, reverse=reverse)


def _ret_head(hh, s_idx, q_ref, k_ref, v_ref, cos_ref, sin_ref, dec_ref, *rest, reverse):
    if reverse:
        gate_ref, o1_ref, o_ref, st_ref = rest
    else:
        o_ref, st_ref, dm_ref = rest
    f32, bf16 = jnp.float32, jnp.bfloat16
    TB = q_ref.shape[0]
    sl = slice(hh * RET_DH, (hh + 1) * RET_DH)

    dec = dec_ref[hh]
    lg = -(jnp.maximum(dec, 0.0) + jnp.log1p(jnp.exp(-jnp.abs(dec))))
    lg_f, lg_b = lg[0, 0:1, 0:1], lg[1, 0:1, 0:1]

    cos, sin = cos_ref[...], sin_ref[...]

    def rot(x):
        return x * cos + pltpu.roll(x, RET_DH // 2, axis=1) * sin

    qr = rot(q_ref[:, sl].astype(f32))
    kr = rot(k_ref[:, sl].astype(f32)) * (RET_DH ** -0.5)
    v = v_ref[:, sl]
    idx = lax.broadcasted_iota(jnp.int32, (TB, 1), 0).astype(f32)
    st = st_ref[hh]
    cross = jnp.dot(qr.astype(bf16), st.astype(bf16), preferred_element_type=f32)
    if reverse:
        cross = cross * jnp.exp(lg_b * (TB - idx))
        kz = (kr * jnp.exp(lg_b * idx)).astype(bf16)
        st_ref[hh] = st * jnp.exp(lg_b * TB) + lax.dot_general(kz, v, (((0,), (0,)), ((), ())),
                                                               preferred_element_type=f32)
        o = _headnorm(o1_ref[:, sl] + cross)
        o_ref[:, sl] = (o * _silu(gate_ref[:, sl].astype(f32))).astype(o_ref.dtype)
    else:
        @pl.when(s_idx == 0)
        def _():
            ri = lax.broadcasted_iota(jnp.int32, (TB, TB), 0)
            ci = lax.broadcasted_iota(jnp.int32, (TB, TB), 1)
            d = (ri - ci).astype(f32)
            dm_ref[hh] = jnp.where(d >= 0, jnp.exp(lg_f * jnp.maximum(d, 0.0)), jnp.exp(lg_b * jnp.maximum(-d, 0.0)))

        cross = cross * jnp.exp(lg_f * (idx + 1.0))
        kz = (kr * jnp.exp(lg_f * (TB - 1.0 - idx))).astype(bf16)
        st_ref[hh] = st * jnp.exp(lg_f * TB) + lax.dot_general(kz, v, (((0,), (0,)), ((), ())),
                                                               preferred_element_type=f32)
        sc = lax.dot_general(qr.astype(bf16), kr.astype(bf16), (((1,), (1,)), ((), ())), preferred_element_type=f32)
        sc = (sc * dm_ref[hh]).astype(bf16)
        o_ref[:, sl] = jnp.dot(sc, v, preferred_element_type=f32) + cross


def _retention(h, seq_first, seq_last, pos_blk, decay, t_max):
    n = h.shape[0]
    TB = SEQ_BLOCK
    nblk = n // TB
    H, dh = RET_HEADS, RET_DH
    cos, sin_lo, sin_hi = _rope_tables(np.arange(t_max), dh, RET_THETA, dh)
    sin = sin_lo + sin_hi
    dec = jnp.broadcast_to(decay.astype(jnp.float32).T[:, :, None, None], (H, 2, 8, LANE))

    HW = H * dh

    def specs(reverse):
        blk = (lambda s: nblk - 1 - s) if reverse else (lambda s: s)
        tok = lambda off: pl.BlockSpec((TB, HW), lambda s, f, p: (blk(s), off // HW))
        tab = pl.BlockSpec((TB, LANE), lambda s, f, p: (p[blk(s)], 0))
        sp = [tok(DQ_OFF), tok(DK_OFF), tok(DV_OFF), tab, tab,
              pl.BlockSpec((H, 2, 8, LANE), lambda s, f, p: (0, 0, 0, 0))]
        return sp, tok, pl.BlockSpec((TB, HW), lambda s, f, p: (blk(s), 0))

    sp, _, out = specs(False)
    o1 = pl.pallas_call(
        functools.partial(_ret_kernel, reverse=False, nblk=nblk), name="ret_fwd",
        grid_spec=pltpu.PrefetchScalarGridSpec(
            num_scalar_prefetch=2, grid=(nblk,), in_specs=sp, out_specs=out,
            scratch_shapes=[pltpu.VMEM((H, dh, dh), jnp.float32), pltpu.VMEM((H, TB, TB), jnp.float32)]),
        out_shape=jax.ShapeDtypeStruct((n, H * dh), jnp.float32),
        compiler_params=_cparams("arbitrary"),
    )(seq_first, pos_blk, h, h, h, cos, sin, dec)
    sp, tok, out = specs(True)
    sp = sp + [tok(DG_OFF), out]
    return pl.pallas_call(
        functools.partial(_ret_kernel, reverse=True, nblk=nblk), name="ret_bwd",
        grid_spec=pltpu.PrefetchScalarGridSpec(
            num_scalar_prefetch=2, grid=(nblk,), in_specs=sp, out_specs=out,
            scratch_shapes=[pltpu.VMEM((H, dh, dh), jnp.float32)]),
        out_shape=jax.ShapeDtypeStruct((n, H * dh), jnp.bfloat16),
        compiler_params=_cparams("arbitrary"),
    )(seq_last, pos_blk, h, h, h, cos, sin, dec, h, o1)


SWA_Q = 128
SWA_SUBS = 4
SWA_HALF = 64
assert all(w // (2 * d) == SWA_HALF for w, d in SWA_GROUPS)
NEG_BIG = -1e30


def _swa_kernel(join_ref, q_ref, kl_ref, kc_ref, kr_ref, vl_ref, vc_ref, vr_ref,
                qcos_ref, qs1_ref, qs2_ref, kcos_ref, ks1_ref, ks2_ref, o_ref, lse_ref):
    f32, bf16 = jnp.float32, jnp.bfloat16
    j = pl.program_id(1)
    Q, HW = SWA_Q, SWA_HALF
    W = Q + 2 * HW
    ri = lax.broadcasted_iota(jnp.int32, (Q, W), 0)
    ci = lax.broadcasted_iota(jnp.int32, (Q, W), 1)
    rel = ci - HW - ri
    band = (rel <= HW) & (rel >= -HW)
    joins = [join_ref[j * (SWA_SUBS + 1) + u] == 1 for u in range(SWA_SUBS + 1)]
    oks = [band & ((ci >= HW) | joins[u]) & ((ci < HW + Q) | joins[u + 1]) for u in range(SWA_SUBS)]
    lane = lax.broadcasted_iota(jnp.int32, (Q, LANE), 1)

    def rot(x, c_ref, s1_ref, s2_ref):
        return (x * c_ref[...] + pltpu.roll(x, LANE - ROPE_DIM // 2, axis=1) * s1_ref[...]
                + pltpu.roll(x, ROPE_DIM // 2, axis=1) * s2_ref[...])

    lse_tiles = [jnp.zeros((Q, LANE), f32) for _ in range(SWA_SUBS)]
    for hh in range(SWA_HEADS):
        sl = slice(hh * SWA_DH, (hh + 1) * SWA_DH)
        q = (rot(q_ref[:, sl].astype(f32), qcos_ref, qs1_ref, qs2_ref) * (SWA_DH ** -0.5)).astype(bf16)
        k = jnp.concatenate([kl_ref[:, sl], kc_ref[:, sl], kr_ref[:, sl]], axis=0).astype(f32)
        k = rot(k, kcos_ref, ks1_ref, ks2_ref).astype(bf16)
        v = jnp.concatenate([vl_ref[:, sl], vc_ref[:, sl], vr_ref[:, sl]], axis=0)
        for u in range(SWA_SUBS):
            s = lax.dot_general(q[u * Q:(u + 1) * Q], k[u * Q:u * Q + W], (((1,), (1,)), ((), ())),
                                preferred_element_type=f32)
            s = jnp.where(oks[u], s, NEG_BIG)
            m = jnp.max(s, -1, keepdims=True)
            p = jnp.exp(s - m)
            den = jnp.sum(p, -1, keepdims=True)
            o = jnp.dot(p.astype(bf16), v[u * Q:u * Q + W], preferred_element_type=f32) / den
            o_ref[u * Q:(u + 1) * Q, sl] = o
            lse_tiles[u] = jnp.where(lane == hh, m + jnp.log(den), lse_tiles[u])
    for u in range(SWA_SUBS):
        lse_ref[u * Q:(u + 1) * Q, :] = lse_tiles[u]


def _swa_merge_kernel(*refs):
    ng = len(SWA_GROUPS)
    o_refs, l_refs, y_ref = refs[:ng], refs[ng:2 * ng], refs[2 * ng]
    scratch = refs[2 * ng + 1:]
    assert SWA_DH == LANE
    os_, ls_ = [], []
    si = 0
    for gi, (_, dil) in enumerate(SWA_GROUPS):
        if dil == 1:
            os_.append(lambda hh, r=o_refs[gi]: r[0, :, hh * LANE:(hh + 1) * LANE])
            ls_.append(l_refs[gi].at[0])
            continue
        o_s, l_s = scratch[si], scratch[si + 1]
        si += 2
        rows = l_s.shape[0] // dil
        for r in range(dil):
            for hh in range(SWA_HEADS):
                o_s[hh, pl.ds(r, rows, stride=dil), :] = o_refs[gi][r, :, hh * LANE:(hh + 1) * LANE]
            l_s[pl.ds(r, rows, stride=dil), :] = l_refs[gi][r]
        os_.append(lambda hh, s=o_s: s[hh])
        ls_.append(l_s)
    for hh in range(SWA_HEADS):
        ls = [r[:, hh:hh + 1] for r in ls_]
        m = functools.reduce(jnp.maximum, ls)
        ws = [jnp.exp(l - m) for l in ls]
        tot = functools.reduce(lambda a, b: a + b, ws)
        y = functools.reduce(lambda a, b: a + b, [w * o(hh) for w, o in zip(ws, os_)])
        y_ref[:, hh * LANE:(hh + 1) * LANE] = (y / tot).astype(y_ref.dtype)


def _proj_strided_kernel(a_ref, b_ref, o_ref, acc_ref, *, dil):
    acc = jnp.dot(a_ref[...], b_ref[...], preferred_element_type=jnp.float32)
    if dil == 1:
        o_ref[0] = acc.astype(o_ref.dtype)
        return
    rows = acc_ref.shape[1] // dil
    for s in range(acc_ref.shape[0]):
        acc_ref[s] = acc[:, s * LANE:(s + 1) * LANE]
        for r in range(dil):
            o_ref[r, :, s * LANE:(s + 1) * LANE] = acc_ref[s, pl.ds(r, rows, stride=dil), :].astype(o_ref.dtype)


def _proj_strided(a, b, dil, bm=1024, bn=512):
    n, K = a.shape
    N = b.shape[1]
    return pl.pallas_call(
        functools.partial(_proj_strided_kernel, dil=dil), name=f"swa_proj_dil{dil}", grid=(n // bm, N // bn),
        in_specs=[pl.BlockSpec((bm, K), lambda i, j: (i, 0)), pl.BlockSpec((K, bn), lambda i, j: (0, j))],
        out_specs=pl.BlockSpec((dil, bm // dil, bn), lambda i, j: (0, i, j)),
        out_shape=jax.ShapeDtypeStruct((dil, n // dil, N), jnp.bfloat16),
        scratch_shapes=[pltpu.VMEM((bn // LANE, bm, LANE), jnp.float32)],
        compiler_params=_cparams("parallel", "arbitrary"),
    )(a, b)


def _swa(xb, w_groups, seqs):
    n = xb.shape[0]
    Q, HW = SWA_Q, SWA_HALF
    GW = SWA_HEADS * SWA_DH
    outs, lses = [], []
    for gi, (window, dil) in enumerate(SWA_GROUPS):
        hs = _proj_strided(xb, w_groups[gi], dil)
        rows = n // dil
        QB = SWA_SUBS * Q
        nblk = rows // QB
        has_l, has_r = [], []
        for num, T in seqs:
            per = T // dil // Q
            assert per * Q * dil == T
            has_l += ([0] + [1] * (per - 1)) * num
            has_r += ([1] * (per - 1) + [0]) * num
        assert len(has_l) == nblk * SWA_SUBS
        joins = np.array(has_l, np.int32).reshape(nblk, SWA_SUBS)
        joins = np.concatenate([joins, np.array(has_r, np.int32).reshape(nblk, SWA_SUBS)[:, -1:]], axis=1)
        joins = jnp.asarray(joins.reshape(-1))
        qpos = np.arange(QB) * dil
        kpos = (np.arange(QB + 2 * HW) - HW) * dil
        qt = _rope_tables(qpos, ROPE_DIM, ROPE_THETA, SWA_DH)
        kt = _rope_tables(kpos, ROPE_DIM, ROPE_THETA, SWA_DH)
        per64 = QB // HW
        last64 = rows // HW - 1

        def center(c):
            return pl.BlockSpec((None, QB, GW), lambda r, j, *_: (r, j, c))

        def left(c):
            return pl.BlockSpec((None, HW, GW), lambda r, j, *_: (r, jnp.maximum(per64 * j - 1, 0), c))

        def right(c):
            return pl.BlockSpec((None, HW, GW), lambda r, j, *_: (r, jnp.minimum(per64 * (j + 1), last64), c))

        qtab = pl.BlockSpec((QB, SWA_DH), lambda r, j, *_: (0, 0))
        ktab = pl.BlockSpec((QB + 2 * HW, SWA_DH), lambda r, j, *_: (0, 0))
        o_g, lse_g = pl.pallas_call(
            _swa_kernel, name=f"swa_dil{dil}",
            grid_spec=pltpu.PrefetchScalarGridSpec(
                num_scalar_prefetch=1, grid=(dil, nblk),
                in_specs=[center(0), left(1), center(1), right(1), left(2), center(2), right(2),
                          qtab, qtab, qtab, ktab, ktab, ktab],
                out_specs=[pl.BlockSpec((None, QB, GW), lambda r, j, *_: (r, j, 0)),
                           pl.BlockSpec((None, QB, LANE), lambda r, j, *_: (r, j, 0))]),
            out_shape=[jax.ShapeDtypeStruct((dil, rows, GW), jnp.float32),
                       jax.ShapeDtypeStruct((dil, rows, LANE), jnp.float32)],
            compiler_params=_cparams("parallel", "parallel"),
        )(joins, hs, hs, hs, hs, hs, hs, hs, *qt, *kt)
        outs.append(o_g)
        lses.append(lse_g)
    bm = 1024
    dils = [d for _, d in SWA_GROUPS]
    osp = [pl.BlockSpec((d, bm // d, GW), lambda i: (0, i, 0)) for d in dils]
    lsp = [pl.BlockSpec((d, bm // d, LANE), lambda i: (0, i, 0)) for d in dils]
    scratch = []
    for d in dils:
        if d > 1:
            scratch += [pltpu.VMEM((SWA_HEADS, bm, LANE), jnp.float32), pltpu.VMEM((bm, LANE), jnp.float32)]
    return pl.pallas_call(
        _swa_merge_kernel, name="swa_merge", grid=(n // bm,), in_specs=osp + lsp,
        out_specs=pl.BlockSpec((bm, GW), lambda i: (i, 0)),
        out_shape=jax.ShapeDtypeStruct((n, GW), jnp.bfloat16), scratch_shapes=scratch,
        compiler_params=_cparams("parallel"),
    )(*outs, *lses)


FFT_T2 = 128
FW = FNET_GROUPS * FNET_DG


def _dft_cos_sin(n_out, n_in, period):
    ang = (2.0 * np.pi / period) * np.mod(np.outer(np.arange(n_out), np.arange(n_in)), period)
    return np.cos(ang), np.sin(ang)


def _fnet_stage1_kernel(z_ref, m_ref, a_ref, *, cols):
    mat = m_ref[...]
    for c in range(cols):
        zr = z_ref[:, c * 2 * FW:c * 2 * FW + FW]
        zi = z_ref[:, c * 2 * FW + FW:(c + 1) * 2 * FW]
        a = jnp.dot(mat, jnp.concatenate([zr, zi], axis=0), preferred_element_type=jnp.float32)
        t1 = zr.shape[0]
        a_ref[:, c * 2 * FW:c * 2 * FW + FW] = a[:t1].astype(a_ref.dtype)
        a_ref[:, c * 2 * FW + FW:(c + 1) * 2 * FW] = a[t1:].astype(a_ref.dtype)


FNET_KB = 8


def _fnet_stage2_kernel(a_ref, g_ref, y_ref, *, scale):
    kw = g_ref.shape[1] // FNET_KB
    for j in range(FNET_KB):
        y = jnp.dot(g_ref[:, j * kw:(j + 1) * kw], a_ref[j], preferred_element_type=jnp.float32)
        y_ref[:, j * FW:(j + 1) * FW] = (y * scale).astype(y_ref.dtype)


def _fnet(h, seqs):
    n = h.shape[0]
    bf16 = jnp.bfloat16
    T2 = FFT_T2
    cc, ss = _dft_cos_sin(FNET_DG, FNET_DG, FNET_DG)
    eye = np.eye(FNET_GROUPS)
    wc = jnp.asarray(np.concatenate([np.kron(eye, cc), -np.kron(eye, ss)], axis=1), bf16)
    z = _matmul(h, wc, bf16, bn=2 * FW, a_col_blk=C_OFF // FW, name="fnet_channel_dft")
    zz = z.reshape(n // T2, T2 * 2 * FW)
    outs = []
    row0 = 0
    for num, T in seqs:
        T1 = T // T2
        assert T1 * T2 == T and T1 % 16 == 0
        c1, s1 = _dft_cos_sin(T1, T1, T1)
        mat = jnp.asarray(np.block([[c1, s1], [-s1, c1]]), bf16)
        cols = min(T2, max(1, 1024 // T1))
        rb0 = row0 // T1
        a = pl.pallas_call(
            functools.partial(_fnet_stage1_kernel, cols=cols), name=f"fnet_dft_t1_{T1}", grid=(num, T2 // cols),
            in_specs=[pl.BlockSpec((T1, cols * 2 * FW), lambda b, j: (rb0 + b, j)),
                      pl.BlockSpec((2 * T1, 2 * T1), lambda b, j: (0, 0))],
            out_specs=pl.BlockSpec((T1, cols * 2 * FW), lambda b, j: (b, j)),
            out_shape=jax.ShapeDtypeStruct((num * T1, T2 * 2 * FW), bf16),
            compiler_params=_cparams("parallel", "parallel"),
        )(zz, mat)
        ec, es = _dft_cos_sin(T, T2, T)
        g = jnp.asarray(np.stack([ec, es], axis=-1).reshape(T2, T1 * 2 * T2), bf16)
        y = pl.pallas_call(
            functools.partial(_fnet_stage2_kernel, scale=float(1.0 / np.sqrt(T * FNET_DG))), name=f"fnet_dft_t2_{T1}",
            grid=(num, T1 // FNET_KB),
            in_specs=[pl.BlockSpec((FNET_KB, 2 * T2, FW), lambda b, k1: (b * (T1 // FNET_KB) + k1, 0, 0)),
                      pl.BlockSpec((T2, FNET_KB * 2 * T2), lambda b, k1: (0, k1))],
            out_specs=pl.BlockSpec((T2, FNET_KB * FW), lambda b, k1: (b, k1)),
            out_shape=jax.ShapeDtypeStruct((num * T2, T1 * FW), bf16),
            compiler_params=_cparams("parallel", "parallel"),
        )(a.reshape(num * T1, 2 * T2, FW), g)
        outs.append(y.reshape(num * T, FW))
        row0 += num * T1
    return jnp.concatenate(outs, axis=0)


def _ffn_up_kernel(x_ref, w1_ref, w3_ref, h_ref, w1b_ref, w3b_ref):
    @pl.when(pl.program_id(2) == 0)
    def _():
        w1b_ref[...] = w1_ref[...].astype(w1b_ref.dtype)
        w3b_ref[...] = w3_ref[...].astype(w3b_ref.dtype)

    x = x_ref[...]
    h1 = jnp.dot(x, w1b_ref[...], preferred_element_type=jnp.float32)
    h3 = jnp.dot(x, w3b_ref[...], preferred_element_type=jnp.float32)
    h_ref[...] = (_silu(h1) * h3).astype(h_ref.dtype)


def _ffn_down_kernel(h_ref, w2_ref, s_ref, o_ref, w2b_ref, *, e0):
    e = pl.program_id(0) + e0

    @pl.when(pl.program_id(1) == 0)
    def _():
        w2b_ref[...] = w2_ref[...].astype(w2b_ref.dtype)

    D = w2_ref.shape[1]
    side = s_ref[...]
    lane = lax.broadcasted_iota(jnp.int32, side.shape, 1)
    mine = (lane == e) | (lane == e + SIDE_MID) | (lane == e + SIDE_LOW)
    gate = jnp.sum(jnp.where(mine, side.astype(jnp.float32), 0.0), axis=1, keepdims=True)
    y = jnp.dot(h_ref[...], w2b_ref[...], preferred_element_type=jnp.float32)
    o_ref[:, :D] = (y * gate).astype(o_ref.dtype)
    o_ref[:, D:] = side


def _expert_ffn(xe, w1, w3, w2, layer, e0, bm=1024, bf=512):
    E, R, DY = xe.shape
    D = DY - LANE
    F = w1.shape[-1]
    wspec = pl.BlockSpec((None, None, D, bf), lambda e, f, i: (layer, e0 + e, 0, f))
    hmid = pl.pallas_call(
        _ffn_up_kernel, name="expert_ffn_up", grid=(E, F // bf, R // bm),
        in_specs=[pl.BlockSpec((None, bm, D), lambda e, f, i: (e, i, 0)), wspec, wspec],
        out_specs=pl.BlockSpec((None, bm, bf), lambda e, f, i: (e, i, f)),
        out_shape=jax.ShapeDtypeStruct((E, R, F), jnp.bfloat16),
        scratch_shapes=[pltpu.VMEM((D, bf), jnp.bfloat16), pltpu.VMEM((D, bf), jnp.bfloat16)],
        compiler_params=_cparams("arbitrary", "arbitrary", "arbitrary"),
    )(xe, w1, w3)
    return pl.pallas_call(
        functools.partial(_ffn_down_kernel, e0=e0), name="expert_ffn_down", grid=(E, R // bm),
        in_specs=[pl.BlockSpec((None, bm, F), lambda e, i: (e, i, 0)),
                  pl.BlockSpec((None, None, F, D), lambda e, i: (layer, e0 + e, 0, 0), pipeline_mode=pl.Buffered(1)),
                  pl.BlockSpec((None, bm, LANE), lambda e, i: (e, i, D // LANE))],
        out_specs=pl.BlockSpec((None, bm, D + LANE), lambda e, i: (e, i, 0)),
        out_shape=jax.ShapeDtypeStruct((E, R, D + LANE), jnp.bfloat16),
        scratch_shapes=[pltpu.VMEM((F, D), jnp.bfloat16)],
        compiler_params=_cparams("arbitrary", "arbitrary"),
    )(hmid, w2, xe)


COMB_BM = 128
COMB_G = 16
COMB_K = 256
MOE_PARTS = 2


def _combine_kernel(lo_ref, x_ref, g_ref, b_ref, *rest, nblk_all, blk_off, want_bf16, parts):
    ye_parts, rest = rest[:parts], rest[parts:]
    if want_bf16:
        xf_ref, xb_ref, buf_ref, acc_ref, sem = rest
    else:
        xf_ref, buf_ref, acc_ref, sem = rest
    f32, bf16 = jnp.float32, jnp.bfloat16
    per = ye_parts[0].shape[0]
    E = per * parts
    D = x_ref.shape[1]
    BM, G, KC = COMB_BM, COMB_G, COMB_K
    i = pl.program_id(0)
    nsteps = pl.num_programs(0)
    blk = i + blk_off
    slot = i % 2

    @pl.when(i == 0)
    def _():
        buf_ref[...] = jnp.zeros_like(buf_ref)

    def granule_copy(sl, e, src_row, dst_row):
        return pltpu.make_async_copy(ye_parts[e // per].at[e % per, pl.ds(src_row, G)],
                                     buf_ref.at[sl, pl.ds(dst_row, G)], sem.at[sl])

    def gather_block(b, sl, start):
        off = jnp.int32(0)
        for e in range(E):
            lo = lo_ref[e * (nblk_all + 1) + b]
            hi = lo_ref[e * (nblk_all + 1) + b + 1]
            lo_al = (lo // G) * G
            ng = jnp.where(hi > lo, (hi - lo_al + G - 1) // G, 0)
            if start:
                def issue(g, carry, e=e, lo_al=lo_al, off=off):
                    granule_copy(sl, e, pl.multiple_of(lo_al + g * G, G), pl.multiple_of(off + g * G, G)).start()
                    return carry

                lax.fori_loop(0, ng, issue, 0)
            off = off + ng * G
        return off

    @pl.when(i == 0)
    def _():
        gather_block(blk, slot, True)

    @pl.when(i + 1 < nsteps)
    def _():
        gather_block(blk + 1, 1 - slot, True)

    off = gather_block(blk, slot, False)

    def wait_one(g, carry):
        granule_copy(slot, 0, 0, 0).wait()
        return carry

    lax.fori_loop(0, off // G, wait_one, 0)

    acc_ref[...] = jnp.zeros_like(acc_ref)
    t0 = (blk * BM).astype(f32)
    lane_tok = lax.broadcasted_iota(jnp.int32, (KC, BM), 1).astype(f32)
    row = lax.broadcasted_iota(jnp.int32, (KC, 1), 0)

    def chunk(c, carry):
        win = buf_ref[slot, pl.ds(pl.multiple_of(c * KC, KC), KC), :]
        tok = (win[:, D + SIDE_TOK:D + SIDE_TOK + 1].astype(f32) * TOK_RADIX
               + win[:, D + SIDE_TOK + 1:D + SIDE_TOK + 2].astype(f32))
        tok = jnp.where(row + c * KC < off, tok, -1.0)
        onehot_t = (tok - t0 == lane_tok).astype(bf16)
        acc_ref[...] += lax.dot_general(onehot_t, win[:, :D], (((0,), (0,)), ((), ())), preferred_element_type=f32)
        return carry

    lax.fori_loop(0, (off + KC - 1) // KC, chunk, 0)
    y = _ln_rows(DN_ALPHA * x_ref[...] + acc_ref[...], g_ref[...], b_ref[...])
    xf_ref[...] = y
    if want_bf16:
        xb_ref[...] = y.astype(bf16)


def _combine_ln(x, ye, lo, g, b, row_off, rows, want_bf16):
    n, D = x.shape
    per, R, DY = ye[0].shape
    E = per * len(ye)
    BM = COMB_BM
    nblk_all = n // BM
    ob = row_off // BM
    kmax = -(-E * (BM + COMB_G) // COMB_K) * COMB_K
    irow = pl.BlockSpec((BM, D), lambda i, lo: (i + ob, 0))
    orow = pl.BlockSpec((BM, D), lambda i, lo: (i, 0))
    vec = pl.BlockSpec((1, D), lambda i, lo: (0, 0))
    out_shape = [jax.ShapeDtypeStruct((rows, D), jnp.float32)]
    out_specs = [orow]
    if want_bf16:
        out_shape.append(jax.ShapeDtypeStruct((rows, D), jnp.bfloat16))
        out_specs.append(orow)
    return pl.pallas_call(
        functools.partial(_combine_kernel, nblk_all=nblk_all, blk_off=ob, want_bf16=want_bf16, parts=len(ye)),
        name="moe_combine_ln",
        grid_spec=pltpu.PrefetchScalarGridSpec(
            num_scalar_prefetch=1, grid=(rows // BM,),
            in_specs=[irow, vec, vec] + [pl.BlockSpec(memory_space=pl.ANY)] * len(ye), out_specs=out_specs,
            scratch_shapes=[pltpu.VMEM((2, kmax, DY), jnp.bfloat16), pltpu.VMEM((BM, D), jnp.float32),
                            pltpu.SemaphoreType.DMA((2,))]),
        out_shape=out_shape,
        compiler_params=_cparams("arbitrary"),
    )(lo, x, g.reshape(1, D), b.reshape(1, D), *ye)


def _route_kernel(all_ref, a_ref, idx_ref, cprev_ref, thr_ref, need_ref, *, cap, tok_off):
    f32, bf16, i32 = jnp.float32, jnp.bfloat16, jnp.int32
    e = pl.program_id(0)

    def total(x):
        return jnp.sum(jnp.sum(x, axis=2, keepdims=True), axis=1, keepdims=True)

    @pl.when(e == 0)
    def _():
        bits_all = pltpu.bitcast(all_ref[...], i32)

        def refine(i, prefix):
            cand = prefix | jnp.left_shift(jnp.int32(1), 30 - i)
            return jnp.where(total((bits_all >= cand).astype(i32)) >= cap, cand, prefix)

        thr_all = lax.fori_loop(0, 31, refine, jnp.zeros((all_ref.shape[0], 1, 1), i32))
        thr_ref[...] = jnp.broadcast_to(thr_all, thr_ref.shape)
        need_ref[...] = jnp.broadcast_to(cap - total((bits_all > thr_all).astype(i32)), need_ref.shape)

    a = a_ref[...]
    R = a.shape[0]
    bits = pltpu.bitcast(a, i32)
    thr = thr_ref[e][0:1, 0:1]
    need = need_ref[e][0:1, 0:1]
    gt, eq = bits > thr, bits == thr

    li = lax.broadcasted_iota(i32, (LANE, LANE), 0)
    lj = lax.broadcasted_iota(i32, (LANE, LANE), 1)
    upto = (li <= lj).astype(bf16)
    ri = lax.broadcasted_iota(i32, (R, R), 0)
    rj = lax.broadcasted_iota(i32, (R, R), 1)
    before = (rj < ri).astype(bf16)

    def counts(mask):
        inc = jnp.dot(mask.astype(bf16), upto, preferred_element_type=f32)
        tot = jnp.broadcast_to(inc[:, LANE - 1:LANE], (R, LANE)).astype(bf16)
        return inc, jnp.dot(before, tot, preferred_element_type=f32)

    eq_inc, eq_prev = counts(eq)
    sel = gt | (eq & (eq_prev + eq_inc - 1.0 < need.astype(f32)))
    cl, cprev = counts(sel)
    cprev_ref[...] = cprev[:, :1].astype(i32)

    selb = sel.astype(bf16)
    c_upto = jnp.sum(lax.dot_general(selb, (ri <= rj).astype(bf16), (((0,), (0,)), ((), ())),
                                     preferred_element_type=f32), axis=0, keepdims=True)
    c_before = jnp.sum(lax.dot_general(selb, (ri < rj).astype(bf16), (((0,), (0,)), ((), ())),
                                       preferred_element_type=f32), axis=0, keepdims=True)
    slot = lax.broadcasted_iota(i32, (cap, 1), 0).astype(f32)
    onehot = ((c_before <= slot) & (slot < c_upto)).astype(bf16)
    lane = lax.broadcasted_iota(i32, (R, LANE), 1)
    cp = cprev.astype(i32)
    aux = jnp.where(lane == 0, cp // TOK_RADIX, jnp.where(lane == 1, cp % TOK_RADIX,
                    jnp.where(lane == 2, lax.broadcasted_iota(i32, (R, LANE), 0), 0)))
    rhs = jnp.concatenate([cl.astype(bf16), aux.astype(f32).astype(bf16)], axis=1)
    m = jnp.dot(onehot, rhs, preferred_element_type=f32)
    base = m[:, LANE:LANE + 1] * TOK_RADIX + m[:, LANE + 1:LANE + 2]
    chunk = m[:, LANE + 2:LANE + 3]
    pos = jnp.sum((m[:, :LANE] <= slot - base).astype(f32), axis=1, keepdims=True)
    idx_ref[...] = (chunk * LANE + pos).astype(i32) + tok_off


def _route(aff_t, cap, tok_off):
    E, R, _ = aff_t.shape
    return pl.pallas_call(
        functools.partial(_route_kernel, cap=cap, tok_off=tok_off), name="moe_route", grid=(E,),
        in_specs=[pl.BlockSpec((E, R, LANE), lambda e: (0, 0, 0)), pl.BlockSpec((None, R, LANE), lambda e: (e, 0, 0))],
        out_specs=[pl.BlockSpec((None, cap, 1), lambda e: (e, 0, 0)), pl.BlockSpec((None, R, 1), lambda e: (e, 0, 0))],
        out_shape=[jax.ShapeDtypeStruct((E, cap, 1), jnp.int32), jax.ShapeDtypeStruct((E, R, 1), jnp.int32)],
        scratch_shapes=[pltpu.VMEM((E, 8, LANE), jnp.int32), pltpu.VMEM((E, 8, LANE), jnp.int32)],
        compiler_params=_cparams("arbitrary"),
    )(aff_t, aff_t)


def _expert_choice(xs, aff, seqs, w1, w3, w2, layer):
    assert COMB_BM == LANE
    aff_t = _aff_chunks(aff)
    idxs, los = [], []
    start = slot0 = 0
    for num, T in seqs:
        m = num * T
        cap = max(1, CAPACITY_FACTOR * m // N_EXPERTS)
        idx, cprev = _route(aff_t[:, start // LANE:(start + m) // LANE], cap, start)
        idxs.append(idx[..., 0])
        los.append(cprev[..., 0] + slot0)
        start += m
        slot0 += cap
    lo = jnp.concatenate(los + [jnp.full((N_EXPERTS, 1), slot0, jnp.int32)], axis=1).reshape(-1)
    idx = jnp.concatenate(idxs, axis=1)
    per = N_EXPERTS // MOE_PARTS
    ye = [_expert_ffn(xs[idx[p * per:(p + 1) * per]], w1, w3, w2, layer, p * per) for p in range(MOE_PARTS)]
    return ye, lo


def _pack_w_in(w_in):
    sizes = (GLA_HEADS * GLA_DK, GLA_HEADS * GLA_DK, GLA_HEADS * GLA_DV, GLA_HEADS * GLA_DV, 2 * GLA_RANK,
             SWA_NH * SWA_DH, SWA_NH * SWA_DH, SWA_NH * SWA_DH, FNET_GROUPS * FNET_DG,
             RET_HEADS * RET_DH, RET_HEADS * RET_DH, RET_HEADS * RET_DH, RET_HEADS * RET_DH)
    offs = np.concatenate([[0], np.cumsum(sizes)])
    part = lambda i: w_in[..., offs[i]:offs[i + 1]]
    a_q, a_k = part(0), part(1)
    cols = []
    for hh in range(GLA_HEADS):
        cols += [a_q[..., hh * GLA_DK:(hh + 1) * GLA_DK], a_k[..., hh * GLA_DK:(hh + 1) * GLA_DK]]
    cols += [part(2), part(3), part(4),
             jnp.zeros(w_in.shape[:2] + (C_OFF - LR_OFF - 2 * GLA_RANK,), w_in.dtype)]
    cols += [part(i) for i in range(8, 13)]
    out = jnp.concatenate(cols, axis=-1)
    assert out.shape[-1] == IN_COLS_P
    GW = SWA_HEADS * SWA_DH
    swa = [jnp.concatenate([part(i)[..., g * GW:(g + 1) * GW] for i in (5, 6, 7)], axis=-1)
           for g in range(len(SWA_GROUPS))]
    return out, swa


def _block_tables(seqs, blk):
    first, last, pos = [], [], []
    for num, T in seqs:
        per = T // blk
        assert per * blk == T
        for _ in range(num):
            first += [1] + [0] * (per - 1)
            last += [0] * (per - 1) + [1]
            pos += list(range(per))
    mk = lambda a: jnp.asarray(np.array(a, np.int32))
    return mk(first), mk(last), mk(pos)


def kernel(x_prompt, x_sample, ln_in_g, ln_in_b, w_in, gla_w_up, gla_b_up, ret_decay, w_gate, b_gate, w_branch,
           w_out, ln1_g, ln1_b, w_router, w_e1, w_e3, w_e2, ln2_g, ln2_b):
    bf16 = jnp.bfloat16
    D = D_MODEL
    seqs = (x_prompt.shape[:2], x_sample.shape[:2])
    n_p, n_s = x_prompt.shape[0] * x_prompt.shape[1], x_sample.shape[0] * x_sample.shape[1]
    n = n_p + n_s
    t_max = max(T for _, T in seqs)
    seq_first, seq_last, pos_blk = _block_tables(seqs, SEQ_BLOCK)

    w_in_p, w_swa = _pack_w_in(w_in.astype(bf16))
    wo = w_out.astype(bf16)
    wr = jnp.pad(w_router, ((0, 0), (0, 0), (0, LANE - N_EXPERTS))).astype(bf16)
    wg, wb, we1, we3, we2 = w_gate, w_branch, w_e1, w_e3, w_e2

    xf, xb = _ln_in(x_prompt.reshape(n_p, D), x_sample.reshape(n_s, D), ln_in_g, ln_in_b)
    for l in range(DEPTH):
        h = _matmul(xb, w_in_p[l], bf16, bm=1024, bn=IN_BN, name="in_proj")
        wup, bup = _pack_gla_up(gla_w_up[l], gla_b_up[l])
        ys = (_gla(h, seq_first, seq_last, wup, bup),
              _swa(xb, [w[l] for w in w_swa], seqs),
              _fnet(h, seqs),
              _retention(h, seq_first, seq_last, pos_blk, ret_decay[l], t_max))
        merged = _merge(xb, wg, b_gate[l], ys, wb, l)
        x1f, x1s, aff = _outproj(merged, wo[l], xf, ln1_g[l], ln1_b[l], wr[l])
        ye, lo = _expert_choice(x1s, aff, seqs, we1, we3, we2, l)
        if l + 1 < DEPTH:
            xf, xb = _combine_ln(x1f, ye, lo, ln2_g[l], ln2_b[l], 0, n, True)
        else:
            y_p, = _combine_ln(x1f, ye, lo, ln2_g[l], ln2_b[l], 0, n_p, False)
            y_s, = _combine_ln(x1f, ye, lo, ln2_g[l], ln2_b[l], n_p, n_s, False)
    return (y_p.reshape(x_prompt.shape), y_s.reshape(x_sample.shape))
```

```python
import functools

import jax
import jax.numpy as jnp
import numpy as np
from jax import lax
from jax.experimental import pallas as pl
from jax.experimental.pallas import tpu as pltpu

D_MODEL = 2048
DEPTH = 2
GLA_HEADS, GLA_DK, GLA_DV, GLA_RANK, GLA_TAU, GLA_CHUNK = 4, 64, 128, 16, 16.0, 64
SWA_GROUPS = ((128, 1), (512, 4), (2048, 16))
SWA_HEADS, SWA_DH = 4, 128
SWA_NH = SWA_HEADS * len(SWA_GROUPS)
ROPE_THETA, ROPE_DIM = 500000.0, SWA_DH // 4
FNET_GROUPS, FNET_DG = 4, 128
RET_HEADS, RET_DH, RET_THETA = 4, 128, 10000.0
N_BRANCH, BRANCH_W = 4, 512
N_EXPERTS, EXPERT_FF, CAPACITY_FACTOR = 16, 2048, 2
DN_ALPHA = (2 * DEPTH) ** 0.25
LN_EPS = 1e-5

VMEM_LIMIT_BYTES = 48 * 1024 * 1024
LANE = 128

QK_OFF = 0
AV_OFF = QK_OFF + GLA_HEADS * LANE
AG_OFF = AV_OFF + GLA_HEADS * GLA_DV
LR_OFF = AG_OFF + GLA_HEADS * GLA_DV
C_OFF = LR_OFF + 4 * LANE
DQ_OFF = C_OFF + FNET_GROUPS * FNET_DG
DK_OFF = DQ_OFF + RET_HEADS * RET_DH
DV_OFF = DK_OFF + RET_HEADS * RET_DH
DG_OFF = DV_OFF + RET_HEADS * RET_DH
IN_USED = DG_OFF + RET_HEADS * RET_DH
IN_BN = 1536
IN_COLS_P = -(-IN_USED // IN_BN) * IN_BN
assert IN_COLS_P == IN_USED

SEQ_BLOCK = 512


def _cparams(*sem):
    return pltpu.CompilerParams(dimension_semantics=sem, vmem_limit_bytes=VMEM_LIMIT_BYTES)


def _mm_kernel(a_ref, b_ref, o_ref):
    o_ref[...] = jnp.dot(a_ref[...], b_ref[...], preferred_element_type=jnp.float32).astype(o_ref.dtype)


def _matmul(a, b, out_dtype, bm=1024, bn=512, a_col_blk=0, name="matmul"):
    batched = a.ndim == 3
    M = a.shape[-2]
    K, N = b.shape[-2:]
    bm = min(bm, M)
    bn = min(bn, N)
    assert M % bm == 0 and N % bn == 0, (M, N, bm, bn)
    if batched:
        E = a.shape[0]
        grid = (E, M // bm, N // bn)
        in_specs = [pl.BlockSpec((None, bm, K), lambda e, i, j: (e, i, 0)),
                    pl.BlockSpec((None, K, bn), lambda e, i, j: (e, 0, j))]
        out_specs = pl.BlockSpec((None, bm, bn), lambda e, i, j: (e, i, j))
        out_shape = jax.ShapeDtypeStruct((E, M, N), out_dtype)
        sem = ("parallel", "parallel", "arbitrary")
    else:
        grid = (M // bm, N // bn)
        in_specs = [pl.BlockSpec((bm, K), lambda i, j: (i, a_col_blk)),
                    pl.BlockSpec((K, bn), lambda i, j: (0, j))]
        out_specs = pl.BlockSpec((bm, bn), lambda i, j: (i, j))
        out_shape = jax.ShapeDtypeStruct((M, N), out_dtype)
        sem = ("parallel", "arbitrary")
    return pl.pallas_call(
        _mm_kernel, name=name, grid=grid, in_specs=in_specs, out_specs=out_specs, out_shape=out_shape,
        compiler_params=_cparams(*sem),
    )(a, b)


def _ln_rows(x, g, b):
    mu = jnp.mean(x, -1, keepdims=True)
    xc = x - mu
    var = jnp.mean(xc * xc, -1, keepdims=True)
    return xc * lax.rsqrt(var + LN_EPS) * g + b


def _ln_in_kernel(xa_ref, xc_ref, g_ref, b_ref, xf_ref, xb_ref, *, na_blocks):
    i = pl.program_id(0)

    def emit(x_ref):
        y = _ln_rows(x_ref[...], g_ref[...], b_ref[...])
        xf_ref[...] = y
        xb_ref[...] = y.astype(jnp.bfloat16)

    pl.when(i < na_blocks)(lambda: emit(xa_ref))
    pl.when(i >= na_blocks)(lambda: emit(xc_ref))


def _ln_in(xa, xc, g, b, bm=512):
    (na, D), nc = xa.shape, xc.shape[0]
    nab, ncb = na // bm, nc // bm
    row = pl.BlockSpec((bm, D), lambda i: (i, 0))
    vec = pl.BlockSpec((1, D), lambda i: (0, 0))
    return pl.pallas_call(
        functools.partial(_ln_in_kernel, na_blocks=nab), name="ln_in", grid=(nab + ncb,),
        in_specs=[pl.BlockSpec((bm, D), lambda i: (jnp.minimum(i, nab - 1), 0)),
                  pl.BlockSpec((bm, D), lambda i: (jnp.maximum(i - nab, 0), 0)), vec, vec],
        out_specs=[row, row],
        out_shape=[jax.ShapeDtypeStruct((na + nc, D), jnp.float32), jax.ShapeDtypeStruct((na + nc, D), jnp.bfloat16)],
        compiler_params=_cparams("arbitrary"))(xa, xc, g.reshape(1, D), b.reshape(1, D))


def _merge_kernel(xb_ref, wg_ref, bg_ref, ya_ref, yb_ref, yc_ref, yd_ref, wb_ref, o_ref, wgb_ref, wbb_ref):
    @pl.when(pl.program_id(1) == 0)
    def _():
        wgb_ref[...] = wg_ref[...].astype(wgb_ref.dtype)
        wbb_ref[...] = wb_ref[...].astype(wbb_ref.dtype)

    xb = xb_ref[...]
    acc = None
    for i, y_ref in enumerate((ya_ref, yb_ref, yc_ref, yd_ref)):
        gate = jax.nn.sigmoid(jnp.dot(xb, wgb_ref[i], preferred_element_type=jnp.float32) + bg_ref[i])
        term = gate * jnp.dot(y_ref[...], wbb_ref[i], preferred_element_type=jnp.float32)
        acc = term if acc is None else acc + term
    o_ref[...] = acc.astype(o_ref.dtype)


def _merge(xb, wg, bg, ys, wb, layer, bm=1024, bn=256):
    n, D = xb.shape
    W = ys[0].shape[1]
    y_spec = pl.BlockSpec((bm, W), lambda j, i: (i, 0))
    return pl.pallas_call(
        _merge_kernel, name="branch_merge", grid=(D // bn, n // bm),
        in_specs=[pl.BlockSpec((bm, D), lambda j, i: (i, 0)),
                  pl.BlockSpec((None, N_BRANCH, D, bn), lambda j, i: (layer, 0, 0, j)),
                  pl.BlockSpec((N_BRANCH, 1, bn), lambda j, i: (0, 0, j)),
                  y_spec, y_spec, y_spec, y_spec,
                  pl.BlockSpec((None, N_BRANCH, W, bn), lambda j, i: (layer, 0, 0, j))],
        out_specs=pl.BlockSpec((bm, bn), lambda j, i: (i, j)),
        out_shape=jax.ShapeDtypeStruct((n, D), jnp.bfloat16),
        scratch_shapes=[pltpu.VMEM((N_BRANCH, D, bn), jnp.bfloat16), pltpu.VMEM((N_BRANCH, W, bn), jnp.bfloat16)],
        compiler_params=_cparams("arbitrary", "arbitrary"),
    )(xb, wg, bg.reshape(N_BRANCH, 1, D), *ys, wb)


OUTPROJ_SUB = 256


TOK_RADIX = 256
SIDE_MID, SIDE_LOW, SIDE_TOK = N_EXPERTS, 2 * N_EXPERTS, 3 * N_EXPERTS


def _outproj_kernel(m_ref, w_ref, x_ref, g_ref, b_ref, wr_ref, xf_ref, xb_ref, aff_ref):
    f32, bf16 = jnp.float32, jnp.bfloat16
    D = x_ref.shape[1]
    for s in range(m_ref.shape[0] // OUTPROJ_SUB):
        rows = pl.ds(s * OUTPROJ_SUB, OUTPROJ_SUB)
        r = jnp.dot(m_ref[rows, :], w_ref[...], preferred_element_type=f32)
        y = _ln_rows(DN_ALPHA * x_ref[rows, :] + r, g_ref[...], b_ref[...])
        yb = y.astype(bf16)
        xf_ref[rows, :] = y
        xb_ref[rows, :D] = yb
        logits = jnp.dot(yb, wr_ref[...], preferred_element_type=f32)
        lane = lax.broadcasted_iota(jnp.int32, logits.shape, 1)
        logits = jnp.where(lane < N_EXPERTS, logits, -jnp.inf)
        e = jnp.exp(logits - jnp.max(logits, -1, keepdims=True))
        aff = e / jnp.sum(e, -1, keepdims=True)
        aff_ref[rows, :] = aff
        hi = aff.astype(bf16).astype(f32)
        mid = (aff - hi).astype(bf16).astype(f32)
        low = (aff - hi - mid).astype(bf16).astype(f32)
        tok = (pl.program_id(0) * m_ref.shape[0] + s * OUTPROJ_SUB
               + lax.broadcasted_iota(jnp.int32, logits.shape, 0))
        side = hi + pltpu.roll(mid, SIDE_MID, axis=1) + pltpu.roll(low, SIDE_LOW, axis=1)
        side = jnp.where(lane == SIDE_TOK, (tok // TOK_RADIX).astype(f32),
                         jnp.where(lane == SIDE_TOK + 1, (tok % TOK_RADIX).astype(f32), side))
        xb_ref[rows, D:] = side.astype(bf16)


def _outproj(merged, w_out, x, g, b, w_router_p, bm=512):
    n, D = x.shape
    row = pl.BlockSpec((bm, D), lambda i: (i, 0))
    vec = pl.BlockSpec((1, D), lambda i: (0, 0))
    return pl.pallas_call(
        _outproj_kernel, name="outproj_ln_router", grid=(n // bm,),
        in_specs=[row, pl.BlockSpec((D, D), lambda i: (0, 0)), row, vec, vec,
                  pl.BlockSpec((D, LANE), lambda i: (0, 0))],
        out_specs=[row, pl.BlockSpec((bm, D + LANE), lambda i: (i, 0)), pl.BlockSpec((bm, LANE), lambda i: (i, 0))],
        out_shape=[jax.ShapeDtypeStruct((n, D), jnp.float32), jax.ShapeDtypeStruct((n, D + LANE), jnp.bfloat16),
                   jax.ShapeDtypeStruct((n, LANE), jnp.float32)],
        compiler_params=_cparams("parallel"),
    )(merged, w_out, x, g.reshape(1, D), b.reshape(1, D), w_router_p)


def _aff_t_kernel(a_ref, o_ref):
    for s in range(a_ref.shape[0] // LANE):
        o_ref[:, s, :] = a_ref[s * LANE:(s + 1) * LANE, :].T[:N_EXPERTS, :]


def _aff_chunks(aff, bm=1024):
    n = aff.shape[0]
    return pl.pallas_call(
        _aff_t_kernel, name="aff_chunks", grid=(n // bm,),
        in_specs=[pl.BlockSpec((bm, LANE), lambda i: (i, 0))],
        out_specs=pl.BlockSpec((N_EXPERTS, bm // LANE, LANE), lambda i: (0, i, 0)),
        out_shape=jax.ShapeDtypeStruct((N_EXPERTS, n // LANE, LANE), jnp.float32),
        compiler_params=_cparams("parallel"),
    )(aff)


def _log_sigmoid(z):
    return jnp.minimum(z, 0.0) - jnp.log(1.0 + jnp.exp(-jnp.abs(z)))


def _headnorm(o):
    mu = jnp.mean(o, -1, keepdims=True)
    oc = o - mu
    var = jnp.mean(oc * oc, -1, keepdims=True)
    return oc * lax.rsqrt(var + LN_EPS)


def _silu(x):
    return x * jax.nn.sigmoid(x)


def _chunk_cumsum(g, chunk, inclusive):
    row = lax.broadcasted_iota(jnp.int32, g.shape, 0) % chunk
    b = g
    s = 1
    while s < chunk:
        b = b + jnp.where(row >= s, pltpu.roll(b, s, axis=0), 0.0)
        s *= 2
    return b if inclusive else b - g


def _gla_kernel(flag_ref, qk_ref, v_ref, lr_ref, wup_ref, bup_ref, *rest, reverse, nblk):
    s_idx = pl.program_id(0)
    blk = nblk - 1 - s_idx if reverse else s_idx

    @pl.when(flag_ref[blk] == 1)
    def _():
        rest[-1][...] = jnp.zeros_like(rest[-1])

    for hh in range(GLA_HEADS):
        _gla_head(hh, qk_ref, v_ref, lr_ref, wup_ref, bup_ref, *rest, reverse=reverse)


def _gla_head(hh, qk_ref, v_ref, lr_ref, wup_ref, bup_ref, *rest, reverse):
    if reverse:
        gate_ref, of_ref, o_ref, st_ref = rest
    else:
        o_ref, st_ref = rest
    f32, bf16 = jnp.float32, jnp.bfloat16
    C = GLA_CHUNK
    TB = qk_ref.shape[0]
    nc = TB // C
    sl = slice(hh * LANE, (hh + 1) * LANE)

    is_q = lax.broadcasted_iota(jnp.int32, (1, LANE), 1) < GLA_DK
    qk = qk_ref[:, sl].astype(f32) * jnp.where(is_q, GLA_DK ** -0.5, 1.0)
    z = jnp.dot(lr_ref[...], wup_ref[hh], preferred_element_type=f32) + bup_ref[hh]
    g = _log_sigmoid(z) * (1.0 / GLA_TAU)
    cs = _chunk_cumsum(g, C, inclusive=not reverse)
    cs3 = cs.reshape(nc, C, LANE)
    mid = cs3[:, C // 2:C // 2 + 1, :]
    if reverse:
        tot3 = cs3[:, C - 1:C, :] + g.reshape(nc, C, LANE)[:, C - 1:C, :]
        edge3 = jnp.where(is_q, tot3 - cs3, cs3)
        mid3 = jnp.where(is_q, mid - cs3, cs3 - mid)
    else:
        tot3 = cs3[:, C - 1:C, :]
        edge3 = jnp.where(is_q, cs3, tot3 - cs3)
        mid3 = jnp.where(is_q, cs3 - mid, mid - cs3)
    x_edge = qk * jnp.exp(edge3).reshape(TB, LANE)
    x_mid = qk * jnp.exp(mid3).reshape(TB, LANE)
    qe = jnp.where(is_q, x_edge, 0.0).astype(bf16).reshape(nc, C, LANE)
    kd = jnp.where(is_q, pltpu.roll(x_edge, GLA_DK, axis=1), 0.0).astype(bf16).reshape(nc, C, LANE)
    qm = jnp.where(is_q, x_mid, 0.0).astype(bf16).reshape(nc, C, LANE)
    km = pltpu.roll(x_mid, GLA_DK, axis=1).astype(bf16).reshape(nc, C, LANE)
    etot = jnp.exp(tot3)
    v = v_ref[:, sl]
    ri = lax.broadcasted_iota(jnp.int32, (C, C), 0)
    ci = lax.broadcasted_iota(jnp.int32, (C, C), 1)
    keep = (ci > ri) if reverse else (ci <= ri)

    st = st_ref[hh]
    outs = [None] * nc
    for c in (range(nc - 1, -1, -1) if reverse else range(nc)):
        vc = v[c * C:(c + 1) * C, :]
        att = lax.dot_general(qm[c], km[c], (((1,), (1,)), ((), ())), preferred_element_type=f32)
        att = jnp.where(keep, att, 0.0).astype(bf16)
        o_c = jnp.dot(att, vc, preferred_element_type=f32)
        o_c = o_c + lax.dot_general(qe[c], st.astype(bf16), (((1,), (1,)), ((), ())), preferred_element_type=f32)
        upd = lax.dot_general(vc, kd[c], (((0,), (0,)), ((), ())), preferred_element_type=f32)
        st = st * etot[c] + upd
        outs[c] = o_c
    st_ref[hh] = st
    o = jnp.concatenate(outs, axis=0)
    if reverse:
        o = _headnorm(o + of_ref[:, sl])
        o_ref[:, sl] = (o * _silu(gate_ref[:, sl].astype(f32))).astype(o_ref.dtype)
    else:
        o_ref[:, sl] = o


def _gla(h, seq_first, seq_last, wup, bup):
    n = h.shape[0]
    TB = SEQ_BLOCK
    nblk = n // TB
    H = GLA_HEADS

    def col(off):
        return off // LANE

    HW = H * LANE

    def specs(reverse):
        blk = (lambda s: nblk - 1 - s) if reverse else (lambda s: s)
        sp = [pl.BlockSpec((TB, HW), lambda s, f: (blk(s), QK_OFF // HW)),
              pl.BlockSpec((TB, HW), lambda s, f: (blk(s), AV_OFF // HW)),
              pl.BlockSpec((TB, LANE), lambda s, f: (blk(s), col(LR_OFF))),
              pl.BlockSpec((None, H, LANE, LANE), lambda s, f: (1 if reverse else 0, 0, 0, 0)),
              pl.BlockSpec((None, H, 1, LANE), lambda s, f: (1 if reverse else 0, 0, 0, 0))]
        out = pl.BlockSpec((TB, HW), lambda s, f: (blk(s), 0))
        return sp, out, blk

    scratch = [pltpu.VMEM((H, GLA_DV, LANE), jnp.float32)]
    sp, out, _ = specs(False)
    o_f = pl.pallas_call(
        functools.partial(_gla_kernel, reverse=False, nblk=nblk), name="gla_fwd",
        grid_spec=pltpu.PrefetchScalarGridSpec(num_scalar_prefetch=1, grid=(nblk,), in_specs=sp, out_specs=out,
                                               scratch_shapes=scratch),
        out_shape=jax.ShapeDtypeStruct((n, H * GLA_DV), jnp.float32),
        compiler_params=_cparams("arbitrary"),
    )(seq_first, h, h, h, wup, bup)
    sp, out, blk = specs(True)
    sp = sp + [pl.BlockSpec((TB, HW), lambda s, f: (blk(s), AG_OFF // HW)), out]
    return pl.pallas_call(
        functools.partial(_gla_kernel, reverse=True, nblk=nblk), name="gla_bwd",
        grid_spec=pltpu.PrefetchScalarGridSpec(num_scalar_prefetch=1, grid=(nblk,), in_specs=sp, out_specs=out,
                                               scratch_shapes=scratch),
        out_shape=jax.ShapeDtypeStruct((n, H * GLA_DV), jnp.bfloat16),
        compiler_params=_cparams("arbitrary"),
    )(seq_last, h, h, h, wup, bup, h, o_f)


def _pack_gla_up(w_up, b_up):
    H, dk, r = GLA_HEADS, GLA_DK, GLA_RANK
    assert 2 * dk == LANE
    w = w_up.reshape(2, r, H, dk).transpose(0, 2, 1, 3)
    wp = jnp.zeros((2, H, LANE, dk), jnp.float32)
    wp = wp.at[0, :, 0:r].set(w[0]).at[1, :, r:2 * r].set(w[1])
    bp = b_up.reshape(2, H, 1, dk).astype(jnp.float32)
    return jnp.concatenate([wp, wp], axis=-1).astype(jnp.bfloat16), jnp.concatenate([bp, bp], axis=-1)


def _rope_tables(pos, dim, theta, width):
    half = dim // 2
    pos = np.asarray(pos, np.float64)
    T = pos.shape[0]
    inv = 1.0 / (float(theta) ** (np.arange(half, dtype=np.float64) / half))
    ang = pos[:, None] * inv[None, :]
    c, s = np.cos(ang), np.sin(ang)
    rest = width - dim
    cos = np.concatenate([c, c, np.ones((T, rest))], axis=1)
    sin_lo = np.concatenate([-s, np.zeros((T, half + rest))], axis=1)
    sin_hi = np.concatenate([np.zeros((T, half)), s, np.zeros((T, rest))], axis=1)
    return tuple(jnp.asarray(a, jnp.float32) for a in (cos, sin_lo, sin_hi))


def _ret_kernel(flag_ref, pos_ref, q_ref, k_ref, v_ref, cos_ref, sin_ref, dec_ref, *rest, reverse, nblk):
    s_idx = pl.program_id(0)
    blk = nblk - 1 - s_idx if reverse else s_idx

    @pl.when(flag_ref[blk] == 1)
    def _():
        st_ref = rest[-1] if reverse else rest[-2]
        st_ref[...] = jnp.zeros_like(st_ref)

    for hh in range(RET_HEADS):
        _ret_head(hh, s_idx, q_ref, k_ref, v_ref, cos_ref, sin_ref, dec_ref, *rest, reverse=reverse)


def _ret_head(hh, s_idx, q_ref, k_ref, v_ref, cos_ref, sin_ref, dec_ref, *rest, reverse):
    if reverse:
        gate_ref, o1_ref, o_ref, st_ref = rest
    else:
        o_ref, st_ref, dm_ref = rest
    f32, bf16 = jnp.float32, jnp.bfloat16
    TB = q_ref.shape[0]
    sl = slice(hh * RET_DH, (hh + 1) * RET_DH)

    dec = dec_ref[hh]
    lg = -(jnp.maximum(dec, 0.0) + jnp.log1p(jnp.exp(-jnp.abs(dec))))
    lg_f, lg_b = lg[0, 0:1, 0:1], lg[1, 0:1, 0:1]

    cos, sin = cos_ref[...], sin_ref[...]

    def rot(x):
        return x * cos + pltpu.roll(x, RET_DH // 2, axis=1) * sin

    qr = rot(q_ref[:, sl].astype(f32))
    kr = rot(k_ref[:, sl].astype(f32)) * (RET_DH ** -0.5)
    v = v_ref[:, sl]
    idx = lax.broadcasted_iota(jnp.int32, (TB, 1), 0).astype(f32)
    st = st_ref[hh]
    cross = jnp.dot(qr.astype(bf16), st.astype(bf16), preferred_element_type=f32)
    if reverse:
        cross = cross * jnp.exp(lg_b * (TB - idx))
        kz = (kr * jnp.exp(lg_b * idx)).astype(bf16)
        st_ref[hh] = st * jnp.exp(lg_b * TB) + lax.dot_general(kz, v, (((0,), (0,)), ((), ())),
                                                               preferred_element_type=f32)
        o = _headnorm(o1_ref[:, sl] + cross)
        o_ref[:, sl] = (o * _silu(gate_ref[:, sl].astype(f32))).astype(o_ref.dtype)
    else:
        @pl.when(s_idx == 0)
        def _():
            ri = lax.broadcasted_iota(jnp.int32, (TB, TB), 0)
            ci = lax.broadcasted_iota(jnp.int32, (TB, TB), 1)
            d = (ri - ci).astype(f32)
            dm_ref[hh] = jnp.where(d >= 0, jnp.exp(lg_f * jnp.maximum(d, 0.0)), jnp.exp(lg_b * jnp.maximum(-d, 0.0)))

        cross = cross * jnp.exp(lg_f * (idx + 1.0))
        kz = (kr * jnp.exp(lg_f * (TB - 1.0 - idx))).astype(bf16)
        st_ref[hh] = st * jnp.exp(lg_f * TB) + lax.dot_general(kz, v, (((0,), (0,)), ((), ())),
                                                               preferred_element_type=f32)
        sc = lax.dot_general(qr.astype(bf16), kr.astype(bf16), (((1,), (1,)), ((), ())), preferred_element_type=f32)
        sc = (sc * dm_ref[hh]).astype(bf16)
        o_ref[:, sl] = jnp.dot(sc, v, preferred_element_type=f32) + cross


def _retention(h, seq_first, seq_last, pos_blk, decay, t_max):
    n = h.shape[0]
    TB = SEQ_BLOCK
    nblk = n // TB
    H, dh = RET_HEADS, RET_DH
    cos, sin_lo, sin_hi = _rope_tables(np.arange(t_max), dh, RET_THETA, dh)
    sin = sin_lo + sin_hi
    dec = jnp.broadcast_to(decay.astype(jnp.float32).T[:, :, None, None], (H, 2, 8, LANE))

    HW = H * dh

    def specs(reverse):
        blk = (lambda s: nblk - 1 - s) if reverse else (lambda s: s)
        tok = lambda off: pl.BlockSpec((TB, HW), lambda s, f, p: (blk(s), off // HW))
        tab = pl.BlockSpec((TB, LANE), lambda s, f, p: (p[blk(s)], 0))
        sp = [tok(DQ_OFF), tok(DK_OFF), tok(DV_OFF), tab, tab,
              pl.BlockSpec((H, 2, 8, LANE), lambda s, f, p: (0, 0, 0, 0))]
        return sp, tok, pl.BlockSpec((TB, HW), lambda s, f, p: (blk(s), 0))

    sp, _, out = specs(False)
    o1 = pl.pallas_call(
        functools.partial(_ret_kernel, reverse=False, nblk=nblk), name="ret_fwd",
        grid_spec=pltpu.PrefetchScalarGridSpec(
            num_scalar_prefetch=2, grid=(nblk,), in_specs=sp, out_specs=out,
            scratch_shapes=[pltpu.VMEM((H, dh, dh), jnp.float32), pltpu.VMEM((H, TB, TB), jnp.float32)]),
        out_shape=jax.ShapeDtypeStruct((n, H * dh), jnp.float32),
        compiler_params=_cparams("arbitrary"),
    )(seq_first, pos_blk, h, h, h, cos, sin, dec)
    sp, tok, out = specs(True)
    sp = sp + [tok(DG_OFF), out]
    return pl.pallas_call(
        functools.partial(_ret_kernel, reverse=True, nblk=nblk), name="ret_bwd",
        grid_spec=pltpu.PrefetchScalarGridSpec(
            num_scalar_prefetch=2, grid=(nblk,), in_specs=sp, out_specs=out,
            scratch_shapes=[pltpu.VMEM((H, dh, dh), jnp.float32)]),
        out_shape=jax.ShapeDtypeStruct((n, H * dh), jnp.bfloat16),
        compiler_params=_cparams("arbitrary"),
    )(seq_last, pos_blk, h, h, h, cos, sin, dec, h, o1)


SWA_Q = 128
SWA_SUBS = 4
SWA_HALF = 64
assert all(w // (2 * d) == SWA_HALF for w, d in SWA_GROUPS)
NEG_BIG = -1e30


def _swa_kernel(join_ref, q_ref, kl_ref, kc_ref, kr_ref, vl_ref, vc_ref, vr_ref,
                qcos_ref, qs1_ref, qs2_ref, kcos_ref, ks1_ref, ks2_ref, o_ref, lse_ref):
    f32, bf16 = jnp.float32, jnp.bfloat16
    j = pl.program_id(1)
    Q, HW = SWA_Q, SWA_HALF
    W = Q + 2 * HW
    ri = lax.broadcasted_iota(jnp.int32, (Q, W), 0)
    ci = lax.broadcasted_iota(jnp.int32, (Q, W), 1)
    rel = ci - HW - ri
    band = (rel <= HW) & (rel >= -HW)
    joins = [join_ref[j * (SWA_SUBS + 1) + u] == 1 for u in range(SWA_SUBS + 1)]
    oks = [band & ((ci >= HW) | joins[u]) & ((ci < HW + Q) | joins[u + 1]) for u in range(SWA_SUBS)]
    lane = lax.broadcasted_iota(jnp.int32, (Q, LANE), 1)

    def rot(x, c_ref, s1_ref, s2_ref):
        return (x * c_ref[...] + pltpu.roll(x, LANE - ROPE_DIM // 2, axis=1) * s1_ref[...]
                + pltpu.roll(x, ROPE_DIM // 2, axis=1) * s2_ref[...])

    lse_tiles = [jnp.zeros((Q, LANE), f32) for _ in range(SWA_SUBS)]
    for hh in range(SWA_HEADS):
        sl = slice(hh * SWA_DH, (hh + 1) * SWA_DH)
        q = (rot(q_ref[:, sl].astype(f32), qcos_ref, qs1_ref, qs2_ref) * (SWA_DH ** -0.5)).astype(bf16)
        k = jnp.concatenate([kl_ref[:, sl], kc_ref[:, sl], kr_ref[:, sl]], axis=0).astype(f32)
        k = rot(k, kcos_ref, ks1_ref, ks2_ref).astype(bf16)
        v = jnp.concatenate([vl_ref[:, sl], vc_ref[:, sl], vr_ref[:, sl]], axis=0)
        for u in range(SWA_SUBS):
            s = lax.dot_general(q[u * Q:(u + 1) * Q], k[u * Q:u * Q + W], (((1,), (1,)), ((), ())),
                                preferred_element_type=f32)
            s = jnp.where(oks[u], s, NEG_BIG)
            m = jnp.max(s, -1, keepdims=True)
            p = jnp.exp(s - m)
            den = jnp.sum(p, -1, keepdims=True)
            o = jnp.dot(p.astype(bf16), v[u * Q:u * Q + W], preferred_element_type=f32) / den
            o_ref[u * Q:(u + 1) * Q, sl] = o
            lse_tiles[u] = jnp.where(lane == hh, m + jnp.log(den), lse_tiles[u])
    for u in range(SWA_SUBS):
        lse_ref[u * Q:(u + 1) * Q, :] = lse_tiles[u]


def _swa_merge_kernel(*refs):
    ng = len(SWA_GROUPS)
    o_refs, l_refs, y_ref = refs[:ng], refs[ng:2 * ng], refs[2 * ng]
    scratch = refs[2 * ng + 1:]
    assert SWA_DH == LANE
    os_, ls_ = [], []
    si = 0
    for gi, (_, dil) in enumerate(SWA_GROUPS):
        if dil == 1:
            os_.append(lambda hh, r=o_refs[gi]: r[0, :, hh * LANE:(hh + 1) * LANE])
            ls_.append(l_refs[gi].at[0])
            continue
        o_s, l_s = scratch[si], scratch[si + 1]
        si += 2
        rows = l_s.shape[0] // dil
        for r in range(dil):
            for hh in range(SWA_HEADS):
                o_s[hh, pl.ds(r, rows, stride=dil), :] = o_refs[gi][r, :, hh * LANE:(hh + 1) * LANE]
            l_s[pl.ds(r, rows, stride=dil), :] = l_refs[gi][r]
        os_.append(lambda hh, s=o_s: s[hh])
        ls_.append(l_s)
    for hh in range(SWA_HEADS):
        ls = [r[:, hh:hh + 1] for r in ls_]
        m = functools.reduce(jnp.maximum, ls)
        ws = [jnp.exp(l - m) for l in ls]
        tot = functools.reduce(lambda a, b: a + b, ws)
        y = functools.reduce(lambda a, b: a + b, [w * o(hh) for w, o in zip(ws, os_)])
        y_ref[:, hh * LANE:(hh + 1) * LANE] = (y / tot).astype(y_ref.dtype)


def _proj_strided_kernel(a_ref, b_ref, o_ref, acc_ref, *, dil):
    acc = jnp.dot(a_ref[...], b_ref[...], preferred_element_type=jnp.float32)
    if dil == 1:
        o_ref[0] = acc.astype(o_ref.dtype)
        return
    rows = acc_ref.shape[1] // dil
    for s in range(acc_ref.shape[0]):
        acc_ref[s] = acc[:, s * LANE:(s + 1) * LANE]
        for r in range(dil):
            o_ref[r, :, s * LANE:(s + 1) * LANE] = acc_ref[s, pl.ds(r, rows, stride=dil), :].astype(o_ref.dtype)


def _proj_strided(a, b, dil, bm=1024, bn=512):
    n, K = a.shape
    N = b.shape[1]
    return pl.pallas_call(
        functools.partial(_proj_strided_kernel, dil=dil), name=f"swa_proj_dil{dil}", grid=(n // bm, N // bn),
        in_specs=[pl.BlockSpec((bm, K), lambda i, j: (i, 0)), pl.BlockSpec((K, bn), lambda i, j: (0, j))],
        out_specs=pl.BlockSpec((dil, bm // dil, bn), lambda i, j: (0, i, j)),
        out_shape=jax.ShapeDtypeStruct((dil, n // dil, N), jnp.bfloat16),
        scratch_shapes=[pltpu.VMEM((bn // LANE, bm, LANE), jnp.float32)],
        compiler_params=_cparams("parallel", "arbitrary"),
    )(a, b)


def _swa(xb, w_groups, seqs):
    n = xb.shape[0]
    Q, HW = SWA_Q, SWA_HALF
    GW = SWA_HEADS * SWA_DH
    outs, lses = [], []
    for gi, (window, dil) in enumerate(SWA_GROUPS):
        hs = _proj_strided(xb, w_groups[gi], dil)
        rows = n // dil
        QB = SWA_SUBS * Q
        nblk = rows // QB
        has_l, has_r = [], []
        for num, T in seqs:
            per = T // dil // Q
            assert per * Q * dil == T
            has_l += ([0] + [1] * (per - 1)) * num
            has_r += ([1] * (per - 1) + [0]) * num
        assert len(has_l) == nblk * SWA_SUBS
        joins = np.array(has_l, np.int32).reshape(nblk, SWA_SUBS)
        joins = np.concatenate([joins, np.array(has_r, np.int32).reshape(nblk, SWA_SUBS)[:, -1:]], axis=1)
        joins = jnp.asarray(joins.reshape(-1))
        qpos = np.arange(QB) * dil
        kpos = (np.arange(QB + 2 * HW) - HW) * dil
        qt = _rope_tables(qpos, ROPE_DIM, ROPE_THETA, SWA_DH)
        kt = _rope_tables(kpos, ROPE_DIM, ROPE_THETA, SWA_DH)
        per64 = QB // HW
        last64 = rows // HW - 1

        def center(c):
            return pl.BlockSpec((None, QB, GW), lambda r, j, *_: (r, j, c))

        def left(c):
            return pl.BlockSpec((None, HW, GW), lambda r, j, *_: (r, jnp.maximum(per64 * j - 1, 0), c))

        def right(c):
            return pl.BlockSpec((None, HW, GW), lambda r, j, *_: (r, jnp.minimum(per64 * (j + 1), last64), c))

        qtab = pl.BlockSpec((QB, SWA_DH), lambda r, j, *_: (0, 0))
        ktab = pl.BlockSpec((QB + 2 * HW, SWA_DH), lambda r, j, *_: (0, 0))
        o_g, lse_g = pl.pallas_call(
            _swa_kernel, name=f"swa_dil{dil}",
            grid_spec=pltpu.PrefetchScalarGridSpec(
                num_scalar_prefetch=1, grid=(dil, nblk),
                in_specs=[center(0), left(1), center(1), right(1), left(2), center(2), right(2),
                          qtab, qtab, qtab, ktab, ktab, ktab],
                out_specs=[pl.BlockSpec((None, QB, GW), lambda r, j, *_: (r, j, 0)),
                           pl.BlockSpec((None, QB, LANE), lambda r, j, *_: (r, j, 0))]),
            out_shape=[jax.ShapeDtypeStruct((dil, rows, GW), jnp.float32),
                       jax.ShapeDtypeStruct((dil, rows, LANE), jnp.float32)],
            compiler_params=_cparams("parallel", "parallel"),
        )(joins, hs, hs, hs, hs, hs, hs, hs, *qt, *kt)
        outs.append(o_g)
        lses.append(lse_g)
    bm = 1024
    dils = [d for _, d in SWA_GROUPS]
    osp = [pl.BlockSpec((d, bm // d, GW), lambda i: (0, i, 0)) for d in dils]
    lsp = [pl.BlockSpec((d, bm // d, LANE), lambda i: (0, i, 0)) for d in dils]
    scratch = []
    for d in dils:
        if d > 1:
            scratch += [pltpu.VMEM((SWA_HEADS, bm, LANE), jnp.float32), pltpu.VMEM((bm, LANE), jnp.float32)]
    return pl.pallas_call(
        _swa_merge_kernel, name="swa_merge", grid=(n // bm,), in_specs=osp + lsp,
        out_specs=pl.BlockSpec((bm, GW), lambda i: (i, 0)),
        out_shape=jax.ShapeDtypeStruct((n, GW), jnp.bfloat16), scratch_shapes=scratch,
        compiler_params=_cparams("parallel"),
    )(*outs, *lses)


FFT_T2 = 128
FW = FNET_GROUPS * FNET_DG


def _dft_cos_sin(n_out, n_in, period):
    ang = (2.0 * np.pi / period) * np.mod(np.outer(np.arange(n_out), np.arange(n_in)), period)
    return np.cos(ang), np.sin(ang)


def _fnet_stage1_kernel(z_ref, m_ref, a_ref, *, cols):
    mat = m_ref[...]
    for c in range(cols):
        zr = z_ref[:, c * 2 * FW:c * 2 * FW + FW]
        zi = z_ref[:, c * 2 * FW + FW:(c + 1) * 2 * FW]
        a = jnp.dot(mat, jnp.concatenate([zr, zi], axis=0), preferred_element_type=jnp.float32)
        t1 = zr.shape[0]
        a_ref[:, c * 2 * FW:c * 2 * FW + FW] = a[:t1].astype(a_ref.dtype)
        a_ref[:, c * 2 * FW + FW:(c + 1) * 2 * FW] = a[t1:].astype(a_ref.dtype)


FNET_KB = 8


def _fnet_stage2_kernel(a_ref, g_ref, y_ref, *, scale):
    kw = g_ref.shape[1] // FNET_KB
    for j in range(FNET_KB):
        y = jnp.dot(g_ref[:, j * kw:(j + 1) * kw], a_ref[j], preferred_element_type=jnp.float32)
        y_ref[:, j * FW:(j + 1) * FW] = (y * scale).astype(y_ref.dtype)


def _fnet(h, seqs):
    n = h.shape[0]
    bf16 = jnp.bfloat16
    T2 = FFT_T2
    cc, ss = _dft_cos_sin(FNET_DG, FNET_DG, FNET_DG)
    eye = np.eye(FNET_GROUPS)
    wc = jnp.asarray(np.concatenate([np.kron(eye, cc), -np.kron(eye, ss)], axis=1), bf16)
    z = _matmul(h, wc, bf16, bn=2 * FW, a_col_blk=C_OFF // FW, name="fnet_channel_dft")
    zz = z.reshape(n // T2, T2 * 2 * FW)
    outs = []
    row0 = 0
    for num, T in seqs:
        T1 = T // T2
        assert T1 * T2 == T and T1 % 16 == 0
        c1, s1 = _dft_cos_sin(T1, T1, T1)
        mat = jnp.asarray(np.block([[c1, s1], [-s1, c1]]), bf16)
        cols = min(T2, max(1, 1024 // T1))
        rb0 = row0 // T1
        a = pl.pallas_call(
            functools.partial(_fnet_stage1_kernel, cols=cols), name=f"fnet_dft_t1_{T1}", grid=(num, T2 // cols),
            in_specs=[pl.BlockSpec((T1, cols * 2 * FW), lambda b, j: (rb0 + b, j)),
                      pl.BlockSpec((2 * T1, 2 * T1), lambda b, j: (0, 0))],
            out_specs=pl.BlockSpec((T1, cols * 2 * FW), lambda b, j: (b, j)),
            out_shape=jax.ShapeDtypeStruct((num * T1, T2 * 2 * FW), bf16),
            compiler_params=_cparams("parallel", "parallel"),
        )(zz, mat)
        ec, es = _dft_cos_sin(T, T2, T)
        g = jnp.asarray(np.stack([ec, es], axis=-1).reshape(T2, T1 * 2 * T2), bf16)
        y = pl.pallas_call(
            functools.partial(_fnet_stage2_kernel, scale=float(1.0 / np.sqrt(T * FNET_DG))), name=f"fnet_dft_t2_{T1}",
            grid=(num, T1 // FNET_KB),
            in_specs=[pl.BlockSpec((FNET_KB, 2 * T2, FW), lambda b, k1: (b * (T1 // FNET_KB) + k1, 0, 0)),
                      pl.BlockSpec((T2, FNET_KB * 2 * T2), lambda b, k1: (0, k1))],
            out_specs=pl.BlockSpec((T2, FNET_KB * FW), lambda b, k1: (b, k1)),
            out_shape=jax.ShapeDtypeStruct((num * T2, T1 * FW), bf16),
            compiler_params=_cparams("parallel", "parallel"),
        )(a.reshape(num * T1, 2 * T2, FW), g)
        outs.append(y.reshape(num * T, FW))
        row0 += num * T1
    return jnp.concatenate(outs, axis=0)


def _ffn_up_kernel(x_ref, w1_ref, w3_ref, h_ref, w1b_ref, w3b_ref):
    @pl.when(pl.program_id(2) == 0)
    def _():
        w1b_ref[...] = w1_ref[...].astype(w1b_ref.dtype)
        w3b_ref[...] = w3_ref[...].astype(w3b_ref.dtype)

    x = x_ref[...]
    h1 = jnp.dot(x, w1b_ref[...], preferred_element_type=jnp.float32)
    h3 = jnp.dot(x, w3b_ref[...], preferred_element_type=jnp.float32)
    h_ref[...] = (_silu(h1) * h3).astype(h_ref.dtype)


def _ffn_down_kernel(h_ref, w2_ref, s_ref, o_ref, w2b_ref, *, e0):
    e = pl.program_id(0) + e0

    @pl.when(pl.program_id(1) == 0)
    def _():
        w2b_ref[...] = w2_ref[...].astype(w2b_ref.dtype)

    D = w2_ref.shape[1]
    side = s_ref[...]
    lane = lax.broadcasted_iota(jnp.int32, side.shape, 1)
    mine = (lane == e) | (lane == e + SIDE_MID) | (lane == e + SIDE_LOW)
    gate = jnp.sum(jnp.where(mine, side.astype(jnp.float32), 0.0), axis=1, keepdims=True)
    y = jnp.dot(h_ref[...], w2b_ref[...], preferred_element_type=jnp.float32)
    o_ref[:, :D] = (y * gate).astype(o_ref.dtype)
    o_ref[:, D:] = side


def _expert_ffn(xe, w1, w3, w2, layer, e0, bm=1024, bf=512):
    E, R, DY = xe.shape
    D = DY - LANE
    F = w1.shape[-1]
    wspec = pl.BlockSpec((None, None, D, bf), lambda e, f, i: (layer, e0 + e, 0, f))
    hmid = pl.pallas_call(
        _ffn_up_kernel, name="expert_ffn_up", grid=(E, F // bf, R // bm),
        in_specs=[pl.BlockSpec((None, bm, D), lambda e, f, i: (e, i, 0)), wspec, wspec],
        out_specs=pl.BlockSpec((None, bm, bf), lambda e, f, i: (e, i, f)),
        out_shape=jax.ShapeDtypeStruct((E, R, F), jnp.bfloat16),
        scratch_shapes=[pltpu.VMEM((D, bf), jnp.bfloat16), pltpu.VMEM((D, bf), jnp.bfloat16)],
        compiler_params=_cparams("arbitrary", "arbitrary", "arbitrary"),
    )(xe, w1, w3)
    return pl.pallas_call(
        functools.partial(_ffn_down_kernel, e0=e0), name="expert_ffn_down", grid=(E, R // bm),
        in_specs=[pl.BlockSpec((None, bm, F), lambda e, i: (e, i, 0)),
                  pl.BlockSpec((None, None, F, D), lambda e, i: (layer, e0 + e, 0, 0), pipeline_mode=pl.Buffered(1)),
                  pl.BlockSpec((None, bm, LANE), lambda e, i: (e, i, D // LANE))],
        out_specs=pl.BlockSpec((None, bm, D + LANE), lambda e, i: (e, i, 0)),
        out_shape=jax.ShapeDtypeStruct((E, R, D + LANE), jnp.bfloat16),
        scratch_shapes=[pltpu.VMEM((F, D), jnp.bfloat16)],
        compiler_params=_cparams("arbitrary", "arbitrary"),
    )(hmid, w2, xe)


COMB_BM = 128
COMB_G = 16
COMB_K = 256
MOE_PARTS = 4


def _combine_kernel(lo_ref, x_ref, g_ref, b_ref, *rest, nblk_all, blk_off, want_bf16, parts):
    ye_parts, rest = rest[:parts], rest[parts:]
    if want_bf16:
        xf_ref, xb_ref, buf_ref, acc_ref, sem = rest
    else:
        xf_ref, buf_ref, acc_ref, sem = rest
    f32, bf16 = jnp.float32, jnp.bfloat16
    per = ye_parts[0].shape[0]
    E = per * parts
    D = x_ref.shape[1]
    BM, G, KC = COMB_BM, COMB_G, COMB_K
    i = pl.program_id(0)
    nsteps = pl.num_programs(0)
    blk = i + blk_off
    slot = i % 2

    @pl.when(i == 0)
    def _():
        buf_ref[...] = jnp.zeros_like(buf_ref)

    def granule_copy(sl, e, src_row, dst_row):
        return pltpu.make_async_copy(ye_parts[e // per].at[e % per, pl.ds(src_row, G)],
                                     buf_ref.at[sl, pl.ds(dst_row, G)], sem.at[sl])

    def gather_block(b, sl, start):
        off = jnp.int32(0)
        for e in range(E):
            lo = lo_ref[e * (nblk_all + 1) + b]
            hi = lo_ref[e * (nblk_all + 1) + b + 1]
            lo_al = (lo // G) * G
            ng = jnp.where(hi > lo, (hi - lo_al + G - 1) // G, 0)
            if start:
                def issue(g, carry, e=e, lo_al=lo_al, off=off):
                    granule_copy(sl, e, pl.multiple_of(lo_al + g * G, G), pl.multiple_of(off + g * G, G)).start()
                    return carry

                lax.fori_loop(0, ng, issue, 0)
            off = off + ng * G
        return off

    @pl.when(i == 0)
    def _():
        gather_block(blk, slot, True)

    @pl.when(i + 1 < nsteps)
    def _():
        gather_block(blk + 1, 1 - slot, True)

    off = gather_block(blk, slot, False)

    def wait_one(g, carry):
        granule_copy(slot, 0, 0, 0).wait()
        return carry

    lax.fori_loop(0, off // G, wait_one, 0)

    acc_ref[...] = jnp.zeros_like(acc_ref)
    t0 = (blk * BM).astype(f32)
    lane_tok = lax.broadcasted_iota(jnp.int32, (KC, BM), 1).astype(f32)
    row = lax.broadcasted_iota(jnp.int32, (KC, 1), 0)

    def chunk(c, carry):
        win = buf_ref[slot, pl.ds(pl.multiple_of(c * KC, KC), KC), :]
        tok = (win[:, D + SIDE_TOK:D + SIDE_TOK + 1].astype(f32) * TOK_RADIX
               + win[:, D + SIDE_TOK + 1:D + SIDE_TOK + 2].astype(f32))
        tok = jnp.where(row + c * KC < off, tok, -1.0)
        onehot_t = (tok - t0 == lane_tok).astype(bf16)
        acc_ref[...] += lax.dot_general(onehot_t, win[:, :D], (((0,), (0,)), ((), ())), preferred_element_type=f32)
        return carry

    lax.fori_loop(0, (off + KC - 1) // KC, chunk, 0)
    y = _ln_rows(DN_ALPHA * x_ref[...] + acc_ref[...], g_ref[...], b_ref[...])
    xf_ref[...] = y
    if want_bf16:
        xb_ref[...] = y.astype(bf16)


def _combine_ln(x, ye, lo, g, b, row_off, rows, want_bf16):
    n, D = x.shape
    per, R, DY = ye[0].shape
    E = per * len(ye)
    BM = COMB_BM
    nblk_all = n // BM
    ob = row_off // BM
    kmax = -(-E * (BM + COMB_G) // COMB_K) * COMB_K
    irow = pl.BlockSpec((BM, D), lambda i, lo: (i + ob, 0))
    orow = pl.BlockSpec((BM, D), lambda i, lo: (i, 0))
    vec = pl.BlockSpec((1, D), lambda i, lo: (0, 0))
    out_shape = [jax.ShapeDtypeStruct((rows, D), jnp.float32)]
    out_specs = [orow]
    if want_bf16:
        out_shape.append(jax.ShapeDtypeStruct((rows, D), jnp.bfloat16))
        out_specs.append(orow)
    return pl.pallas_call(
        functools.partial(_combine_kernel, nblk_all=nblk_all, blk_off=ob, want_bf16=want_bf16, parts=len(ye)),
        name="moe_combine_ln",
        grid_spec=pltpu.PrefetchScalarGridSpec(
            num_scalar_prefetch=1, grid=(rows // BM,),
            in_specs=[irow, vec, vec] + [pl.BlockSpec(memory_space=pl.ANY)] * len(ye), out_specs=out_specs,
            scratch_shapes=[pltpu.VMEM((2, kmax, DY), jnp.bfloat16), pltpu.VMEM((BM, D), jnp.float32),
                            pltpu.SemaphoreType.DMA((2,))]),
        out_shape=out_shape,
        compiler_params=_cparams("arbitrary"),
    )(lo, x, g.reshape(1, D), b.reshape(1, D), *ye)


def _route_kernel(all_ref, a_ref, idx_ref, cprev_ref, thr_ref, need_ref, *, cap, tok_off):
    f32, bf16, i32 = jnp.float32, jnp.bfloat16, jnp.int32
    e = pl.program_id(0)

    def total(x):
        return jnp.sum(jnp.sum(x, axis=2, keepdims=True), axis=1, keepdims=True)

    @pl.when(e == 0)
    def _():
        bits_all = pltpu.bitcast(all_ref[...], i32)

        def refine(i, prefix):
            cand = prefix | jnp.left_shift(jnp.int32(1), 30 - i)
            return jnp.where(total((bits_all >= cand).astype(i32)) >= cap, cand, prefix)

        thr_all = lax.fori_loop(0, 31, refine, jnp.zeros((all_ref.shape[0], 1, 1), i32))
        thr_ref[...] = jnp.broadcast_to(thr_all, thr_ref.shape)
        need_ref[...] = jnp.broadcast_to(cap - total((bits_all > thr_all).astype(i32)), need_ref.shape)

    a = a_ref[...]
    R = a.shape[0]
    bits = pltpu.bitcast(a, i32)
    thr = thr_ref[e][0:1, 0:1]
    need = need_ref[e][0:1, 0:1]
    gt, eq = bits > thr, bits == thr

    li = lax.broadcasted_iota(i32, (LANE, LANE), 0)
    lj = lax.broadcasted_iota(i32, (LANE, LANE), 1)
    upto = (li <= lj).astype(bf16)
    ri = lax.broadcasted_iota(i32, (R, R), 0)
    rj = lax.broadcasted_iota(i32, (R, R), 1)
    before = (rj < ri).astype(bf16)

    def counts(mask):
        inc = jnp.dot(mask.astype(bf16), upto, preferred_element_type=f32)
        tot = jnp.broadcast_to(inc[:, LANE - 1:LANE], (R, LANE)).astype(bf16)
        return inc, jnp.dot(before, tot, preferred_element_type=f32)

    eq_inc, eq_prev = counts(eq)
    sel = gt | (eq & (eq_prev + eq_inc - 1.0 < need.astype(f32)))
    cl, cprev = counts(sel)
    cprev_ref[...] = cprev[:, :1].astype(i32)

    selb = sel.astype(bf16)
    c_upto = jnp.sum(lax.dot_general(selb, (ri <= rj).astype(bf16), (((0,), (0,)), ((), ())),
                                     preferred_element_type=f32), axis=0, keepdims=True)
    c_before = jnp.sum(lax.dot_general(selb, (ri < rj).astype(bf16), (((0,), (0,)), ((), ())),
                                       preferred_element_type=f32), axis=0, keepdims=True)
    slot = lax.broadcasted_iota(i32, (cap, 1), 0).astype(f32)
    onehot = ((c_before <= slot) & (slot < c_upto)).astype(bf16)
    lane = lax.broadcasted_iota(i32, (R, LANE), 1)
    cp = cprev.astype(i32)
    aux = jnp.where(lane == 0, cp // TOK_RADIX, jnp.where(lane == 1, cp % TOK_RADIX,
                    jnp.where(lane == 2, lax.broadcasted_iota(i32, (R, LANE), 0), 0)))
    rhs = jnp.concatenate([cl.astype(bf16), aux.astype(f32).astype(bf16)], axis=1)
    m = jnp.dot(onehot, rhs, preferred_element_type=f32)
    base = m[:, LANE:LANE + 1] * TOK_RADIX + m[:, LANE + 1:LANE + 2]
    chunk = m[:, LANE + 2:LANE + 3]
    pos = jnp.sum((m[:, :LANE] <= slot - base).astype(f32), axis=1, keepdims=True)
    idx_ref[...] = (chunk * LANE + pos).astype(i32) + tok_off


def _route(aff_t, cap, tok_off):
    E, R, _ = aff_t.shape
    return pl.pallas_call(
        functools.partial(_route_kernel, cap=cap, tok_off=tok_off), name="moe_route", grid=(E,),
        in_specs=[pl.BlockSpec((E, R, LANE), lambda e: (0, 0, 0)), pl.BlockSpec((None, R, LANE), lambda e: (e, 0, 0))],
        out_specs=[pl.BlockSpec((None, cap, 1), lambda e: (e, 0, 0)), pl.BlockSpec((None, R, 1), lambda e: (e, 0, 0))],
        out_shape=[jax.ShapeDtypeStruct((E, cap, 1), jnp.int32), jax.ShapeDtypeStruct((E, R, 1), jnp.int32)],
        scratch_shapes=[pltpu.VMEM((E, 8, LANE), jnp.int32), pltpu.VMEM((E, 8, LANE), jnp.int32)],
        compiler_params=_cparams("arbitrary"),
    )(aff_t, aff_t)


def _expert_choice(xs, aff, seqs, w1, w3, w2, layer):
    assert COMB_BM == LANE
    aff_t = _aff_chunks(aff)
    idxs, los = [], []
    start = slot0 = 0
    for num, T in seqs:
        m = num * T
        cap = max(1, CAPACITY_FACTOR * m // N_EXPERTS)
        idx, cprev = _route(aff_t[:, start // LANE:(start + m) // LANE], cap, start)
        idxs.append(idx[..., 0])
        los.append(cprev[..., 0] + slot0)
        start += m
        slot0 += cap
    lo = jnp.concatenate(los + [jnp.full((N_EXPERTS, 1), slot0, jnp.int32)], axis=1).reshape(-1)
    idx = jnp.concatenate(idxs, axis=1)
    per = N_EXPERTS // MOE_PARTS
    ye = [_expert_ffn(xs[idx[p * per:(p + 1) * per]], w1, w3, w2, layer, p * per) for p in range(MOE_PARTS)]
    return ye, lo


def _pack_w_in(w_in):
    sizes = (GLA_HEADS * GLA_DK, GLA_HEADS * GLA_DK, GLA_HEADS * GLA_DV, GLA_HEADS * GLA_DV, 2 * GLA_RANK,
             SWA_NH * SWA_DH, SWA_NH * SWA_DH, SWA_NH * SWA_DH, FNET_GROUPS * FNET_DG,
             RET_HEADS * RET_DH, RET_HEADS * RET_DH, RET_HEADS * RET_DH, RET_HEADS * RET_DH)
    offs = np.concatenate([[0], np.cumsum(sizes)])
    part = lambda i: w_in[..., offs[i]:offs[i + 1]]
    a_q, a_k = part(0), part(1)
    cols = []
    for hh in range(GLA_HEADS):
        cols += [a_q[..., hh * GLA_DK:(hh + 1) * GLA_DK], a_k[..., hh * GLA_DK:(hh + 1) * GLA_DK]]
    cols += [part(2), part(3), part(4),
             jnp.zeros(w_in.shape[:2] + (C_OFF - LR_OFF - 2 * GLA_RANK,), w_in.dtype)]
    cols += [part(i) for i in range(8, 13)]
    out = jnp.concatenate(cols, axis=-1)
    assert out.shape[-1] == IN_COLS_P
    GW = SWA_HEADS * SWA_DH
    swa = [jnp.concatenate([part(i)[..., g * GW:(g + 1) * GW] for i in (5, 6, 7)], axis=-1)
           for g in range(len(SWA_GROUPS))]
    return out, swa


def _block_tables(seqs, blk):
    first, last, pos = [], [], []
    for num, T in seqs:
        per = T // blk
        assert per * blk == T
        for _ in range(num):
            first += [1] + [0] * (per - 1)
            last += [0] * (per - 1) + [1]
            pos += list(range(per))
    mk = lambda a: jnp.asarray(np.array(a, np.int32))
    return mk(first), mk(last), mk(pos)


def kernel(x_prompt, x_sample, ln_in_g, ln_in_b, w_in, gla_w_up, gla_b_up, ret_decay, w_gate, b_gate, w_branch,
           w_out, ln1_g, ln1_b, w_router, w_e1, w_e3, w_e2, ln2_g, ln2_b):
    bf16 = jnp.bfloat16
    D = D_MODEL
    seqs = (x_prompt.shape[:2], x_sample.shape[:2])
    n_p, n_s = x_prompt.shape[0] * x_prompt.shape[1], x_sample.shape[0] * x_sample.shape[1]
    n = n_p + n_s
    t_max = max(T for _, T in seqs)
    seq_first, seq_last, pos_blk = _block_tables(seqs, SEQ_BLOCK)

    w_in_p, w_swa = _pack_w_in(w_in.astype(bf16))
    wo = w_out.astype(bf16)
    wr = jnp.pad(w_router, ((0, 0), (0, 0), (0, LANE - N_EXPERTS))).astype(bf16)
    wg, wb, we1, we3, we2 = w_gate, w_branch, w_e1, w_e3, w_e2

    xf, xb = _ln_in(x_prompt.reshape(n_p, D), x_sample.reshape(n_s, D), ln_in_g, ln_in_b)
    for l in range(DEPTH):
        h = _matmul(xb, w_in_p[l], bf16, bm=1024, bn=IN_BN, name="in_proj")
        wup, bup = _pack_gla_up(gla_w_up[l], gla_b_up[l])
        ys = (_gla(h, seq_first, seq_last, wup, bup),
              _swa(xb, [w[l] for w in w_swa], seqs),
              _fnet(h, seqs),
              _retention(h, seq_first, seq_last, pos_blk, ret_decay[l], t_max))
        merged = _merge(xb, wg, b_gate[l], ys, wb, l)
        x1f, x1s, aff = _outproj(merged, wo[l], xf, ln1_g[l], ln1_b[l], wr[l])
        ye, lo = _expert_choice(x1s, aff, seqs, we1, we3, we2, l)
        if l + 1 < DEPTH:
            xf, xb = _combine_ln(x1f, ye, lo, ln2_g[l], ln2_b[l], 0, n, True)
        else:
            y_p, = _combine_ln(x1f, ye, lo, ln2_g[l], ln2_b[l], 0, n_p, False)
            y_s, = _combine_ln(x1f, ye, lo, ln2_g[l], ln2_b[l], n_p, n_s, False)
    return (y_p.reshape(x_prompt.shape), y_s.reshape(x_sample.shape))
```

```python
import functools

import jax
import jax.numpy as jnp
import numpy as np
from jax import lax
from jax.experimental import pallas as pl
from jax.experimental.pallas import tpu as pltpu

D_MODEL = 2048
DEPTH = 2
GLA_HEADS, GLA_DK, GLA_DV, GLA_RANK, GLA_TAU, GLA_CHUNK = 4, 64, 128, 16, 16.0, 64
SWA_GROUPS = ((128, 1), (512, 4), (2048, 16))
SWA_HEADS, SWA_DH = 4, 128
SWA_NH = SWA_HEADS * len(SWA_GROUPS)
ROPE_THETA, ROPE_DIM = 500000.0, SWA_DH // 4
FNET_GROUPS, FNET_DG = 4, 128
RET_HEADS, RET_DH, RET_THETA = 4, 128, 10000.0
N_BRANCH, BRANCH_W = 4, 512
N_EXPERTS, EXPERT_FF, CAPACITY_FACTOR = 16, 2048, 2
DN_ALPHA = (2 * DEPTH) ** 0.25
LN_EPS = 1e-5

VMEM_LIMIT_BYTES = 48 * 1024 * 1024
LANE = 128

QK_OFF = 0
AV_OFF = QK_OFF + GLA_HEADS * LANE
AG_OFF = AV_OFF + GLA_HEADS * GLA_DV
LR_OFF = AG_OFF + GLA_HEADS * GLA_DV
C_OFF = LR_OFF + 4 * LANE
DQ_OFF = C_OFF + FNET_GROUPS * FNET_DG
DK_OFF = DQ_OFF + RET_HEADS * RET_DH
DV_OFF = DK_OFF + RET_HEADS * RET_DH
DG_OFF = DV_OFF + RET_HEADS * RET_DH
IN_USED = DG_OFF + RET_HEADS * RET_DH
IN_BN = 1536
IN_COLS_P = -(-IN_USED // IN_BN) * IN_BN
assert IN_COLS_P == IN_USED

SEQ_BLOCK = 512


def _cparams(*sem):
    return pltpu.CompilerParams(dimension_semantics=sem, vmem_limit_bytes=VMEM_LIMIT_BYTES)


def _mm_kernel(a_ref, b_ref, o_ref):
    o_ref[...] = jnp.dot(a_ref[...], b_ref[...], preferred_element_type=jnp.float32).astype(o_ref.dtype)


def _matmul(a, b, out_dtype, bm=1024, bn=512, a_col_blk=0, name="matmul"):
    batched = a.ndim == 3
    M = a.shape[-2]
    K, N = b.shape[-2:]
    bm = min(bm, M)
    bn = min(bn, N)
    assert M % bm == 0 and N % bn == 0, (M, N, bm, bn)
    if batched:
        E = a.shape[0]
        grid = (E, M // bm, N // bn)
        in_specs = [pl.BlockSpec((None, bm, K), lambda e, i, j: (e, i, 0)),
                    pl.BlockSpec((None, K, bn), lambda e, i, j: (e, 0, j))]
        out_specs = pl.BlockSpec((None, bm, bn), lambda e, i, j: (e, i, j))
        out_shape = jax.ShapeDtypeStruct((E, M, N), out_dtype)
        sem = ("parallel", "parallel", "arbitrary")
    else:
        grid = (M // bm, N // bn)
        in_specs = [pl.BlockSpec((bm, K), lambda i, j: (i, a_col_blk)),
                    pl.BlockSpec((K, bn), lambda i, j: (0, j))]
        out_specs = pl.BlockSpec((bm, bn), lambda i, j: (i, j))
        out_shape = jax.ShapeDtypeStruct((M, N), out_dtype)
        sem = ("parallel", "arbitrary")
    return pl.pallas_call(
        _mm_kernel, name=name, grid=grid, in_specs=in_specs, out_specs=out_specs, out_shape=out_shape,
        compiler_params=_cparams(*sem),
    )(a, b)


def _ln_rows(x, g, b):
    mu = jnp.mean(x, -1, keepdims=True)
    xc = x - mu
    var = jnp.mean(xc * xc, -1, keepdims=True)
    return xc * lax.rsqrt(var + LN_EPS) * g + b


def _ln_in_kernel(xa_ref, xc_ref, g_ref, b_ref, xf_ref, xb_ref, *, na_blocks):
    i = pl.program_id(0)

    def emit(x_ref):
        y = _ln_rows(x_ref[...], g_ref[...], b_ref[...])
        xf_ref[...] = y
        xb_ref[...] = y.astype(jnp.bfloat16)

    pl.when(i < na_blocks)(lambda: emit(xa_ref))
    pl.when(i >= na_blocks)(lambda: emit(xc_ref))


def _ln_in(xa, xc, g, b, bm=512):
    (na, D), nc = xa.shape, xc.shape[0]
    nab, ncb = na // bm, nc // bm
    row = pl.BlockSpec((bm, D), lambda i: (i, 0))
    vec = pl.BlockSpec((1, D), lambda i: (0, 0))
    return pl.pallas_call(
        functools.partial(_ln_in_kernel, na_blocks=nab), name="ln_in", grid=(nab + ncb,),
        in_specs=[pl.BlockSpec((bm, D), lambda i: (jnp.minimum(i, nab - 1), 0)),
                  pl.BlockSpec((bm, D), lambda i: (jnp.maximum(i - nab, 0), 0)), vec, vec],
        out_specs=[row, row],
        out_shape=[jax.ShapeDtypeStruct((na + nc, D), jnp.float32), jax.ShapeDtypeStruct((na + nc, D), jnp.bfloat16)],
        compiler_params=_cparams("arbitrary"))(xa, xc, g.reshape(1, D), b.reshape(1, D))


def _merge_kernel(xb_ref, wg_ref, bg_ref, ya_ref, yb_ref, yc_ref, yd_ref, wb_ref, o_ref, wgb_ref, wbb_ref):
    @pl.when(pl.program_id(1) == 0)
    def _():
        wgb_ref[...] = wg_ref[...].astype(wgb_ref.dtype)
        wbb_ref[...] = wb_ref[...].astype(wbb_ref.dtype)

    xb = xb_ref[...]
    acc = None
    for i, y_ref in enumerate((ya_ref, yb_ref, yc_ref, yd_ref)):
        gate = jax.nn.sigmoid(jnp.dot(xb, wgb_ref[i], preferred_element_type=jnp.float32) + bg_ref[i])
        term = gate * jnp.dot(y_ref[...], wbb_ref[i], preferred_element_type=jnp.float32)
        acc = term if acc is None else acc + term
    o_ref[...] = acc.astype(o_ref.dtype)


def _merge(xb, wg, bg, ys, wb, layer, bm=1024, bn=256):
    n, D = xb.shape
    W = ys[0].shape[1]
    y_spec = pl.BlockSpec((bm, W), lambda j, i: (i, 0))
    return pl.pallas_call(
        _merge_kernel, name="branch_merge", grid=(D // bn, n // bm),
        in_specs=[pl.BlockSpec((bm, D), lambda j, i: (i, 0)),
                  pl.BlockSpec((None, N_BRANCH, D, bn), lambda j, i: (layer, 0, 0, j)),
                  pl.BlockSpec((N_BRANCH, 1, bn), lambda j, i: (0, 0, j)),
                  y_spec, y_spec, y_spec, y_spec,
                  pl.BlockSpec((None, N_BRANCH, W, bn), lambda j, i: (layer, 0, 0, j))],
        out_specs=pl.BlockSpec((bm, bn), lambda j, i: (i, j)),
        out_shape=jax.ShapeDtypeStruct((n, D), jnp.bfloat16),
        scratch_shapes=[pltpu.VMEM((N_BRANCH, D, bn), jnp.bfloat16), pltpu.VMEM((N_BRANCH, W, bn), jnp.bfloat16)],
        compiler_params=_cparams("arbitrary", "arbitrary"),
    )(xb, wg, bg.reshape(N_BRANCH, 1, D), *ys, wb)


OUTPROJ_SUB = 256


TOK_RADIX = 256
SIDE_MID, SIDE_LOW, SIDE_TOK = N_EXPERTS, 2 * N_EXPERTS, 3 * N_EXPERTS


def _outproj_kernel(m_ref, w_ref, x_ref, g_ref, b_ref, wr_ref, xf_ref, xb_ref, aff_ref):
    f32, bf16 = jnp.float32, jnp.bfloat16
    D = x_ref.shape[1]
    for s in range(m_ref.shape[0] // OUTPROJ_SUB):
        rows = pl.ds(s * OUTPROJ_SUB, OUTPROJ_SUB)
        r = jnp.dot(m_ref[rows, :], w_ref[...], preferred_element_type=f32)
        y = _ln_rows(DN_ALPHA * x_ref[rows, :] + r, g_ref[...], b_ref[...])
        yb = y.astype(bf16)
        xf_ref[rows, :] = y
        xb_ref[rows, :D] = yb
        logits = jnp.dot(yb, wr_ref[...], preferred_element_type=f32)
        lane = lax.broadcasted_iota(jnp.int32, logits.shape, 1)
        logits = jnp.where(lane < N_EXPERTS, logits, -jnp.inf)
        e = jnp.exp(logits - jnp.max(logits, -1, keepdims=True))
        aff = e / jnp.sum(e, -1, keepdims=True)
        aff_ref[rows, :] = aff
        hi = aff.astype(bf16).astype(f32)
        mid = (aff - hi).astype(bf16).astype(f32)
        low = (aff - hi - mid).astype(bf16).astype(f32)
        tok = (pl.program_id(0) * m_ref.shape[0] + s * OUTPROJ_SUB
               + lax.broadcasted_iota(jnp.int32, logits.shape, 0))
        side = hi + pltpu.roll(mid, SIDE_MID, axis=1) + pltpu.roll(low, SIDE_LOW, axis=1)
        side = jnp.where(lane == SIDE_TOK, (tok // TOK_RADIX).astype(f32),
                         jnp.where(lane == SIDE_TOK + 1, (tok % TOK_RADIX).astype(f32), side))
        xb_ref[rows, D:] = side.astype(bf16)


def _outproj(merged, w_out, x, g, b, w_router_p, bm=512):
    n, D = x.shape
    row = pl.BlockSpec((bm, D), lambda i: (i, 0))
    vec = pl.BlockSpec((1, D), lambda i: (0, 0))
    return pl.pallas_call(
        _outproj_kernel, name="outproj_ln_router", grid=(n // bm,),
        in_specs=[row, pl.BlockSpec((D, D), lambda i: (0, 0)), row, vec, vec,
                  pl.BlockSpec((D, LANE), lambda i: (0, 0))],
        out_specs=[row, pl.BlockSpec((bm, D + LANE), lambda i: (i, 0)), pl.BlockSpec((bm, LANE), lambda i: (i, 0))],
        out_shape=[jax.ShapeDtypeStruct((n, D), jnp.float32), jax.ShapeDtypeStruct((n, D + LANE), jnp.bfloat16),
                   jax.ShapeDtypeStruct((n, LANE), jnp.float32)],
        compiler_params=_cparams("parallel"),
    )(merged, w_out, x, g.reshape(1, D), b.reshape(1, D), w_router_p)


def _aff_t_kernel(a_ref, o_ref):
    for s in range(a_ref.shape[0] // LANE):
        o_ref[:, s, :] = a_ref[s * LANE:(s + 1) * LANE, :].T[:N_EXPERTS, :]


def _aff_chunks(aff, bm=1024):
    n = aff.shape[0]
    return pl.pallas_call(
        _aff_t_kernel, name="aff_chunks", grid=(n // bm,),
        in_specs=[pl.BlockSpec((bm, LANE), lambda i: (i, 0))],
        out_specs=pl.BlockSpec((N_EXPERTS, bm // LANE, LANE), lambda i: (0, i, 0)),
        out_shape=jax.ShapeDtypeStruct((N_EXPERTS, n // LANE, LANE), jnp.float32),
        compiler_params=_cparams("parallel"),
    )(aff)


def _log_sigmoid(z):
    return jnp.minimum(z, 0.0) - jnp.log(1.0 + jnp.exp(-jnp.abs(z)))


def _headnorm(o):
    mu = jnp.mean(o, -1, keepdims=True)
    oc = o - mu
    var = jnp.mean(oc * oc, -1, keepdims=True)
    return oc * lax.rsqrt(var + LN_EPS)


def _silu(x):
    return x * jax.nn.sigmoid(x)


def _chunk_cumsum(g, chunk, inclusive):
    row = lax.broadcasted_iota(jnp.int32, g.shape, 0) % chunk
    b = g
    s = 1
    while s < chunk:
        b = b + jnp.where(row >= s, pltpu.roll(b, s, axis=0), 0.0)
        s *= 2
    return b if inclusive else b - g


def _gla_kernel(flag_ref, qk_ref, v_ref, lr_ref, wup_ref, bup_ref, *rest, reverse, nblk):
    s_idx = pl.program_id(0)
    blk = nblk - 1 - s_idx if reverse else s_idx

    @pl.when(flag_ref[blk] == 1)
    def _():
        rest[-1][...] = jnp.zeros_like(rest[-1])

    for hh in range(GLA_HEADS):
        _gla_head(hh, qk_ref, v_ref, lr_ref, wup_ref, bup_ref, *rest, reverse=reverse)


def _gla_head(hh, qk_ref, v_ref, lr_ref, wup_ref, bup_ref, *rest, reverse):
    if reverse:
        gate_ref, of_ref, o_ref, st_ref = rest
    else:
        o_ref, st_ref = rest
    f32, bf16 = jnp.float32, jnp.bfloat16
    C = GLA_CHUNK
    TB = qk_ref.shape[0]
    nc = TB // C
    sl = slice(hh * LANE, (hh + 1) * LANE)

    is_q = lax.broadcasted_iota(jnp.int32, (1, LANE), 1) < GLA_DK
    qk = qk_ref[:, sl].astype(f32) * jnp.where(is_q, GLA_DK ** -0.5, 1.0)
    z = jnp.dot(lr_ref[...], wup_ref[hh], preferred_element_type=f32) + bup_ref[hh]
    g = _log_sigmoid(z) * (1.0 / GLA_TAU)
    cs = _chunk_cumsum(g, C, inclusive=not reverse)
    cs3 = cs.reshape(nc, C, LANE)
    mid = cs3[:, C // 2:C // 2 + 1, :]
    if reverse:
        tot3 = cs3[:, C - 1:C, :] + g.reshape(nc, C, LANE)[:, C - 1:C, :]
        edge3 = jnp.where(is_q, tot3 - cs3, cs3)
        mid3 = jnp.where(is_q, mid - cs3, cs3 - mid)
    else:
        tot3 = cs3[:, C - 1:C, :]
        edge3 = jnp.where(is_q, cs3, tot3 - cs3)
        mid3 = jnp.where(is_q, cs3 - mid, mid - cs3)
    x_edge = qk * jnp.exp(edge3).reshape(TB, LANE)
    x_mid = qk * jnp.exp(mid3).reshape(TB, LANE)
    qe = jnp.where(is_q, x_edge, 0.0).astype(bf16).reshape(nc, C, LANE)
    kd = jnp.where(is_q, pltpu.roll(x_edge, GLA_DK, axis=1), 0.0).astype(bf16).reshape(nc, C, LANE)
    qm = jnp.where(is_q, x_mid, 0.0).astype(bf16).reshape(nc, C, LANE)
    km = pltpu.roll(x_mid, GLA_DK, axis=1).astype(bf16).reshape(nc, C, LANE)
    etot = jnp.exp(tot3)
    v = v_ref[:, sl]
    ri = lax.broadcasted_iota(jnp.int32, (C, C), 0)
    ci = lax.broadcasted_iota(jnp.int32, (C, C), 1)
    keep = (ci > ri) if reverse else (ci <= ri)

    st = st_ref[hh]
    outs = [None] * nc
    for c in (range(nc - 1, -1, -1) if reverse else range(nc)):
        vc = v[c * C:(c + 1) * C, :]
        att = lax.dot_general(qm[c], km[c], (((1,), (1,)), ((), ())), preferred_element_type=f32)
        att = jnp.where(keep, att, 0.0).astype(bf16)
        o_c = jnp.dot(att, vc, preferred_element_type=f32)
        o_c = o_c + lax.dot_general(qe[c], st.astype(bf16), (((1,), (1,)), ((), ())), preferred_element_type=f32)
        upd = lax.dot_general(vc, kd[c], (((0,), (0,)), ((), ())), preferred_element_type=f32)
        st = st * etot[c] + upd
        outs[c] = o_c
    st_ref[hh] = st
    o = jnp.concatenate(outs, axis=0)
    if reverse:
        o = _headnorm(o + of_ref[:, sl])
        o_ref[:, sl] = (o * _silu(gate_ref[:, sl].astype(f32))).astype(o_ref.dtype)
    else:
        o_ref[:, sl] = o


def _gla(h, seq_first, seq_last, wup, bup):
    n = h.shape[0]
    TB = SEQ_BLOCK
    nblk = n // TB
    H = GLA_HEADS

    def col(off):
        return off // LANE

    HW = H * LANE

    def specs(reverse):
        blk = (lambda s: nblk - 1 - s) if reverse else (lambda s: s)
        sp = [pl.BlockSpec((TB, HW), lambda s, f: (blk(s), QK_OFF // HW)),
              pl.BlockSpec((TB, HW), lambda s, f: (blk(s), AV_OFF // HW)),
              pl.BlockSpec((TB, LANE), lambda s, f: (blk(s), col(LR_OFF))),
              pl.BlockSpec((None, H, LANE, LANE), lambda s, f: (1 if reverse else 0, 0, 0, 0)),
              pl.BlockSpec((None, H, 1, LANE), lambda s, f: (1 if reverse else 0, 0, 0, 0))]
        out = pl.BlockSpec((TB, HW), lambda s, f: (blk(s), 0))
        return sp, out, blk

    scratch = [pltpu.VMEM((H, GLA_DV, LANE), jnp.float32)]
    sp, out, _ = specs(False)
    o_f = pl.pallas_call(
        functools.partial(_gla_kernel, reverse=False, nblk=nblk), name="gla_fwd",
        grid_spec=pltpu.PrefetchScalarGridSpec(num_scalar_prefetch=1, grid=(nblk,), in_specs=sp, out_specs=out,
                                               scratch_shapes=scratch),
        out_shape=jax.ShapeDtypeStruct((n, H * GLA_DV), jnp.float32),
        compiler_params=_cparams("arbitrary"),
    )(seq_first, h, h, h, wup, bup)
    sp, out, blk = specs(True)
    sp = sp + [pl.BlockSpec((TB, HW), lambda s, f: (blk(s), AG_OFF // HW)), out]
    return pl.pallas_call(
        functools.partial(_gla_kernel, reverse=True, nblk=nblk), name="gla_bwd",
        grid_spec=pltpu.PrefetchScalarGridSpec(num_scalar_prefetch=1, grid=(nblk,), in_specs=sp, out_specs=out,
                                               scratch_shapes=scratch),
        out_shape=jax.ShapeDtypeStruct((n, H * GLA_DV), jnp.bfloat16),
        compiler_params=_cparams("arbitrary"),
    )(seq_last, h, h, h, wup, bup, h, o_f)


def _pack_gla_up(w_up, b_up):
    H, dk, r = GLA_HEADS, GLA_DK, GLA_RANK
    assert 2 * dk == LANE
    w = w_up.reshape(2, r, H, dk).transpose(0, 2, 1, 3)
    wp = jnp.zeros((2, H, LANE, dk), jnp.float32)
    wp = wp.at[0, :, 0:r].set(w[0]).at[1, :, r:2 * r].set(w[1])
    bp = b_up.reshape(2, H, 1, dk).astype(jnp.float32)
    return jnp.concatenate([wp, wp], axis=-1).astype(jnp.bfloat16), jnp.concatenate([bp, bp], axis=-1)


def _rope_tables(pos, dim, theta, width):
    half = dim // 2
    pos = np.asarray(pos, np.float64)
    T = pos.shape[0]
    inv = 1.0 / (float(theta) ** (np.arange(half, dtype=np.float64) / half))
    ang = pos[:, None] * inv[None, :]
    c, s = np.cos(ang), np.sin(ang)
    rest = width - dim
    cos = np.concatenate([c, c, np.ones((T, rest))], axis=1)
    sin_lo = np.concatenate([-s, np.zeros((T, half + rest))], axis=1)
    sin_hi = np.concatenate([np.zeros((T, half)), s, np.zeros((T, rest))], axis=1)
    return tuple(jnp.asarray(a, jnp.float32) for a in (cos, sin_lo, sin_hi))


def _ret_kernel(flag_ref, pos_ref, q_ref, k_ref, v_ref, cos_ref, sin_ref, dec_ref, *rest, reverse, nblk):
    s_idx = pl.program_id(0)
    blk = nblk - 1 - s_idx if reverse else s_idx

    @pl.when(flag_ref[blk] == 1)
    def _():
        st_ref = rest[-1] if reverse else rest[-2]
        st_ref[...] = jnp.zeros_like(st_ref)

    for hh in range(RET_HEADS):
        _ret_head(hh, s_idx, q_ref, k_ref, v_ref, cos_ref, sin_ref, dec_ref, *rest, reverse=reverse)


def _ret_head(hh, s_idx, q_ref, k_ref, v_ref, cos_ref, sin_ref, dec_ref, *rest, reverse):
    if reverse:
        gate_ref, o1_ref, o_ref, st_ref = rest
    else:
        o_ref, st_ref, dm_ref = rest
    f32, bf16 = jnp.float32, jnp.bfloat16
    TB = q_ref.shape[0]
    sl = slice(hh * RET_DH, (hh + 1) * RET_DH)

    dec = dec_ref[hh]
    lg = -(jnp.maximum(dec, 0.0) + jnp.log1p(jnp.exp(-jnp.abs(dec))))
    lg_f, lg_b = lg[0, 0:1, 0:1], lg[1, 0:1, 0:1]

    cos, sin = cos_ref[...], sin_ref[...]

    def rot(x):
        return x * cos + pltpu.roll(x, RET_DH // 2, axis=1) * sin

    qr = rot(q_ref[:, sl].astype(f32))
    kr = rot(k_ref[:, sl].astype(f32)) * (RET_DH ** -0.5)
    v = v_ref[:, sl]
    idx = lax.broadcasted_iota(jnp.int32, (TB, 1), 0).astype(f32)
    st = st_ref[hh]
    cross = jnp.dot(qr.astype(bf16), st.astype(bf16), preferred_element_type=f32)
    if reverse:
        cross = cross * jnp.exp(lg_b * (TB - idx))
        kz = (kr * jnp.exp(lg_b * idx)).astype(bf16)
        st_ref[hh] = st * jnp.exp(lg_b * TB) + lax.dot_general(kz, v, (((0,), (0,)), ((), ())),
                                                               preferred_element_type=f32)
        o = _headnorm(o1_ref[:, sl] + cross)
        o_ref[:, sl] = (o * _silu(gate_ref[:, sl].astype(f32))).astype(o_ref.dtype)
    else:
        @pl.when(s_idx == 0)
        def _():
            ri = lax.broadcasted_iota(jnp.int32, (TB, TB), 0)
            ci = lax.broadcasted_iota(jnp.int32, (TB, TB), 1)
            d = (ri - ci).astype(f32)
            dm_ref[hh] = jnp.where(d >= 0, jnp.exp(lg_f * jnp.maximum(d, 0.0)), jnp.exp(lg_b * jnp.maximum(-d, 0.0)))

        cross = cross * jnp.exp(lg_f * (idx + 1.0))
        kz = (kr * jnp.exp(lg_f * (TB - 1.0 - idx))).astype(bf16)
        st_ref[hh] = st * jnp.exp(lg_f * TB) + lax.dot_general(kz, v, (((0,), (0,)), ((), ())),
                                                               preferred_element_type=f32)
        sc = lax.dot_general(qr.astype(bf16), kr.astype(bf16), (((1,), (1,)), ((), ())), preferred_element_type=f32)
        sc = (sc * dm_ref[hh]).astype(bf16)
        o_ref[:, sl] = jnp.dot(sc, v, preferred_element_type=f32) + cross


def _retention(h, seq_first, seq_last, pos_blk, decay, t_max):
    n = h.shape[0]
    TB = SEQ_BLOCK
    nblk = n // TB
    H, dh = RET_HEADS, RET_DH
    cos, sin_lo, sin_hi = _rope_tables(np.arange(t_max), dh, RET_THETA, dh)
    sin = sin_lo + sin_hi
    dec = jnp.broadcast_to(decay.astype(jnp.float32).T[:, :, None, None], (H, 2, 8, LANE))

    HW = H * dh

    def specs(reverse):
        blk = (lambda s: nblk - 1 - s) if reverse else (lambda s: s)
        tok = lambda off: pl.BlockSpec((TB, HW), lambda s, f, p: (blk(s), off // HW))
        tab = pl.BlockSpec((TB, LANE), lambda s, f, p: (p[blk(s)], 0))
        sp = [tok(DQ_OFF), tok(DK_OFF), tok(DV_OFF), tab, tab,
              pl.BlockSpec((H, 2, 8, LANE), lambda s, f, p: (0, 0, 0, 0))]
        return sp, tok, pl.BlockSpec((TB, HW), lambda s, f, p: (blk(s), 0))

    sp, _, out = specs(False)
    o1 = pl.pallas_call(
        functools.partial(_ret_kernel, reverse=False, nblk=nblk), name="ret_fwd",
        grid_spec=pltpu.PrefetchScalarGridSpec(
            num_scalar_prefetch=2, grid=(nblk,), in_specs=sp, out_specs=out,
            scratch_shapes=[pltpu.VMEM((H, dh, dh), jnp.float32), pltpu.VMEM((H, TB, TB), jnp.float32)]),
        out_shape=jax.ShapeDtypeStruct((n, H * dh), jnp.float32),
        compiler_params=_cparams("arbitrary"),
    )(seq_first, pos_blk, h, h, h, cos, sin, dec)
    sp, tok, out = specs(True)
    sp = sp + [tok(DG_OFF), out]
    return pl.pallas_call(
        functools.partial(_ret_kernel, reverse=True, nblk=nblk), name="ret_bwd",
        grid_spec=pltpu.PrefetchScalarGridSpec(
            num_scalar_prefetch=2, grid=(nblk,), in_specs=sp, out_specs=out,
            scratch_shapes=[pltpu.VMEM((H, dh, dh), jnp.float32)]),
        out_shape=jax.ShapeDtypeStruct((n, H * dh), jnp.bfloat16),
        compiler_params=_cparams("arbitrary"),
    )(seq_last, pos_blk, h, h, h, cos, sin, dec, h, o1)


SWA_Q = 128
SWA_SUBS = 4
SWA_HALF = 64
assert all(w // (2 * d) == SWA_HALF for w, d in SWA_GROUPS)
NEG_BIG = -1e30


def _swa_kernel(join_ref, q_ref, kl_ref, kc_ref, kr_ref, vl_ref, vc_ref, vr_ref,
                qcos_ref, qs1_ref, qs2_ref, kcos_ref, ks1_ref, ks2_ref, o_ref, lse_ref):
    f32, bf16 = jnp.float32, jnp.bfloat16
    j = pl.program_id(1)
    Q, HW = SWA_Q, SWA_HALF
    W = Q + 2 * HW
    ri = lax.broadcasted_iota(jnp.int32, (Q, W), 0)
    ci = lax.broadcasted_iota(jnp.int32, (Q, W), 1)
    rel = ci - HW - ri
    band = (rel <= HW) & (rel >= -HW)
    joins = [join_ref[j * (SWA_SUBS + 1) + u] == 1 for u in range(SWA_SUBS + 1)]
    oks = [band & ((ci >= HW) | joins[u]) & ((ci < HW + Q) | joins[u + 1]) for u in range(SWA_SUBS)]
    lane = lax.broadcasted_iota(jnp.int32, (Q, LANE), 1)

    def rot(x, c_ref, s1_ref, s2_ref):
        return (x * c_ref[...] + pltpu.roll(x, LANE - ROPE_DIM // 2, axis=1) * s1_ref[...]
                + pltpu.roll(x, ROPE_DIM // 2, axis=1) * s2_ref[...])

    lse_tiles = [jnp.zeros((Q, LANE), f32) for _ in range(SWA_SUBS)]
    for hh in range(SWA_HEADS):
        sl = slice(hh * SWA_DH, (hh + 1) * SWA_DH)
        q = (rot(q_ref[:, sl].astype(f32), qcos_ref, qs1_ref, qs2_ref) * (SWA_DH ** -0.5)).astype(bf16)
        k = jnp.concatenate([kl_ref[:, sl], kc_ref[:, sl], kr_ref[:, sl]], axis=0).astype(f32)
        k = rot(k, kcos_ref, ks1_ref, ks2_ref).astype(bf16)
        v = jnp.concatenate([vl_ref[:, sl], vc_ref[:, sl], vr_ref[:, sl]], axis=0)
        for u in range(SWA_SUBS):
            s = lax.dot_general(q[u * Q:(u + 1) * Q], k[u * Q:u * Q + W], (((1,), (1,)), ((), ())),
                                preferred_element_type=f32)
            s = jnp.where(oks[u], s, NEG_BIG)
            m = jnp.max(s, -1, keepdims=True)
            p = jnp.exp(s - m)
            den = jnp.sum(p, -1, keepdims=True)
            o = jnp.dot(p.astype(bf16), v[u * Q:u * Q + W], preferred_element_type=f32) / den
            o_ref[u * Q:(u + 1) * Q, sl] = o.astype(o_ref.dtype)
            lse_tiles[u] = jnp.where(lane == hh, m + jnp.log(den), lse_tiles[u])
    for u in range(SWA_SUBS):
        lse_ref[u * Q:(u + 1) * Q, :] = lse_tiles[u]


def _swa_merge_kernel(*refs):
    ng = len(SWA_GROUPS)
    o_refs, l_refs, y_ref = refs[:ng], refs[ng:2 * ng], refs[2 * ng]
    scratch = refs[2 * ng + 1:]
    assert SWA_DH == LANE
    os_, ls_ = [], []
    si = 0
    for gi, (_, dil) in enumerate(SWA_GROUPS):
        if dil == 1:
            os_.append(lambda hh, r=o_refs[gi]: r[0, :, hh * LANE:(hh + 1) * LANE])
            ls_.append(l_refs[gi].at[0])
            continue
        o_s, l_s = scratch[si], scratch[si + 1]
        si += 2
        rows = l_s.shape[0] // dil
        for r in range(dil):
            for hh in range(SWA_HEADS):
                o_s[hh, pl.ds(r, rows, stride=dil), :] = (
                    o_refs[gi][r, :, hh * LANE:(hh + 1) * LANE].astype(jnp.float32))
            l_s[pl.ds(r, rows, stride=dil), :] = l_refs[gi][r]
        os_.append(lambda hh, s=o_s: s[hh])
        ls_.append(l_s)
    for hh in range(SWA_HEADS):
        ls = [r[:, hh:hh + 1] for r in ls_]
        m = functools.reduce(jnp.maximum, ls)
        ws = [jnp.exp(l - m) for l in ls]
        tot = functools.reduce(lambda a, b: a + b, ws)
        y = functools.reduce(lambda a, b: a + b, [w * o(hh) for w, o in zip(ws, os_)])
        y_ref[:, hh * LANE:(hh + 1) * LANE] = (y / tot).astype(y_ref.dtype)


def _proj_strided_kernel(a_ref, b_ref, o_ref, acc_ref, *, dil):
    acc = jnp.dot(a_ref[...], b_ref[...], preferred_element_type=jnp.float32)
    if dil == 1:
        o_ref[0] = acc.astype(o_ref.dtype)
        return
    rows = acc_ref.shape[1] // dil
    for s in range(acc_ref.shape[0]):
        acc_ref[s] = acc[:, s * LANE:(s + 1) * LANE]
        for r in range(dil):
            o_ref[r, :, s * LANE:(s + 1) * LANE] = acc_ref[s, pl.ds(r, rows, stride=dil), :].astype(o_ref.dtype)


def _proj_strided(a, b, dil, bm=1024, bn=512):
    n, K = a.shape
    N = b.shape[1]
    return pl.pallas_call(
        functools.partial(_proj_strided_kernel, dil=dil), name=f"swa_proj_dil{dil}", grid=(n // bm, N // bn),
        in_specs=[pl.BlockSpec((bm, K), lambda i, j: (i, 0)), pl.BlockSpec((K, bn), lambda i, j: (0, j))],
        out_specs=pl.BlockSpec((dil, bm // dil, bn), lambda i, j: (0, i, j)),
        out_shape=jax.ShapeDtypeStruct((dil, n // dil, N), jnp.bfloat16),
        scratch_shapes=[pltpu.VMEM((bn // LANE, bm, LANE), jnp.float32)],
        compiler_params=_cparams("parallel", "arbitrary"),
    )(a, b)


def _swa(xb, w_groups, seqs):
    n = xb.shape[0]
    Q, HW = SWA_Q, SWA_HALF
    GW = SWA_HEADS * SWA_DH
    outs, lses = [], []
    for gi, (window, dil) in enumerate(SWA_GROUPS):
        hs = _proj_strided(xb, w_groups[gi], dil)
        rows = n // dil
        QB = SWA_SUBS * Q
        nblk = rows // QB
        has_l, has_r = [], []
        for num, T in seqs:
            per = T // dil // Q
            assert per * Q * dil == T
            has_l += ([0] + [1] * (per - 1)) * num
            has_r += ([1] * (per - 1) + [0]) * num
        assert len(has_l) == nblk * SWA_SUBS
        joins = np.array(has_l, np.int32).reshape(nblk, SWA_SUBS)
        joins = np.concatenate([joins, np.array(has_r, np.int32).reshape(nblk, SWA_SUBS)[:, -1:]], axis=1)
        joins = jnp.asarray(joins.reshape(-1))
        qpos = np.arange(QB) * dil
        kpos = (np.arange(QB + 2 * HW) - HW) * dil
        qt = _rope_tables(qpos, ROPE_DIM, ROPE_THETA, SWA_DH)
        kt = _rope_tables(kpos, ROPE_DIM, ROPE_THETA, SWA_DH)
        per64 = QB // HW
        last64 = rows // HW - 1

        def center(c):
            return pl.BlockSpec((None, QB, GW), lambda r, j, *_: (r, j, c))

        def left(c):
            return pl.BlockSpec((None, HW, GW), lambda r, j, *_: (r, jnp.maximum(per64 * j - 1, 0), c))

        def right(c):
            return pl.BlockSpec((None, HW, GW), lambda r, j, *_: (r, jnp.minimum(per64 * (j + 1), last64), c))

        qtab = pl.BlockSpec((QB, SWA_DH), lambda r, j, *_: (0, 0))
        ktab = pl.BlockSpec((QB + 2 * HW, SWA_DH), lambda r, j, *_: (0, 0))
        o_g, lse_g = pl.pallas_call(
            _swa_kernel, name=f"swa_dil{dil}",
            grid_spec=pltpu.PrefetchScalarGridSpec(
                num_scalar_prefetch=1, grid=(dil, nblk),
                in_specs=[center(0), left(1), center(1), right(1), left(2), center(2), right(2),
                          qtab, qtab, qtab, ktab, ktab, ktab],
                out_specs=[pl.BlockSpec((None, QB, GW), lambda r, j, *_: (r, j, 0)),
                           pl.BlockSpec((None, QB, LANE), lambda r, j, *_: (r, j, 0))]),
            out_shape=[jax.ShapeDtypeStruct((dil, rows, GW), jnp.bfloat16),
                       jax.ShapeDtypeStruct((dil, rows, LANE), jnp.float32)],
            compiler_params=_cparams("parallel", "parallel"),
        )(joins, hs, hs, hs, hs, hs, hs, hs, *qt, *kt)
        outs.append(o_g)
        lses.append(lse_g)
    bm = 1024
    dils = [d for _, d in SWA_GROUPS]
    osp = [pl.BlockSpec((d, bm // d, GW), lambda i: (0, i, 0)) for d in dils]
    lsp = [pl.BlockSpec((d, bm // d, LANE), lambda i: (0, i, 0)) for d in dils]
    scratch = []
    for d in dils:
        if d > 1:
            scratch += [pltpu.VMEM((SWA_HEADS, bm, LANE), jnp.float32), pltpu.VMEM((bm, LANE), jnp.float32)]
    return pl.pallas_call(
        _swa_merge_kernel, name="swa_merge", grid=(n // bm,), in_specs=osp + lsp,
        out_specs=pl.BlockSpec((bm, GW), lambda i: (i, 0)),
        out_shape=jax.ShapeDtypeStruct((n, GW), jnp.bfloat16), scratch_shapes=scratch,
        compiler_params=_cparams("parallel"),
    )(*outs, *lses)


FFT_T2 = 128
FW = FNET_GROUPS * FNET_DG


def _dft_cos_sin(n_out, n_in, period):
    ang = (2.0 * np.pi / period) * np.mod(np.outer(np.arange(n_out), np.arange(n_in)), period)
    return np.cos(ang), np.sin(ang)


def _fnet_stage1_kernel(z_ref, m_ref, a_ref, *, cols):
    mat = m_ref[...]
    for c in range(cols):
        zr = z_ref[:, c * 2 * FW:c * 2 * FW + FW]
        zi = z_ref[:, c * 2 * FW + FW:(c + 1) * 2 * FW]
        a = jnp.dot(mat, jnp.concatenate([zr, zi], axis=0), preferred_element_type=jnp.float32)
        t1 = zr.shape[0]
        a_ref[:, c * 2 * FW:c * 2 * FW + FW] = a[:t1].astype(a_ref.dtype)
        a_ref[:, c * 2 * FW + FW:(c + 1) * 2 * FW] = a[t1:].astype(a_ref.dtype)


FNET_KB = 8


def _fnet_stage2_kernel(a_ref, g_ref, y_ref, *, scale):
    kw = g_ref.shape[1] // FNET_KB
    for j in range(FNET_KB):
        y = jnp.dot(g_ref[:, j * kw:(j + 1) * kw], a_ref[j], preferred_element_type=jnp.float32)
        y_ref[:, j * FW:(j + 1) * FW] = (y * scale).astype(y_ref.dtype)


def _fnet(h, seqs):
    n = h.shape[0]
    bf16 = jnp.bfloat16
    T2 = FFT_T2
    cc, ss = _dft_cos_sin(FNET_DG, FNET_DG, FNET_DG)
    eye = np.eye(FNET_GROUPS)
    wc = jnp.asarray(np.concatenate([np.kron(eye, cc), -np.kron(eye, ss)], axis=1), bf16)
    z = _matmul(h, wc, bf16, bn=2 * FW, a_col_blk=C_OFF // FW, name="fnet_channel_dft")
    zz = z.reshape(n // T2, T2 * 2 * FW)
    outs = []
    row0 = 0
    for num, T in seqs:
        T1 = T // T2
        assert T1 * T2 == T and T1 % 16 == 0
        c1, s1 = _dft_cos_sin(T1, T1, T1)
        mat = jnp.asarray(np.block([[c1, s1], [-s1, c1]]), bf16)
        cols = min(T2, max(1, 1024 // T1))
        rb0 = row0 // T1
        a = pl.pallas_call(
            functools.partial(_fnet_stage1_kernel, cols=cols), name=f"fnet_dft_t1_{T1}", grid=(num, T2 // cols),
            in_specs=[pl.BlockSpec((T1, cols * 2 * FW), lambda b, j: (rb0 + b, j)),
                      pl.BlockSpec((2 * T1, 2 * T1), lambda b, j: (0, 0))],
            out_specs=pl.BlockSpec((T1, cols * 2 * FW), lambda b, j: (b, j)),
            out_shape=jax.ShapeDtypeStruct((num * T1, T2 * 2 * FW), bf16),
            compiler_params=_cparams("parallel", "parallel"),
        )(zz, mat)
        ec, es = _dft_cos_sin(T, T2, T)
        g = jnp.asarray(np.stack([ec, es], axis=-1).reshape(T2, T1 * 2 * T2), bf16)
        y = pl.pallas_call(
            functools.partial(_fnet_stage2_kernel, scale=float(1.0 / np.sqrt(T * FNET_DG))), name=f"fnet_dft_t2_{T1}",
            grid=(num, T1 // FNET_KB),
            in_specs=[pl.BlockSpec((FNET_KB, 2 * T2, FW), lambda b, k1: (b * (T1 // FNET_KB) + k1, 0, 0)),
                      pl.BlockSpec((T2, FNET_KB * 2 * T2), lambda b, k1: (0, k1))],
            out_specs=pl.BlockSpec((T2, FNET_KB * FW), lambda b, k1: (b, k1)),
            out_shape=jax.ShapeDtypeStruct((num * T2, T1 * FW), bf16),
            compiler_params=_cparams("parallel", "parallel"),
        )(a.reshape(num * T1, 2 * T2, FW), g)
        outs.append(y.reshape(num * T, FW))
        row0 += num * T1
    return jnp.concatenate(outs, axis=0)


def _ffn_up_kernel(x_ref, w1_ref, w3_ref, h_ref, w1b_ref, w3b_ref):
    @pl.when(pl.program_id(2) == 0)
    def _():
        w1b_ref[...] = w1_ref[...].astype(w1b_ref.dtype)
        w3b_ref[...] = w3_ref[...].astype(w3b_ref.dtype)

    x = x_ref[...]
    h1 = jnp.dot(x, w1b_ref[...], preferred_element_type=jnp.float32)
    h3 = jnp.dot(x, w3b_ref[...], preferred_element_type=jnp.float32)
    h_ref[...] = (_silu(h1) * h3).astype(h_ref.dtype)


def _ffn_down_kernel(h_ref, w2_ref, s_ref, o_ref, w2b_ref, *, e0):
    e = pl.program_id(0) + e0

    @pl.when(pl.program_id(1) == 0)
    def _():
        w2b_ref[...] = w2_ref[...].astype(w2b_ref.dtype)

    D = w2_ref.shape[1]
    side = s_ref[...]
    lane = lax.broadcasted_iota(jnp.int32, side.shape, 1)
    mine = (lane == e) | (lane == e + SIDE_MID) | (lane == e + SIDE_LOW)
    gate = jnp.sum(jnp.where(mine, side.astype(jnp.float32), 0.0), axis=1, keepdims=True)
    y = jnp.dot(h_ref[...], w2b_ref[...], preferred_element_type=jnp.float32)
    o_ref[:, :D] = (y * gate).astype(o_ref.dtype)
    o_ref[:, D:] = side


def _expert_ffn(xe, w1, w3, w2, layer, e0, bm=1024, bf=512):
    E, R, DY = xe.shape
    D = DY - LANE
    F = w1.shape[-1]
    wspec = pl.BlockSpec((None, None, D, bf), lambda e, f, i: (layer, e0 + e, 0, f))
    hmid = pl.pallas_call(
        _ffn_up_kernel, name="expert_ffn_up", grid=(E, F // bf, R // bm),
        in_specs=[pl.BlockSpec((None, bm, D), lambda e, f, i: (e, i, 0)), wspec, wspec],
        out_specs=pl.BlockSpec((None, bm, bf), lambda e, f, i: (e, i, f)),
        out_shape=jax.ShapeDtypeStruct((E, R, F), jnp.bfloat16),
        scratch_shapes=[pltpu.VMEM((D, bf), jnp.bfloat16), pltpu.VMEM((D, bf), jnp.bfloat16)],
        compiler_params=_cparams("arbitrary", "arbitrary", "arbitrary"),
    )(xe, w1, w3)
    return pl.pallas_call(
        functools.partial(_ffn_down_kernel, e0=e0), name="expert_ffn_down", grid=(E, R // bm),
        in_specs=[pl.BlockSpec((None, bm, F), lambda e, i: (e, i, 0)),
                  pl.BlockSpec((None, None, F, D), lambda e, i: (layer, e0 + e, 0, 0), pipeline_mode=pl.Buffered(1)),
                  pl.BlockSpec((None, bm, LANE), lambda e, i: (e, i, D // LANE))],
        out_specs=pl.BlockSpec((None, bm, D + LANE), lambda e, i: (e, i, 0)),
        out_shape=jax.ShapeDtypeStruct((E, R, D + LANE), jnp.bfloat16),
        scratch_shapes=[pltpu.VMEM((F, D), jnp.bfloat16)],
        compiler_params=_cparams("arbitrary", "arbitrary"),
    )(hmid, w2, xe)


COMB_BM = 128
COMB_G = 16
COMB_K = 256
MOE_PARTS = 4


def _combine_kernel(lo_ref, x_ref, g_ref, b_ref, *rest, nblk_all, blk_off, want_bf16, parts):
    ye_parts, rest = rest[:parts], rest[parts:]
    if want_bf16:
        xf_ref, xb_ref, buf_ref, acc_ref, sem = rest
    else:
        xf_ref, buf_ref, acc_ref, sem = rest
    f32, bf16 = jnp.float32, jnp.bfloat16
    per = ye_parts[0].shape[0]
    E = per * parts
    D = x_ref.shape[1]
    BM, G, KC = COMB_BM, COMB_G, COMB_K
    i = pl.program_id(0)
    nsteps = pl.num_programs(0)
    blk = i + blk_off
    slot = i % 2

    @pl.when(i == 0)
    def _():
        buf_ref[...] = jnp.zeros_like(buf_ref)

    def granule_copy(sl, e, src_row, dst_row):
        return pltpu.make_async_copy(ye_parts[e // per].at[e % per, pl.ds(src_row, G)],
                                     buf_ref.at[sl, pl.ds(dst_row, G)], sem.at[sl])

    def gather_block(b, sl, start):
        off = jnp.int32(0)
        for e in range(E):
            lo = lo_ref[e * (nblk_all + 1) + b]
            hi = lo_ref[e * (nblk_all + 1) + b + 1]
            lo_al = (lo // G) * G
            ng = jnp.where(hi > lo, (hi - lo_al + G - 1) // G, 0)
            if start:
                def issue(g, carry, e=e, lo_al=lo_al, off=off):
                    granule_copy(sl, e, pl.multiple_of(lo_al + g * G, G), pl.multiple_of(off + g * G, G)).start()
                    return carry

                lax.fori_loop(0, ng, issue, 0)
            off = off + ng * G
        return off

    @pl.when(i == 0)
    def _():
        gather_block(blk, slot, True)

    @pl.when(i + 1 < nsteps)
    def _():
        gather_block(blk + 1, 1 - slot, True)

    off = gather_block(blk, slot, False)

    def wait_one(g, carry):
        granule_copy(slot, 0, 0, 0).wait()
        return carry

    lax.fori_loop(0, off // G, wait_one, 0)

    acc_ref[...] = jnp.zeros_like(acc_ref)
    t0 = (blk * BM).astype(f32)
    lane_tok = lax.broadcasted_iota(jnp.int32, (KC, BM), 1).astype(f32)
    row = lax.broadcasted_iota(jnp.int32, (KC, 1), 0)

    def chunk(c, carry):
        win = buf_ref[slot, pl.ds(pl.multiple_of(c * KC, KC), KC), :]
        tok = (win[:, D + SIDE_TOK:D + SIDE_TOK + 1].astype(f32) * TOK_RADIX
               + win[:, D + SIDE_TOK + 1:D + SIDE_TOK + 2].astype(f32))
        tok = jnp.where(row + c * KC < off, tok, -1.0)
        onehot_t = (tok - t0 == lane_tok).astype(bf16)
        acc_ref[...] += lax.dot_general(onehot_t, win[:, :D], (((0,), (0,)), ((), ())), preferred_element_type=f32)
        return carry

    lax.fori_loop(0, (off + KC - 1) // KC, chunk, 0)
    y = _ln_rows(DN_ALPHA * x_ref[...] + acc_ref[...], g_ref[...], b_ref[...])
    xf_ref[...] = y
    if want_bf16:
        xb_ref[...] = y.astype(bf16)


def _combine_ln(x, ye, lo, g, b, row_off, rows, want_bf16):
    n, D = x.shape
    per, R, DY = ye[0].shape
    E = per * len(ye)
    BM = COMB_BM
    nblk_all = n // BM
    ob = row_off // BM
    kmax = -(-E * (BM + COMB_G) // COMB_K) * COMB_K
    irow = pl.BlockSpec((BM, D), lambda i, lo: (i + ob, 0))
    orow = pl.BlockSpec((BM, D), lambda i, lo: (i, 0))
    vec = pl.BlockSpec((1, D), lambda i, lo: (0, 0))
    out_shape = [jax.ShapeDtypeStruct((rows, D), jnp.float32)]
    out_specs = [orow]
    if want_bf16:
        out_shape.append(jax.ShapeDtypeStruct((rows, D), jnp.bfloat16))
        out_specs.append(orow)
    return pl.pallas_call(
        functools.partial(_combine_kernel, nblk_all=nblk_all, blk_off=ob, want_bf16=want_bf16, parts=len(ye)),
        name="moe_combine_ln",
        grid_spec=pltpu.PrefetchScalarGridSpec(
            num_scalar_prefetch=1, grid=(rows // BM,),
            in_specs=[irow, vec, vec] + [pl.BlockSpec(memory_space=pl.ANY)] * len(ye), out_specs=out_specs,
            scratch_shapes=[pltpu.VMEM((2, kmax, DY), jnp.bfloat16), pltpu.VMEM((BM, D), jnp.float32),
                            pltpu.SemaphoreType.DMA((2,))]),
        out_shape=out_shape,
        compiler_params=_cparams("arbitrary"),
    )(lo, x, g.reshape(1, D), b.reshape(1, D), *ye)


def _route_kernel(all_ref, a_ref, idx_ref, cprev_ref, thr_ref, need_ref, *, cap, tok_off):
    f32, bf16, i32 = jnp.float32, jnp.bfloat16, jnp.int32
    e = pl.program_id(0)

    def total(x):
        return jnp.sum(jnp.sum(x, axis=2, keepdims=True), axis=1, keepdims=True)

    @pl.when(e == 0)
    def _():
        bits_all = pltpu.bitcast(all_ref[...], i32)

        def refine(i, prefix):
            cand = prefix | jnp.left_shift(jnp.int32(1), 30 - i)
            return jnp.where(total((bits_all >= cand).astype(i32)) >= cap, cand, prefix)

        thr_all = lax.fori_loop(0, 31, refine, jnp.zeros((all_ref.shape[0], 1, 1), i32))
        thr_ref[...] = jnp.broadcast_to(thr_all, thr_ref.shape)
        need_ref[...] = jnp.broadcast_to(cap - total((bits_all > thr_all).astype(i32)), need_ref.shape)

    a = a_ref[...]
    R = a.shape[0]
    bits = pltpu.bitcast(a, i32)
    thr = thr_ref[e][0:1, 0:1]
    need = need_ref[e][0:1, 0:1]
    gt, eq = bits > thr, bits == thr

    li = lax.broadcasted_iota(i32, (LANE, LANE), 0)
    lj = lax.broadcasted_iota(i32, (LANE, LANE), 1)
    upto = (li <= lj).astype(bf16)
    ri = lax.broadcasted_iota(i32, (R, R), 0)
    rj = lax.broadcasted_iota(i32, (R, R), 1)
    before = (rj < ri).astype(bf16)

    def counts(mask):
        inc = jnp.dot(mask.astype(bf16), upto, preferred_element_type=f32)
        tot = jnp.broadcast_to(inc[:, LANE - 1:LANE], (R, LANE)).astype(bf16)
        return inc, jnp.dot(before, tot, preferred_element_type=f32)

    eq_inc, eq_prev = counts(eq)
    sel = gt | (eq & (eq_prev + eq_inc - 1.0 < need.astype(f32)))
    cl, cprev = counts(sel)
    cprev_ref[...] = cprev[:, :1].astype(i32)

    selb = sel.astype(bf16)
    c_upto = jnp.sum(lax.dot_general(selb, (ri <= rj).astype(bf16), (((0,), (0,)), ((), ())),
                                     preferred_element_type=f32), axis=0, keepdims=True)
    c_before = jnp.sum(lax.dot_general(selb, (ri < rj).astype(bf16), (((0,), (0,)), ((), ())),
                                       preferred_element_type=f32), axis=0, keepdims=True)
    slot = lax.broadcasted_iota(i32, (cap, 1), 0).astype(f32)
    onehot = ((c_before <= slot) & (slot < c_upto)).astype(bf16)
    lane = lax.broadcasted_iota(i32, (R, LANE), 1)
    cp = cprev.astype(i32)
    aux = jnp.where(lane == 0, cp // TOK_RADIX, jnp.where(lane == 1, cp % TOK_RADIX,
                    jnp.where(lane == 2, lax.broadcasted_iota(i32, (R, LANE), 0), 0)))
    rhs = jnp.concatenate([cl.astype(bf16), aux.astype(f32).astype(bf16)], axis=1)
    m = jnp.dot(onehot, rhs, preferred_element_type=f32)
    base = m[:, LANE:LANE + 1] * TOK_RADIX + m[:, LANE + 1:LANE + 2]
    chunk = m[:, LANE + 2:LANE + 3]
    pos = jnp.sum((m[:, :LANE] <= slot - base).astype(f32), axis=1, keepdims=True)
    idx_ref[...] = (chunk * LANE + pos).astype(i32) + tok_off


def _route(aff_t, cap, tok_off):
    E, R, _ = aff_t.shape
    return pl.pallas_call(
        functools.partial(_route_kernel, cap=cap, tok_off=tok_off), name="moe_route", grid=(E,),
        in_specs=[pl.BlockSpec((E, R, LANE), lambda e: (0, 0, 0)), pl.BlockSpec((None, R, LANE), lambda e: (e, 0, 0))],
        out_specs=[pl.BlockSpec((None, cap, 1), lambda e: (e, 0, 0)), pl.BlockSpec((None, R, 1), lambda e: (e, 0, 0))],
        out_shape=[jax.ShapeDtypeStruct((E, cap, 1), jnp.int32), jax.ShapeDtypeStruct((E, R, 1), jnp.int32)],
        scratch_shapes=[pltpu.VMEM((E, 8, LANE), jnp.int32), pltpu.VMEM((E, 8, LANE), jnp.int32)],
        compiler_params=_cparams("arbitrary"),
    )(aff_t, aff_t)


def _expert_choice(xs, aff, seqs, w1, w3, w2, layer):
    assert COMB_BM == LANE
    aff_t = _aff_chunks(aff)
    idxs, los = [], []
    start = slot0 = 0
    for num, T in seqs:
        m = num * T
        cap = max(1, CAPACITY_FACTOR * m // N_EXPERTS)
        idx, cprev = _route(aff_t[:, start // LANE:(start + m) // LANE], cap, start)
        idxs.append(idx[..., 0])
        los.append(cprev[..., 0] + slot0)
        start += m
        slot0 += cap
    lo = jnp.concatenate(los + [jnp.full((N_EXPERTS, 1), slot0, jnp.int32)], axis=1).reshape(-1)
    idx = jnp.concatenate(idxs, axis=1)
    per = N_EXPERTS // MOE_PARTS
    ye = [_expert_ffn(xs[idx[p * per:(p + 1) * per]], w1, w3, w2, layer, p * per) for p in range(MOE_PARTS)]
    return ye, lo


def _pack_w_in(w_in):
    sizes = (GLA_HEADS * GLA_DK, GLA_HEADS * GLA_DK, GLA_HEADS * GLA_DV, GLA_HEADS * GLA_DV, 2 * GLA_RANK,
             SWA_NH * SWA_DH, SWA_NH * SWA_DH, SWA_NH * SWA_DH, FNET_GROUPS * FNET_DG,
             RET_HEADS * RET_DH, RET_HEADS * RET_DH, RET_HEADS * RET_DH, RET_HEADS * RET_DH)
    offs = np.concatenate([[0], np.cumsum(sizes)])
    part = lambda i: w_in[..., offs[i]:offs[i + 1]]
    a_q, a_k = part(0), part(1)
    cols = []
    for hh in range(GLA_HEADS):
        cols += [a_q[..., hh * GLA_DK:(hh + 1) * GLA_DK], a_k[..., hh * GLA_DK:(hh + 1) * GLA_DK]]
    cols += [part(2), part(3), part(4),
             jnp.zeros(w_in.shape[:2] + (C_OFF - LR_OFF - 2 * GLA_RANK,), w_in.dtype)]
    cols += [part(i) for i in range(8, 13)]
    out = jnp.concatenate(cols, axis=-1)
    assert out.shape[-1] == IN_COLS_P
    GW = SWA_HEADS * SWA_DH
    swa = [jnp.concatenate([part(i)[..., g * GW:(g + 1) * GW] for i in (5, 6, 7)], axis=-1)
           for g in range(len(SWA_GROUPS))]
    return out, swa


def _block_tables(seqs, blk):
    first, last, pos = [], [], []
    for num, T in seqs:
        per = T // blk
        assert per * blk == T
        for _ in range(num):
            first += [1] + [0] * (per - 1)
            last += [0] * (per - 1) + [1]
            pos += list(range(per))
    mk = lambda a: jnp.asarray(np.array(a, np.int32))
    return mk(first), mk(last), mk(pos)


def kernel(x_prompt, x_sample, ln_in_g, ln_in_b, w_in, gla_w_up, gla_b_up, ret_decay, w_gate, b_gate, w_branch,
           w_out, ln1_g, ln1_b, w_router, w_e1, w_e3, w_e2, ln2_g, ln2_b):
    bf16 = jnp.bfloat16
    D = D_MODEL
    seqs = (x_prompt.shape[:2], x_sample.shape[:2])
    n_p, n_s = x_prompt.shape[0] * x_prompt.shape[1], x_sample.shape[0] * x_sample.shape[1]
    n = n_p + n_s
    t_max = max(T for _, T in seqs)
    seq_first, seq_last, pos_blk = _block_tables(seqs, SEQ_BLOCK)

    w_in_p, w_swa = _pack_w_in(w_in.astype(bf16))
    wo = w_out.astype(bf16)
    wr = jnp.pad(w_router, ((0, 0), (0, 0), (0, LANE - N_EXPERTS))).astype(bf16)
    wg, wb, we1, we3, we2 = w_gate, w_branch, w_e1, w_e3, w_e2

    xf, xb = _ln_in(x_prompt.reshape(n_p, D), x_sample.reshape(n_s, D), ln_in_g, ln_in_b)
    for l in range(DEPTH):
        h = _matmul(xb, w_in_p[l], bf16, bm=1024, bn=IN_BN, name="in_proj")
        wup, bup = _pack_gla_up(gla_w_up[l], gla_b_up[l])
        ys = (_gla(h, seq_first, seq_last, wup, bup),
              _swa(xb, [w[l] for w in w_swa], seqs),
              _fnet(h, seqs),
              _retention(h, seq_first, seq_last, pos_blk, ret_decay[l], t_max))
        merged = _merge(xb, wg, b_gate[l], ys, wb, l)
        x1f, x1s, aff = _outproj(merged, wo[l], xf, ln1_g[l], ln1_b[l], wr[l])
        ye, lo = _expert_choice(x1s, aff, seqs, we1, we3, we2, l)
        if l + 1 < DEPTH:
            xf, xb = _combine_ln(x1f, ye, lo, ln2_g[l], ln2_b[l], 0, n, True)
        else:
            y_p, = _combine_ln(x1f, ye, lo, ln2_g[l], ln2_b[l], 0, n_p, False)
            y_s, = _combine_ln(x1f, ye, lo, ln2_g[l], ln2_b[l], n_p, n_s, False)
    return (y_p.reshape(x_prompt.shape), y_s.reshape(x_sample.shape))
```
